```python
import math
import jax, jax.numpy as jnp
from jax import lax
import numpy as np

D_MODEL = 2048
BATCH = 4
SEQ = 4096
DEPTH = 2

PLE_DIM = 256
D_FF = 5632
ALPHA = (2 * DEPTH) ** 0.25
BETA = (8 * DEPTH) ** -0.25
Q_BLOCK = 128
ROPE_THETA = 10000.0
NEG_INF = -1e30
LN_EPS = 1e-5
RMS_EPS = 1e-6

S5_WIDTH = 512
S5_GROUP = 16
S5_GROUPS = S5_WIDTH // S5_GROUP
S5_STATE = 64
S5_DT_MIN = 1e-3
S5_DT_MAX = 1e-1

MLA_HEADS = 4
MLA_Q_RANK = 512
MLA_KV_RANK = 128
MLA_NOPE = 128
MLA_ROPE = 64
MLA_V = 128

RET_HEADS = 4
RET_QK = 64
RET_V = 128
RET_CHUNK = 128

DIFF_HEADS = 4
DIFF_QK = 64
DIFF_V = 128

T5_BUCKETS = 32
T5_MAX_DIST = 128

IN_SPLIT_SIZES = (S5_WIDTH, MLA_Q_RANK, MLA_KV_RANK, MLA_ROPE,
                  RET_HEADS * RET_QK, RET_HEADS * RET_QK, RET_HEADS * RET_V, RET_HEADS * RET_V,
                  DIFF_HEADS * 2 * DIFF_QK, DIFF_HEADS * 2 * DIFF_QK, DIFF_HEADS * DIFF_V)
IN_WIDTH = sum(IN_SPLIT_SIZES)
MIX_WIDTH = S5_WIDTH + MLA_HEADS * MLA_V + RET_HEADS * RET_V + DIFF_HEADS * DIFF_V

kernel_name = 'hybrid_s5_mla_retention_diffattn_deepnorm'

F32 = jnp.float32


def layer_norm(x, g, b):
    xf = x.astype(F32)
    mu = jnp.mean(xf, -1, keepdims=True)
    var = jnp.mean(jnp.square(xf - mu), -1, keepdims=True)
    return ((xf - mu) * lax.rsqrt(var + LN_EPS) * g.astype(F32) + b.astype(F32)).astype(x.dtype)


def rms_norm(x, g):
    xf = x.astype(F32)
    xf = xf * lax.rsqrt(jnp.mean(jnp.square(xf), -1, keepdims=True) + RMS_EPS)
    return (xf * g.astype(F32)).astype(x.dtype)


def swiglu(x, w_gate, w_up, w_down):
    return (jax.nn.silu(x @ w_gate) * (x @ w_up)) @ w_down


def rope_cos_sin(positions, dim):
    inv_freq = 1.0 / (ROPE_THETA ** (jnp.arange(0, dim, 2, dtype=F32) / dim))
    ang = positions.astype(F32)[..., None] * inv_freq
    return jnp.cos(ang)[:, :, None, :], jnp.sin(ang)[:, :, None, :]


def apply_rope(x, cos, sin):
    x1, x2 = jnp.split(x.astype(F32), 2, axis=-1)
    return jnp.concatenate([x1 * cos - x2 * sin, x1 * sin + x2 * cos], -1).astype(x.dtype)


def t5_bucket(dist):
    n = jnp.maximum(dist, 0)
    max_exact = T5_BUCKETS // 2
    nf = jnp.maximum(n, 1).astype(F32)
    large = max_exact + (jnp.log(nf / max_exact) / math.log(T5_MAX_DIST / max_exact)
                         * (T5_BUCKETS - max_exact)).astype(jnp.int32)
    large = jnp.minimum(large, T5_BUCKETS - 1)
    return jnp.where(n < max_exact, n, large)


def causal_block_sweep(block_fn, n_pos):
    out = lax.map(block_fn, jnp.arange(n_pos // Q_BLOCK))
    out = jnp.moveaxis(out, 0, 1)
    return out.reshape(out.shape[0], n_pos, out.shape[3], out.shape[4])


def complex_linear_combine(e1, e2):
    a1r, a1i, b1r, b1i = e1
    a2r, a2i, b2r, b2i = e2
    return (a2r * a1r - a2i * a1i,
            a2r * a1i + a2i * a1r,
            a2r * b1r - a2i * b1i + b2r,
            a2r * b1i + a2i * b1r + b2i)


def s5_mixer(u, lam_re, lam_im, log_dt, b_re, b_im, c_re, c_im, d, w_glu, b_glu):
    bsz, n_pos, _ = u.shape
    uf = u.astype(F32).reshape(bsz, n_pos, S5_GROUPS, S5_GROUP)
    lr = lam_re.astype(F32)
    li = lam_im.astype(F32)
    dt = jnp.exp(log_dt.astype(F32))[:, None]
    mag = jnp.exp(lr * dt)
    ar = mag * jnp.cos(li * dt)
    ai = mag * jnp.sin(li * dt)
    den = lr * lr + li * li
    fr = ((ar - 1.0) * lr + ai * li) / den
    fi = (ai * lr - (ar - 1.0) * li) / den
    br = b_re.astype(F32)
    bi = b_im.astype(F32)
    bbr = fr[..., None] * br - fi[..., None] * bi
    bbi = fr[..., None] * bi + fi[..., None] * br
    bu_r = jnp.einsum('bsgc,gpc->bsgp', uf, bbr)
    bu_i = jnp.einsum('bsgc,gpc->bsgp', uf, bbi)
    a_r = jnp.broadcast_to(ar, bu_r.shape)
    a_i = jnp.broadcast_to(ai, bu_i.shape)
    _, _, h_r, h_i = lax.associative_scan(complex_linear_combine, (a_r, a_i, bu_r, bu_i), axis=1)
    y = (jnp.einsum('bsgp,gcp->bsgc', h_r, c_re.astype(F32))
         - jnp.einsum('bsgp,gcp->bsgc', h_i, c_im.astype(F32))
         + d.astype(F32).reshape(S5_GROUPS, S5_GROUP) * uf)
    y = jax.nn.gelu(y.reshape(bsz, n_pos, S5_WIDTH)).astype(u.dtype)
    return y * jax.nn.sigmoid(y @ w_glu + b_glu)


def mla_mixer(c_q, c_kv, k_r, cos, sin, q_norm_g, w_uq, kv_norm_g, w_ukv):
    bsz, n_pos, _ = c_q.shape
    q = (rms_norm(c_q, q_norm_g) @ w_uq).reshape(bsz, n_pos, MLA_HEADS, MLA_NOPE + MLA_ROPE)
    q = jnp.concatenate([q[..., :MLA_NOPE], apply_rope(q[..., MLA_NOPE:], cos, sin)], -1)
    kv = (rms_norm(c_kv, kv_norm_g) @ w_ukv).reshape(bsz, n_pos, MLA_HEADS, MLA_NOPE + MLA_V)
    k_rope = apply_rope(k_r[:, :, None, :], cos, sin)
    k = jnp.concatenate([kv[..., :MLA_NOPE],
                         jnp.broadcast_to(k_rope, (bsz, n_pos, MLA_HEADS, MLA_ROPE))], -1)
    v = kv[..., MLA_NOPE:]
    scale = (MLA_NOPE + MLA_ROPE) ** -0.5
    key_idx = jnp.arange(n_pos)

    def block(i):
        start = i * Q_BLOCK
        qb = lax.dynamic_slice_in_dim(q, start, Q_BLOCK, axis=1)
        s = jnp.einsum('bqhd,bkhd->bhqk', qb, k).astype(F32) * scale
        causal = (start + jnp.arange(Q_BLOCK))[:, None] >= key_idx[None, :]
        pr = jax.nn.softmax(jnp.where(causal, s, NEG_INF), axis=-1)
        return jnp.einsum('bhqk,bkhd->bqhd', pr.astype(v.dtype), v)

    return causal_block_sweep(block, n_pos).reshape(bsz, n_pos, MLA_HEADS * MLA_V)


def retention_mixer(q, k, v, g, cos, sin):
    bsz, n_pos, _ = q.shape
    n_chunks = n_pos // RET_CHUNK
    q = apply_rope(q.reshape(bsz, n_pos, RET_HEADS, RET_QK), cos, sin).astype(F32)
    k = apply_rope(k.reshape(bsz, n_pos, RET_HEADS, RET_QK), cos, sin).astype(F32) * (RET_QK ** -0.5)
    v = v.reshape(bsz, n_pos, RET_HEADS, RET_V).astype(F32)
    log_gamma = jnp.log(1.0 - jnp.power(2.0, -5.0 - jnp.arange(RET_HEADS, dtype=F32)))
    idx = jnp.arange(RET_CHUNK, dtype=F32)
    rel = idx[:, None] - idx[None, :]
    intra = jnp.where(rel >= 0, jnp.exp(log_gamma[:, None, None] * jnp.maximum(rel, 0.0)), 0.0)
    qc = q.reshape(bsz, n_chunks, RET_CHUNK, RET_HEADS, RET_QK)
    kc = k.reshape(bsz, n_chunks, RET_CHUNK, RET_HEADS, RET_QK)
    vc = v.reshape(bsz, n_chunks, RET_CHUNK, RET_HEADS, RET_V)
    scores = jnp.einsum('bnihd,bnjhd->bnhij', qc, kc) * intra
    inner = jnp.einsum('bnhij,bnjhv->bnihv', scores, vc)
    k_decay = jnp.exp(log_gamma[None, :] * (RET_CHUNK - 1 - idx)[:, None])
    kv = jnp.einsum('bnjhd,jh,bnjhv->bnhdv', kc, k_decay, vc)
    chunk_decay = jnp.exp(log_gamma * RET_CHUNK)[:, None, None]

    def step(state, kv_n):
        return state * chunk_decay + kv_n, state

    _, prev = lax.scan(step, jnp.zeros((bsz, RET_HEADS, RET_QK, RET_V), F32), jnp.moveaxis(kv, 1, 0))
    prev = jnp.moveaxis(prev, 0, 1)
    q_decay = jnp.exp(log_gamma[None, :] * (idx + 1.0)[:, None])
    cross = jnp.einsum('bnihd,ih,bnhdv->bnihv', qc, q_decay, prev)
    o = (inner + cross).reshape(bsz, n_pos, RET_HEADS, RET_V)
    mu = jnp.mean(o, -1, keepdims=True)
    var = jnp.mean(jnp.square(o - mu), -1, keepdims=True)
    o = (o - mu) * lax.rsqrt(var + LN_EPS)
    return (jax.nn.silu(g.astype(F32)) * o.reshape(bsz, n_pos, RET_HEADS * RET_V)).astype(g.dtype)


def diff_mixer(q, k, v, positions, rel_bias, lq1, lk1, lq2, lk2, subln_g, lambda_init):
    bsz, n_pos, _ = q.shape
    q = q.reshape(bsz, n_pos, DIFF_HEADS, 2, DIFF_QK)
    k = k.reshape(bsz, n_pos, DIFF_HEADS, 2, DIFF_QK)
    v = v.reshape(bsz, n_pos, DIFF_HEADS, DIFF_V)
    lam = (jnp.exp(jnp.sum(lq1.astype(F32) * lk1.astype(F32)))
           - jnp.exp(jnp.sum(lq2.astype(F32) * lk2.astype(F32))) + lambda_init)
    scale = DIFF_QK ** -0.5
    key_idx = jnp.arange(n_pos)
    table = rel_bias.astype(F32)

    def block(i):
        start = i * Q_BLOCK
        qb = lax.dynamic_slice_in_dim(q, start, Q_BLOCK, axis=1)
        pos_q = lax.dynamic_slice_in_dim(positions, start, Q_BLOCK, axis=1)
        bucket = t5_bucket(pos_q[:, :, None] - positions[:, None, :])
        bias = jnp.transpose(jnp.take(table, bucket, axis=0), (0, 3, 1, 2))
        s = jnp.einsum('bqhmd,bkhmd->bhmqk', qb, k).astype(F32) * scale + bias[:, :, None]
        causal = (start + jnp.arange(Q_BLOCK))[:, None] >= key_idx[None, :]
        pr = jax.nn.softmax(jnp.where(causal, s, NEG_INF), axis=-1)
        a = pr[:, :, 0] - lam * pr[:, :, 1]
        return jnp.einsum('bhqk,bkhd->bqhd', a.astype(v.dtype), v)

    o = causal_block_sweep(block, n_pos)
    o = rms_norm(o, subln_g) * (1.0 - lambda_init)
    return o.reshape(bsz, n_pos, DIFF_HEADS * DIFF_V)


def setup_inputs(seed: int = 0) -> dict:
    key = jax.random.key(seed)
    ks = list(jax.random.split(key, 64))
    ctr = [0]

    def nk():
        ctr[0] += 1
        return ks[ctr[0] - 1]

    def nrm(shape, scale):
        return scale * jax.random.normal(nk(), shape, F32)

    def gain(shape):
        return 1.0 + nrm(shape, 0.02)

    L, D, F = DEPTH, D_MODEL, D_FF
    G, P = S5_GROUPS, S5_STATE
    x = nrm((BATCH, SEQ, D), 1.0)
    p = nrm((L, BATCH, SEQ, PLE_DIM), 1.0)
    offsets = jax.random.randint(nk(), (BATCH, 1), 0, 1024, jnp.int32)
    positions = offsets + jnp.arange(SEQ, dtype=jnp.int32)[None, :]
    rel_bias = nrm((T5_BUCKETS, DIFF_HEADS), 0.5)
    ffn1_w_gate = nrm((L, D, F), D ** -0.5)
    ffn1_w_up = nrm((L, D, F), D ** -0.5)
    ffn1_w_down = nrm((L, F, D), BETA * F ** -0.5)
    ln1_g = gain((L, D))
    ln1_b = nrm((L, D), 0.02)
    w_in = nrm((L, D, IN_WIDTH), D ** -0.5)
    w_out = nrm((L, MIX_WIDTH, D), BETA * MIX_WIDTH ** -0.5)
    ln2_g = gain((L, D))
    ln2_b = nrm((L, D), 0.02)
    s5_lambda_re = -0.5 + nrm((L, G, P), 0.01)
    s5_lambda_im = jnp.pi * jnp.arange(P, dtype=F32)[None, None, :] + nrm((L, G, P), 0.01)
    s5_log_dt = jax.random.uniform(nk(), (L, G), F32, math.log(S5_DT_MIN), math.log(S5_DT_MAX))
    s5_b_re = nrm((L, G, P, S5_GROUP), (2.0 * S5_GROUP) ** -0.5)
    s5_b_im = nrm((L, G, P, S5_GROUP), (2.0 * S5_GROUP) ** -0.5)
    s5_c_re = nrm((L, G, S5_GROUP, P), (2.0 * P) ** -0.5)
    s5_c_im = nrm((L, G, S5_GROUP, P), (2.0 * P) ** -0.5)
    s5_d = nrm((L, S5_WIDTH), 1.0)
    s5_w_glu = nrm((L, S5_WIDTH, S5_WIDTH), S5_WIDTH ** -0.5)
    s5_b_glu = nrm((L, S5_WIDTH), 0.02)
    mla_q_norm_g = gain((L, MLA_Q_RANK))
    mla_w_uq = nrm((L, MLA_Q_RANK, MLA_HEADS * (MLA_NOPE + MLA_ROPE)), MLA_Q_RANK ** -0.5)
    mla_kv_norm_g = gain((L, MLA_KV_RANK))
    mla_w_ukv = nrm((L, MLA_KV_RANK, MLA_HEADS * (MLA_NOPE + MLA_V)), MLA_KV_RANK ** -0.5)
    diff_lambda_q1 = nrm((L, DIFF_QK), 0.1)
    diff_lambda_k1 = nrm((L, DIFF_QK), 0.1)
    diff_lambda_q2 = nrm((L, DIFF_QK), 0.1)
    diff_lambda_k2 = nrm((L, DIFF_QK), 0.1)
    diff_subln_g = gain((L, DIFF_V))
    ffn2_w_gate = nrm((L, D, F), D ** -0.5)
    ffn2_w_up = nrm((L, D, F), D ** -0.5)
    ffn2_w_down = nrm((L, F, D), BETA * F ** -0.5)
    ple_w_gate = nrm((L, D, D), D ** -0.5)
    ple_b_gate = nrm((L, D), 0.02)
    ple_w_proj = nrm((L, PLE_DIM, D), BETA * PLE_DIM ** -0.5)
    ln3_g = gain((L, D))
    ln3_b = nrm((L, D), 0.02)
    return {'x': x, 'p': p, 'positions': positions, 'rel_bias': rel_bias,
            'ffn1_w_gate': ffn1_w_gate, 'ffn1_w_up': ffn1_w_up, 'ffn1_w_down': ffn1_w_down,
            'ln1_g': ln1_g, 'ln1_b': ln1_b, 'w_in': w_in, 'w_out': w_out,
            'ln2_g': ln2_g, 'ln2_b': ln2_b,
            's5_lambda_re': s5_lambda_re, 's5_lambda_im': s5_lambda_im, 's5_log_dt': s5_log_dt,
            's5_b_re': s5_b_re, 's5_b_im': s5_b_im, 's5_c_re': s5_c_re, 's5_c_im': s5_c_im,
            's5_d': s5_d, 's5_w_glu': s5_w_glu, 's5_b_glu': s5_b_glu,
            'mla_q_norm_g': mla_q_norm_g, 'mla_w_uq': mla_w_uq,
            'mla_kv_norm_g': mla_kv_norm_g, 'mla_w_ukv': mla_w_ukv,
            'diff_lambda_q1': diff_lambda_q1, 'diff_lambda_k1': diff_lambda_k1,
            'diff_lambda_q2': diff_lambda_q2, 'diff_lambda_k2': diff_lambda_k2,
            'diff_subln_g': diff_subln_g,
            'ffn2_w_gate': ffn2_w_gate, 'ffn2_w_up': ffn2_w_up, 'ffn2_w_down': ffn2_w_down,
            'ple_w_gate': ple_w_gate, 'ple_b_gate': ple_b_gate, 'ple_w_proj': ple_w_proj,
            'ln3_g': ln3_g, 'ln3_b': ln3_b}


def reference(x, p, positions, rel_bias,
              ffn1_w_gate, ffn1_w_up, ffn1_w_down, ln1_g, ln1_b, w_in, w_out,
              ln2_g, ln2_b,
              s5_lambda_re, s5_lambda_im, s5_log_dt, s5_b_re, s5_b_im, s5_c_re, s5_c_im,
              s5_d, s5_w_glu, s5_b_glu,
              mla_q_norm_g, mla_w_uq, mla_kv_norm_g, mla_w_ukv,
              diff_lambda_q1, diff_lambda_k1, diff_lambda_q2, diff_lambda_k2, diff_subln_g,
              ffn2_w_gate, ffn2_w_up, ffn2_w_down, ple_w_gate, ple_b_gate, ple_w_proj,
              ln3_g, ln3_b):
    cos_m, sin_m = rope_cos_sin(positions, MLA_ROPE)
    cos_r, sin_r = rope_cos_sin(positions, RET_QK)
    split_points = [int(c) for c in np.cumsum(IN_SPLIT_SIZES)[:-1]]
    for i in range(DEPTH):
        h = 0.5 * swiglu(x, ffn1_w_gate[i], ffn1_w_up[i], ffn1_w_down[i])
        x = layer_norm(ALPHA * x + h, ln1_g[i], ln1_b[i])
        (s5_u, mla_cq, mla_ckv, mla_kr, ret_q, ret_k, ret_v, ret_g,
         diff_q, diff_k, diff_v) = jnp.split(x @ w_in[i], split_points, axis=-1)
        y_s5 = s5_mixer(s5_u, s5_lambda_re[i], s5_lambda_im[i], s5_log_dt[i], s5_b_re[i], s5_b_im[i],
                        s5_c_re[i], s5_c_im[i], s5_d[i], s5_w_glu[i], s5_b_glu[i])
        y_mla = mla_mixer(mla_cq, mla_ckv, mla_kr, cos_m, sin_m, mla_q_norm_g[i], mla_w_uq[i],
                          mla_kv_norm_g[i], mla_w_ukv[i])
        y_ret = retention_mixer(ret_q, ret_k, ret_v, ret_g, cos_r, sin_r)
        lambda_init = 0.8 - 0.6 * math.exp(-0.3 * i)
        y_diff = diff_mixer(diff_q, diff_k, diff_v, positions, rel_bias, diff_lambda_q1[i],
                            diff_lambda_k1[i], diff_lambda_q2[i], diff_lambda_k2[i],
                            diff_subln_g[i], lambda_init)
        mix = jnp.concatenate([y_s5, y_mla.astype(x.dtype), y_ret.astype(x.dtype),
                               y_diff.astype(x.dtype)], axis=-1) @ w_out[i]
        x = layer_norm(ALPHA * x + mix, ln2_g[i], ln2_b[i])
        gate = jax.nn.sigmoid(x @ ple_w_gate[i] + ple_b_gate[i])
        h = 0.5 * swiglu(x, ffn2_w_gate[i], ffn2_w_up[i], ffn2_w_down[i]) + gate * (p[i] @ ple_w_proj[i])
        x = layer_norm(ALPHA * x + h, ln3_g[i], ln3_b[i])
    return x
```

```python
import functools
import math

import numpy as np
import jax
import jax.numpy as jnp
from jax import lax
from jax.experimental import pallas as pl
from jax.experimental.pallas import tpu as pltpu

F32 = jnp.float32
BF16 = jnp.bfloat16

D_MODEL = 2048
DEPTH = 2
PLE_DIM = 256
D_FF = 5632
ALPHA = (2 * DEPTH) ** 0.25
ROPE_THETA = 10000.0
NEG_INF = -1e30
LN_EPS = 1e-5
RMS_EPS = 1e-6

S5_WIDTH = 512
S5_GROUP = 16
S5_GROUPS = 32
S5_STATE = 64
S5_NSTATE = S5_GROUPS * S5_STATE

MLA_HEADS = 4
MLA_Q_RANK = 512
MLA_KV_RANK = 128
MLA_NOPE = 128
MLA_ROPE = 64
MLA_V = 128
MLA_QK_PAD = 256

RET_HEADS = 4
RET_QK = 64
RET_V = 128
RET_CHUNK = 128

DIFF_HEADS = 4
DIFF_QK = 64
DIFF_V = 128

T5_BUCKETS = 32
T5_MAX_DIST = 128
T5_TABLE = 128

LANES = 128
SUBLANES = 8
VMEM_LIMIT_BYTES = 56 * 1024 * 1024

OFF_S5 = 0
OFF_CQ = 512
OFF_RV = 1024
OFF_RG = 1536
OFF_DQ = 2048
OFF_DK = 2560
OFF_DV = 3072
OFF_RQ = 3584
OFF_RQR = 3840
OFF_RK = 4096
OFF_RKR = 4352
OFF_CKV = 4608
OFF_KR = 4736
OFF_KRR = 4864
IN_WIDE = 4992
IN_TN = 1664


def _cparams(sem):
    return pltpu.CompilerParams(dimension_semantics=sem, vmem_limit_bytes=VMEM_LIMIT_BYTES)


def _layer_norm(y, g, b):
    mu = jnp.mean(y, axis=-1, keepdims=True)
    yc = y - mu
    var = jnp.mean(yc * yc, axis=-1, keepdims=True)
    return yc * lax.rsqrt(var + LN_EPS) * g + b


def _ffn_ln_kernel(*refs, nf, has_extra):
    if has_extra:
        x_ref, wg_ref, wu_ref, wd_ref, g_ref, b_ref, e_ref, o_ref, xb_ref, acc_ref = refs
    else:
        x_ref, wg_ref, wu_ref, wd_ref, g_ref, b_ref, o_ref, xb_ref, acc_ref = refs
        e_ref = None
    f = pl.program_id(1)

    @pl.when(f == 0)
    def _():
        xb_ref[...] = x_ref[...].astype(BF16)
        acc_ref[...] = jnp.zeros_like(acc_ref)

    xb = xb_ref[...]
    gate = jnp.dot(xb, wg_ref[...], preferred_element_type=F32)
    up = jnp.dot(xb, wu_ref[...], preferred_element_type=F32)
    h = gate * jax.nn.sigmoid(gate) * up
    acc_ref[...] += jnp.dot(h.astype(BF16), wd_ref[...], preferred_element_type=F32)

    @pl.when(f == nf - 1)
    def _():
        y = ALPHA * x_ref[...] + 0.5 * acc_ref[...]
        if e_ref is not None:
            y = y + e_ref[...]
        o_ref[...] = _layer_norm(y, g_ref[...], b_ref[...])


def _ffn_ln(x, wg, wu, wd, g, b, extra=None, *, tm=512, tf=512):
    m, d = x.shape
    f_dim = wg.shape[1]
    tm = min(tm, m)
    nf = f_dim // tf
    in_specs = [
        pl.BlockSpec((tm, d), lambda i, f: (i, 0)),
        pl.BlockSpec((d, tf), lambda i, f: (0, f)),
        pl.BlockSpec((d, tf), lambda i, f: (0, f)),
        pl.BlockSpec((tf, d), lambda i, f: (f, 0)),
        pl.BlockSpec((1, d), lambda i, f: (0, 0)),
        pl.BlockSpec((1, d), lambda i, f: (0, 0)),
    ]
    args = [x, wg, wu, wd, g, b]
    if extra is not None:
        in_specs.append(pl.BlockSpec((tm, d), lambda i, f: (i, 0)))
        args.append(extra)
    return pl.pallas_call(
        functools.partial(_ffn_ln_kernel, nf=nf, has_extra=extra is not None),
        grid=(m // tm, nf),
        in_specs=in_specs,
        out_specs=pl.BlockSpec((tm, d), lambda i, f: (i, 0)),
        out_shape=jax.ShapeDtypeStruct((m, d), F32),
        scratch_shapes=[pltpu.VMEM((tm, d), BF16), pltpu.VMEM((tm, d), F32)],
        compiler_params=_cparams(("parallel", "arbitrary")),
        name="ffn_ln",
    )(*args)


def _ple_kernel(x_ref, p_ref, wg_ref, bg_ref, wp_ref, o_ref):
    xb = x_ref[...].astype(BF16)
    pb = p_ref[...].astype(BF16)
    gate = jax.nn.sigmoid(jnp.dot(xb, wg_ref[...], preferred_element_type=F32) + bg_ref[...])
    o_ref[...] = gate * jnp.dot(pb, wp_ref[...], preferred_element_type=F32)


def _ple(x, p, wg, bg, wp, *, tm=512):
    m, d = x.shape
    tm = min(tm, m)
    return pl.pallas_call(
        _ple_kernel,
        grid=(m // tm,),
        in_specs=[
            pl.BlockSpec((tm, d), lambda i: (i, 0)),
            pl.BlockSpec((tm, PLE_DIM), lambda i: (i, 0)),
            pl.BlockSpec((d, d), lambda i: (0, 0)),
            pl.BlockSpec((1, d), lambda i: (0, 0)),
            pl.BlockSpec((PLE_DIM, d), lambda i: (0, 0)),
        ],
        out_specs=pl.BlockSpec((tm, d), lambda i: (i, 0)),
        out_shape=jax.ShapeDtypeStruct((m, d), F32),
        compiler_params=_cparams(("parallel",)),
        name="ple",
    )(x, p, wg, bg, wp)


def _inproj_kernel(x_ref, w_ref, o_ref):
    o_ref[...] = jnp.dot(x_ref[...].astype(BF16), w_ref[...], preferred_element_type=F32)


def _inproj(x, w, *, tm=512):
    m, d = x.shape
    n = w.shape[1]
    tm = min(tm, m)
    return pl.pallas_call(
        _inproj_kernel,
        grid=(n // IN_TN, m // tm),
        in_specs=[
            pl.BlockSpec((tm, d), lambda j, i: (i, 0)),
            pl.BlockSpec((d, IN_TN), lambda j, i: (0, j)),
        ],
        out_specs=pl.BlockSpec((tm, IN_TN), lambda j, i: (i, j)),
        out_shape=jax.ShapeDtypeStruct((m, n), F32),
        compiler_params=_cparams(("parallel", "parallel")),
        name="inproj",
    )(x, w)


def _rope_kernel(pos_ref, freq_ref, cos_ref, sin_ref):
    ang = pos_ref[...].astype(F32) * freq_ref[...]
    cos_ref[...] = jnp.cos(ang)
    sin_ref[...] = jnp.sin(ang)


def _rope_tables(pos_col, *, tm=1024):
    m = pos_col.shape[0]
    tm = min(tm, m)
    half = MLA_ROPE // 2
    inv = 1.0 / (ROPE_THETA ** (np.arange(0, MLA_ROPE, 2, dtype=np.float64) / MLA_ROPE))
    freq = jnp.asarray(np.tile(inv, LANES // half)[None, :], F32)
    return pl.pallas_call(
        _rope_kernel,
        grid=(m // tm,),
        in_specs=[pl.BlockSpec((tm, 1), lambda i: (i, 0)), pl.BlockSpec((1, LANES), lambda i: (0, 0))],
        out_specs=[pl.BlockSpec((tm, LANES), lambda i: (i, 0))] * 2,
        out_shape=[jax.ShapeDtypeStruct((m, LANES), F32)] * 2,
        compiler_params=_cparams(("parallel",)),
        name="rope_tables",
    )(pos_col, freq)


def _t5_bucket_static():
    n = np.arange(T5_TABLE)
    max_exact = T5_BUCKETS // 2
    nf = np.maximum(n, 1).astype(np.float64)
    large = max_exact + (np.log(nf / max_exact) / math.log(T5_MAX_DIST / max_exact)
                         * (T5_BUCKETS - max_exact)).astype(np.int64)
    large = np.minimum(large, T5_BUCKETS - 1)
    return np.where(n < max_exact, n, large)


def _bias_table_kernel(rb_ref, onehot_ref, o_ref):
    rb = rb_ref[...]
    oh = onehot_ref[...]
    rows = [jnp.sum(oh * rb[:, h:h + 1], axis=0, keepdims=True) for h in range(DIFF_HEADS)]
    o_ref[...] = jnp.concatenate(rows, axis=0)


def _bias_table(rel_bias):
    bucket = _t5_bucket_static()
    onehot = jnp.asarray((np.arange(T5_BUCKETS)[:, None] == bucket[None, :]).astype(np.float32))
    return pl.pallas_call(
        _bias_table_kernel,
        out_shape=jax.ShapeDtypeStruct((DIFF_HEADS, T5_TABLE), F32),
        name="t5_bias_table",
    )(rel_bias, onehot)


def _s5_disc_kernel(lr_ref, li_ref, ldt_ref, br_ref, bi_ref, apr_ref, api_ref, bbr_ref, bbi_ref):
    lr = lr_ref[...]
    li = li_ref[...]
    dt = jnp.exp(ldt_ref[...])
    k = (lax.broadcasted_iota(jnp.int32, (SUBLANES, 1), 0) + 1).astype(F32)
    mag = jnp.exp(lr * dt * k)
    apr_ref[...] = mag * jnp.cos(li * dt * k)
    api_ref[...] = mag * jnp.sin(li * dt * k)
    mag1 = jnp.exp(lr * dt)
    ar = mag1 * jnp.cos(li * dt)
    ai = mag1 * jnp.sin(li * dt)
    den = lr * lr + li * li
    fr = ((ar - 1.0) * lr + ai * li) / den
    fi = (ai * lr - (ar - 1.0) * li) / den
    br = br_ref[...]
    bi = bi_ref[...]
    bbr_ref[...] = fr * br - fi * bi
    bbi_ref[...] = fr * bi + fi * br


def _s5_discretise(lam_re, lam_im, log_dt, b_re, b_im):
    n = S5_NSTATE
    lr = lam_re.reshape(1, n)
    li = lam_im.reshape(1, n)
    ldt = jnp.repeat(log_dt, S5_STATE).reshape(1, n)
    br = jnp.transpose(b_re, (2, 0, 1)).reshape(S5_GROUP, n)
    bi = jnp.transpose(b_im, (2, 0, 1)).reshape(S5_GROUP, n)
    return pl.pallas_call(
        _s5_disc_kernel,
        out_shape=[jax.ShapeDtypeStruct((SUBLANES, n), F32)] * 2 + [jax.ShapeDtypeStruct((S5_GROUP, n), F32)] * 2,
        name="s5_discretise",
    )(lr, li, ldt, br, bi)


def _block_diag_in(bb):
    eye = jnp.eye(S5_GROUPS, dtype=bb.dtype)
    v = bb.reshape(S5_GROUP, S5_GROUPS, S5_STATE)
    out = eye[:, None, :, None] * v[None, :, :, :]
    return out.reshape(S5_WIDTH, S5_NSTATE)


def _block_diag_out(c):
    eye = jnp.eye(S5_GROUPS, dtype=c.dtype)
    v = jnp.transpose(c, (0, 2, 1))
    out = v[:, :, None, :] * eye[:, None, :, None]
    return out.reshape(S5_NSTATE, S5_WIDTH)


S5_LANE_CHUNK = 512


def _s5_kernel(u_ref, bd_ref, cd_ref, apr_ref, api_ref, d_ref, wglu_ref, bglu_ref, o_ref,
               hr_ref, hi_ref, cr_ref, ci_ref, *, tm):
    t = pl.program_id(1)

    @pl.when(t == 0)
    def _():
        cr_ref[...] = jnp.zeros_like(cr_ref)
        ci_ref[...] = jnp.zeros_like(ci_ref)

    u = u_ref[...]
    ub = u.astype(BF16)
    n = S5_NSTATE
    hr_ref[...] = jnp.dot(ub, bd_ref[:, :n], preferred_element_type=F32)
    hi_ref[...] = jnp.dot(ub, bd_ref[:, n:], preferred_element_type=F32)

    row = lax.broadcasted_iota(jnp.int32, (SUBLANES, S5_LANE_CHUNK), 0)
    for c in range(n // S5_LANE_CHUNK):
        lanes = pl.ds(c * S5_LANE_CHUNK, S5_LANE_CHUNK)
        apr = apr_ref[:, lanes]
        api = api_ref[:, lanes]
        steps = [(1, apr[0:1], api[0:1]), (2, apr[1:2], api[1:2]), (4, apr[3:4], api[3:4])]

        def body(rg, carry, lanes=lanes, apr=apr, api=api, steps=steps):
            cr, ci = carry
            r0 = pl.multiple_of(rg * SUBLANES, SUBLANES)
            xr = hr_ref[pl.ds(r0, SUBLANES), lanes]
            xi = hi_ref[pl.ds(r0, SUBLANES), lanes]
            for sh, pr, pi in steps:
                keep = row >= sh
                sr = jnp.where(keep, pltpu.roll(xr, sh, 0), 0.0)
                si = jnp.where(keep, pltpu.roll(xi, sh, 0), 0.0)
                xr, xi = xr + pr * sr - pi * si, xi + pr * si + pi * sr
            hr = xr + apr * cr - api * ci
            hi = xi + apr * ci + api * cr
            hr_ref[pl.ds(r0, SUBLANES), lanes] = hr
            hi_ref[pl.ds(r0, SUBLANES), lanes] = hi
            return hr[SUBLANES - 1:SUBLANES], hi[SUBLANES - 1:SUBLANES]

        cr, ci = lax.fori_loop(0, tm // SUBLANES, body, (cr_ref[:, lanes], ci_ref[:, lanes]))
        cr_ref[:, lanes] = cr
        ci_ref[:, lanes] = ci

    y = (jnp.dot(hr_ref[...].astype(BF16), cd_ref[:n, :], preferred_element_type=F32)
         + jnp.dot(hi_ref[...].astype(BF16), cd_ref[n:, :], preferred_element_type=F32)
         + d_ref[...] * u)
    c0 = math.sqrt(2.0 / math.pi)
    y = 0.5 * y * (1.0 + jnp.tanh(c0 * (y + 0.044715 * (y * y * y))))
    gate = jax.nn.sigmoid(jnp.dot(y.astype(BF16), wglu_ref[...], preferred_element_type=F32) + bglu_ref[...])
    o_ref[...] = (y * gate).astype(BF16)


def _s5_mixer(z, bd, cd, apr, api, d, wglu, bglu, *, bsz, n_pos, tm=256):
    tm = min(tm, n_pos)
    nt = n_pos // tm
    m = bsz * n_pos
    n2 = 2 * S5_NSTATE
    const = lambda b, t: (0, 0)
    return pl.pallas_call(
        functools.partial(_s5_kernel, tm=tm),
        grid=(bsz, nt),
        in_specs=[
            pl.BlockSpec((tm, S5_WIDTH), lambda b, t: (b * nt + t, OFF_S5 // S5_WIDTH)),
            pl.BlockSpec((S5_WIDTH, n2), const),
            pl.BlockSpec((n2, S5_WIDTH), const),
            pl.BlockSpec((SUBLANES, S5_NSTATE), const),
            pl.BlockSpec((SUBLANES, S5_NSTATE), const),
            pl.BlockSpec((1, S5_WIDTH), const),
            pl.BlockSpec((S5_WIDTH, S5_WIDTH), const),
            pl.BlockSpec((1, S5_WIDTH), const),
        ],
        out_specs=pl.BlockSpec((tm, S5_WIDTH), lambda b, t: (b * nt + t, 0)),
        out_shape=jax.ShapeDtypeStruct((m, S5_WIDTH), BF16),
        scratch_shapes=[pltpu.VMEM((tm, S5_NSTATE), F32), pltpu.VMEM((tm, S5_NSTATE), F32),
                        pltpu.VMEM((1, S5_NSTATE), F32), pltpu.VMEM((1, S5_NSTATE), F32)],
        compiler_params=_cparams(("parallel", "arbitrary")),
        name="s5_mixer",
    )(z, bd, cd, apr, api, d, wglu, bglu)


def _rms(x, g):
    return x * lax.rsqrt(jnp.mean(x * x, axis=-1, keepdims=True) + RMS_EPS) * g


def _mla_prep_kernel(cq_ref, ckv_ref, kr_ref, krr_ref, cos_ref, sin_ref, qg_ref, kvg_ref,
                     wq1_ref, wq2_ref, wkv_ref, q_ref, k_ref, v_ref):
    cos = cos_ref[...]
    sin = sin_ref[...]
    scale = (MLA_NOPE + MLA_ROPE) ** -0.5
    cqn = _rms(cq_ref[...], qg_ref[...]).astype(BF16)
    z1 = jnp.dot(cqn, wq1_ref[...], preferred_element_type=F32)
    z2 = jnp.dot(cqn, wq2_ref[...], preferred_element_type=F32)
    ckvn = _rms(ckv_ref[...], kvg_ref[...]).astype(BF16)
    zkv = jnp.dot(ckvn, wkv_ref[...], preferred_element_type=F32)
    k_rope = (kr_ref[...] * cos + krr_ref[...] * sin).astype(BF16)
    for h in range(MLA_HEADS):
        a = h * MLA_QK_PAD
        q_nope = z1[:, a:a + MLA_NOPE]
        q_rope = z1[:, a + MLA_NOPE:a + MLA_QK_PAD] * cos + z2[:, h * LANES:(h + 1) * LANES] * sin
        q_ref[0, h, :, 0:MLA_NOPE] = (q_nope * scale).astype(BF16)
        q_ref[0, h, :, MLA_NOPE:MLA_QK_PAD] = (q_rope * scale).astype(BF16)
        k_ref[0, h, :, 0:MLA_NOPE] = zkv[:, a:a + MLA_NOPE].astype(BF16)
        k_ref[0, h, :, MLA_NOPE:MLA_QK_PAD] = k_rope
        v_ref[0, h] = zkv[:, a + MLA_NOPE:a + MLA_NOPE + MLA_V].astype(BF16)


def _mla_prep(z, cos, sin, qg, kvg, wq1, wq2, wkv, *, bsz, n_pos, tm=512):
    tm = min(tm, n_pos)
    nt = n_pos // tm
    const = lambda b, t: (0, 0)
    zspec = lambda width, off: pl.BlockSpec((tm, width), lambda b, t: (b * nt + t, off // width))
    hspec = lambda width: pl.BlockSpec((1, MLA_HEADS, tm, width), lambda b, t: (b, 0, t, 0))
    return pl.pallas_call(
        _mla_prep_kernel,
        grid=(bsz, nt),
        in_specs=[
            zspec(MLA_Q_RANK, OFF_CQ), zspec(MLA_KV_RANK, OFF_CKV), zspec(LANES, OFF_KR), zspec(LANES, OFF_KRR),
            pl.BlockSpec((tm, LANES), lambda b, t: (b * nt + t, 0)),
            pl.BlockSpec((tm, LANES), lambda b, t: (b * nt + t, 0)),
            pl.BlockSpec((1, MLA_Q_RANK), const), pl.BlockSpec((1, MLA_KV_RANK), const),
            pl.BlockSpec(wq1.shape, const), pl.BlockSpec(wq2.shape, const), pl.BlockSpec(wkv.shape, const),
        ],
        out_specs=[hspec(MLA_QK_PAD), hspec(MLA_QK_PAD), hspec(MLA_V)],
        out_shape=[jax.ShapeDtypeStruct((bsz, MLA_HEADS, n_pos, MLA_QK_PAD), BF16),
                   jax.ShapeDtypeStruct((bsz, MLA_HEADS, n_pos, MLA_QK_PAD), BF16),
                   jax.ShapeDtypeStruct((bsz, MLA_HEADS, n_pos, MLA_V), BF16)],
        compiler_params=_cparams(("parallel", "parallel")),
        name="mla_prep",
    )(z, z, z, z, cos, sin, qg, kvg, wq1, wq2, wkv)


def _online_softmax_step(s, v, m_ref, l_ref, acc_ref, idx):
    m_old = m_ref[idx]
    m_new = jnp.maximum(m_old, jnp.max(s, axis=-1, keepdims=True))
    p = jnp.exp(s - m_new)
    corr = jnp.exp(m_old - m_new)
    l_ref[idx] = corr * l_ref[idx] + jnp.sum(p, axis=-1, keepdims=True)
    acc_ref[idx] = corr * acc_ref[idx] + jnp.dot(p.astype(BF16), v, preferred_element_type=F32)
    m_ref[idx] = m_new


def _causal_mask(t):
    r = lax.broadcasted_iota(jnp.int32, (t, t), 0)
    c = lax.broadcasted_iota(jnp.int32, (t, t), 1)
    return r >= c


def _mla_flash_kernel(q_ref, k_ref, v_ref, o_ref, m_ref, l_ref, acc_ref, *, t):
    qi = pl.program_id(2)
    ki = pl.program_id(3)

    @pl.when(ki == 0)
    def _():
        m_ref[...] = jnp.full_like(m_ref, NEG_INF)
        l_ref[...] = jnp.zeros_like(l_ref)
        acc_ref[...] = jnp.zeros_like(acc_ref)

    def scores():
        return lax.dot_general(q_ref[0, 0], k_ref[0, 0], (((1,), (1,)), ((), ())), preferred_element_type=F32)

    @pl.when(ki < qi)
    def _():
        _online_softmax_step(scores(), v_ref[0, 0], m_ref, l_ref, acc_ref, 0)

    @pl.when(ki == qi)
    def _():
        s = jnp.where(_causal_mask(t), scores(), NEG_INF)
        _online_softmax_step(s, v_ref[0, 0], m_ref, l_ref, acc_ref, 0)
        o_ref[0] = (acc_ref[0] / l_ref[0]).astype(BF16)


def _mla_flash(q, k, v, *, t=512):
    bsz, nh, n_pos, _ = q.shape
    t = min(t, n_pos)
    nb = n_pos // t
    kv_map = lambda b, h, qi, ki: (b, h, jnp.minimum(ki, qi), 0)
    return pl.pallas_call(
        functools.partial(_mla_flash_kernel, t=t),
        grid=(bsz, nh, nb, nb),
        in_specs=[
            pl.BlockSpec((1, 1, t, MLA_QK_PAD), lambda b, h, qi, ki: (b, h, qi, 0)),
            pl.BlockSpec((1, 1, t, MLA_QK_PAD), kv_map),
            pl.BlockSpec((1, 1, t, MLA_V), kv_map),
        ],
        out_specs=pl.BlockSpec((1, t, MLA_V), lambda b, h, qi, ki: (b, qi, h)),
        out_shape=jax.ShapeDtypeStruct((bsz, n_pos, nh * MLA_V), BF16),
        scratch_shapes=[pltpu.VMEM((1, t, 1), F32), pltpu.VMEM((1, t, 1), F32), pltpu.VMEM((1, t, MLA_V), F32)],
        compiler_params=_cparams(("parallel", "parallel", "parallel", "arbitrary")),
        name="mla_flash",
    )(q, k, v)


def _diff_kernel(q_ref, k_ref, v_ref, pq_ref, pk_ref, tab_ref, lq1_ref, lk1_ref, lq2_ref, lk2_ref, sg_ref,
                 o_ref, qm_ref, m_ref, l_ref, acc_ref, *, t, lambda_init):
    qi = pl.program_id(1)
    ki = pl.program_id(2)
    nh = DIFF_HEADS
    hw = 2 * DIFF_QK

    @pl.when(ki == 0)
    def _():
        m_ref[...] = jnp.full_like(m_ref, NEG_INF)
        l_ref[...] = jnp.zeros_like(l_ref)
        acc_ref[...] = jnp.zeros_like(acc_ref)
        lane = lax.broadcasted_iota(jnp.int32, (t, hw), 1)
        scale = DIFF_QK ** -0.5
        for h in range(nh):
            qh = q_ref[:, h * hw:(h + 1) * hw] * scale
            qm_ref[2 * h] = jnp.where(lane < DIFF_QK, qh, 0.0).astype(BF16)
            qm_ref[2 * h + 1] = jnp.where(lane >= DIFF_QK, qh, 0.0).astype(BF16)

    def step(masked):
        dist = jnp.clip(pq_ref[0] - pk_ref[0], 0, T5_TABLE - 1)
        mask = _causal_mask(t) if masked else None
        for h in range(nh):
            tab = jnp.broadcast_to(tab_ref[h:h + 1, :], (t, T5_TABLE))
            bias = jnp.concatenate(
                [jnp.take_along_axis(tab, dist[:, c * LANES:(c + 1) * LANES], axis=1) for c in range(t // LANES)],
                axis=1)
            kh = k_ref[:, h * hw:(h + 1) * hw].astype(BF16)
            vh = v_ref[:, h * DIFF_V:(h + 1) * DIFF_V].astype(BF16)
            for mp in range(2):
                s = lax.dot_general(qm_ref[2 * h + mp], kh, (((1,), (1,)), ((), ())),
                                    preferred_element_type=F32) + bias
                if masked:
                    s = jnp.where(mask, s, NEG_INF)
                _online_softmax_step(s, vh, m_ref, l_ref, acc_ref, 2 * h + mp)

    @pl.when(ki < qi)
    def _():
        step(False)

    @pl.when(ki == qi)
    def _():
        step(True)
        lam = (jnp.exp(jnp.sum(lq1_ref[...] * lk1_ref[...], axis=-1, keepdims=True))
               - jnp.exp(jnp.sum(lq2_ref[...] * lk2_ref[...], axis=-1, keepdims=True)) + lambda_init)
        for h in range(nh):
            o1 = acc_ref[2 * h] / l_ref[2 * h]
            o2 = acc_ref[2 * h + 1] / l_ref[2 * h + 1]
            o = _rms(o1 - lam * o2, sg_ref[...]) * (1.0 - lambda_init)
            o_ref[:, h * DIFF_V:(h + 1) * DIFF_V] = o.astype(BF16)


def _diff_attn(z, pos_q, pos_k, table, lq1, lk1, lq2, lk2, sg, *, bsz, n_pos, lambda_init, t=512):
    t = min(t, n_pos)
    nb = n_pos // t
    w = DIFF_HEADS * DIFF_V
    const = lambda b, qi, ki: (0, 0)
    kmin = lambda qi, ki: jnp.minimum(ki, qi)
    return pl.pallas_call(
        functools.partial(_diff_kernel, t=t, lambda_init=lambda_init),
        grid=(bsz, nb, nb),
        in_specs=[
            pl.BlockSpec((t, w), lambda b, qi, ki: (b * nb + qi, OFF_DQ // w)),
            pl.BlockSpec((t, w), lambda b, qi, ki: (b * nb + kmin(qi, ki), OFF_DK // w)),
            pl.BlockSpec((t, w), lambda b, qi, ki: (b * nb + kmin(qi, ki), OFF_DV // w)),
            pl.BlockSpec((1, t, 1), lambda b, qi, ki: (b, qi, 0)),
            pl.BlockSpec((1, 1, t), lambda b, qi, ki: (b * nb + kmin(qi, ki), 0, 0)),
            pl.BlockSpec((DIFF_HEADS, T5_TABLE), const),
            pl.BlockSpec((1, DIFF_QK), const), pl.BlockSpec((1, DIFF_QK), const),
            pl.BlockSpec((1, DIFF_QK), const), pl.BlockSpec((1, DIFF_QK), const),
            pl.BlockSpec((1, DIFF_V), const),
        ],
        out_specs=pl.BlockSpec((t, w), lambda b, qi, ki: (b * nb + qi, 0)),
        out_shape=jax.ShapeDtypeStruct((bsz * n_pos, w), BF16),
        scratch_shapes=[pltpu.VMEM((2 * DIFF_HEADS, t, 2 * DIFF_QK), BF16),
                        pltpu.VMEM((2 * DIFF_HEADS, t, 1), F32),
                        pltpu.VMEM((2 * DIFF_HEADS, t, 1), F32),
                        pltpu.VMEM((2 * DIFF_HEADS, t, DIFF_V), F32)],
        compiler_params=_cparams(("parallel", "parallel", "arbitrary")),
        name="diff_attn",
    )(z, z, z, pos_q, pos_k, table, lq1, lk1, lq2, lk2, sg)


def _ret_kernel(q_ref, qr_ref, k_ref, kr_ref, v_ref, g_ref, cos_ref, sin_ref, o_ref, st_ref, *, tm):
    t = pl.program_id(1)
    c = RET_CHUNK
    nh = RET_HEADS
    w = nh * RET_QK

    @pl.when(t == 0)
    def _():
        st_ref[...] = jnp.zeros_like(st_ref)

    log_gamma = [math.log(1.0 - 2.0 ** (-5.0 - h)) for h in range(nh)]
    lane = lax.broadcasted_iota(jnp.int32, (1, w), 1)
    lg_lane = jnp.zeros((1, w), F32)
    for h in range(nh):
        lg_lane = jnp.where(lane // RET_QK == h, log_gamma[h], lg_lane)
    tok = lax.broadcasted_iota(jnp.int32, (c, 1), 0).astype(F32)
    q_decay = jnp.exp(lg_lane * (tok + 1.0))
    k_decay = jnp.exp(lg_lane * (c - 1.0 - tok))
    ri = lax.broadcasted_iota(jnp.int32, (c, c), 0)
    ci = lax.broadcasted_iota(jnp.int32, (c, c), 1)
    rel = (ri - ci).astype(F32)
    intra = [jnp.where(rel >= 0, jnp.exp(log_gamma[h] * jnp.maximum(rel, 0.0)), 0.0) for h in range(nh)]
    head_lanes = [(lane // RET_QK == h) for h in range(nh)]

    for j in range(tm // c):
        rows = slice(j * c, (j + 1) * c)
        cos = jnp.concatenate([cos_ref[rows, :]] * (w // LANES), axis=1)
        sin = jnp.concatenate([sin_ref[rows, :]] * (w // LANES), axis=1)
        q = q_ref[rows, :] * cos + qr_ref[rows, :] * sin
        k = (k_ref[rows, :] * cos + kr_ref[rows, :] * sin) * (RET_QK ** -0.5)
        kb = k.astype(BF16)
        qd = q * q_decay
        kdt = jnp.transpose(k * k_decay).astype(BF16)
        for h in range(nh):
            vh = v_ref[rows, h * RET_V:(h + 1) * RET_V].astype(BF16)
            qh = jnp.where(head_lanes[h], q, 0.0).astype(BF16)
            scores = lax.dot_general(qh, kb, (((1,), (1,)), ((), ())), preferred_element_type=F32) * intra[h]
            inner = jnp.dot(scores.astype(BF16), vh, preferred_element_type=F32)
            qdh = jnp.where(head_lanes[h], qd, 0.0).astype(BF16)
            state = st_ref[...]
            cross = jnp.dot(qdh, state.astype(BF16), preferred_element_type=F32)
            o = inner + cross
            mu = jnp.mean(o, axis=-1, keepdims=True)
            oc = o - mu
            var = jnp.mean(oc * oc, axis=-1, keepdims=True)
            o = oc * lax.rsqrt(var + LN_EPS)
            gh = g_ref[rows, h * RET_V:(h + 1) * RET_V]
            o_ref[rows, h * RET_V:(h + 1) * RET_V] = (gh * jax.nn.sigmoid(gh) * o).astype(BF16)
            hs = slice(h * RET_QK, (h + 1) * RET_QK)
            kv = jnp.dot(kdt[hs, :], vh, preferred_element_type=F32)
            st_ref[hs, :] = state[hs, :] * math.exp(log_gamma[h] * c) + kv


def _retention(z, cos, sin, *, bsz, n_pos, tm=512):
    tm = min(tm, n_pos)
    nt = n_pos // tm
    w = RET_HEADS * RET_QK
    wv = RET_HEADS * RET_V
    zspec = lambda width, off: pl.BlockSpec((tm, width), lambda b, t: (b * nt + t, off // width))
    tspec = pl.BlockSpec((tm, LANES), lambda b, t: (b * nt + t, 0))
    return pl.pallas_call(
        functools.partial(_ret_kernel, tm=tm),
        grid=(bsz, nt),
        in_specs=[zspec(w, OFF_RQ), zspec(w, OFF_RQR), zspec(w, OFF_RK), zspec(w, OFF_RKR),
                  zspec(wv, OFF_RV), zspec(wv, OFF_RG), tspec, tspec],
        out_specs=pl.BlockSpec((tm, wv), lambda b, t: (b * nt + t, 0)),
        out_shape=jax.ShapeDtypeStruct((bsz * n_pos, wv), BF16),
        scratch_shapes=[pltpu.VMEM((w, RET_V), F32)],
        compiler_params=_cparams(("parallel", "arbitrary")),
        name="retention",
    )(z, z, z, z, z, z, cos, sin)


def _outproj_ln_kernel(y0_ref, y1_ref, y2_ref, y3_ref, w_ref, x_ref, g_ref, b_ref, o_ref):
    acc = None
    for j, y_ref in enumerate((y0_ref, y1_ref, y2_ref, y3_ref)):
        part = jnp.dot(y_ref[...], w_ref[j * 512:(j + 1) * 512, :], preferred_element_type=F32)
        acc = part if acc is None else acc + part
    o_ref[...] = _layer_norm(ALPHA * x_ref[...] + acc, g_ref[...], b_ref[...])


def _outproj_ln(ys, w, x, g, b, *, tm=512):
    m, d = x.shape
    tm = min(tm, m)
    yspec = pl.BlockSpec((tm, 512), lambda i: (i, 0))
    return pl.pallas_call(
        _outproj_ln_kernel,
        grid=(m // tm,),
        in_specs=[yspec, yspec, yspec, yspec,
                  pl.BlockSpec(w.shape, lambda i: (0, 0)),
                  pl.BlockSpec((tm, d), lambda i: (i, 0)),
                  pl.BlockSpec((1, d), lambda i: (0, 0)),
                  pl.BlockSpec((1, d), lambda i: (0, 0))],
        out_specs=pl.BlockSpec((tm, d), lambda i: (i, 0)),
        out_shape=jax.ShapeDtypeStruct((m, d), F32),
        compiler_params=_cparams(("parallel",)),
        name="outproj_ln",
    )(*ys, w, x, g, b)


def _rot_cols(w, heads, dim):
    k = w.shape[0]
    w = w.reshape(k, heads, 2, dim // 2)
    return jnp.concatenate([-w[:, :, 1], w[:, :, 0]], axis=-1).reshape(k, heads * dim)


def _pad_cols(w, width):
    return jnp.pad(w, ((0, 0), (0, width - w.shape[1])))


def _wide_w_in(w_in):
    sizes = (S5_WIDTH, MLA_Q_RANK, MLA_KV_RANK, MLA_ROPE,
             RET_HEADS * RET_QK, RET_HEADS * RET_QK, RET_HEADS * RET_V, RET_HEADS * RET_V,
             DIFF_HEADS * 2 * DIFF_QK, DIFF_HEADS * 2 * DIFF_QK, DIFF_HEADS * DIFF_V)
    offs = np.concatenate([[0], np.cumsum(sizes)])
    (s5_u, cq, ckv, kr, rq, rk, rv, rg, dq, dk, dv) = [w_in[:, offs[i]:offs[i + 1]] for i in range(len(sizes))]
    cols = [s5_u, cq, rv, rg, dq, dk, dv,
            rq, _rot_cols(rq, RET_HEADS, RET_QK), rk, _rot_cols(rk, RET_HEADS, RET_QK),
            ckv, _pad_cols(kr, LANES), _pad_cols(_rot_cols(kr, 1, MLA_ROPE), LANES)]
    return jnp.concatenate(cols, axis=1).astype(BF16)


def _mla_q_weights(w_uq):
    k = w_uq.shape[0]
    w = w_uq.reshape(k, MLA_HEADS, MLA_NOPE + MLA_ROPE)
    rope = w[:, :, MLA_NOPE:]
    pad = MLA_QK_PAD - MLA_NOPE - MLA_ROPE
    w1 = jnp.pad(w, ((0, 0), (0, 0), (0, pad))).reshape(k, MLA_HEADS * MLA_QK_PAD)
    rot = _rot_cols(rope.reshape(k, MLA_HEADS * MLA_ROPE), MLA_HEADS, MLA_ROPE).reshape(k, MLA_HEADS, MLA_ROPE)
    w2 = jnp.pad(rot, ((0, 0), (0, 0), (0, LANES - MLA_ROPE))).reshape(k, MLA_HEADS * LANES)
    return w1.astype(BF16), w2.astype(BF16)


def kernel(x, p, positions, rel_bias, ffn1_w_gate, ffn1_w_up, ffn1_w_down, ln1_g, ln1_b, w_in, w_out, ln2_g, ln2_b, s5_lambda_re, s5_lambda_im, s5_log_dt, s5_b_re, s5_b_im, s5_c_re, s5_c_im, s5_d, s5_w_glu, s5_b_glu, mla_q_norm_g, mla_w_uq, mla_kv_norm_g, mla_w_ukv, diff_lambda_q1, diff_lambda_k1, diff_lambda_q2, diff_lambda_k2, diff_subln_g, ffn2_w_gate, ffn2_w_up, ffn2_w_down, ple_w_gate, ple_b_gate, ple_w_proj, ln3_g, ln3_b):
    bsz, n_pos, d = x.shape
    m = bsz * n_pos
    depth = ffn1_w_gate.shape[0]
    xf = x.reshape(m, d)
    row = lambda v: v.reshape(1, -1)

    cos, sin = _rope_tables(positions.reshape(m, 1))
    table = _bias_table(rel_bias)
    pos_q = positions.reshape(bsz, n_pos, 1)
    t_diff = min(512, n_pos)
    pos_k = positions.reshape(bsz * (n_pos // t_diff), 1, t_diff)

    for i in range(depth):
        xf = _ffn_ln(xf, ffn1_w_gate[i].astype(BF16), ffn1_w_up[i].astype(BF16), ffn1_w_down[i].astype(BF16),
                     row(ln1_g[i]), row(ln1_b[i]))
        z = _inproj(xf, _wide_w_in(w_in[i]))

        apr, api, bbr, bbi = _s5_discretise(s5_lambda_re[i], s5_lambda_im[i], s5_log_dt[i], s5_b_re[i], s5_b_im[i])
        bd = jnp.concatenate([_block_diag_in(bbr), _block_diag_in(bbi)], axis=1).astype(BF16)
        cd = jnp.concatenate([_block_diag_out(s5_c_re[i]), -_block_diag_out(s5_c_im[i])], axis=0).astype(BF16)
        y_s5 = _s5_mixer(z, bd, cd, apr, api, row(s5_d[i]), s5_w_glu[i].astype(BF16), row(s5_b_glu[i]),
                         bsz=bsz, n_pos=n_pos)

        wq1, wq2 = _mla_q_weights(mla_w_uq[i])
        q, k, v = _mla_prep(z, cos, sin, row(mla_q_norm_g[i]), row(mla_kv_norm_g[i]), wq1, wq2,
                            mla_w_ukv[i].astype(BF16), bsz=bsz, n_pos=n_pos)
        y_mla = _mla_flash(q, k, v).reshape(m, MLA_HEADS * MLA_V)

        y_ret = _retention(z, cos, sin, bsz=bsz, n_pos=n_pos)

        lambda_init = 0.8 - 0.6 * math.exp(-0.3 * i)
        y_diff = _diff_attn(z, pos_q, pos_k, table, row(diff_lambda_q1[i]), row(diff_lambda_k1[i]),
                            row(diff_lambda_q2[i]), row(diff_lambda_k2[i]), row(diff_subln_g[i]),
                            bsz=bsz, n_pos=n_pos, lambda_init=lambda_init, t=t_diff)

        xf = _outproj_ln((y_s5, y_mla, y_ret, y_diff), w_out[i].astype(BF16), xf, row(ln2_g[i]), row(ln2_b[i]))

        extra = _ple(xf, p[i].reshape(m, PLE_DIM), ple_w_gate[i].astype(BF16), row(ple_b_gate[i]),
                     ple_w_proj[i].astype(BF16))
        xf = _ffn_ln(xf, ffn2_w_gate[i].astype(BF16), ffn2_w_up[i].astype(BF16), ffn2_w_down[i].astype(BF16),
                     row(ln3_g[i]), row(ln3_b[i]), extra)
    return xf.reshape(bsz, n_pos, d)
```

```python
import functools
import math

import numpy as np
import jax
import jax.numpy as jnp
from jax import lax
from jax.experimental import pallas as pl
from jax.experimental.pallas import tpu as pltpu

F32 = jnp.float32
BF16 = jnp.bfloat16

D_MODEL = 2048
DEPTH = 2
PLE_DIM = 256
D_FF = 5632
ALPHA = (2 * DEPTH) ** 0.25
ROPE_THETA = 10000.0
NEG_INF = -1e30
LN_EPS = 1e-5
RMS_EPS = 1e-6

S5_WIDTH = 512
S5_GROUP = 16
S5_GROUPS = 32
S5_STATE = 64
S5_NSTATE = S5_GROUPS * S5_STATE

MLA_HEADS = 4
MLA_Q_RANK = 512
MLA_KV_RANK = 128
MLA_NOPE = 128
MLA_ROPE = 64
MLA_V = 128
MLA_QK_PAD = 256

RET_HEADS = 4
RET_QK = 64
RET_V = 128
RET_CHUNK = 128

DIFF_HEADS = 4
DIFF_QK = 64
DIFF_V = 128

T5_BUCKETS = 32
T5_MAX_DIST = 128
T5_TABLE = 128

LOG2E = math.log2(math.e)
VT_ROWS = 144

LANES = 128
SUBLANES = 8
VMEM_LIMIT_BYTES = 56 * 1024 * 1024

OFF_S5 = 0
OFF_CQ = 512
OFF_RV = 1024
OFF_RG = 1536
OFF_DQ = 2048
OFF_DK = 2560
OFF_DV = 3072
OFF_RQ = 3584
OFF_RQR = 3840
OFF_RK = 4096
OFF_RKR = 4352
OFF_CKV = 4608
OFF_KR = 4736
OFF_KRR = 4864
IN_WIDE = 4992
IN_TN = 1664


def _cparams(sem):
    return pltpu.CompilerParams(dimension_semantics=sem, vmem_limit_bytes=VMEM_LIMIT_BYTES)


def _layer_norm(y, g, b):
    mu = jnp.mean(y, axis=-1, keepdims=True)
    yc = y - mu
    var = jnp.mean(yc * yc, axis=-1, keepdims=True)
    return yc * lax.rsqrt(var + LN_EPS) * g + b


def _ffn_ln_kernel(*refs, nf, has_extra):
    if has_extra:
        x_ref, wg_ref, wu_ref, wd_ref, g_ref, b_ref, e_ref, o_ref, xb_ref, acc_ref = refs
    else:
        x_ref, wg_ref, wu_ref, wd_ref, g_ref, b_ref, o_ref, xb_ref, acc_ref = refs
        e_ref = None
    f = pl.program_id(1)

    @pl.when(f == 0)
    def _():
        xb_ref[...] = x_ref[...].astype(BF16)
        acc_ref[...] = jnp.zeros_like(acc_ref)

    xb = xb_ref[...]
    gate = jnp.dot(xb, wg_ref[...], preferred_element_type=F32)
    up = jnp.dot(xb, wu_ref[...], preferred_element_type=F32)
    h = gate * jax.nn.sigmoid(gate) * up
    acc_ref[...] += jnp.dot(h.astype(BF16), wd_ref[...], preferred_element_type=F32)

    @pl.when(f == nf - 1)
    def _():
        y = ALPHA * x_ref[...] + 0.5 * acc_ref[...]
        if e_ref is not None:
            y = y + e_ref[...]
        o_ref[...] = _layer_norm(y, g_ref[...], b_ref[...])


def _ffn_ln(x, wg, wu, wd, g, b, extra=None, *, tm=512, tf=512):
    m, d = x.shape
    f_dim = wg.shape[1]
    tm = min(tm, m)
    nf = f_dim // tf
    in_specs = [
        pl.BlockSpec((tm, d), lambda i, f: (i, 0)),
        pl.BlockSpec((d, tf), lambda i, f: (0, f)),
        pl.BlockSpec((d, tf), lambda i, f: (0, f)),
        pl.BlockSpec((tf, d), lambda i, f: (f, 0)),
        pl.BlockSpec((1, d), lambda i, f: (0, 0)),
        pl.BlockSpec((1, d), lambda i, f: (0, 0)),
    ]
    args = [x, wg, wu, wd, g, b]
    if extra is not None:
        in_specs.append(pl.BlockSpec((tm, d), lambda i, f: (i, 0)))
        args.append(extra)
    return pl.pallas_call(
        functools.partial(_ffn_ln_kernel, nf=nf, has_extra=extra is not None),
        grid=(m // tm, nf),
        in_specs=in_specs,
        out_specs=pl.BlockSpec((tm, d), lambda i, f: (i, 0)),
        out_shape=jax.ShapeDtypeStruct((m, d), F32),
        scratch_shapes=[pltpu.VMEM((tm, d), BF16), pltpu.VMEM((tm, d), F32)],
        compiler_params=_cparams(("parallel", "arbitrary")),
        name="ffn_ln",
    )(*args)


def _ple_kernel(x_ref, p_ref, wg_ref, bg_ref, wp_ref, o_ref):
    xb = x_ref[...].astype(BF16)
    pb = p_ref[...].astype(BF16)
    gate = jax.nn.sigmoid(jnp.dot(xb, wg_ref[...], preferred_element_type=F32) + bg_ref[...])
    o_ref[...] = gate * jnp.dot(pb, wp_ref[...], preferred_element_type=F32)


def _ple(x, p, wg, bg, wp, *, tm=512):
    m, d = x.shape
    tm = min(tm, m)
    return pl.pallas_call(
        _ple_kernel,
        grid=(m // tm,),
        in_specs=[
            pl.BlockSpec((tm, d), lambda i: (i, 0)),
            pl.BlockSpec((tm, PLE_DIM), lambda i: (i, 0)),
            pl.BlockSpec((d, d), lambda i: (0, 0)),
            pl.BlockSpec((1, d), lambda i: (0, 0)),
            pl.BlockSpec((PLE_DIM, d), lambda i: (0, 0)),
        ],
        out_specs=pl.BlockSpec((tm, d), lambda i: (i, 0)),
        out_shape=jax.ShapeDtypeStruct((m, d), F32),
        compiler_params=_cparams(("parallel",)),
        name="ple",
    )(x, p, wg, bg, wp)


def _inproj_kernel(x_ref, w_ref, o_ref):
    o_ref[...] = jnp.dot(x_ref[...].astype(BF16), w_ref[...], preferred_element_type=F32)


def _inproj(x, w, *, tm=512):
    m, d = x.shape
    n = w.shape[1]
    tm = min(tm, m)
    return pl.pallas_call(
        _inproj_kernel,
        grid=(n // IN_TN, m // tm),
        in_specs=[
            pl.BlockSpec((tm, d), lambda j, i: (i, 0)),
            pl.BlockSpec((d, IN_TN), lambda j, i: (0, j)),
        ],
        out_specs=pl.BlockSpec((tm, IN_TN), lambda j, i: (i, j)),
        out_shape=jax.ShapeDtypeStruct((m, n), F32),
        compiler_params=_cparams(("parallel", "parallel")),
        name="inproj",
    )(x, w)


def _rope_kernel(pos_ref, freq_ref, cos_ref, sin_ref):
    ang = pos_ref[...].astype(F32) * freq_ref[...]
    cos_ref[...] = jnp.cos(ang)
    sin_ref[...] = jnp.sin(ang)


def _rope_tables(pos_col, *, tm=1024):
    m = pos_col.shape[0]
    tm = min(tm, m)
    half = MLA_ROPE // 2
    inv = 1.0 / (ROPE_THETA ** (np.arange(0, MLA_ROPE, 2, dtype=np.float64) / MLA_ROPE))
    freq = jnp.asarray(np.tile(inv, LANES // half)[None, :], F32)
    return pl.pallas_call(
        _rope_kernel,
        grid=(m // tm,),
        in_specs=[pl.BlockSpec((tm, 1), lambda i: (i, 0)), pl.BlockSpec((1, LANES), lambda i: (0, 0))],
        out_specs=[pl.BlockSpec((tm, LANES), lambda i: (i, 0))] * 2,
        out_shape=[jax.ShapeDtypeStruct((m, LANES), F32)] * 2,
        compiler_params=_cparams(("parallel",)),
        name="rope_tables",
    )(pos_col, freq)


def _t5_bucket_static():
    n = np.arange(T5_TABLE)
    max_exact = T5_BUCKETS // 2
    nf = np.maximum(n, 1).astype(np.float64)
    large = max_exact + (np.log(nf / max_exact) / math.log(T5_MAX_DIST / max_exact)
                         * (T5_BUCKETS - max_exact)).astype(np.int64)
    large = np.minimum(large, T5_BUCKETS - 1)
    return np.where(n < max_exact, n, large)


def _bias_table_kernel(rb_ref, onehot_ref, o_ref):
    rb = rb_ref[...]
    oh = onehot_ref[...]
    rows = [jnp.sum(oh * rb[:, h:h + 1], axis=0, keepdims=True) for h in range(DIFF_HEADS)]
    o_ref[...] = jnp.concatenate(rows, axis=0) * LOG2E


def _bias_table(rel_bias):
    bucket = _t5_bucket_static()
    onehot = jnp.asarray((np.arange(T5_BUCKETS)[:, None] == bucket[None, :]).astype(np.float32))
    return pl.pallas_call(
        _bias_table_kernel,
        out_shape=jax.ShapeDtypeStruct((DIFF_HEADS, T5_TABLE), F32),
        name="t5_bias_table",
    )(rel_bias, onehot)


def _s5_disc_kernel(lr_ref, li_ref, ldt_ref, br_ref, bi_ref, apr_ref, api_ref, bbr_ref, bbi_ref):
    lr = lr_ref[...]
    li = li_ref[...]
    dt = jnp.exp(ldt_ref[...])
    k = (lax.broadcasted_iota(jnp.int32, (SUBLANES, 1), 0) + 1).astype(F32)
    mag = jnp.exp(lr * dt * k)
    apr_ref[...] = mag * jnp.cos(li * dt * k)
    api_ref[...] = mag * jnp.sin(li * dt * k)
    mag1 = jnp.exp(lr * dt)
    ar = mag1 * jnp.cos(li * dt)
    ai = mag1 * jnp.sin(li * dt)
    den = lr * lr + li * li
    fr = ((ar - 1.0) * lr + ai * li) / den
    fi = (ai * lr - (ar - 1.0) * li) / den
    br = br_ref[...]
    bi = bi_ref[...]
    bbr_ref[...] = fr * br - fi * bi
    bbi_ref[...] = fr * bi + fi * br


def _s5_discretise(lam_re, lam_im, log_dt, b_re, b_im):
    n = S5_NSTATE
    lr = lam_re.reshape(1, n)
    li = lam_im.reshape(1, n)
    ldt = jnp.repeat(log_dt, S5_STATE).reshape(1, n)
    br = jnp.transpose(b_re, (2, 0, 1)).reshape(S5_GROUP, n)
    bi = jnp.transpose(b_im, (2, 0, 1)).reshape(S5_GROUP, n)
    return pl.pallas_call(
        _s5_disc_kernel,
        out_shape=[jax.ShapeDtypeStruct((SUBLANES, n), F32)] * 2 + [jax.ShapeDtypeStruct((S5_GROUP, n), F32)] * 2,
        name="s5_discretise",
    )(lr, li, ldt, br, bi)


def _block_diag_in(bb):
    eye = jnp.eye(S5_GROUPS, dtype=bb.dtype)
    v = bb.reshape(S5_GROUP, S5_GROUPS, S5_STATE)
    out = eye[:, None, :, None] * v[None, :, :, :]
    return out.reshape(S5_WIDTH, S5_NSTATE)


def _block_diag_out(c):
    eye = jnp.eye(S5_GROUPS, dtype=c.dtype)
    v = jnp.transpose(c, (0, 2, 1))
    out = v[:, :, None, :] * eye[:, None, :, None]
    return out.reshape(S5_NSTATE, S5_WIDTH)


S5_LANE_CHUNK = 512


def _s5_kernel(u_ref, bd_ref, cd_ref, apr_ref, api_ref, d_ref, wglu_ref, bglu_ref, o_ref,
               hr_ref, hi_ref, cr_ref, ci_ref, *, tm):
    t = pl.program_id(1)

    @pl.when(t == 0)
    def _():
        cr_ref[...] = jnp.zeros_like(cr_ref)
        ci_ref[...] = jnp.zeros_like(ci_ref)

    u = u_ref[...]
    ub = u.astype(BF16)
    n = S5_NSTATE
    hr_ref[...] = jnp.dot(ub, bd_ref[:, :n], preferred_element_type=F32)
    hi_ref[...] = jnp.dot(ub, bd_ref[:, n:], preferred_element_type=F32)

    row = lax.broadcasted_iota(jnp.int32, (SUBLANES, S5_LANE_CHUNK), 0)
    for c in range(n // S5_LANE_CHUNK):
        lanes = pl.ds(c * S5_LANE_CHUNK, S5_LANE_CHUNK)
        apr = apr_ref[:, lanes]
        api = api_ref[:, lanes]
        steps = [(1, apr[0:1], api[0:1]), (2, apr[1:2], api[1:2]), (4, apr[3:4], api[3:4])]

        def body(rg, carry, lanes=lanes, apr=apr, api=api, steps=steps):
            cr, ci = carry
            r0 = pl.multiple_of(rg * SUBLANES, SUBLANES)
            xr = hr_ref[pl.ds(r0, SUBLANES), lanes]
            xi = hi_ref[pl.ds(r0, SUBLANES), lanes]
            for sh, pr, pi in steps:
                keep = row >= sh
                sr = jnp.where(keep, pltpu.roll(xr, sh, 0), 0.0)
                si = jnp.where(keep, pltpu.roll(xi, sh, 0), 0.0)
                xr, xi = xr + pr * sr - pi * si, xi + pr * si + pi * sr
            hr = xr + apr * cr - api * ci
            hi = xi + apr * ci + api * cr
            hr_ref[pl.ds(r0, SUBLANES), lanes] = hr
            hi_ref[pl.ds(r0, SUBLANES), lanes] = hi
            return hr[SUBLANES - 1:SUBLANES], hi[SUBLANES - 1:SUBLANES]

        cr, ci = lax.fori_loop(0, tm // SUBLANES, body, (cr_ref[:, lanes], ci_ref[:, lanes]))
        cr_ref[:, lanes] = cr
        ci_ref[:, lanes] = ci

    y = (jnp.dot(hr_ref[...].astype(BF16), cd_ref[:n, :], preferred_element_type=F32)
         + jnp.dot(hi_ref[...].astype(BF16), cd_ref[n:, :], preferred_element_type=F32)
         + d_ref[...] * u)
    c0 = math.sqrt(2.0 / math.pi)
    y = 0.5 * y * (1.0 + jnp.tanh(c0 * (y + 0.044715 * (y * y * y))))
    gate = jax.nn.sigmoid(jnp.dot(y.astype(BF16), wglu_ref[...], preferred_element_type=F32) + bglu_ref[...])
    o_ref[...] = (y * gate).astype(BF16)


def _s5_mixer(z, bd, cd, apr, api, d, wglu, bglu, *, bsz, n_pos, tm=256):
    tm = min(tm, n_pos)
    nt = n_pos // tm
    m = bsz * n_pos
    n2 = 2 * S5_NSTATE
    const = lambda b, t: (0, 0)
    return pl.pallas_call(
        functools.partial(_s5_kernel, tm=tm),
        grid=(bsz, nt),
        in_specs=[
            pl.BlockSpec((tm, S5_WIDTH), lambda b, t: (b * nt + t, OFF_S5 // S5_WIDTH)),
            pl.BlockSpec((S5_WIDTH, n2), const),
            pl.BlockSpec((n2, S5_WIDTH), const),
            pl.BlockSpec((SUBLANES, S5_NSTATE), const),
            pl.BlockSpec((SUBLANES, S5_NSTATE), const),
            pl.BlockSpec((1, S5_WIDTH), const),
            pl.BlockSpec((S5_WIDTH, S5_WIDTH), const),
            pl.BlockSpec((1, S5_WIDTH), const),
        ],
        out_specs=pl.BlockSpec((tm, S5_WIDTH), lambda b, t: (b * nt + t, 0)),
        out_shape=jax.ShapeDtypeStruct((m, S5_WIDTH), BF16),
        scratch_shapes=[pltpu.VMEM((tm, S5_NSTATE), F32), pltpu.VMEM((tm, S5_NSTATE), F32),
                        pltpu.VMEM((1, S5_NSTATE), F32), pltpu.VMEM((1, S5_NSTATE), F32)],
        compiler_params=_cparams(("parallel", "arbitrary")),
        name="s5_mixer",
    )(z, bd, cd, apr, api, d, wglu, bglu)


def _rms(x, g):
    return x * lax.rsqrt(jnp.mean(x * x, axis=-1, keepdims=True) + RMS_EPS) * g


def _mla_prep_kernel(cq_ref, ckv_ref, kr_ref, krr_ref, cos_ref, sin_ref, qg_ref, kvg_ref,
                     wq1_ref, wq2_ref, wkv_ref, q_ref, k_ref, vt_ref):
    cos = cos_ref[...]
    sin = sin_ref[...]
    scale = (MLA_NOPE + MLA_ROPE) ** -0.5 * LOG2E
    ones = jnp.ones((VT_ROWS - MLA_V, cos.shape[0]), BF16)
    cqn = _rms(cq_ref[...], qg_ref[...]).astype(BF16)
    z1 = jnp.dot(cqn, wq1_ref[...], preferred_element_type=F32)
    z2 = jnp.dot(cqn, wq2_ref[...], preferred_element_type=F32)
    ckvn = _rms(ckv_ref[...], kvg_ref[...]).astype(BF16)
    zkv = jnp.dot(ckvn, wkv_ref[...], preferred_element_type=F32)
    k_rope = (kr_ref[...] * cos + krr_ref[...] * sin).astype(BF16)
    for h in range(MLA_HEADS):
        a = h * MLA_QK_PAD
        q_nope = z1[:, a:a + MLA_NOPE]
        q_rope = z1[:, a + MLA_NOPE:a + MLA_QK_PAD] * cos + z2[:, h * LANES:(h + 1) * LANES] * sin
        q_ref[0, h, :, 0:MLA_NOPE] = (q_nope * scale).astype(BF16)
        q_ref[0, h, :, MLA_NOPE:MLA_QK_PAD] = (q_rope * scale).astype(BF16)
        k_ref[0, h, :, 0:MLA_NOPE] = zkv[:, a:a + MLA_NOPE].astype(BF16)
        k_ref[0, h, :, MLA_NOPE:MLA_QK_PAD] = k_rope
        vt_ref[0, h, 0:MLA_V, :] = jnp.transpose(zkv[:, a + MLA_NOPE:a + MLA_NOPE + MLA_V]).astype(BF16)
        vt_ref[0, h, MLA_V:VT_ROWS, :] = ones


def _mla_prep(z, cos, sin, qg, kvg, wq1, wq2, wkv, *, bsz, n_pos, tm=512):
    tm = min(tm, n_pos)
    nt = n_pos // tm
    const = lambda b, t: (0, 0)
    zspec = lambda width, off: pl.BlockSpec((tm, width), lambda b, t: (b * nt + t, off // width))
    hspec = lambda width: pl.BlockSpec((1, MLA_HEADS, tm, width), lambda b, t: (b, 0, t, 0))
    return pl.pallas_call(
        _mla_prep_kernel,
        grid=(bsz, nt),
        in_specs=[
            zspec(MLA_Q_RANK, OFF_CQ), zspec(MLA_KV_RANK, OFF_CKV), zspec(LANES, OFF_KR), zspec(LANES, OFF_KRR),
            pl.BlockSpec((tm, LANES), lambda b, t: (b * nt + t, 0)),
            pl.BlockSpec((tm, LANES), lambda b, t: (b * nt + t, 0)),
            pl.BlockSpec((1, MLA_Q_RANK), const), pl.BlockSpec((1, MLA_KV_RANK), const),
            pl.BlockSpec(wq1.shape, const), pl.BlockSpec(wq2.shape, const), pl.BlockSpec(wkv.shape, const),
        ],
        out_specs=[hspec(MLA_QK_PAD), hspec(MLA_QK_PAD),
                   pl.BlockSpec((1, MLA_HEADS, VT_ROWS, tm), lambda b, t: (b, 0, 0, t))],
        out_shape=[jax.ShapeDtypeStruct((bsz, MLA_HEADS, n_pos, MLA_QK_PAD), BF16),
                   jax.ShapeDtypeStruct((bsz, MLA_HEADS, n_pos, MLA_QK_PAD), BF16),
                   jax.ShapeDtypeStruct((bsz, MLA_HEADS, VT_ROWS, n_pos), BF16)],
        compiler_params=_cparams(("parallel", "parallel")),
        name="mla_prep",
    )(z, z, z, z, cos, sin, qg, kvg, wq1, wq2, wkv)


def _softmax_step_t(st, vt, m_ref, acc_ref, idx):
    m_old = m_ref[idx]
    m_new = jnp.maximum(m_old, jnp.max(st, axis=0, keepdims=True))
    p = jnp.exp2(st - m_new).astype(BF16)
    corr = jnp.exp2(m_old - m_new)
    acc_ref[idx] = corr * acc_ref[idx] + jnp.dot(vt, p, preferred_element_type=F32)
    m_ref[idx] = m_new


def _causal_mask_t(t):
    k = lax.broadcasted_iota(jnp.int32, (t, t), 0)
    q = lax.broadcasted_iota(jnp.int32, (t, t), 1)
    return q >= k


def _finish_t(acc):
    return jnp.transpose(acc[0:MLA_V] / acc[MLA_V:MLA_V + 1])


def _mla_flash_kernel(q_ref, k_ref, vt_ref, o_ref, m_ref, acc_ref, *, t):
    qi = pl.program_id(1)
    ki = pl.program_id(2)

    @pl.when(ki == 0)
    def _():
        m_ref[...] = jnp.full_like(m_ref, NEG_INF)
        acc_ref[...] = jnp.zeros_like(acc_ref)

    def step(masked):
        mask = _causal_mask_t(t) if masked else None
        for h in range(MLA_HEADS):
            st = lax.dot_general(k_ref[0, h], q_ref[0, h], (((1,), (1,)), ((), ())), preferred_element_type=F32)
            if masked:
                st = jnp.where(mask, st, NEG_INF)
            _softmax_step_t(st, vt_ref[0, h], m_ref, acc_ref, h)

    @pl.when(ki < qi)
    def _():
        step(False)

    @pl.when(ki == qi)
    def _():
        step(True)
        for h in range(MLA_HEADS):
            o_ref[0, :, h * MLA_V:(h + 1) * MLA_V] = _finish_t(acc_ref[h]).astype(BF16)


def _mla_flash(q, k, vt, *, t=512):
    bsz, nh, n_pos, _ = q.shape
    t = min(t, n_pos)
    nb = n_pos // t
    return pl.pallas_call(
        functools.partial(_mla_flash_kernel, t=t),
        grid=(bsz, nb, nb),
        in_specs=[
            pl.BlockSpec((1, nh, t, MLA_QK_PAD), lambda b, qi, ki: (b, 0, qi, 0)),
            pl.BlockSpec((1, nh, t, MLA_QK_PAD), lambda b, qi, ki: (b, 0, jnp.minimum(ki, qi), 0)),
            pl.BlockSpec((1, nh, VT_ROWS, t), lambda b, qi, ki: (b, 0, 0, jnp.minimum(ki, qi))),
        ],
        out_specs=pl.BlockSpec((1, t, nh * MLA_V), lambda b, qi, ki: (b, qi, 0)),
        out_shape=jax.ShapeDtypeStruct((bsz, n_pos, nh * MLA_V), BF16),
        scratch_shapes=[pltpu.VMEM((nh, 1, t), F32), pltpu.VMEM((nh, VT_ROWS, t), F32)],
        compiler_params=_cparams(("parallel", "parallel", "arbitrary")),
        name="mla_flash",
    )(q, k, vt)


def _diff_prep_kernel(q_ref, k_ref, v_ref, qm_ref, kb_ref, vt_ref):
    hw = 2 * DIFF_QK
    tm = q_ref.shape[0]
    lane = lax.broadcasted_iota(jnp.int32, (tm, hw), 1)
    scale = DIFF_QK ** -0.5 * LOG2E
    ones = jnp.ones((VT_ROWS - DIFF_V, tm), BF16)
    kb_ref[...] = k_ref[...].astype(BF16)
    for h in range(DIFF_HEADS):
        qh = q_ref[:, h * hw:(h + 1) * hw] * scale
        qm_ref[0, 2 * h] = jnp.where(lane < DIFF_QK, qh, 0.0).astype(BF16)
        qm_ref[0, 2 * h + 1] = jnp.where(lane >= DIFF_QK, qh, 0.0).astype(BF16)
        vt_ref[0, h, 0:DIFF_V, :] = jnp.transpose(v_ref[:, h * DIFF_V:(h + 1) * DIFF_V]).astype(BF16)
        vt_ref[0, h, DIFF_V:VT_ROWS, :] = ones


def _diff_prep(z, *, bsz, n_pos, tm=512):
    tm = min(tm, n_pos)
    nt = n_pos // tm
    w = DIFF_HEADS * DIFF_V
    zspec = lambda off: pl.BlockSpec((tm, w), lambda b, t: (b * nt + t, off // w))
    return pl.pallas_call(
        _diff_prep_kernel,
        grid=(bsz, nt),
        in_specs=[zspec(OFF_DQ), zspec(OFF_DK), zspec(OFF_DV)],
        out_specs=[pl.BlockSpec((1, 2 * DIFF_HEADS, tm, 2 * DIFF_QK), lambda b, t: (b, 0, t, 0)),
                   pl.BlockSpec((tm, w), lambda b, t: (b * nt + t, 0)),
                   pl.BlockSpec((1, DIFF_HEADS, VT_ROWS, tm), lambda b, t: (b, 0, 0, t))],
        out_shape=[jax.ShapeDtypeStruct((bsz, 2 * DIFF_HEADS, n_pos, 2 * DIFF_QK), BF16),
                   jax.ShapeDtypeStruct((bsz * n_pos, w), BF16),
                   jax.ShapeDtypeStruct((bsz, DIFF_HEADS, VT_ROWS, n_pos), BF16)],
        compiler_params=_cparams(("parallel", "parallel")),
        name="diff_prep",
    )(z, z, z)


def _diff_kernel(qm_ref, k_ref, vt_ref, pq_ref, pk_ref, tab_ref, lq1_ref, lk1_ref, lq2_ref, lk2_ref, sg_ref,
                 o_ref, m_ref, acc_ref, *, t, lambda_init):
    qi = pl.program_id(1)
    ki = pl.program_id(2)
    nh = DIFF_HEADS
    hw = 2 * DIFF_QK

    @pl.when(ki == 0)
    def _():
        m_ref[...] = jnp.full_like(m_ref, NEG_INF)
        acc_ref[...] = jnp.zeros_like(acc_ref)

    def step(masked):
        dist = jnp.clip(pq_ref[0] - pk_ref[0], 0, T5_TABLE - 1)
        mask = _causal_mask_t(t) if masked else None
        for h in range(nh):
            tab = jnp.broadcast_to(tab_ref[h:h + 1, :], (t, T5_TABLE))
            bias = jnp.concatenate(
                [jnp.take_along_axis(tab, dist[:, c * LANES:(c + 1) * LANES], axis=1) for c in range(t // LANES)],
                axis=1)
            if masked:
                bias = jnp.where(mask, bias, NEG_INF)
            kh = k_ref[:, h * hw:(h + 1) * hw]
            for mp in range(2):
                st = lax.dot_general(kh, qm_ref[0, 2 * h + mp], (((1,), (1,)), ((), ())),
                                     preferred_element_type=F32) + bias
                _softmax_step_t(st, vt_ref[0, h], m_ref, acc_ref, 2 * h + mp)

    @pl.when(ki < qi)
    def _():
        step(False)

    @pl.when(ki == qi)
    def _():
        step(True)
        lam = (jnp.exp(jnp.sum(lq1_ref[...] * lk1_ref[...], axis=-1, keepdims=True))
               - jnp.exp(jnp.sum(lq2_ref[...] * lk2_ref[...], axis=-1, keepdims=True)) + lambda_init)
        for h in range(nh):
            o = _finish_t(acc_ref[2 * h]) - lam * _finish_t(acc_ref[2 * h + 1])
            o = _rms(o, sg_ref[...]) * (1.0 - lambda_init)
            o_ref[:, h * DIFF_V:(h + 1) * DIFF_V] = o.astype(BF16)


def _diff_attn(qm, kb, vt, pos_q, pos_k, table, lq1, lk1, lq2, lk2, sg, *, bsz, n_pos, lambda_init, t=512):
    t = min(t, n_pos)
    nb = n_pos // t
    w = DIFF_HEADS * DIFF_V
    const = lambda b, qi, ki: (0, 0)
    kmin = lambda qi, ki: jnp.minimum(ki, qi)
    return pl.pallas_call(
        functools.partial(_diff_kernel, t=t, lambda_init=lambda_init),
        grid=(bsz, nb, nb),
        in_specs=[
            pl.BlockSpec((1, 2 * DIFF_HEADS, t, 2 * DIFF_QK), lambda b, qi, ki: (b, 0, qi, 0)),
            pl.BlockSpec((t, w), lambda b, qi, ki: (b * nb + kmin(qi, ki), 0)),
            pl.BlockSpec((1, DIFF_HEADS, VT_ROWS, t), lambda b, qi, ki: (b, 0, 0, kmin(qi, ki))),
            pl.BlockSpec((1, 1, t), lambda b, qi, ki: (b * nb + qi, 0, 0)),
            pl.BlockSpec((1, t, 1), lambda b, qi, ki: (b, kmin(qi, ki), 0)),
            pl.BlockSpec((DIFF_HEADS, T5_TABLE), const),
            pl.BlockSpec((1, DIFF_QK), const), pl.BlockSpec((1, DIFF_QK), const),
            pl.BlockSpec((1, DIFF_QK), const), pl.BlockSpec((1, DIFF_QK), const),
            pl.BlockSpec((1, DIFF_V), const),
        ],
        out_specs=pl.BlockSpec((t, w), lambda b, qi, ki: (b * nb + qi, 0)),
        out_shape=jax.ShapeDtypeStruct((bsz * n_pos, w), BF16),
        scratch_shapes=[pltpu.VMEM((2 * DIFF_HEADS, 1, t), F32),
                        pltpu.VMEM((2 * DIFF_HEADS, VT_ROWS, t), F32)],
        compiler_params=_cparams(("parallel", "parallel", "arbitrary")),
        name="diff_attn",
    )(qm, kb, vt, pos_q, pos_k, table, lq1, lk1, lq2, lk2, sg)


def _ret_kernel(q_ref, qr_ref, k_ref, kr_ref, v_ref, g_ref, cos_ref, sin_ref, o_ref, st_ref, *, tm):
    t = pl.program_id(1)
    c = RET_CHUNK
    nh = RET_HEADS
    w = nh * RET_QK

    @pl.when(t == 0)
    def _():
        st_ref[...] = jnp.zeros_like(st_ref)

    log_gamma = [math.log(1.0 - 2.0 ** (-5.0 - h)) for h in range(nh)]
    lane = lax.broadcasted_iota(jnp.int32, (1, w), 1)
    lg_lane = jnp.zeros((1, w), F32)
    for h in range(nh):
        lg_lane = jnp.where(lane // RET_QK == h, log_gamma[h], lg_lane)
    tok = lax.broadcasted_iota(jnp.int32, (c, 1), 0).astype(F32)
    q_decay = jnp.exp(lg_lane * (tok + 1.0))
    k_decay = jnp.exp(lg_lane * (c - 1.0 - tok))
    ri = lax.broadcasted_iota(jnp.int32, (c, c), 0)
    ci = lax.broadcasted_iota(jnp.int32, (c, c), 1)
    rel = (ri - ci).astype(F32)
    intra = [jnp.where(rel >= 0, jnp.exp(log_gamma[h] * jnp.maximum(rel, 0.0)), 0.0) for h in range(nh)]
    head_lanes = [(lane // RET_QK == h) for h in range(nh)]

    for j in range(tm // c):
        rows = slice(j * c, (j + 1) * c)
        cos = jnp.concatenate([cos_ref[rows, :]] * (w // LANES), axis=1)
        sin = jnp.concatenate([sin_ref[rows, :]] * (w // LANES), axis=1)
        q = q_ref[rows, :] * cos + qr_ref[rows, :] * sin
        k = (k_ref[rows, :] * cos + kr_ref[rows, :] * sin) * (RET_QK ** -0.5)
        kb = k.astype(BF16)
        qd = q * q_decay
        kdt = jnp.transpose(k * k_decay).astype(BF16)
        for h in range(nh):
            vh = v_ref[rows, h * RET_V:(h + 1) * RET_V].astype(BF16)
            qh = jnp.where(head_lanes[h], q, 0.0).astype(BF16)
            scores = lax.dot_general(qh, kb, (((1,), (1,)), ((), ())), preferred_element_type=F32) * intra[h]
            inner = jnp.dot(scores.astype(BF16), vh, preferred_element_type=F32)
            qdh = jnp.where(head_lanes[h], qd, 0.0).astype(BF16)
            state = st_ref[...]
            cross = jnp.dot(qdh, state.astype(BF16), preferred_element_type=F32)
            o = inner + cross
            mu = jnp.mean(o, axis=-1, keepdims=True)
            oc = o - mu
            var = jnp.mean(oc * oc, axis=-1, keepdims=True)
            o = oc * lax.rsqrt(var + LN_EPS)
            gh = g_ref[rows, h * RET_V:(h + 1) * RET_V]
            o_ref[rows, h * RET_V:(h + 1) * RET_V] = (gh * jax.nn.sigmoid(gh) * o).astype(BF16)
            hs = slice(h * RET_QK, (h + 1) * RET_QK)
            kv = jnp.dot(kdt[hs, :], vh, preferred_element_type=F32)
            st_ref[hs, :] = state[hs, :] * math.exp(log_gamma[h] * c) + kv


def _retention(z, cos, sin, *, bsz, n_pos, tm=512):
    tm = min(tm, n_pos)
    nt = n_pos // tm
    w = RET_HEADS * RET_QK
    wv = RET_HEADS * RET_V
    zspec = lambda width, off: pl.BlockSpec((tm, width), lambda b, t: (b * nt + t, off // width))
    tspec = pl.BlockSpec((tm, LANES), lambda b, t: (b * nt + t, 0))
    return pl.pallas_call(
        functools.partial(_ret_kernel, tm=tm),
        grid=(bsz, nt),
        in_specs=[zspec(w, OFF_RQ), zspec(w, OFF_RQR), zspec(w, OFF_RK), zspec(w, OFF_RKR),
                  zspec(wv, OFF_RV), zspec(wv, OFF_RG), tspec, tspec],
        out_specs=pl.BlockSpec((tm, wv), lambda b, t: (b * nt + t, 0)),
        out_shape=jax.ShapeDtypeStruct((bsz * n_pos, wv), BF16),
        scratch_shapes=[pltpu.VMEM((w, RET_V), F32)],
        compiler_params=_cparams(("parallel", "arbitrary")),
        name="retention",
    )(z, z, z, z, z, z, cos, sin)


def _outproj_ln_kernel(y0_ref, y1_ref, y2_ref, y3_ref, w_ref, x_ref, g_ref, b_ref, o_ref):
    acc = None
    for j, y_ref in enumerate((y0_ref, y1_ref, y2_ref, y3_ref)):
        part = jnp.dot(y_ref[...], w_ref[j * 512:(j + 1) * 512, :], preferred_element_type=F32)
        acc = part if acc is None else acc + part
    o_ref[...] = _layer_norm(ALPHA * x_ref[...] + acc, g_ref[...], b_ref[...])


def _outproj_ln(ys, w, x, g, b, *, tm=512):
    m, d = x.shape
    tm = min(tm, m)
    yspec = pl.BlockSpec((tm, 512), lambda i: (i, 0))
    return pl.pallas_call(
        _outproj_ln_kernel,
        grid=(m // tm,),
        in_specs=[yspec, yspec, yspec, yspec,
                  pl.BlockSpec(w.shape, lambda i: (0, 0)),
                  pl.BlockSpec((tm, d), lambda i: (i, 0)),
                  pl.BlockSpec((1, d), lambda i: (0, 0)),
                  pl.BlockSpec((1, d), lambda i: (0, 0))],
        out_specs=pl.BlockSpec((tm, d), lambda i: (i, 0)),
        out_shape=jax.ShapeDtypeStruct((m, d), F32),
        compiler_params=_cparams(("parallel",)),
        name="outproj_ln",
    )(*ys, w, x, g, b)


def _rot_cols(w, heads, dim):
    k = w.shape[0]
    w = w.reshape(k, heads, 2, dim // 2)
    return jnp.concatenate([-w[:, :, 1], w[:, :, 0]], axis=-1).reshape(k, heads * dim)


def _pad_cols(w, width):
    return jnp.pad(w, ((0, 0), (0, width - w.shape[1])))


def _wide_w_in(w_in):
    sizes = (S5_WIDTH, MLA_Q_RANK, MLA_KV_RANK, MLA_ROPE,
             RET_HEADS * RET_QK, RET_HEADS * RET_QK, RET_HEADS * RET_V, RET_HEADS * RET_V,
             DIFF_HEADS * 2 * DIFF_QK, DIFF_HEADS * 2 * DIFF_QK, DIFF_HEADS * DIFF_V)
    offs = np.concatenate([[0], np.cumsum(sizes)])
    (s5_u, cq, ckv, kr, rq, rk, rv, rg, dq, dk, dv) = [w_in[:, offs[i]:offs[i + 1]] for i in range(len(sizes))]
    cols = [s5_u, cq, rv, rg, dq, dk, dv,
            rq, _rot_cols(rq, RET_HEADS, RET_QK), rk, _rot_cols(rk, RET_HEADS, RET_QK),
            ckv, _pad_cols(kr, LANES), _pad_cols(_rot_cols(kr, 1, MLA_ROPE), LANES)]
    return jnp.concatenate(cols, axis=1).astype(BF16)


def _mla_q_weights(w_uq):
    k = w_uq.shape[0]
    w = w_uq.reshape(k, MLA_HEADS, MLA_NOPE + MLA_ROPE)
    rope = w[:, :, MLA_NOPE:]
    pad = MLA_QK_PAD - MLA_NOPE - MLA_ROPE
    w1 = jnp.pad(w, ((0, 0), (0, 0), (0, pad))).reshape(k, MLA_HEADS * MLA_QK_PAD)
    rot = _rot_cols(rope.reshape(k, MLA_HEADS * MLA_ROPE), MLA_HEADS, MLA_ROPE).reshape(k, MLA_HEADS, MLA_ROPE)
    w2 = jnp.pad(rot, ((0, 0), (0, 0), (0, LANES - MLA_ROPE))).reshape(k, MLA_HEADS * LANES)
    return w1.astype(BF16), w2.astype(BF16)


def kernel(x, p, positions, rel_bias, ffn1_w_gate, ffn1_w_up, ffn1_w_down, ln1_g, ln1_b, w_in, w_out, ln2_g, ln2_b, s5_lambda_re, s5_lambda_im, s5_log_dt, s5_b_re, s5_b_im, s5_c_re, s5_c_im, s5_d, s5_w_glu, s5_b_glu, mla_q_norm_g, mla_w_uq, mla_kv_norm_g, mla_w_ukv, diff_lambda_q1, diff_lambda_k1, diff_lambda_q2, diff_lambda_k2, diff_subln_g, ffn2_w_gate, ffn2_w_up, ffn2_w_down, ple_w_gate, ple_b_gate, ple_w_proj, ln3_g, ln3_b):
    bsz, n_pos, d = x.shape
    m = bsz * n_pos
    depth = ffn1_w_gate.shape[0]
    xf = x.reshape(m, d)
    row = lambda v: v.reshape(1, -1)

    cos, sin = _rope_tables(positions.reshape(m, 1))
    table = _bias_table(rel_bias)
    t_diff = min(512, n_pos)
    pos_q = positions.reshape(bsz * (n_pos // t_diff), 1, t_diff)
    pos_k = positions.reshape(bsz, n_pos, 1)

    for i in range(depth):
        xf = _ffn_ln(xf, ffn1_w_gate[i].astype(BF16), ffn1_w_up[i].astype(BF16), ffn1_w_down[i].astype(BF16),
                     row(ln1_g[i]), row(ln1_b[i]))
        z = _inproj(xf, _wide_w_in(w_in[i]))

        apr, api, bbr, bbi = _s5_discretise(s5_lambda_re[i], s5_lambda_im[i], s5_log_dt[i], s5_b_re[i], s5_b_im[i])
        bd = jnp.concatenate([_block_diag_in(bbr), _block_diag_in(bbi)], axis=1).astype(BF16)
        cd = jnp.concatenate([_block_diag_out(s5_c_re[i]), -_block_diag_out(s5_c_im[i])], axis=0).astype(BF16)
        y_s5 = _s5_mixer(z, bd, cd, apr, api, row(s5_d[i]), s5_w_glu[i].astype(BF16), row(s5_b_glu[i]),
                         bsz=bsz, n_pos=n_pos)

        wq1, wq2 = _mla_q_weights(mla_w_uq[i])
        q, k, vt = _mla_prep(z, cos, sin, row(mla_q_norm_g[i]), row(mla_kv_norm_g[i]), wq1, wq2,
                             mla_w_ukv[i].astype(BF16), bsz=bsz, n_pos=n_pos)
        y_mla = _mla_flash(q, k, vt).reshape(m, MLA_HEADS * MLA_V)

        y_ret = _retention(z, cos, sin, bsz=bsz, n_pos=n_pos)

        lambda_init = 0.8 - 0.6 * math.exp(-0.3 * i)
        dqm, dkb, dvt = _diff_prep(z, bsz=bsz, n_pos=n_pos)
        y_diff = _diff_attn(dqm, dkb, dvt, pos_q, pos_k, table, row(diff_lambda_q1[i]), row(diff_lambda_k1[i]),
                            row(diff_lambda_q2[i]), row(diff_lambda_k2[i]), row(diff_subln_g[i]),
                            bsz=bsz, n_pos=n_pos, lambda_init=lambda_init, t=t_diff)

        xf = _outproj_ln((y_s5, y_mla, y_ret, y_diff), w_out[i].astype(BF16), xf, row(ln2_g[i]), row(ln2_b[i]))

        extra = _ple(xf, p[i].reshape(m, PLE_DIM), ple_w_gate[i].astype(BF16), row(ple_b_gate[i]),
                     ple_w_proj[i].astype(BF16))
        xf = _ffn_ln(xf, ffn2_w_gate[i].astype(BF16), ffn2_w_up[i].astype(BF16), ffn2_w_down[i].astype(BF16),
                     row(ln3_g[i]), row(ln3_b[i]), extra)
    return xf.reshape(bsz, n_pos, d)
```

```python
import functools
import math

import numpy as np
import jax
import jax.numpy as jnp
from jax import lax
from jax.experimental import pallas as pl
from jax.experimental.pallas import tpu as pltpu

F32 = jnp.float32
BF16 = jnp.bfloat16

D_MODEL = 2048
DEPTH = 2
PLE_DIM = 256
D_FF = 5632
ALPHA = (2 * DEPTH) ** 0.25
ROPE_THETA = 10000.0
NEG_INF = -1e30
LN_EPS = 1e-5
RMS_EPS = 1e-6

S5_WIDTH = 512
S5_GROUP = 16
S5_GROUPS = 32
S5_STATE = 64
S5_NSTATE = S5_GROUPS * S5_STATE

MLA_HEADS = 4
MLA_Q_RANK = 512
MLA_KV_RANK = 128
MLA_NOPE = 128
MLA_ROPE = 64
MLA_V = 128
MLA_QK_PAD = 256

RET_HEADS = 4
RET_QK = 64
RET_V = 128
RET_CHUNK = 128

DIFF_HEADS = 4
DIFF_QK = 64
DIFF_V = 128

T5_BUCKETS = 32
T5_MAX_DIST = 128
T5_TABLE = 128

LOG2E = math.log2(math.e)
VT_ROWS = 144

LANES = 128
SUBLANES = 8
VMEM_LIMIT_BYTES = 56 * 1024 * 1024

OFF_S5 = 0
OFF_CQ = 512
OFF_RV = 1024
OFF_RG = 1536
OFF_DQ = 2048
OFF_DK = 2560
OFF_DV = 3072
OFF_RQ = 3584
OFF_RQR = 3840
OFF_RK = 4096
OFF_RKR = 4352
OFF_CKV = 4608
OFF_KR = 4736
OFF_KRR = 4864
IN_WIDE = 4992
IN_TN = 1664


def _cparams(sem):
    return pltpu.CompilerParams(dimension_semantics=sem, vmem_limit_bytes=VMEM_LIMIT_BYTES)


def _layer_norm(y, g, b):
    mu = jnp.mean(y, axis=-1, keepdims=True)
    yc = y - mu
    var = jnp.mean(yc * yc, axis=-1, keepdims=True)
    return yc * lax.rsqrt(var + LN_EPS) * g + b


def _ffn_ln_kernel(*refs, nf, has_extra):
    if has_extra:
        x_ref, wg_ref, wu_ref, wd_ref, g_ref, b_ref, e_ref, o_ref, xb_ref, acc_ref = refs
    else:
        x_ref, wg_ref, wu_ref, wd_ref, g_ref, b_ref, o_ref, xb_ref, acc_ref = refs
        e_ref = None
    f = pl.program_id(1)

    @pl.when(f == 0)
    def _():
        xb_ref[...] = x_ref[...].astype(BF16)
        acc_ref[...] = jnp.zeros_like(acc_ref)

    xb = xb_ref[...]
    gate = jnp.dot(xb, wg_ref[...], preferred_element_type=F32)
    up = jnp.dot(xb, wu_ref[...], preferred_element_type=F32)
    h = gate * jax.nn.sigmoid(gate) * up
    acc_ref[...] += jnp.dot(h.astype(BF16), wd_ref[...], preferred_element_type=F32)

    @pl.when(f == nf - 1)
    def _():
        y = ALPHA * x_ref[...] + 0.5 * acc_ref[...]
        if e_ref is not None:
            y = y + e_ref[...]
        o_ref[...] = _layer_norm(y, g_ref[...], b_ref[...])


def _ffn_ln(x, wg, wu, wd, g, b, extra=None, *, tm=512, tf=512):
    m, d = x.shape
    f_dim = wg.shape[1]
    tm = min(tm, m)
    nf = f_dim // tf
    in_specs = [
        pl.BlockSpec((tm, d), lambda i, f: (i, 0)),
        pl.BlockSpec((d, tf), lambda i, f: (0, f)),
        pl.BlockSpec((d, tf), lambda i, f: (0, f)),
        pl.BlockSpec((tf, d), lambda i, f: (f, 0)),
        pl.BlockSpec((1, d), lambda i, f: (0, 0)),
        pl.BlockSpec((1, d), lambda i, f: (0, 0)),
    ]
    args = [x, wg, wu, wd, g, b]
    if extra is not None:
        in_specs.append(pl.BlockSpec((tm, d), lambda i, f: (i, 0)))
        args.append(extra)
    return pl.pallas_call(
        functools.partial(_ffn_ln_kernel, nf=nf, has_extra=extra is not None),
        grid=(m // tm, nf),
        in_specs=in_specs,
        out_specs=pl.BlockSpec((tm, d), lambda i, f: (i, 0)),
        out_shape=jax.ShapeDtypeStruct((m, d), F32),
        scratch_shapes=[pltpu.VMEM((tm, d), BF16), pltpu.VMEM((tm, d), F32)],
        compiler_params=_cparams(("parallel", "arbitrary")),
        name="ffn_ln",
    )(*args)


def _ple_kernel(x_ref, p_ref, wg_ref, bg_ref, wp_ref, o_ref):
    xb = x_ref[...].astype(BF16)
    pb = p_ref[...].astype(BF16)
    gate = jax.nn.sigmoid(jnp.dot(xb, wg_ref[...], preferred_element_type=F32) + bg_ref[...])
    o_ref[...] = gate * jnp.dot(pb, wp_ref[...], preferred_element_type=F32)


def _ple(x, p, wg, bg, wp, *, tm=512):
    m, d = x.shape
    tm = min(tm, m)
    return pl.pallas_call(
        _ple_kernel,
        grid=(m // tm,),
        in_specs=[
            pl.BlockSpec((tm, d), lambda i: (i, 0)),
            pl.BlockSpec((tm, PLE_DIM), lambda i: (i, 0)),
            pl.BlockSpec((d, d), lambda i: (0, 0)),
            pl.BlockSpec((1, d), lambda i: (0, 0)),
            pl.BlockSpec((PLE_DIM, d), lambda i: (0, 0)),
        ],
        out_specs=pl.BlockSpec((tm, d), lambda i: (i, 0)),
        out_shape=jax.ShapeDtypeStruct((m, d), F32),
        compiler_params=_cparams(("parallel",)),
        name="ple",
    )(x, p, wg, bg, wp)


def _inproj_kernel(x_ref, w_ref, o_ref):
    o_ref[...] = jnp.dot(x_ref[...].astype(BF16), w_ref[...], preferred_element_type=F32)


def _inproj(x, w, *, tm=512):
    m, d = x.shape
    n = w.shape[1]
    tm = min(tm, m)
    return pl.pallas_call(
        _inproj_kernel,
        grid=(n // IN_TN, m // tm),
        in_specs=[
            pl.BlockSpec((tm, d), lambda j, i: (i, 0)),
            pl.BlockSpec((d, IN_TN), lambda j, i: (0, j)),
        ],
        out_specs=pl.BlockSpec((tm, IN_TN), lambda j, i: (i, j)),
        out_shape=jax.ShapeDtypeStruct((m, n), F32),
        compiler_params=_cparams(("parallel", "parallel")),
        name="inproj",
    )(x, w)


def _rope_kernel(pos_ref, freq_ref, cos_ref, sin_ref):
    ang = pos_ref[...].astype(F32) * freq_ref[...]
    cos_ref[...] = jnp.cos(ang)
    sin_ref[...] = jnp.sin(ang)


def _rope_tables(pos_col, *, tm=1024):
    m = pos_col.shape[0]
    tm = min(tm, m)
    half = MLA_ROPE // 2
    inv = 1.0 / (ROPE_THETA ** (np.arange(0, MLA_ROPE, 2, dtype=np.float64) / MLA_ROPE))
    freq = jnp.asarray(np.tile(inv, LANES // half)[None, :], F32)
    return pl.pallas_call(
        _rope_kernel,
        grid=(m // tm,),
        in_specs=[pl.BlockSpec((tm, 1), lambda i: (i, 0)), pl.BlockSpec((1, LANES), lambda i: (0, 0))],
        out_specs=[pl.BlockSpec((tm, LANES), lambda i: (i, 0))] * 2,
        out_shape=[jax.ShapeDtypeStruct((m, LANES), F32)] * 2,
        compiler_params=_cparams(("parallel",)),
        name="rope_tables",
    )(pos_col, freq)


def _t5_bucket_static():
    n = np.arange(T5_TABLE)
    max_exact = T5_BUCKETS // 2
    nf = np.maximum(n, 1).astype(np.float64)
    large = max_exact + (np.log(nf / max_exact) / math.log(T5_MAX_DIST / max_exact)
                         * (T5_BUCKETS - max_exact)).astype(np.int64)
    large = np.minimum(large, T5_BUCKETS - 1)
    return np.where(n < max_exact, n, large)


def _bias_table_kernel(rb_ref, onehot_ref, o_ref):
    rb = rb_ref[...]
    oh = onehot_ref[...]
    rows = [jnp.sum(oh * rb[:, h:h + 1], axis=0, keepdims=True) for h in range(DIFF_HEADS)]
    o_ref[...] = jnp.concatenate(rows, axis=0) * LOG2E


def _bias_table(rel_bias):
    bucket = _t5_bucket_static()
    onehot = jnp.asarray((np.arange(T5_BUCKETS)[:, None] == bucket[None, :]).astype(np.float32))
    return pl.pallas_call(
        _bias_table_kernel,
        out_shape=jax.ShapeDtypeStruct((DIFF_HEADS, T5_TABLE), F32),
        name="t5_bias_table",
    )(rel_bias, onehot)


def _s5_disc_kernel(lr_ref, li_ref, ldt_ref, br_ref, bi_ref, apr_ref, api_ref, bbr_ref, bbi_ref):
    lr = lr_ref[...]
    li = li_ref[...]
    dt = jnp.exp(ldt_ref[...])
    k = (lax.broadcasted_iota(jnp.int32, (SUBLANES, 1), 0) + 1).astype(F32)
    mag = jnp.exp(lr * dt * k)
    apr_ref[...] = mag * jnp.cos(li * dt * k)
    api_ref[...] = mag * jnp.sin(li * dt * k)
    mag1 = jnp.exp(lr * dt)
    ar = mag1 * jnp.cos(li * dt)
    ai = mag1 * jnp.sin(li * dt)
    den = lr * lr + li * li
    fr = ((ar - 1.0) * lr + ai * li) / den
    fi = (ai * lr - (ar - 1.0) * li) / den
    br = br_ref[...]
    bi = bi_ref[...]
    bbr_ref[...] = fr * br - fi * bi
    bbi_ref[...] = fr * bi + fi * br


def _s5_discretise(lam_re, lam_im, log_dt, b_re, b_im):
    n = S5_NSTATE
    lr = lam_re.reshape(1, n)
    li = lam_im.reshape(1, n)
    ldt = jnp.repeat(log_dt, S5_STATE).reshape(1, n)
    br = jnp.transpose(b_re, (2, 0, 1)).reshape(S5_GROUP, n)
    bi = jnp.transpose(b_im, (2, 0, 1)).reshape(S5_GROUP, n)
    return pl.pallas_call(
        _s5_disc_kernel,
        out_shape=[jax.ShapeDtypeStruct((SUBLANES, n), F32)] * 2 + [jax.ShapeDtypeStruct((S5_GROUP, n), F32)] * 2,
        name="s5_discretise",
    )(lr, li, ldt, br, bi)


def _block_diag_in(bb):
    eye = jnp.eye(S5_GROUPS, dtype=bb.dtype)
    v = bb.reshape(S5_GROUP, S5_GROUPS, S5_STATE)
    out = eye[:, None, :, None] * v[None, :, :, :]
    return out.reshape(S5_WIDTH, S5_NSTATE)


def _block_diag_out(c):
    eye = jnp.eye(S5_GROUPS, dtype=c.dtype)
    v = jnp.transpose(c, (0, 2, 1))
    out = v[:, :, None, :] * eye[:, None, :, None]
    return out.reshape(S5_NSTATE, S5_WIDTH)


S5_LANE_CHUNK = 512


def _s5_kernel(u_ref, bd_ref, cd_ref, apr_ref, api_ref, d_ref, wglu_ref, bglu_ref, o_ref,
               hr_ref, hi_ref, cr_ref, ci_ref, *, tm):
    t = pl.program_id(1)

    @pl.when(t == 0)
    def _():
        cr_ref[...] = jnp.zeros_like(cr_ref)
        ci_ref[...] = jnp.zeros_like(ci_ref)

    u = u_ref[...]
    ub = u.astype(BF16)
    n = S5_NSTATE
    hr_ref[...] = jnp.dot(ub, bd_ref[:, :n], preferred_element_type=F32)
    hi_ref[...] = jnp.dot(ub, bd_ref[:, n:], preferred_element_type=F32)

    row = lax.broadcasted_iota(jnp.int32, (SUBLANES, S5_LANE_CHUNK), 0)
    for c in range(n // S5_LANE_CHUNK):
        lanes = pl.ds(c * S5_LANE_CHUNK, S5_LANE_CHUNK)
        apr = apr_ref[:, lanes]
        api = api_ref[:, lanes]
        steps = [(1, apr[0:1], api[0:1]), (2, apr[1:2], api[1:2]), (4, apr[3:4], api[3:4])]

        def body(rg, carry, lanes=lanes, apr=apr, api=api, steps=steps):
            cr, ci = carry
            r0 = pl.multiple_of(rg * SUBLANES, SUBLANES)
            xr = hr_ref[pl.ds(r0, SUBLANES), lanes]
            xi = hi_ref[pl.ds(r0, SUBLANES), lanes]
            for sh, pr, pi in steps:
                keep = row >= sh
                sr = jnp.where(keep, pltpu.roll(xr, sh, 0), 0.0)
                si = jnp.where(keep, pltpu.roll(xi, sh, 0), 0.0)
                xr, xi = xr + pr * sr - pi * si, xi + pr * si + pi * sr
            hr = xr + apr * cr - api * ci
            hi = xi + apr * ci + api * cr
            hr_ref[pl.ds(r0, SUBLANES), lanes] = hr
            hi_ref[pl.ds(r0, SUBLANES), lanes] = hi
            return hr[SUBLANES - 1:SUBLANES], hi[SUBLANES - 1:SUBLANES]

        cr, ci = lax.fori_loop(0, tm // SUBLANES, body, (cr_ref[:, lanes], ci_ref[:, lanes]))
        cr_ref[:, lanes] = cr
        ci_ref[:, lanes] = ci

    y = (jnp.dot(hr_ref[...].astype(BF16), cd_ref[:n, :], preferred_element_type=F32)
         + jnp.dot(hi_ref[...].astype(BF16), cd_ref[n:, :], preferred_element_type=F32)
         + d_ref[...] * u)
    c0 = math.sqrt(2.0 / math.pi)
    y = 0.5 * y * (1.0 + jnp.tanh(c0 * (y + 0.044715 * (y * y * y))))
    gate = jax.nn.sigmoid(jnp.dot(y.astype(BF16), wglu_ref[...], preferred_element_type=F32) + bglu_ref[...])
    o_ref[...] = (y * gate).astype(BF16)


def _s5_mixer(z, bd, cd, apr, api, d, wglu, bglu, *, bsz, n_pos, tm=256):
    tm = min(tm, n_pos)
    nt = n_pos // tm
    m = bsz * n_pos
    n2 = 2 * S5_NSTATE
    const = lambda b, t: (0, 0)
    return pl.pallas_call(
        functools.partial(_s5_kernel, tm=tm),
        grid=(bsz, nt),
        in_specs=[
            pl.BlockSpec((tm, S5_WIDTH), lambda b, t: (b * nt + t, OFF_S5 // S5_WIDTH)),
            pl.BlockSpec((S5_WIDTH, n2), const),
            pl.BlockSpec((n2, S5_WIDTH), const),
            pl.BlockSpec((SUBLANES, S5_NSTATE), const),
            pl.BlockSpec((SUBLANES, S5_NSTATE), const),
            pl.BlockSpec((1, S5_WIDTH), const),
            pl.BlockSpec((S5_WIDTH, S5_WIDTH), const),
            pl.BlockSpec((1, S5_WIDTH), const),
        ],
        out_specs=pl.BlockSpec((tm, S5_WIDTH), lambda b, t: (b * nt + t, 0)),
        out_shape=jax.ShapeDtypeStruct((m, S5_WIDTH), BF16),
        scratch_shapes=[pltpu.VMEM((tm, S5_NSTATE), F32), pltpu.VMEM((tm, S5_NSTATE), F32),
                        pltpu.VMEM((1, S5_NSTATE), F32), pltpu.VMEM((1, S5_NSTATE), F32)],
        compiler_params=_cparams(("parallel", "arbitrary")),
        name="s5_mixer",
    )(z, bd, cd, apr, api, d, wglu, bglu)


def _rms(x, g):
    return x * lax.rsqrt(jnp.mean(x * x, axis=-1, keepdims=True) + RMS_EPS) * g


def _mla_prep_kernel(cq_ref, ckv_ref, kr_ref, krr_ref, cos_ref, sin_ref, qg_ref, kvg_ref,
                     wq1_ref, wq2_ref, wkv_ref, q_ref, k_ref, vt_ref):
    cos = cos_ref[...]
    sin = sin_ref[...]
    scale = (MLA_NOPE + MLA_ROPE) ** -0.5 * LOG2E
    ones = jnp.ones((VT_ROWS - MLA_V, cos.shape[0]), BF16)
    cqn = _rms(cq_ref[...], qg_ref[...]).astype(BF16)
    z1 = jnp.dot(cqn, wq1_ref[...], preferred_element_type=F32)
    z2 = jnp.dot(cqn, wq2_ref[...], preferred_element_type=F32)
    ckvn = _rms(ckv_ref[...], kvg_ref[...]).astype(BF16)
    zkv = jnp.dot(ckvn, wkv_ref[...], preferred_element_type=F32)
    k_rope = (kr_ref[...] * cos + krr_ref[...] * sin).astype(BF16)
    for h in range(MLA_HEADS):
        a = h * MLA_QK_PAD
        q_nope = z1[:, a:a + MLA_NOPE]
        q_rope = z1[:, a + MLA_NOPE:a + MLA_QK_PAD] * cos + z2[:, h * LANES:(h + 1) * LANES] * sin
        q_ref[0, h, :, 0:MLA_NOPE] = (q_nope * scale).astype(BF16)
        q_ref[0, h, :, MLA_NOPE:MLA_QK_PAD] = (q_rope * scale).astype(BF16)
        k_ref[0, h, :, 0:MLA_NOPE] = zkv[:, a:a + MLA_NOPE].astype(BF16)
        k_ref[0, h, :, MLA_NOPE:MLA_QK_PAD] = k_rope
        vt_ref[0, h, 0:MLA_V, :] = jnp.transpose(zkv[:, a + MLA_NOPE:a + MLA_NOPE + MLA_V]).astype(BF16)
        vt_ref[0, h, MLA_V:VT_ROWS, :] = ones


def _mla_prep(z, cos, sin, qg, kvg, wq1, wq2, wkv, *, bsz, n_pos, tm=512):
    tm = min(tm, n_pos)
    nt = n_pos // tm
    const = lambda b, t: (0, 0)
    zspec = lambda width, off: pl.BlockSpec((tm, width), lambda b, t: (b * nt + t, off // width))
    hspec = lambda width: pl.BlockSpec((1, MLA_HEADS, tm, width), lambda b, t: (b, 0, t, 0))
    return pl.pallas_call(
        _mla_prep_kernel,
        grid=(bsz, nt),
        in_specs=[
            zspec(MLA_Q_RANK, OFF_CQ), zspec(MLA_KV_RANK, OFF_CKV), zspec(LANES, OFF_KR), zspec(LANES, OFF_KRR),
            pl.BlockSpec((tm, LANES), lambda b, t: (b * nt + t, 0)),
            pl.BlockSpec((tm, LANES), lambda b, t: (b * nt + t, 0)),
            pl.BlockSpec((1, MLA_Q_RANK), const), pl.BlockSpec((1, MLA_KV_RANK), const),
            pl.BlockSpec(wq1.shape, const), pl.BlockSpec(wq2.shape, const), pl.BlockSpec(wkv.shape, const),
        ],
        out_specs=[hspec(MLA_QK_PAD), hspec(MLA_QK_PAD),
                   pl.BlockSpec((1, MLA_HEADS, VT_ROWS, tm), lambda b, t: (b, 0, 0, t))],
        out_shape=[jax.ShapeDtypeStruct((bsz, MLA_HEADS, n_pos, MLA_QK_PAD), BF16),
                   jax.ShapeDtypeStruct((bsz, MLA_HEADS, n_pos, MLA_QK_PAD), BF16),
                   jax.ShapeDtypeStruct((bsz, MLA_HEADS, VT_ROWS, n_pos), BF16)],
        compiler_params=_cparams(("parallel", "parallel")),
        name="mla_prep",
    )(z, z, z, z, cos, sin, qg, kvg, wq1, wq2, wkv)


def _softmax_step_t(st, vt, m_ref, acc_ref, idx, shift=None):
    m_old = m_ref[idx]
    cur = jnp.max(st, axis=0, keepdims=True)
    if shift is not None:
        cur = cur + shift
    m_new = jnp.maximum(m_old, cur)
    p = jnp.exp2(st - (m_new if shift is None else m_new - shift)).astype(BF16)
    corr = jnp.exp2(m_old - m_new)
    acc_ref[idx] = corr * acc_ref[idx] + jnp.dot(vt, p, preferred_element_type=F32)
    m_ref[idx] = m_new


def _causal_mask_t(t):
    k = lax.broadcasted_iota(jnp.int32, (t, t), 0)
    q = lax.broadcasted_iota(jnp.int32, (t, t), 1)
    return q >= k


def _finish_t(acc):
    return jnp.transpose(acc[0:MLA_V] / acc[MLA_V:MLA_V + 1])


def _mla_flash_kernel(q_ref, k_ref, vt_ref, o_ref, m_ref, acc_ref, *, t):
    qi = pl.program_id(1)
    ki = pl.program_id(2)

    @pl.when(ki == 0)
    def _():
        m_ref[...] = jnp.full_like(m_ref, NEG_INF)
        acc_ref[...] = jnp.zeros_like(acc_ref)

    def step(masked):
        mask = _causal_mask_t(t) if masked else None
        def scores(h):
            st = lax.dot_general(k_ref[0, h], q_ref[0, h], (((1,), (1,)), ((), ())), preferred_element_type=F32)
            return jnp.where(mask, st, NEG_INF) if masked else st

        sts = [scores(h) for h in range(MLA_HEADS)]
        for h in range(MLA_HEADS):
            _softmax_step_t(sts[h], vt_ref[0, h], m_ref, acc_ref, h)

    @pl.when(ki < qi)
    def _():
        step(False)

    @pl.when(ki == qi)
    def _():
        step(True)
        for h in range(MLA_HEADS):
            o_ref[0, :, h * MLA_V:(h + 1) * MLA_V] = _finish_t(acc_ref[h]).astype(BF16)


def _mla_flash(q, k, vt, *, t=512):
    bsz, nh, n_pos, _ = q.shape
    t = min(t, n_pos)
    nb = n_pos // t
    return pl.pallas_call(
        functools.partial(_mla_flash_kernel, t=t),
        grid=(bsz, nb, nb),
        in_specs=[
            pl.BlockSpec((1, nh, t, MLA_QK_PAD), lambda b, qi, ki: (b, 0, qi, 0)),
            pl.BlockSpec((1, nh, t, MLA_QK_PAD), lambda b, qi, ki: (b, 0, jnp.minimum(ki, qi), 0)),
            pl.BlockSpec((1, nh, VT_ROWS, t), lambda b, qi, ki: (b, 0, 0, jnp.minimum(ki, qi))),
        ],
        out_specs=pl.BlockSpec((1, t, nh * MLA_V), lambda b, qi, ki: (b, qi, 0)),
        out_shape=jax.ShapeDtypeStruct((bsz, n_pos, nh * MLA_V), BF16),
        scratch_shapes=[pltpu.VMEM((nh, 1, t), F32), pltpu.VMEM((nh, VT_ROWS, t), F32)],
        compiler_params=_cparams(("parallel", "parallel", "arbitrary")),
        name="mla_flash",
    )(q, k, vt)


def _diff_prep_kernel(q_ref, k_ref, v_ref, qm_ref, kb_ref, vt_ref):
    hw = 2 * DIFF_QK
    tm = q_ref.shape[0]
    lane = lax.broadcasted_iota(jnp.int32, (tm, hw), 1)
    scale = DIFF_QK ** -0.5 * LOG2E
    ones = jnp.ones((VT_ROWS - DIFF_V, tm), BF16)
    kb_ref[...] = k_ref[...].astype(BF16)
    for h in range(DIFF_HEADS):
        qh = q_ref[:, h * hw:(h + 1) * hw] * scale
        qm_ref[0, 2 * h] = jnp.where(lane < DIFF_QK, qh, 0.0).astype(BF16)
        qm_ref[0, 2 * h + 1] = jnp.where(lane >= DIFF_QK, qh, 0.0).astype(BF16)
        vt_ref[0, h, 0:DIFF_V, :] = jnp.transpose(v_ref[:, h * DIFF_V:(h + 1) * DIFF_V]).astype(BF16)
        vt_ref[0, h, DIFF_V:VT_ROWS, :] = ones


def _diff_prep(z, *, bsz, n_pos, tm=512):
    tm = min(tm, n_pos)
    nt = n_pos // tm
    w = DIFF_HEADS * DIFF_V
    zspec = lambda off: pl.BlockSpec((tm, w), lambda b, t: (b * nt + t, off // w))
    return pl.pallas_call(
        _diff_prep_kernel,
        grid=(bsz, nt),
        in_specs=[zspec(OFF_DQ), zspec(OFF_DK), zspec(OFF_DV)],
        out_specs=[pl.BlockSpec((1, 2 * DIFF_HEADS, tm, 2 * DIFF_QK), lambda b, t: (b, 0, t, 0)),
                   pl.BlockSpec((tm, w), lambda b, t: (b * nt + t, 0)),
                   pl.BlockSpec((1, DIFF_HEADS, VT_ROWS, tm), lambda b, t: (b, 0, 0, t))],
        out_shape=[jax.ShapeDtypeStruct((bsz, 2 * DIFF_HEADS, n_pos, 2 * DIFF_QK), BF16),
                   jax.ShapeDtypeStruct((bsz * n_pos, w), BF16),
                   jax.ShapeDtypeStruct((bsz, DIFF_HEADS, VT_ROWS, n_pos), BF16)],
        compiler_params=_cparams(("parallel", "parallel")),
        name="diff_prep",
    )(z, z, z)


def _diff_kernel(qm_ref, k_ref, vt_ref, pq_ref, pk_ref, tab_ref, lq1_ref, lk1_ref, lq2_ref, lk2_ref, sg_ref,
                 o_ref, m_ref, acc_ref, *, t, lambda_init):
    qi = pl.program_id(1)
    ki = pl.program_id(2)
    nh = DIFF_HEADS
    hw = 2 * DIFF_QK

    @pl.when(ki == 0)
    def _():
        m_ref[...] = jnp.full_like(m_ref, NEG_INF)
        acc_ref[...] = jnp.zeros_like(acc_ref)

    def step(masked):
        dist = jnp.clip(pq_ref[0] - pk_ref[0], 0, T5_TABLE - 1)
        mask = _causal_mask_t(t) if masked else None
        sts = []
        for h in range(nh):
            tab = jnp.broadcast_to(tab_ref[h:h + 1, :], (t, T5_TABLE))
            kh = k_ref[:, h * hw:(h + 1) * hw]
            bias = jnp.concatenate(
                [jnp.take_along_axis(tab, dist[:, c:c + LANES], axis=1, mode="promise_in_bounds")
                 for c in range(0, t, LANES)], axis=1)
            if masked:
                bias = jnp.where(mask, bias, NEG_INF)
            for mp in range(2):
                sts.append(lax.dot_general(kh, qm_ref[0, 2 * h + mp], (((1,), (1,)), ((), ())),
                                           preferred_element_type=F32) + bias)
        for j in range(2 * nh):
            _softmax_step_t(sts[j], vt_ref[0, j // 2], m_ref, acc_ref, j)

    def step_far():
        sts = [lax.dot_general(k_ref[:, (j // 2) * hw:(j // 2 + 1) * hw], qm_ref[0, j], (((1,), (1,)), ((), ())),
                               preferred_element_type=F32) for j in range(2 * nh)]
        for j in range(2 * nh):
            shift = tab_ref[j // 2:j // 2 + 1, T5_TABLE - 1:T5_TABLE]
            _softmax_step_t(sts[j], vt_ref[0, j // 2], m_ref, acc_ref, j, shift)

    far = jnp.min(pq_ref[0]) - jnp.max(pk_ref[0]) >= T5_TABLE - 1

    @pl.when(jnp.logical_and(ki < qi, far))
    def _():
        step_far()

    @pl.when(jnp.logical_and(ki < qi, jnp.logical_not(far)))
    def _():
        step(False)

    @pl.when(ki == qi)
    def _():
        step(True)
        lam = (jnp.exp(jnp.sum(lq1_ref[...] * lk1_ref[...], axis=-1, keepdims=True))
               - jnp.exp(jnp.sum(lq2_ref[...] * lk2_ref[...], axis=-1, keepdims=True)) + lambda_init)
        for h in range(nh):
            o = _finish_t(acc_ref[2 * h]) - lam * _finish_t(acc_ref[2 * h + 1])
            o = _rms(o, sg_ref[...]) * (1.0 - lambda_init)
            o_ref[:, h * DIFF_V:(h + 1) * DIFF_V] = o.astype(BF16)


def _diff_attn(qm, kb, vt, pos_q, pos_k, table, lq1, lk1, lq2, lk2, sg, *, bsz, n_pos, lambda_init, t=512):
    t = min(t, n_pos)
    nb = n_pos // t
    w = DIFF_HEADS * DIFF_V
    const = lambda b, qi, ki: (0, 0)
    kmin = lambda qi, ki: jnp.minimum(ki, qi)
    return pl.pallas_call(
        functools.partial(_diff_kernel, t=t, lambda_init=lambda_init),
        grid=(bsz, nb, nb),
        in_specs=[
            pl.BlockSpec((1, 2 * DIFF_HEADS, t, 2 * DIFF_QK), lambda b, qi, ki: (b, 0, qi, 0)),
            pl.BlockSpec((t, w), lambda b, qi, ki: (b * nb + kmin(qi, ki), 0)),
            pl.BlockSpec((1, DIFF_HEADS, VT_ROWS, t), lambda b, qi, ki: (b, 0, 0, kmin(qi, ki))),
            pl.BlockSpec((1, 1, t), lambda b, qi, ki: (b * nb + qi, 0, 0)),
            pl.BlockSpec((1, t, 1), lambda b, qi, ki: (b, kmin(qi, ki), 0)),
            pl.BlockSpec((DIFF_HEADS, T5_TABLE), const),
            pl.BlockSpec((1, DIFF_QK), const), pl.BlockSpec((1, DIFF_QK), const),
            pl.BlockSpec((1, DIFF_QK), const), pl.BlockSpec((1, DIFF_QK), const),
            pl.BlockSpec((1, DIFF_V), const),
        ],
        out_specs=pl.BlockSpec((t, w), lambda b, qi, ki: (b * nb + qi, 0)),
        out_shape=jax.ShapeDtypeStruct((bsz * n_pos, w), BF16),
        scratch_shapes=[pltpu.VMEM((2 * DIFF_HEADS, 1, t), F32),
                        pltpu.VMEM((2 * DIFF_HEADS, VT_ROWS, t), F32)],
        compiler_params=_cparams(("parallel", "parallel", "arbitrary")),
        name="diff_attn",
    )(qm, kb, vt, pos_q, pos_k, table, lq1, lk1, lq2, lk2, sg)


def _ret_kernel(q_ref, qr_ref, k_ref, kr_ref, v_ref, g_ref, cos_ref, sin_ref, o_ref, st_ref, *, tm):
    t = pl.program_id(1)
    c = RET_CHUNK
    nh = RET_HEADS
    w = nh * RET_QK

    @pl.when(t == 0)
    def _():
        st_ref[...] = jnp.zeros_like(st_ref)

    log_gamma = [math.log(1.0 - 2.0 ** (-5.0 - h)) for h in range(nh)]
    lane = lax.broadcasted_iota(jnp.int32, (1, w), 1)
    lg_lane = jnp.zeros((1, w), F32)
    for h in range(nh):
        lg_lane = jnp.where(lane // RET_QK == h, log_gamma[h], lg_lane)
    tok = lax.broadcasted_iota(jnp.int32, (c, 1), 0).astype(F32)
    q_decay = jnp.exp(lg_lane * (tok + 1.0))
    k_decay = jnp.exp(lg_lane * (c - 1.0 - tok))
    ri = lax.broadcasted_iota(jnp.int32, (c, c), 0)
    ci = lax.broadcasted_iota(jnp.int32, (c, c), 1)
    rel = (ri - ci).astype(F32)
    intra = [jnp.where(rel >= 0, jnp.exp(log_gamma[h] * jnp.maximum(rel, 0.0)), 0.0) for h in range(nh)]
    head_lanes = [(lane // RET_QK == h) for h in range(nh)]

    for j in range(tm // c):
        rows = slice(j * c, (j + 1) * c)
        cos = jnp.concatenate([cos_ref[rows, :]] * (w // LANES), axis=1)
        sin = jnp.concatenate([sin_ref[rows, :]] * (w // LANES), axis=1)
        q = q_ref[rows, :] * cos + qr_ref[rows, :] * sin
        k = (k_ref[rows, :] * cos + kr_ref[rows, :] * sin) * (RET_QK ** -0.5)
        kb = k.astype(BF16)
        qd = q * q_decay
        kdt = jnp.transpose(k * k_decay).astype(BF16)
        for h in range(nh):
            vh = v_ref[rows, h * RET_V:(h + 1) * RET_V].astype(BF16)
            qh = jnp.where(head_lanes[h], q, 0.0).astype(BF16)
            scores = lax.dot_general(qh, kb, (((1,), (1,)), ((), ())), preferred_element_type=F32) * intra[h]
            inner = jnp.dot(scores.astype(BF16), vh, preferred_element_type=F32)
            qdh = jnp.where(head_lanes[h], qd, 0.0).astype(BF16)
            state = st_ref[...]
            cross = jnp.dot(qdh, state.astype(BF16), preferred_element_type=F32)
            o = inner + cross
            mu = jnp.mean(o, axis=-1, keepdims=True)
            oc = o - mu
            var = jnp.mean(oc * oc, axis=-1, keepdims=True)
            o = oc * lax.rsqrt(var + LN_EPS)
            gh = g_ref[rows, h * RET_V:(h + 1) * RET_V]
            o_ref[rows, h * RET_V:(h + 1) * RET_V] = (gh * jax.nn.sigmoid(gh) * o).astype(BF16)
            hs = slice(h * RET_QK, (h + 1) * RET_QK)
            kv = jnp.dot(kdt[hs, :], vh, preferred_element_type=F32)
            st_ref[hs, :] = state[hs, :] * math.exp(log_gamma[h] * c) + kv


def _retention(z, cos, sin, *, bsz, n_pos, tm=512):
    tm = min(tm, n_pos)
    nt = n_pos // tm
    w = RET_HEADS * RET_QK
    wv = RET_HEADS * RET_V
    zspec = lambda width, off: pl.BlockSpec((tm, width), lambda b, t: (b * nt + t, off // width))
    tspec = pl.BlockSpec((tm, LANES), lambda b, t: (b * nt + t, 0))
    return pl.pallas_call(
        functools.partial(_ret_kernel, tm=tm),
        grid=(bsz, nt),
        in_specs=[zspec(w, OFF_RQ), zspec(w, OFF_RQR), zspec(w, OFF_RK), zspec(w, OFF_RKR),
                  zspec(wv, OFF_RV), zspec(wv, OFF_RG), tspec, tspec],
        out_specs=pl.BlockSpec((tm, wv), lambda b, t: (b * nt + t, 0)),
        out_shape=jax.ShapeDtypeStruct((bsz * n_pos, wv), BF16),
        scratch_shapes=[pltpu.VMEM((w, RET_V), F32)],
        compiler_params=_cparams(("parallel", "arbitrary")),
        name="retention",
    )(z, z, z, z, z, z, cos, sin)


def _outproj_ln_kernel(y0_ref, y1_ref, y2_ref, y3_ref, w_ref, x_ref, g_ref, b_ref, o_ref):
    acc = None
    for j, y_ref in enumerate((y0_ref, y1_ref, y2_ref, y3_ref)):
        part = jnp.dot(y_ref[...], w_ref[j * 512:(j + 1) * 512, :], preferred_element_type=F32)
        acc = part if acc is None else acc + part
    o_ref[...] = _layer_norm(ALPHA * x_ref[...] + acc, g_ref[...], b_ref[...])


def _outproj_ln(ys, w, x, g, b, *, tm=512):
    m, d = x.shape
    tm = min(tm, m)
    yspec = pl.BlockSpec((tm, 512), lambda i: (i, 0))
    return pl.pallas_call(
        _outproj_ln_kernel,
        grid=(m // tm,),
        in_specs=[yspec, yspec, yspec, yspec,
                  pl.BlockSpec(w.shape, lambda i: (0, 0)),
                  pl.BlockSpec((tm, d), lambda i: (i, 0)),
                  pl.BlockSpec((1, d), lambda i: (0, 0)),
                  pl.BlockSpec((1, d), lambda i: (0, 0))],
        out_specs=pl.BlockSpec((tm, d), lambda i: (i, 0)),
        out_shape=jax.ShapeDtypeStruct((m, d), F32),
        compiler_params=_cparams(("parallel",)),
        name="outproj_ln",
    )(*ys, w, x, g, b)


def _rot_cols(w, heads, dim):
    k = w.shape[0]
    w = w.reshape(k, heads, 2, dim // 2)
    return jnp.concatenate([-w[:, :, 1], w[:, :, 0]], axis=-1).reshape(k, heads * dim)


def _pad_cols(w, width):
    return jnp.pad(w, ((0, 0), (0, width - w.shape[1])))


def _wide_w_in(w_in):
    sizes = (S5_WIDTH, MLA_Q_RANK, MLA_KV_RANK, MLA_ROPE,
             RET_HEADS * RET_QK, RET_HEADS * RET_QK, RET_HEADS * RET_V, RET_HEADS * RET_V,
             DIFF_HEADS * 2 * DIFF_QK, DIFF_HEADS * 2 * DIFF_QK, DIFF_HEADS * DIFF_V)
    offs = np.concatenate([[0], np.cumsum(sizes)])
    (s5_u, cq, ckv, kr, rq, rk, rv, rg, dq, dk, dv) = [w_in[:, offs[i]:offs[i + 1]] for i in range(len(sizes))]
    cols = [s5_u, cq, rv, rg, dq, dk, dv,
            rq, _rot_cols(rq, RET_HEADS, RET_QK), rk, _rot_cols(rk, RET_HEADS, RET_QK),
            ckv, _pad_cols(kr, LANES), _pad_cols(_rot_cols(kr, 1, MLA_ROPE), LANES)]
    return jnp.concatenate(cols, axis=1).astype(BF16)


def _mla_q_weights(w_uq):
    k = w_uq.shape[0]
    w = w_uq.reshape(k, MLA_HEADS, MLA_NOPE + MLA_ROPE)
    rope = w[:, :, MLA_NOPE:]
    pad = MLA_QK_PAD - MLA_NOPE - MLA_ROPE
    w1 = jnp.pad(w, ((0, 0), (0, 0), (0, pad))).reshape(k, MLA_HEADS * MLA_QK_PAD)
    rot = _rot_cols(rope.reshape(k, MLA_HEADS * MLA_ROPE), MLA_HEADS, MLA_ROPE).reshape(k, MLA_HEADS, MLA_ROPE)
    w2 = jnp.pad(rot, ((0, 0), (0, 0), (0, LANES - MLA_ROPE))).reshape(k, MLA_HEADS * LANES)
    return w1.astype(BF16), w2.astype(BF16)


def kernel(x, p, positions, rel_bias, ffn1_w_gate, ffn1_w_up, ffn1_w_down, ln1_g, ln1_b, w_in, w_out, ln2_g, ln2_b, s5_lambda_re, s5_lambda_im, s5_log_dt, s5_b_re, s5_b_im, s5_c_re, s5_c_im, s5_d, s5_w_glu, s5_b_glu, mla_q_norm_g, mla_w_uq, mla_kv_norm_g, mla_w_ukv, diff_lambda_q1, diff_lambda_k1, diff_lambda_q2, diff_lambda_k2, diff_subln_g, ffn2_w_gate, ffn2_w_up, ffn2_w_down, ple_w_gate, ple_b_gate, ple_w_proj, ln3_g, ln3_b):
    bsz, n_pos, d = x.shape
    m = bsz * n_pos
    depth = ffn1_w_gate.shape[0]
    xf = x.reshape(m, d)
    row = lambda v: v.reshape(1, -1)

    cos, sin = _rope_tables(positions.reshape(m, 1))
    table = _bias_table(rel_bias)
    t_diff = min(512, n_pos)
    pos_q = positions.reshape(bsz * (n_pos // t_diff), 1, t_diff)
    pos_k = positions.reshape(bsz, n_pos, 1)

    for i in range(depth):
        xf = _ffn_ln(xf, ffn1_w_gate[i].astype(BF16), ffn1_w_up[i].astype(BF16), ffn1_w_down[i].astype(BF16),
                     row(ln1_g[i]), row(ln1_b[i]))
        z = _inproj(xf, _wide_w_in(w_in[i]))

        apr, api, bbr, bbi = _s5_discretise(s5_lambda_re[i], s5_lambda_im[i], s5_log_dt[i], s5_b_re[i], s5_b_im[i])
        bd = jnp.concatenate([_block_diag_in(bbr), _block_diag_in(bbi)], axis=1).astype(BF16)
        cd = jnp.concatenate([_block_diag_out(s5_c_re[i]), -_block_diag_out(s5_c_im[i])], axis=0).astype(BF16)
        y_s5 = _s5_mixer(z, bd, cd, apr, api, row(s5_d[i]), s5_w_glu[i].astype(BF16), row(s5_b_glu[i]),
                         bsz=bsz, n_pos=n_pos)

        wq1, wq2 = _mla_q_weights(mla_w_uq[i])
        q, k, vt = _mla_prep(z, cos, sin, row(mla_q_norm_g[i]), row(mla_kv_norm_g[i]), wq1, wq2,
                             mla_w_ukv[i].astype(BF16), bsz=bsz, n_pos=n_pos)
        y_mla = _mla_flash(q, k, vt).reshape(m, MLA_HEADS * MLA_V)

        y_ret = _retention(z, cos, sin, bsz=bsz, n_pos=n_pos)

        lambda_init = 0.8 - 0.6 * math.exp(-0.3 * i)
        dqm, dkb, dvt = _diff_prep(z, bsz=bsz, n_pos=n_pos)
        y_diff = _diff_attn(dqm, dkb, dvt, pos_q, pos_k, table, row(diff_lambda_q1[i]), row(diff_lambda_k1[i]),
                            row(diff_lambda_q2[i]), row(diff_lambda_k2[i]), row(diff_subln_g[i]),
                            bsz=bsz, n_pos=n_pos, lambda_init=lambda_init, t=t_diff)

        xf = _outproj_ln((y_s5, y_mla, y_ret, y_diff), w_out[i].astype(BF16), xf, row(ln2_g[i]), row(ln2_b[i]))

        extra = _ple(xf, p[i].reshape(m, PLE_DIM), ple_w_gate[i].astype(BF16), row(ple_b_gate[i]),
                     ple_w_proj[i].astype(BF16))
        xf = _ffn_ln(xf, ffn2_w_gate[i].astype(BF16), ffn2_w_up[i].astype(BF16), ffn2_w_down[i].astype(BF16),
                     row(ln3_g[i]), row(ln3_b[i]), extra)
    return xf.reshape(bsz, n_pos, d)
```

```python
import functools
import math

import numpy as np
import jax
import jax.numpy as jnp
from jax import lax
from jax.experimental import pallas as pl
from jax.experimental.pallas import tpu as pltpu

F32 = jnp.float32
BF16 = jnp.bfloat16

D_MODEL = 2048
DEPTH = 2
PLE_DIM = 256
D_FF = 5632
ALPHA = (2 * DEPTH) ** 0.25
ROPE_THETA = 10000.0
NEG_INF = -1e30
LN_EPS = 1e-5
RMS_EPS = 1e-6

S5_WIDTH = 512
S5_GROUP = 16
S5_GROUPS = 32
S5_STATE = 64
S5_NSTATE = S5_GROUPS * S5_STATE
S5_HALF = S5_NSTATE // 2
S5_TM = 512
S5_STEPS = S5_TM // 8

MLA_HEADS = 4
MLA_Q_RANK = 512
MLA_KV_RANK = 128
MLA_NOPE = 128
MLA_ROPE = 64
MLA_V = 128
MLA_QK_PAD = 256

RET_HEADS = 4
RET_QK = 64
RET_V = 128
RET_CHUNK = 128

DIFF_HEADS = 4
DIFF_QK = 64
DIFF_V = 128

T5_BUCKETS = 32
T5_MAX_DIST = 128
T5_TABLE = 128

LOG2E = math.log2(math.e)
VT_ROWS = 144

LANES = 128
SUBLANES = 8
VMEM_LIMIT_BYTES = 56 * 1024 * 1024

OFF_S5 = 0
OFF_CQ = 512
OFF_RV = 1024
OFF_RG = 1536
OFF_DQ = 2048
OFF_DK = 2560
OFF_DV = 3072
OFF_RQ = 3584
OFF_RQR = 3840
OFF_RK = 4096
OFF_RKR = 4352
OFF_CKV = 4608
OFF_KR = 4736
OFF_KRR = 4864
IN_WIDE = 4992
IN_TN = 1664


def _cparams(sem):
    return pltpu.CompilerParams(dimension_semantics=sem, vmem_limit_bytes=VMEM_LIMIT_BYTES)


def _layer_norm(y, g, b):
    mu = jnp.mean(y, axis=-1, keepdims=True)
    yc = y - mu
    var = jnp.mean(yc * yc, axis=-1, keepdims=True)
    return yc * lax.rsqrt(var + LN_EPS) * g + b


CAST_STEPS = 16


def _cast_kernel(*refs):
    n = len(refs) // 2
    for src, dst in zip(refs[:n], refs[n:]):
        dst[...] = src[0].astype(BF16)


def _cast_layer(layer, *stacked):
    in_specs, out_specs, out_shape = [], [], []
    for w in stacked:
        _, r, c = w.shape
        tr = r // CAST_STEPS
        in_specs.append(pl.BlockSpec((1, tr, c), lambda s: (layer, s, 0)))
        out_specs.append(pl.BlockSpec((tr, c), lambda s: (s, 0)))
        out_shape.append(jax.ShapeDtypeStruct((r, c), BF16))
    return pl.pallas_call(
        _cast_kernel,
        grid=(CAST_STEPS,),
        in_specs=in_specs,
        out_specs=out_specs,
        out_shape=out_shape,
        compiler_params=_cparams(("parallel",)),
        name="cast_weights",
    )(*stacked)


def _ffn_ln_kernel(*refs, nf, has_extra):
    if has_extra:
        x_ref, wg_ref, wu_ref, wd_ref, g_ref, b_ref, e_ref, o_ref, xb_ref, acc_ref = refs
    else:
        x_ref, wg_ref, wu_ref, wd_ref, g_ref, b_ref, o_ref, xb_ref, acc_ref = refs
        e_ref = None
    f = pl.program_id(1)

    @pl.when(f == 0)
    def _():
        xb_ref[...] = x_ref[...].astype(BF16)
        acc_ref[...] = jnp.zeros_like(acc_ref)

    xb = xb_ref[...]
    gate = jnp.dot(xb, wg_ref[...], preferred_element_type=F32)
    up = jnp.dot(xb, wu_ref[...], preferred_element_type=F32)
    h = gate * jax.nn.sigmoid(gate) * up
    acc_ref[...] += jnp.dot(h.astype(BF16), wd_ref[...], preferred_element_type=F32)

    @pl.when(f == nf - 1)
    def _():
        y = ALPHA * x_ref[...] + 0.5 * acc_ref[...]
        if e_ref is not None:
            y = y + e_ref[...]
        o_ref[...] = _layer_norm(y, g_ref[...], b_ref[...])


def _ffn_ln(x, wg, wu, wd, g, b, extra=None, *, tm=512, tf=512):
    m, d = x.shape
    f_dim = wg.shape[1]
    tm = min(tm, m)
    nf = f_dim // tf
    in_specs = [
        pl.BlockSpec((tm, d), lambda i, f: (i, 0)),
        pl.BlockSpec((d, tf), lambda i, f: (0, f)),
        pl.BlockSpec((d, tf), lambda i, f: (0, f)),
        pl.BlockSpec((tf, d), lambda i, f: (f, 0)),
        pl.BlockSpec((1, d), lambda i, f: (0, 0)),
        pl.BlockSpec((1, d), lambda i, f: (0, 0)),
    ]
    args = [x, wg, wu, wd, g, b]
    if extra is not None:
        in_specs.append(pl.BlockSpec((tm, d), lambda i, f: (i, 0)))
        args.append(extra)
    return pl.pallas_call(
        functools.partial(_ffn_ln_kernel, nf=nf, has_extra=extra is not None),
        grid=(m // tm, nf),
        in_specs=in_specs,
        out_specs=pl.BlockSpec((tm, d), lambda i, f: (i, 0)),
        out_shape=jax.ShapeDtypeStruct((m, d), F32),
        scratch_shapes=[pltpu.VMEM((tm, d), BF16), pltpu.VMEM((tm, d), F32)],
        compiler_params=_cparams(("parallel", "arbitrary")),
        name="ffn_ln",
    )(*args)


def _ple_kernel(x_ref, p_ref, wg_ref, bg_ref, wp_ref, o_ref):
    xb = x_ref[...].astype(BF16)
    pb = p_ref[...].astype(BF16)
    gate = jax.nn.sigmoid(jnp.dot(xb, wg_ref[...], preferred_element_type=F32) + bg_ref[...])
    o_ref[...] = gate * jnp.dot(pb, wp_ref[...], preferred_element_type=F32)


def _ple(x, p, wg, bg, wp, *, tm=512):
    m, d = x.shape
    tm = min(tm, m)
    return pl.pallas_call(
        _ple_kernel,
        grid=(m // tm,),
        in_specs=[
            pl.BlockSpec((tm, d), lambda i: (i, 0)),
            pl.BlockSpec((tm, PLE_DIM), lambda i: (i, 0)),
            pl.BlockSpec((d, d), lambda i: (0, 0)),
            pl.BlockSpec((1, d), lambda i: (0, 0)),
            pl.BlockSpec((PLE_DIM, d), lambda i: (0, 0)),
        ],
        out_specs=pl.BlockSpec((tm, d), lambda i: (i, 0)),
        out_shape=jax.ShapeDtypeStruct((m, d), F32),
        compiler_params=_cparams(("parallel",)),
        name="ple",
    )(x, p, wg, bg, wp)


def _inproj_kernel(x_ref, w_ref, o_ref):
    o_ref[...] = jnp.dot(x_ref[...].astype(BF16), w_ref[...], preferred_element_type=F32)


def _inproj(x, w, *, tm=512):
    m, d = x.shape
    n = w.shape[1]
    tm = min(tm, m)
    return pl.pallas_call(
        _inproj_kernel,
        grid=(n // IN_TN, m // tm),
        in_specs=[
            pl.BlockSpec((tm, d), lambda j, i: (i, 0)),
            pl.BlockSpec((d, IN_TN), lambda j, i: (0, j)),
        ],
        out_specs=pl.BlockSpec((tm, IN_TN), lambda j, i: (i, j)),
        out_shape=jax.ShapeDtypeStruct((m, n), F32),
        compiler_params=_cparams(("parallel", "parallel")),
        name="inproj",
    )(x, w)


def _rope_kernel(pos_ref, freq_ref, cos_ref, sin_ref):
    ang = pos_ref[...].astype(F32) * freq_ref[...]
    cos_ref[...] = jnp.cos(ang)
    sin_ref[...] = jnp.sin(ang)


def _rope_tables(pos_col, *, tm=1024):
    m = pos_col.shape[0]
    tm = min(tm, m)
    half = MLA_ROPE // 2
    inv = 1.0 / (ROPE_THETA ** (np.arange(0, MLA_ROPE, 2, dtype=np.float64) / MLA_ROPE))
    freq = jnp.asarray(np.tile(inv, LANES // half)[None, :], F32)
    return pl.pallas_call(
        _rope_kernel,
        grid=(m // tm,),
        in_specs=[pl.BlockSpec((tm, 1), lambda i: (i, 0)), pl.BlockSpec((1, LANES), lambda i: (0, 0))],
        out_specs=[pl.BlockSpec((tm, LANES), lambda i: (i, 0))] * 2,
        out_shape=[jax.ShapeDtypeStruct((m, LANES), F32)] * 2,
        compiler_params=_cparams(("parallel",)),
        name="rope_tables",
    )(pos_col, freq)


def _t5_bucket_static():
    n = np.arange(T5_TABLE)
    max_exact = T5_BUCKETS // 2
    nf = np.maximum(n, 1).astype(np.float64)
    large = max_exact + (np.log(nf / max_exact) / math.log(T5_MAX_DIST / max_exact)
                         * (T5_BUCKETS - max_exact)).astype(np.int64)
    large = np.minimum(large, T5_BUCKETS - 1)
    return np.where(n < max_exact, n, large)


def _bias_table_kernel(rb_ref, onehot_ref, o_ref):
    rb = rb_ref[...]
    oh = onehot_ref[...]
    rows = [jnp.sum(oh * rb[:, h:h + 1], axis=0, keepdims=True) for h in range(DIFF_HEADS)]
    o_ref[...] = jnp.concatenate(rows, axis=0) * LOG2E


def _bias_table(rel_bias):
    bucket = _t5_bucket_static()
    onehot = jnp.asarray((np.arange(T5_BUCKETS)[:, None] == bucket[None, :]).astype(np.float32))
    return pl.pallas_call(
        _bias_table_kernel,
        out_shape=jax.ShapeDtypeStruct((DIFF_HEADS, T5_TABLE), F32),
        name="t5_bias_table",
    )(rel_bias, onehot)


def _s5_disc_kernel(lr_ref, li_ref, ldt_ref, br_ref, bi_ref, apr_ref, api_ref, bbr_ref, bbi_ref):
    lr = lr_ref[...]
    li = li_ref[...]
    dt = jnp.exp(ldt_ref[...])
    k = (lax.broadcasted_iota(jnp.int32, (S5_STEPS, 1), 0) + 1).astype(F32)
    mag = jnp.exp(lr * dt * k)
    apr_ref[...] = mag * jnp.cos(li * dt * k)
    api_ref[...] = mag * jnp.sin(li * dt * k)
    mag1 = jnp.exp(lr * dt)
    ar = mag1 * jnp.cos(li * dt)
    ai = mag1 * jnp.sin(li * dt)
    den = lr * lr + li * li
    fr = ((ar - 1.0) * lr + ai * li) / den
    fi = (ai * lr - (ar - 1.0) * li) / den
    br = br_ref[...]
    bi = bi_ref[...]
    bbr_ref[...] = fr * br - fi * bi
    bbi_ref[...] = fr * bi + fi * br


def _s5_discretise(lam_re, lam_im, log_dt, b_re, b_im):
    n = S5_NSTATE
    lr = lam_re.reshape(1, n)
    li = lam_im.reshape(1, n)
    ldt = jnp.repeat(log_dt, S5_STATE).reshape(1, n)
    br = jnp.transpose(b_re, (2, 0, 1)).reshape(S5_GROUP, n)
    bi = jnp.transpose(b_im, (2, 0, 1)).reshape(S5_GROUP, n)
    return pl.pallas_call(
        _s5_disc_kernel,
        out_shape=[jax.ShapeDtypeStruct((S5_STEPS, n), F32)] * 2 + [jax.ShapeDtypeStruct((S5_GROUP, n), F32)] * 2,
        name="s5_discretise",
    )(lr, li, ldt, br, bi)


def _block_diag_in(bb):
    gh = S5_GROUPS // 2
    eye = jnp.eye(gh, dtype=bb.dtype)
    v = bb.reshape(S5_GROUP, 2, gh, S5_STATE)
    out = eye[None, :, None, :, None] * jnp.transpose(v, (1, 0, 2, 3))[:, None, :, :, :]
    return out.reshape(2, gh * S5_GROUP, S5_HALF)


def _block_diag_out(c):
    gh = S5_GROUPS // 2
    eye = jnp.eye(gh, dtype=c.dtype)
    v = jnp.transpose(c, (0, 2, 1)).reshape(2, gh, S5_STATE, S5_GROUP)
    out = v[:, :, :, None, :] * eye[None, :, None, :, None]
    return out.reshape(2, S5_HALF, gh * S5_GROUP)


def _s5_kernel(u_ref, bh_ref, ch_ref, pwr_ref, pwi_ref, d_ref, wglu_ref, bglu_ref, o_ref,
               h_ref, hb_ref, cr_ref, ci_ref, cinr_ref, cini_ref):
    t = pl.program_id(1)
    nh = S5_HALF
    cw = S5_WIDTH // 2

    @pl.when(t == 0)
    def _():
        cr_ref[...] = jnp.zeros_like(cr_ref)
        ci_ref[...] = jnp.zeros_like(ci_ref)

    u = u_ref[...]
    ub = u.astype(BF16)
    ys = []
    for half in range(2):
        st = pl.ds(half * nh, nh)
        re = pl.ds(2 * half * nh, nh)
        im = pl.ds((2 * half + 1) * nh, nh)
        h_ref[:, pl.ds(2 * half * nh, 2 * nh)] = jnp.dot(ub[:, half * cw:(half + 1) * cw], bh_ref[half],
                                                          preferred_element_type=F32)
        ar = pwr_ref[0:SUBLANES, st]
        ai = pwi_ref[0:SUBLANES, st]

        def scan(j, carry, re=re, im=im, ar=ar, ai=ai):
            hr, hi = carry
            r0 = pl.multiple_of(j * SUBLANES, SUBLANES)
            nr = ar * hr - ai * hi + h_ref[pl.ds(r0, SUBLANES), re]
            ni = ar * hi + ai * hr + h_ref[pl.ds(r0, SUBLANES), im]
            h_ref[pl.ds(r0, SUBLANES), re] = nr
            h_ref[pl.ds(r0, SUBLANES), im] = ni
            return nr, ni

        zero = jnp.zeros((SUBLANES, nh), F32)
        er, ei = lax.fori_loop(0, S5_STEPS, scan, (zero, zero))

        a_seg_r = pwr_ref[S5_TM - 1:S5_TM, st]
        a_seg_i = pwi_ref[S5_TM - 1:S5_TM, st]
        cr = cr_ref[:, st]
        ci = ci_ref[:, st]
        for s in range(SUBLANES):
            cinr_ref[s:s + 1, :] = cr
            cini_ref[s:s + 1, :] = ci
            cr, ci = (er[s:s + 1] + a_seg_r * cr - a_seg_i * ci, ei[s:s + 1] + a_seg_r * ci + a_seg_i * cr)
        cr_ref[:, st] = cr
        ci_ref[:, st] = ci

        def fix(jj, _, re=re, im=im, st=st):
            r0 = pl.multiple_of(jj * 2 * SUBLANES, 2 * SUBLANES)
            out_r, out_i = [], []
            cin_r = cinr_ref[...]
            cin_i = cini_ref[...]
            for k in range(2):
                rows = pl.ds(r0 + k * SUBLANES, SUBLANES)
                pr = pwr_ref[rows, st]
                pi = pwi_ref[rows, st]
                out_r.append(h_ref[rows, re] + pr * cin_r - pi * cin_i)
                out_i.append(h_ref[rows, im] + pr * cin_i + pi * cin_r)
            hb_ref[pl.ds(r0, 2 * SUBLANES), re] = jnp.concatenate(out_r, axis=0).astype(BF16)
            hb_ref[pl.ds(r0, 2 * SUBLANES), im] = jnp.concatenate(out_i, axis=0).astype(BF16)
            return 0

        lax.fori_loop(0, S5_STEPS // 2, fix, 0)
        ys.append(jnp.dot(hb_ref[:, pl.ds(2 * half * nh, 2 * nh)], ch_ref[half], preferred_element_type=F32))

    y = jnp.concatenate(ys, axis=1) + d_ref[...] * u
    c0 = math.sqrt(2.0 / math.pi)
    y = 0.5 * y * (1.0 + jnp.tanh(c0 * (y + 0.044715 * (y * y * y))))
    gate = jax.nn.sigmoid(jnp.dot(y.astype(BF16), wglu_ref[...], preferred_element_type=F32) + bglu_ref[...])
    o_ref[...] = (y * gate).astype(BF16)


def _s5_segment_order(a, n_pos, inverse=False):
    m, w = a.shape
    inner = (S5_STEPS, SUBLANES) if inverse else (SUBLANES, S5_STEPS)
    return jnp.transpose(a.reshape(m // S5_TM, *inner, w), (0, 2, 1, 3)).reshape(m, w)


def _s5_mixer(z, bh, ch, pwr, pwi, d, wglu, bglu, *, bsz, n_pos):
    tm = S5_TM
    nt = n_pos // tm
    m = bsz * n_pos
    u = _s5_segment_order(z[:, OFF_S5:OFF_S5 + S5_WIDTH], n_pos)
    const = lambda b, t: (0, 0)
    const3 = lambda b, t: (0, 0, 0)
    y = pl.pallas_call(
        _s5_kernel,
        grid=(bsz, nt),
        in_specs=[
            pl.BlockSpec((tm, S5_WIDTH), lambda b, t: (b * nt + t, 0)),
            pl.BlockSpec(bh.shape, const3),
            pl.BlockSpec(ch.shape, const3),
            pl.BlockSpec((S5_TM, S5_NSTATE), const),
            pl.BlockSpec((S5_TM, S5_NSTATE), const),
            pl.BlockSpec((1, S5_WIDTH), const),
            pl.BlockSpec((S5_WIDTH, S5_WIDTH), const),
            pl.BlockSpec((1, S5_WIDTH), const),
        ],
        out_specs=pl.BlockSpec((tm, S5_WIDTH), lambda b, t: (b * nt + t, 0)),
        out_shape=jax.ShapeDtypeStruct((m, S5_WIDTH), BF16),
        scratch_shapes=[pltpu.VMEM((tm, 2 * S5_NSTATE), F32), pltpu.VMEM((tm, 2 * S5_NSTATE), BF16),
                        pltpu.VMEM((1, S5_NSTATE), F32), pltpu.VMEM((1, S5_NSTATE), F32),
                        pltpu.VMEM((8, S5_HALF), F32), pltpu.VMEM((8, S5_HALF), F32)],
        compiler_params=_cparams(("parallel", "arbitrary")),
        name="s5_mixer",
    )(u, bh, ch, pwr, pwi, d, wglu, bglu)
    return _s5_segment_order(y, n_pos, inverse=True)


def _rms(x, g):
    return x * lax.rsqrt(jnp.mean(x * x, axis=-1, keepdims=True) + RMS_EPS) * g


def _mla_prep_kernel(cq_ref, ckv_ref, kr_ref, krr_ref, cos_ref, sin_ref, qg_ref, kvg_ref,
                     wq1_ref, wq2_ref, wkv_ref, q_ref, k_ref, vt_ref):
    cos = cos_ref[...]
    sin = sin_ref[...]
    scale = (MLA_NOPE + MLA_ROPE) ** -0.5 * LOG2E
    ones = jnp.ones((VT_ROWS - MLA_V, cos.shape[0]), BF16)
    cqn = _rms(cq_ref[...], qg_ref[...]).astype(BF16)
    z1 = jnp.dot(cqn, wq1_ref[...], preferred_element_type=F32)
    z2 = jnp.dot(cqn, wq2_ref[...], preferred_element_type=F32)
    ckvn = _rms(ckv_ref[...], kvg_ref[...]).astype(BF16)
    zkv = jnp.dot(ckvn, wkv_ref[...], preferred_element_type=F32)
    k_rope = (kr_ref[...] * cos + krr_ref[...] * sin).astype(BF16)
    for h in range(MLA_HEADS):
        a = h * MLA_QK_PAD
        q_nope = z1[:, a:a + MLA_NOPE]
        q_rope = z1[:, a + MLA_NOPE:a + MLA_QK_PAD] * cos + z2[:, h * LANES:(h + 1) * LANES] * sin
        q_ref[0, h, :, 0:MLA_NOPE] = (q_nope * scale).astype(BF16)
        q_ref[0, h, :, MLA_NOPE:MLA_QK_PAD] = (q_rope * scale).astype(BF16)
        k_ref[0, h, :, 0:MLA_NOPE] = zkv[:, a:a + MLA_NOPE].astype(BF16)
        k_ref[0, h, :, MLA_NOPE:MLA_QK_PAD] = k_rope
        vt_ref[0, h, 0:MLA_V, :] = jnp.transpose(zkv[:, a + MLA_NOPE:a + MLA_NOPE + MLA_V]).astype(BF16)
        vt_ref[0, h, MLA_V:VT_ROWS, :] = ones


def _mla_prep(z, cos, sin, qg, kvg, wq1, wq2, wkv, *, bsz, n_pos, tm=512):
    tm = min(tm, n_pos)
    nt = n_pos // tm
    const = lambda b, t: (0, 0)
    zspec = lambda width, off: pl.BlockSpec((tm, width), lambda b, t: (b * nt + t, off // width))
    hspec = lambda width: pl.BlockSpec((1, MLA_HEADS, tm, width), lambda b, t: (b, 0, t, 0))
    return pl.pallas_call(
        _mla_prep_kernel,
        grid=(bsz, nt),
        in_specs=[
            zspec(MLA_Q_RANK, OFF_CQ), zspec(MLA_KV_RANK, OFF_CKV), zspec(LANES, OFF_KR), zspec(LANES, OFF_KRR),
            pl.BlockSpec((tm, LANES), lambda b, t: (b * nt + t, 0)),
            pl.BlockSpec((tm, LANES), lambda b, t: (b * nt + t, 0)),
            pl.BlockSpec((1, MLA_Q_RANK), const), pl.BlockSpec((1, MLA_KV_RANK), const),
            pl.BlockSpec(wq1.shape, const), pl.BlockSpec(wq2.shape, const), pl.BlockSpec(wkv.shape, const),
        ],
        out_specs=[hspec(MLA_QK_PAD), hspec(MLA_QK_PAD),
                   pl.BlockSpec((1, MLA_HEADS, VT_ROWS, tm), lambda b, t: (b, 0, 0, t))],
        out_shape=[jax.ShapeDtypeStruct((bsz, MLA_HEADS, n_pos, MLA_QK_PAD), BF16),
                   jax.ShapeDtypeStruct((bsz, MLA_HEADS, n_pos, MLA_QK_PAD), BF16),
                   jax.ShapeDtypeStruct((bsz, MLA_HEADS, VT_ROWS, n_pos), BF16)],
        compiler_params=_cparams(("parallel", "parallel")),
        name="mla_prep",
    )(z, z, z, z, cos, sin, qg, kvg, wq1, wq2, wkv)


def _softmax_step_t(st, vt, m_ref, acc_ref, idx, shift=None):
    m_old = m_ref[idx]
    cur = jnp.max(st, axis=0, keepdims=True)
    if shift is not None:
        cur = cur + shift
    m_new = jnp.maximum(m_old, cur)
    p = jnp.exp2(st - (m_new if shift is None else m_new - shift)).astype(BF16)
    corr = jnp.exp2(m_old - m_new)
    acc_ref[idx] = corr * acc_ref[idx] + jnp.dot(vt, p, preferred_element_type=F32)
    m_ref[idx] = m_new


def _causal_mask_t(t):
    k = lax.broadcasted_iota(jnp.int32, (t, t), 0)
    q = lax.broadcasted_iota(jnp.int32, (t, t), 1)
    return q >= k


def _finish_t(acc):
    return jnp.transpose(acc[0:MLA_V] / acc[MLA_V:MLA_V + 1])


def _mla_flash_kernel(q_ref, k_ref, vt_ref, o_ref, m_ref, acc_ref, *, t):
    qi = pl.program_id(1)
    ki = pl.program_id(2)

    @pl.when(ki == 0)
    def _():
        m_ref[...] = jnp.full_like(m_ref, NEG_INF)
        acc_ref[...] = jnp.zeros_like(acc_ref)

    def step(masked):
        mask = _causal_mask_t(t) if masked else None
        def scores(h):
            st = lax.dot_general(k_ref[0, h], q_ref[0, h], (((1,), (1,)), ((), ())), preferred_element_type=F32)
            return jnp.where(mask, st, NEG_INF) if masked else st

        sts = [scores(h) for h in range(MLA_HEADS)]
        for h in range(MLA_HEADS):
            _softmax_step_t(sts[h], vt_ref[0, h], m_ref, acc_ref, h)

    @pl.when(ki < qi)
    def _():
        step(False)

    @pl.when(ki == qi)
    def _():
        step(True)
        for h in range(MLA_HEADS):
            o_ref[0, :, h * MLA_V:(h + 1) * MLA_V] = _finish_t(acc_ref[h]).astype(BF16)


def _mla_flash(q, k, vt, *, t=512):
    bsz, nh, n_pos, _ = q.shape
    t = min(t, n_pos)
    nb = n_pos // t
    return pl.pallas_call(
        functools.partial(_mla_flash_kernel, t=t),
        grid=(bsz, nb, nb),
        in_specs=[
            pl.BlockSpec((1, nh, t, MLA_QK_PAD), lambda b, qi, ki: (b, 0, qi, 0)),
            pl.BlockSpec((1, nh, t, MLA_QK_PAD), lambda b, qi, ki: (b, 0, jnp.minimum(ki, qi), 0)),
            pl.BlockSpec((1, nh, VT_ROWS, t), lambda b, qi, ki: (b, 0, 0, jnp.minimum(ki, qi))),
        ],
        out_specs=pl.BlockSpec((1, t, nh * MLA_V), lambda b, qi, ki: (b, qi, 0)),
        out_shape=jax.ShapeDtypeStruct((bsz, n_pos, nh * MLA_V), BF16),
        scratch_shapes=[pltpu.VMEM((nh, 1, t), F32), pltpu.VMEM((nh, VT_ROWS, t), F32)],
        compiler_params=_cparams(("parallel", "parallel", "arbitrary")),
        name="mla_flash",
    )(q, k, vt)


def _diff_prep_kernel(q_ref, k_ref, v_ref, qm_ref, kb_ref, vt_ref):
    hw = 2 * DIFF_QK
    tm = q_ref.shape[0]
    lane = lax.broadcasted_iota(jnp.int32, (tm, hw), 1)
    scale = DIFF_QK ** -0.5 * LOG2E
    ones = jnp.ones((VT_ROWS - DIFF_V, tm), BF16)
    kb_ref[...] = k_ref[...].astype(BF16)
    for h in range(DIFF_HEADS):
        qh = q_ref[:, h * hw:(h + 1) * hw] * scale
        qm_ref[0, 2 * h] = jnp.where(lane < DIFF_QK, qh, 0.0).astype(BF16)
        qm_ref[0, 2 * h + 1] = jnp.where(lane >= DIFF_QK, qh, 0.0).astype(BF16)
        vt_ref[0, h, 0:DIFF_V, :] = jnp.transpose(v_ref[:, h * DIFF_V:(h + 1) * DIFF_V]).astype(BF16)
        vt_ref[0, h, DIFF_V:VT_ROWS, :] = ones


def _diff_prep(z, *, bsz, n_pos, tm=512):
    tm = min(tm, n_pos)
    nt = n_pos // tm
    w = DIFF_HEADS * DIFF_V
    zspec = lambda off: pl.BlockSpec((tm, w), lambda b, t: (b * nt + t, off // w))
    return pl.pallas_call(
        _diff_prep_kernel,
        grid=(bsz, nt),
        in_specs=[zspec(OFF_DQ), zspec(OFF_DK), zspec(OFF_DV)],
        out_specs=[pl.BlockSpec((1, 2 * DIFF_HEADS, tm, 2 * DIFF_QK), lambda b, t: (b, 0, t, 0)),
                   pl.BlockSpec((tm, w), lambda b, t: (b * nt + t, 0)),
                   pl.BlockSpec((1, DIFF_HEADS, VT_ROWS, tm), lambda b, t: (b, 0, 0, t))],
        out_shape=[jax.ShapeDtypeStruct((bsz, 2 * DIFF_HEADS, n_pos, 2 * DIFF_QK), BF16),
                   jax.ShapeDtypeStruct((bsz * n_pos, w), BF16),
                   jax.ShapeDtypeStruct((bsz, DIFF_HEADS, VT_ROWS, n_pos), BF16)],
        compiler_params=_cparams(("parallel", "parallel")),
        name="diff_prep",
    )(z, z, z)


def _diff_kernel(qm_ref, k_ref, vt_ref, pq_ref, pk_ref, tab_ref, lq1_ref, lk1_ref, lq2_ref, lk2_ref, sg_ref,
                 o_ref, m_ref, acc_ref, *, t, lambda_init):
    qi = pl.program_id(1)
    ki = pl.program_id(2)
    nh = DIFF_HEADS
    hw = 2 * DIFF_QK

    @pl.when(ki == 0)
    def _():
        m_ref[...] = jnp.full_like(m_ref, NEG_INF)
        acc_ref[...] = jnp.zeros_like(acc_ref)

    def step(masked):
        dist = jnp.clip(pq_ref[0] - pk_ref[0], 0, T5_TABLE - 1)
        mask = _causal_mask_t(t) if masked else None
        sts = []
        for h in range(nh):
            tab = jnp.broadcast_to(tab_ref[h:h + 1, :], (t, T5_TABLE))
            kh = k_ref[:, h * hw:(h + 1) * hw]
            bias = jnp.concatenate(
                [jnp.take_along_axis(tab, dist[:, c:c + LANES], axis=1, mode="promise_in_bounds")
                 for c in range(0, t, LANES)], axis=1)
            if masked:
                bias = jnp.where(mask, bias, NEG_INF)
            for mp in range(2):
                sts.append(lax.dot_general(kh, qm_ref[0, 2 * h + mp], (((1,), (1,)), ((), ())),
                                           preferred_element_type=F32) + bias)
        for j in range(2 * nh):
            _softmax_step_t(sts[j], vt_ref[0, j // 2], m_ref, acc_ref, j)

    def step_far():
        sts = [lax.dot_general(k_ref[:, (j // 2) * hw:(j // 2 + 1) * hw], qm_ref[0, j], (((1,), (1,)), ((), ())),
                               preferred_element_type=F32) for j in range(2 * nh)]
        for j in range(2 * nh):
            shift = tab_ref[j // 2:j // 2 + 1, T5_TABLE - 1:T5_TABLE]
            _softmax_step_t(sts[j], vt_ref[0, j // 2], m_ref, acc_ref, j, shift)

    far = jnp.min(pq_ref[0]) - jnp.max(pk_ref[0]) >= T5_TABLE - 1

    @pl.when(jnp.logical_and(ki < qi, far))
    def _():
        step_far()

    @pl.when(jnp.logical_and(ki < qi, jnp.logical_not(far)))
    def _():
        step(False)

    @pl.when(ki == qi)
    def _():
        step(True)
        lam = (jnp.exp(jnp.sum(lq1_ref[...] * lk1_ref[...], axis=-1, keepdims=True))
               - jnp.exp(jnp.sum(lq2_ref[...] * lk2_ref[...], axis=-1, keepdims=True)) + lambda_init)
        for h in range(nh):
            o = _finish_t(acc_ref[2 * h]) - lam * _finish_t(acc_ref[2 * h + 1])
            o = _rms(o, sg_ref[...]) * (1.0 - lambda_init)
            o_ref[:, h * DIFF_V:(h + 1) * DIFF_V] = o.astype(BF16)


def _diff_attn(qm, kb, vt, pos_q, pos_k, table, lq1, lk1, lq2, lk2, sg, *, bsz, n_pos, lambda_init, t=512):
    t = min(t, n_pos)
    nb = n_pos // t
    w = DIFF_HEADS * DIFF_V
    const = lambda b, qi, ki: (0, 0)
    kmin = lambda qi, ki: jnp.minimum(ki, qi)
    return pl.pallas_call(
        functools.partial(_diff_kernel, t=t, lambda_init=lambda_init),
        grid=(bsz, nb, nb),
        in_specs=[
            pl.BlockSpec((1, 2 * DIFF_HEADS, t, 2 * DIFF_QK), lambda b, qi, ki: (b, 0, qi, 0)),
            pl.BlockSpec((t, w), lambda b, qi, ki: (b * nb + kmin(qi, ki), 0)),
            pl.BlockSpec((1, DIFF_HEADS, VT_ROWS, t), lambda b, qi, ki: (b, 0, 0, kmin(qi, ki))),
            pl.BlockSpec((1, 1, t), lambda b, qi, ki: (b * nb + qi, 0, 0)),
            pl.BlockSpec((1, t, 1), lambda b, qi, ki: (b, kmin(qi, ki), 0)),
            pl.BlockSpec((DIFF_HEADS, T5_TABLE), const),
            pl.BlockSpec((1, DIFF_QK), const), pl.BlockSpec((1, DIFF_QK), const),
            pl.BlockSpec((1, DIFF_QK), const), pl.BlockSpec((1, DIFF_QK), const),
            pl.BlockSpec((1, DIFF_V), const),
        ],
        out_specs=pl.BlockSpec((t, w), lambda b, qi, ki: (b * nb + qi, 0)),
        out_shape=jax.ShapeDtypeStruct((bsz * n_pos, w), BF16),
        scratch_shapes=[pltpu.VMEM((2 * DIFF_HEADS, 1, t), F32),
                        pltpu.VMEM((2 * DIFF_HEADS, VT_ROWS, t), F32)],
        compiler_params=_cparams(("parallel", "parallel", "arbitrary")),
        name="diff_attn",
    )(qm, kb, vt, pos_q, pos_k, table, lq1, lk1, lq2, lk2, sg)


def _ret_kernel(q_ref, qr_ref, k_ref, kr_ref, v_ref, g_ref, cos_ref, sin_ref, o_ref, st_ref, *, tm):
    t = pl.program_id(1)
    c = RET_CHUNK
    nh = RET_HEADS
    w = nh * RET_QK

    @pl.when(t == 0)
    def _():
        st_ref[...] = jnp.zeros_like(st_ref)

    log_gamma = [math.log(1.0 - 2.0 ** (-5.0 - h)) for h in range(nh)]
    lane = lax.broadcasted_iota(jnp.int32, (1, w), 1)
    lg_lane = jnp.zeros((1, w), F32)
    for h in range(nh):
        lg_lane = jnp.where(lane // RET_QK == h, log_gamma[h], lg_lane)
    tok = lax.broadcasted_iota(jnp.int32, (c, 1), 0).astype(F32)
    q_decay = jnp.exp(lg_lane * (tok + 1.0))
    k_decay = jnp.exp(lg_lane * (c - 1.0 - tok))
    ri = lax.broadcasted_iota(jnp.int32, (c, c), 0)
    ci = lax.broadcasted_iota(jnp.int32, (c, c), 1)
    rel = (ri - ci).astype(F32)
    intra = [jnp.where(rel >= 0, jnp.exp(log_gamma[h] * jnp.maximum(rel, 0.0)), 0.0) for h in range(nh)]
    head_lanes = [(lane // RET_QK == h) for h in range(nh)]

    for j in range(tm // c):
        rows = slice(j * c, (j + 1) * c)
        cos = jnp.concatenate([cos_ref[rows, :]] * (w // LANES), axis=1)
        sin = jnp.concatenate([sin_ref[rows, :]] * (w // LANES), axis=1)
        q = q_ref[rows, :] * cos + qr_ref[rows, :] * sin
        k = (k_ref[rows, :] * cos + kr_ref[rows, :] * sin) * (RET_QK ** -0.5)
        kb = k.astype(BF16)
        qd = q * q_decay
        kdt = jnp.transpose(k * k_decay).astype(BF16)
        for h in range(nh):
            vh = v_ref[rows, h * RET_V:(h + 1) * RET_V].astype(BF16)
            qh = jnp.where(head_lanes[h], q, 0.0).astype(BF16)
            scores = lax.dot_general(qh, kb, (((1,), (1,)), ((), ())), preferred_element_type=F32) * intra[h]
            inner = jnp.dot(scores.astype(BF16), vh, preferred_element_type=F32)
            qdh = jnp.where(head_lanes[h], qd, 0.0).astype(BF16)
            state = st_ref[...]
            cross = jnp.dot(qdh, state.astype(BF16), preferred_element_type=F32)
            o = inner + cross
            mu = jnp.mean(o, axis=-1, keepdims=True)
            oc = o - mu
            var = jnp.mean(oc * oc, axis=-1, keepdims=True)
            o = oc * lax.rsqrt(var + LN_EPS)
            gh = g_ref[rows, h * RET_V:(h + 1) * RET_V]
            o_ref[rows, h * RET_V:(h + 1) * RET_V] = (gh * jax.nn.sigmoid(gh) * o).astype(BF16)
            hs = slice(h * RET_QK, (h + 1) * RET_QK)
            kv = jnp.dot(kdt[hs, :], vh, preferred_element_type=F32)
            st_ref[hs, :] = state[hs, :] * math.exp(log_gamma[h] * c) + kv


def _retention(z, cos, sin, *, bsz, n_pos, tm=512):
    tm = min(tm, n_pos)
    nt = n_pos // tm
    w = RET_HEADS * RET_QK
    wv = RET_HEADS * RET_V
    zspec = lambda width, off: pl.BlockSpec((tm, width), lambda b, t: (b * nt + t, off // width))
    tspec = pl.BlockSpec((tm, LANES), lambda b, t: (b * nt + t, 0))
    return pl.pallas_call(
        functools.partial(_ret_kernel, tm=tm),
        grid=(bsz, nt),
        in_specs=[zspec(w, OFF_RQ), zspec(w, OFF_RQR), zspec(w, OFF_RK), zspec(w, OFF_RKR),
                  zspec(wv, OFF_RV), zspec(wv, OFF_RG), tspec, tspec],
        out_specs=pl.BlockSpec((tm, wv), lambda b, t: (b * nt + t, 0)),
        out_shape=jax.ShapeDtypeStruct((bsz * n_pos, wv), BF16),
        scratch_shapes=[pltpu.VMEM((w, RET_V), F32)],
        compiler_params=_cparams(("parallel", "arbitrary")),
        name="retention",
    )(z, z, z, z, z, z, cos, sin)


def _outproj_ln_kernel(y0_ref, y1_ref, y2_ref, y3_ref, w_ref, x_ref, g_ref, b_ref, o_ref):
    acc = None
    for j, y_ref in enumerate((y0_ref, y1_ref, y2_ref, y3_ref)):
        part = jnp.dot(y_ref[...], w_ref[j * 512:(j + 1) * 512, :], preferred_element_type=F32)
        acc = part if acc is None else acc + part
    o_ref[...] = _layer_norm(ALPHA * x_ref[...] + acc, g_ref[...], b_ref[...])


def _outproj_ln(ys, w, x, g, b, *, tm=512):
    m, d = x.shape
    tm = min(tm, m)
    yspec = pl.BlockSpec((tm, 512), lambda i: (i, 0))
    return pl.pallas_call(
        _outproj_ln_kernel,
        grid=(m // tm,),
        in_specs=[yspec, yspec, yspec, yspec,
                  pl.BlockSpec(w.shape, lambda i: (0, 0)),
                  pl.BlockSpec((tm, d), lambda i: (i, 0)),
                  pl.BlockSpec((1, d), lambda i: (0, 0)),
                  pl.BlockSpec((1, d), lambda i: (0, 0))],
        out_specs=pl.BlockSpec((tm, d), lambda i: (i, 0)),
        out_shape=jax.ShapeDtypeStruct((m, d), F32),
        compiler_params=_cparams(("parallel",)),
        name="outproj_ln",
    )(*ys, w, x, g, b)


def _rot_cols(w, heads, dim):
    k = w.shape[0]
    w = w.reshape(k, heads, 2, dim // 2)
    return jnp.concatenate([-w[:, :, 1], w[:, :, 0]], axis=-1).reshape(k, heads * dim)


def _pad_cols(w, width):
    return jnp.pad(w, ((0, 0), (0, width - w.shape[1])))


def _wide_w_in(w_in):
    sizes = (S5_WIDTH, MLA_Q_RANK, MLA_KV_RANK, MLA_ROPE,
             RET_HEADS * RET_QK, RET_HEADS * RET_QK, RET_HEADS * RET_V, RET_HEADS * RET_V,
             DIFF_HEADS * 2 * DIFF_QK, DIFF_HEADS * 2 * DIFF_QK, DIFF_HEADS * DIFF_V)
    offs = np.concatenate([[0], np.cumsum(sizes)])
    (s5_u, cq, ckv, kr, rq, rk, rv, rg, dq, dk, dv) = [w_in[:, offs[i]:offs[i + 1]] for i in range(len(sizes))]
    cols = [s5_u, cq, rv, rg, dq, dk, dv,
            rq, _rot_cols(rq, RET_HEADS, RET_QK), rk, _rot_cols(rk, RET_HEADS, RET_QK),
            ckv, _pad_cols(kr, LANES), _pad_cols(_rot_cols(kr, 1, MLA_ROPE), LANES)]
    return jnp.concatenate(cols, axis=1).astype(BF16)


def _mla_q_weights(w_uq):
    k = w_uq.shape[0]
    w = w_uq.reshape(k, MLA_HEADS, MLA_NOPE + MLA_ROPE)
    rope = w[:, :, MLA_NOPE:]
    pad = MLA_QK_PAD - MLA_NOPE - MLA_ROPE
    w1 = jnp.pad(w, ((0, 0), (0, 0), (0, pad))).reshape(k, MLA_HEADS * MLA_QK_PAD)
    rot = _rot_cols(rope.reshape(k, MLA_HEADS * MLA_ROPE), MLA_HEADS, MLA_ROPE).reshape(k, MLA_HEADS, MLA_ROPE)
    w2 = jnp.pad(rot, ((0, 0), (0, 0), (0, LANES - MLA_ROPE))).reshape(k, MLA_HEADS * LANES)
    return w1.astype(BF16), w2.astype(BF16)


def kernel(x, p, positions, rel_bias, ffn1_w_gate, ffn1_w_up, ffn1_w_down, ln1_g, ln1_b, w_in, w_out, ln2_g, ln2_b, s5_lambda_re, s5_lambda_im, s5_log_dt, s5_b_re, s5_b_im, s5_c_re, s5_c_im, s5_d, s5_w_glu, s5_b_glu, mla_q_norm_g, mla_w_uq, mla_kv_norm_g, mla_w_ukv, diff_lambda_q1, diff_lambda_k1, diff_lambda_q2, diff_lambda_k2, diff_subln_g, ffn2_w_gate, ffn2_w_up, ffn2_w_down, ple_w_gate, ple_b_gate, ple_w_proj, ln3_g, ln3_b):
    bsz, n_pos, d = x.shape
    m = bsz * n_pos
    depth = ffn1_w_gate.shape[0]
    xf = x.reshape(m, d)
    row = lambda v: v.reshape(1, -1)

    cos, sin = _rope_tables(positions.reshape(m, 1))
    table = _bias_table(rel_bias)
    t_diff = min(512, n_pos)
    pos_q = positions.reshape(bsz * (n_pos // t_diff), 1, t_diff)
    pos_k = positions.reshape(bsz, n_pos, 1)

    for i in range(depth):
        f1g, f1u, f1d = _cast_layer(i, ffn1_w_gate, ffn1_w_up, ffn1_w_down)
        f2g, f2u, f2d, pwg, wo = _cast_layer(i, ffn2_w_gate, ffn2_w_up, ffn2_w_down, ple_w_gate, w_out)
        xf = _ffn_ln(xf, f1g, f1u, f1d, row(ln1_g[i]), row(ln1_b[i]))
        z = _inproj(xf, _wide_w_in(w_in[i]))

        apr, api, bbr, bbi = _s5_discretise(s5_lambda_re[i], s5_lambda_im[i], s5_log_dt[i], s5_b_re[i], s5_b_im[i])
        bh = jnp.concatenate([_block_diag_in(bbr), _block_diag_in(bbi)], axis=2).astype(BF16)
        ch = jnp.concatenate([_block_diag_out(s5_c_re[i]), -_block_diag_out(s5_c_im[i])], axis=1).astype(BF16)
        apr = jnp.repeat(apr, S5_TM // S5_STEPS, axis=0)
        api = jnp.repeat(api, S5_TM // S5_STEPS, axis=0)
        y_s5 = _s5_mixer(z, bh, ch, apr, api, row(s5_d[i]), s5_w_glu[i].astype(BF16), row(s5_b_glu[i]),
                         bsz=bsz, n_pos=n_pos)

        wq1, wq2 = _mla_q_weights(mla_w_uq[i])
        q, k, vt = _mla_prep(z, cos, sin, row(mla_q_norm_g[i]), row(mla_kv_norm_g[i]), wq1, wq2,
                             mla_w_ukv[i].astype(BF16), bsz=bsz, n_pos=n_pos)
        y_mla = _mla_flash(q, k, vt).reshape(m, MLA_HEADS * MLA_V)

        y_ret = _retention(z, cos, sin, bsz=bsz, n_pos=n_pos)

        lambda_init = 0.8 - 0.6 * math.exp(-0.3 * i)
        dqm, dkb, dvt = _diff_prep(z, bsz=bsz, n_pos=n_pos)
        y_diff = _diff_attn(dqm, dkb, dvt, pos_q, pos_k, table, row(diff_lambda_q1[i]), row(diff_lambda_k1[i]),
                            row(diff_lambda_q2[i]), row(diff_lambda_k2[i]), row(diff_subln_g[i]),
                            bsz=bsz, n_pos=n_pos, lambda_init=lambda_init, t=t_diff)

        xf = _outproj_ln((y_s5, y_mla, y_ret, y_diff), wo, xf, row(ln2_g[i]), row(ln2_b[i]))

        extra = _ple(xf, p[i].reshape(m, PLE_DIM), pwg, row(ple_b_gate[i]), ple_w_proj[i].astype(BF16))
        xf = _ffn_ln(xf, f2g, f2u, f2d, row(ln3_g[i]), row(ln3_b[i]), extra)
    return xf.reshape(bsz, n_pos, d)
```

```python
import functools
import math

import numpy as np
import jax
import jax.numpy as jnp
from jax import lax
from jax.experimental import pallas as pl
from jax.experimental.pallas import tpu as pltpu

F32 = jnp.float32
BF16 = jnp.bfloat16

D_MODEL = 2048
DEPTH = 2
PLE_DIM = 256
D_FF = 5632
ALPHA = (2 * DEPTH) ** 0.25
ROPE_THETA = 10000.0
NEG_INF = -1e30
LN_EPS = 1e-5
RMS_EPS = 1e-6

S5_WIDTH = 512
S5_GROUP = 16
S5_GROUPS = 32
S5_STATE = 64
S5_NSTATE = S5_GROUPS * S5_STATE
S5_HALF = S5_NSTATE // 2
S5_TM = 512
S5_STEPS = S5_TM // 8

MLA_HEADS = 4
MLA_Q_RANK = 512
MLA_KV_RANK = 128
MLA_NOPE = 128
MLA_ROPE = 64
MLA_V = 128
MLA_QK_PAD = 256

RET_HEADS = 4
RET_QK = 64
RET_V = 128
RET_CHUNK = 128

DIFF_HEADS = 4
DIFF_QK = 64
DIFF_V = 128

T5_BUCKETS = 32
T5_MAX_DIST = 128
T5_TABLE = 128

LOG2E = math.log2(math.e)
VT_ROWS = 144

LANES = 128
SUBLANES = 8
VMEM_LIMIT_BYTES = 56 * 1024 * 1024

OFF_S5 = 0
OFF_CQ = 512
OFF_RV = 1024
OFF_RG = 1536
OFF_DQ = 2048
OFF_DK = 2560
OFF_DV = 3072
OFF_RQ = 3584
OFF_RQR = 3840
OFF_RK = 4096
OFF_RKR = 4352
OFF_CKV = 4608
OFF_KR = 4736
OFF_KRR = 4864
IN_USED = 4992
IN_WIDE = 5120
IN_TN = 2560


def _cparams(sem):
    return pltpu.CompilerParams(dimension_semantics=sem, vmem_limit_bytes=VMEM_LIMIT_BYTES)


def _layer_norm(y, g, b):
    mu = jnp.mean(y, axis=-1, keepdims=True)
    yc = y - mu
    var = jnp.mean(yc * yc, axis=-1, keepdims=True)
    return yc * lax.rsqrt(var + LN_EPS) * g + b


CAST_STEPS = 16


def _cast_kernel(*refs):
    n = len(refs) // 2
    for src, dst in zip(refs[:n], refs[n:]):
        dst[...] = src[0].astype(BF16)


def _cast_layer(layer, *stacked):
    in_specs, out_specs, out_shape = [], [], []
    for w in stacked:
        _, r, c = w.shape
        tr = r // CAST_STEPS
        in_specs.append(pl.BlockSpec((1, tr, c), lambda s: (layer, s, 0)))
        out_specs.append(pl.BlockSpec((tr, c), lambda s: (s, 0)))
        out_shape.append(jax.ShapeDtypeStruct((r, c), BF16))
    return pl.pallas_call(
        _cast_kernel,
        grid=(CAST_STEPS,),
        in_specs=in_specs,
        out_specs=out_specs,
        out_shape=out_shape,
        compiler_params=_cparams(("parallel",)),
        name="cast_weights",
    )(*stacked)


def _ffn_ln_kernel(*refs, nf, has_extra):
    if has_extra:
        x_ref, wg_ref, wu_ref, wd_ref, g_ref, b_ref, e_ref, o_ref, xb_ref, acc_ref = refs
    else:
        x_ref, wg_ref, wu_ref, wd_ref, g_ref, b_ref, o_ref, xb_ref, acc_ref = refs
        e_ref = None
    f = pl.program_id(1)

    @pl.when(f == 0)
    def _():
        xb_ref[...] = x_ref[...].astype(BF16)
        acc_ref[...] = jnp.zeros_like(acc_ref)

    xb = xb_ref[...]
    gate = jnp.dot(xb, wg_ref[...], preferred_element_type=F32)
    up = jnp.dot(xb, wu_ref[...], preferred_element_type=F32)
    h = gate * jax.nn.sigmoid(gate) * up
    acc_ref[...] += jnp.dot(h.astype(BF16), wd_ref[...], preferred_element_type=F32)

    @pl.when(f == nf - 1)
    def _():
        y = ALPHA * x_ref[...] + 0.5 * acc_ref[...]
        if e_ref is not None:
            y = y + e_ref[...]
        o_ref[...] = _layer_norm(y, g_ref[...], b_ref[...])


def _ffn_ln(x, wg, wu, wd, g, b, extra=None, *, tm=512, tf=512):
    m, d = x.shape
    f_dim = wg.shape[1]
    tm = min(tm, m)
    nf = f_dim // tf
    in_specs = [
        pl.BlockSpec((tm, d), lambda i, f: (i, 0)),
        pl.BlockSpec((d, tf), lambda i, f: (0, f)),
        pl.BlockSpec((d, tf), lambda i, f: (0, f)),
        pl.BlockSpec((tf, d), lambda i, f: (f, 0)),
        pl.BlockSpec((1, d), lambda i, f: (0, 0)),
        pl.BlockSpec((1, d), lambda i, f: (0, 0)),
    ]
    args = [x, wg, wu, wd, g, b]
    if extra is not None:
        in_specs.append(pl.BlockSpec((tm, d), lambda i, f: (i, 0)))
        args.append(extra)
    return pl.pallas_call(
        functools.partial(_ffn_ln_kernel, nf=nf, has_extra=extra is not None),
        grid=(m // tm, nf),
        in_specs=in_specs,
        out_specs=pl.BlockSpec((tm, d), lambda i, f: (i, 0)),
        out_shape=jax.ShapeDtypeStruct((m, d), F32),
        scratch_shapes=[pltpu.VMEM((tm, d), BF16), pltpu.VMEM((tm, d), F32)],
        compiler_params=_cparams(("parallel", "arbitrary")),
        name="ffn_ln",
    )(*args)


def _ple_kernel(x_ref, p_ref, wg_ref, bg_ref, wp_ref, o_ref):
    xb = x_ref[...].astype(BF16)
    pb = p_ref[...].astype(BF16)
    gate = jax.nn.sigmoid(jnp.dot(xb, wg_ref[...], preferred_element_type=F32) + bg_ref[...])
    o_ref[...] = gate * jnp.dot(pb, wp_ref[...], preferred_element_type=F32)


def _ple(x, p, wg, bg, wp, *, tm=512):
    m, d = x.shape
    tm = min(tm, m)
    return pl.pallas_call(
        _ple_kernel,
        grid=(m // tm,),
        in_specs=[
            pl.BlockSpec((tm, d), lambda i: (i, 0)),
            pl.BlockSpec((tm, PLE_DIM), lambda i: (i, 0)),
            pl.BlockSpec((d, d), lambda i: (0, 0)),
            pl.BlockSpec((1, d), lambda i: (0, 0)),
            pl.BlockSpec((PLE_DIM, d), lambda i: (0, 0)),
        ],
        out_specs=pl.BlockSpec((tm, d), lambda i: (i, 0)),
        out_shape=jax.ShapeDtypeStruct((m, d), F32),
        compiler_params=_cparams(("parallel",)),
        name="ple",
    )(x, p, wg, bg, wp)


def _inproj_kernel(x_ref, w_ref, o_ref):
    o_ref[...] = jnp.dot(x_ref[...].astype(BF16), w_ref[...], preferred_element_type=F32)


def _inproj(x, w, *, tm=512):
    m, d = x.shape
    n = w.shape[1]
    tm = min(tm, m)
    return pl.pallas_call(
        _inproj_kernel,
        grid=(n // IN_TN, m // tm),
        in_specs=[
            pl.BlockSpec((tm, d), lambda j, i: (i, 0)),
            pl.BlockSpec((d, IN_TN), lambda j, i: (0, j)),
        ],
        out_specs=pl.BlockSpec((tm, IN_TN), lambda j, i: (i, j)),
        out_shape=jax.ShapeDtypeStruct((m, n), F32),
        compiler_params=_cparams(("parallel", "parallel")),
        name="inproj",
    )(x, w)


def _rope_kernel(pos_ref, freq_ref, cos_ref, sin_ref):
    ang = pos_ref[...].astype(F32) * freq_ref[...]
    cos_ref[...] = jnp.cos(ang)
    sin_ref[...] = jnp.sin(ang)


def _rope_tables(pos_col, *, tm=1024):
    m = pos_col.shape[0]
    tm = min(tm, m)
    half = MLA_ROPE // 2
    inv = 1.0 / (ROPE_THETA ** (np.arange(0, MLA_ROPE, 2, dtype=np.float64) / MLA_ROPE))
    freq = jnp.asarray(np.tile(inv, LANES // half)[None, :], F32)
    return pl.pallas_call(
        _rope_kernel,
        grid=(m // tm,),
        in_specs=[pl.BlockSpec((tm, 1), lambda i: (i, 0)), pl.BlockSpec((1, LANES), lambda i: (0, 0))],
        out_specs=[pl.BlockSpec((tm, LANES), lambda i: (i, 0))] * 2,
        out_shape=[jax.ShapeDtypeStruct((m, LANES), F32)] * 2,
        compiler_params=_cparams(("parallel",)),
        name="rope_tables",
    )(pos_col, freq)


def _t5_bucket_static():
    n = np.arange(T5_TABLE)
    max_exact = T5_BUCKETS // 2
    nf = np.maximum(n, 1).astype(np.float64)
    large = max_exact + (np.log(nf / max_exact) / math.log(T5_MAX_DIST / max_exact)
                         * (T5_BUCKETS - max_exact)).astype(np.int64)
    large = np.minimum(large, T5_BUCKETS - 1)
    return np.where(n < max_exact, n, large)


def _bias_table_kernel(rb_ref, onehot_ref, o_ref):
    rb = rb_ref[...]
    oh = onehot_ref[...]
    rows = [jnp.sum(oh * rb[:, h:h + 1], axis=0, keepdims=True) for h in range(DIFF_HEADS)]
    o_ref[...] = jnp.concatenate(rows, axis=0) * LOG2E


def _bias_table(rel_bias):
    bucket = _t5_bucket_static()
    onehot = jnp.asarray((np.arange(T5_BUCKETS)[:, None] == bucket[None, :]).astype(np.float32))
    return pl.pallas_call(
        _bias_table_kernel,
        out_shape=jax.ShapeDtypeStruct((DIFF_HEADS, T5_TABLE), F32),
        name="t5_bias_table",
    )(rel_bias, onehot)


def _s5_disc_kernel(lr_ref, li_ref, ldt_ref, br_ref, bi_ref, apr_ref, api_ref, bbr_ref, bbi_ref):
    lr = lr_ref[...]
    li = li_ref[...]
    dt = jnp.exp(ldt_ref[...])
    k = (lax.broadcasted_iota(jnp.int32, (S5_STEPS, 1), 0) + 1).astype(F32)
    mag = jnp.exp(lr * dt * k)
    apr_ref[...] = mag * jnp.cos(li * dt * k)
    api_ref[...] = mag * jnp.sin(li * dt * k)
    mag1 = jnp.exp(lr * dt)
    ar = mag1 * jnp.cos(li * dt)
    ai = mag1 * jnp.sin(li * dt)
    den = lr * lr + li * li
    fr = ((ar - 1.0) * lr + ai * li) / den
    fi = (ai * lr - (ar - 1.0) * li) / den
    br = br_ref[...]
    bi = bi_ref[...]
    bbr_ref[...] = fr * br - fi * bi
    bbi_ref[...] = fr * bi + fi * br


def _s5_discretise(lam_re, lam_im, log_dt, b_re, b_im):
    n = S5_NSTATE
    lr = lam_re.reshape(1, n)
    li = lam_im.reshape(1, n)
    ldt = jnp.repeat(log_dt, S5_STATE).reshape(1, n)
    br = jnp.transpose(b_re, (2, 0, 1)).reshape(S5_GROUP, n)
    bi = jnp.transpose(b_im, (2, 0, 1)).reshape(S5_GROUP, n)
    return pl.pallas_call(
        _s5_disc_kernel,
        out_shape=[jax.ShapeDtypeStruct((S5_STEPS, n), F32)] * 2 + [jax.ShapeDtypeStruct((S5_GROUP, n), F32)] * 2,
        name="s5_discretise",
    )(lr, li, ldt, br, bi)


def _block_diag_in(bb):
    gh = S5_GROUPS // 2
    eye = jnp.eye(gh, dtype=bb.dtype)
    v = bb.reshape(S5_GROUP, 2, gh, S5_STATE)
    out = eye[None, :, None, :, None] * jnp.transpose(v, (1, 0, 2, 3))[:, None, :, :, :]
    return out.reshape(2, gh * S5_GROUP, S5_HALF)


def _block_diag_out(c):
    gh = S5_GROUPS // 2
    eye = jnp.eye(gh, dtype=c.dtype)
    v = jnp.transpose(c, (0, 2, 1)).reshape(2, gh, S5_STATE, S5_GROUP)
    out = v[:, :, :, None, :] * eye[None, :, None, :, None]
    return out.reshape(2, S5_HALF, gh * S5_GROUP)


def _s5_kernel(u_ref, bh_ref, ch_ref, pwr_ref, pwi_ref, d_ref, wglu_ref, bglu_ref, o_ref,
               h_ref, hb_ref, cr_ref, ci_ref, cinr_ref, cini_ref):
    t = pl.program_id(1)
    nh = S5_HALF
    cw = S5_WIDTH // 2

    @pl.when(t == 0)
    def _():
        cr_ref[...] = jnp.zeros_like(cr_ref)
        ci_ref[...] = jnp.zeros_like(ci_ref)

    u = u_ref[...]
    ub = u.astype(BF16)
    ys = []
    for half in range(2):
        st = pl.ds(half * nh, nh)
        re = pl.ds(2 * half * nh, nh)
        im = pl.ds((2 * half + 1) * nh, nh)
        h_ref[:, pl.ds(2 * half * nh, 2 * nh)] = jnp.dot(ub[:, half * cw:(half + 1) * cw], bh_ref[half],
                                                          preferred_element_type=F32)
        ar = pwr_ref[0:SUBLANES, st]
        ai = pwi_ref[0:SUBLANES, st]

        def scan(j, carry, re=re, im=im, ar=ar, ai=ai):
            hr, hi = carry
            r0 = pl.multiple_of(j * SUBLANES, SUBLANES)
            nr = ar * hr - ai * hi + h_ref[pl.ds(r0, SUBLANES), re]
            ni = ar * hi + ai * hr + h_ref[pl.ds(r0, SUBLANES), im]
            h_ref[pl.ds(r0, SUBLANES), re] = nr
            h_ref[pl.ds(r0, SUBLANES), im] = ni
            return nr, ni

        zero = jnp.zeros((SUBLANES, nh), F32)
        er, ei = lax.fori_loop(0, S5_STEPS, scan, (zero, zero))

        a_seg_r = pwr_ref[S5_TM - 1:S5_TM, st]
        a_seg_i = pwi_ref[S5_TM - 1:S5_TM, st]
        cr = cr_ref[:, st]
        ci = ci_ref[:, st]
        for s in range(SUBLANES):
            cinr_ref[s:s + 1, :] = cr
            cini_ref[s:s + 1, :] = ci
            cr, ci = (er[s:s + 1] + a_seg_r * cr - a_seg_i * ci, ei[s:s + 1] + a_seg_r * ci + a_seg_i * cr)
        cr_ref[:, st] = cr
        ci_ref[:, st] = ci

        def fix(jj, _, re=re, im=im, st=st):
            r0 = pl.multiple_of(jj * 2 * SUBLANES, 2 * SUBLANES)
            out_r, out_i = [], []
            cin_r = cinr_ref[...]
            cin_i = cini_ref[...]
            for k in range(2):
                rows = pl.ds(r0 + k * SUBLANES, SUBLANES)
                pr = pwr_ref[rows, st]
                pi = pwi_ref[rows, st]
                out_r.append(h_ref[rows, re] + pr * cin_r - pi * cin_i)
                out_i.append(h_ref[rows, im] + pr * cin_i + pi * cin_r)
            hb_ref[pl.ds(r0, 2 * SUBLANES), re] = jnp.concatenate(out_r, axis=0).astype(BF16)
            hb_ref[pl.ds(r0, 2 * SUBLANES), im] = jnp.concatenate(out_i, axis=0).astype(BF16)
            return 0

        lax.fori_loop(0, S5_STEPS // 2, fix, 0)
        ys.append(jnp.dot(hb_ref[:, pl.ds(2 * half * nh, 2 * nh)], ch_ref[half], preferred_element_type=F32))

    y = jnp.concatenate(ys, axis=1) + d_ref[...] * u
    c0 = math.sqrt(2.0 / math.pi)
    y = 0.5 * y * (1.0 + jnp.tanh(c0 * (y + 0.044715 * (y * y * y))))
    gate = jax.nn.sigmoid(jnp.dot(y.astype(BF16), wglu_ref[...], preferred_element_type=F32) + bglu_ref[...])
    o_ref[...] = (y * gate).astype(BF16)


def _s5_segment_order(a, n_pos, inverse=False):
    m, w = a.shape
    inner = (S5_STEPS, SUBLANES) if inverse else (SUBLANES, S5_STEPS)
    return jnp.transpose(a.reshape(m // S5_TM, *inner, w), (0, 2, 1, 3)).reshape(m, w)


def _s5_mixer(z, bh, ch, pwr, pwi, d, wglu, bglu, *, bsz, n_pos):
    tm = S5_TM
    nt = n_pos // tm
    m = bsz * n_pos
    u = _s5_segment_order(z[:, OFF_S5:OFF_S5 + S5_WIDTH], n_pos)
    const = lambda b, t: (0, 0)
    const3 = lambda b, t: (0, 0, 0)
    y = pl.pallas_call(
        _s5_kernel,
        grid=(bsz, nt),
        in_specs=[
            pl.BlockSpec((tm, S5_WIDTH), lambda b, t: (b * nt + t, 0)),
            pl.BlockSpec(bh.shape, const3),
            pl.BlockSpec(ch.shape, const3),
            pl.BlockSpec((S5_TM, S5_NSTATE), const),
            pl.BlockSpec((S5_TM, S5_NSTATE), const),
            pl.BlockSpec((1, S5_WIDTH), const),
            pl.BlockSpec((S5_WIDTH, S5_WIDTH), const),
            pl.BlockSpec((1, S5_WIDTH), const),
        ],
        out_specs=pl.BlockSpec((tm, S5_WIDTH), lambda b, t: (b * nt + t, 0)),
        out_shape=jax.ShapeDtypeStruct((m, S5_WIDTH), BF16),
        scratch_shapes=[pltpu.VMEM((tm, 2 * S5_NSTATE), F32), pltpu.VMEM((tm, 2 * S5_NSTATE), BF16),
                        pltpu.VMEM((1, S5_NSTATE), F32), pltpu.VMEM((1, S5_NSTATE), F32),
                        pltpu.VMEM((8, S5_HALF), F32), pltpu.VMEM((8, S5_HALF), F32)],
        compiler_params=_cparams(("parallel", "arbitrary")),
        name="s5_mixer",
    )(u, bh, ch, pwr, pwi, d, wglu, bglu)
    return _s5_segment_order(y, n_pos, inverse=True)


def _rms(x, g):
    return x * lax.rsqrt(jnp.mean(x * x, axis=-1, keepdims=True) + RMS_EPS) * g


def _mla_prep_kernel(cq_ref, ckv_ref, kr_ref, krr_ref, cos_ref, sin_ref, qg_ref, kvg_ref,
                     wq1_ref, wq2_ref, wkv_ref, q_ref, k_ref, vt_ref):
    cos = cos_ref[...]
    sin = sin_ref[...]
    scale = (MLA_NOPE + MLA_ROPE) ** -0.5 * LOG2E
    ones = jnp.ones((VT_ROWS - MLA_V, cos.shape[0]), BF16)
    cqn = _rms(cq_ref[...], qg_ref[...]).astype(BF16)
    z1 = jnp.dot(cqn, wq1_ref[...], preferred_element_type=F32)
    z2 = jnp.dot(cqn, wq2_ref[...], preferred_element_type=F32)
    ckvn = _rms(ckv_ref[...], kvg_ref[...]).astype(BF16)
    zkv = jnp.dot(ckvn, wkv_ref[...], preferred_element_type=F32)
    k_rope = (kr_ref[...] * cos + krr_ref[...] * sin).astype(BF16)
    for h in range(MLA_HEADS):
        a = h * MLA_QK_PAD
        q_nope = z1[:, a:a + MLA_NOPE]
        q_rope = z1[:, a + MLA_NOPE:a + MLA_QK_PAD] * cos + z2[:, h * LANES:(h + 1) * LANES] * sin
        q_ref[0, h, :, 0:MLA_NOPE] = (q_nope * scale).astype(BF16)
        q_ref[0, h, :, MLA_NOPE:MLA_QK_PAD] = (q_rope * scale).astype(BF16)
        k_ref[0, h, :, 0:MLA_NOPE] = zkv[:, a:a + MLA_NOPE].astype(BF16)
        k_ref[0, h, :, MLA_NOPE:MLA_QK_PAD] = k_rope
        vt_ref[0, 0, h, 0:MLA_V, :] = jnp.transpose(zkv[:, a + MLA_NOPE:a + MLA_NOPE + MLA_V]).astype(BF16)
        vt_ref[0, 0, h, MLA_V:VT_ROWS, :] = ones


def _mla_prep(z, cos, sin, qg, kvg, wq1, wq2, wkv, *, bsz, n_pos, tm=512):
    tm = min(tm, n_pos)
    nt = n_pos // tm
    const = lambda b, t: (0, 0)
    zspec = lambda width, off: pl.BlockSpec((tm, width), lambda b, t: (b * nt + t, off // width))
    hspec = lambda width: pl.BlockSpec((1, MLA_HEADS, tm, width), lambda b, t: (b, 0, t, 0))
    return pl.pallas_call(
        _mla_prep_kernel,
        grid=(bsz, nt),
        in_specs=[
            zspec(MLA_Q_RANK, OFF_CQ), zspec(MLA_KV_RANK, OFF_CKV), zspec(LANES, OFF_KR), zspec(LANES, OFF_KRR),
            pl.BlockSpec((tm, LANES), lambda b, t: (b * nt + t, 0)),
            pl.BlockSpec((tm, LANES), lambda b, t: (b * nt + t, 0)),
            pl.BlockSpec((1, MLA_Q_RANK), const), pl.BlockSpec((1, MLA_KV_RANK), const),
            pl.BlockSpec(wq1.shape, const), pl.BlockSpec(wq2.shape, const), pl.BlockSpec(wkv.shape, const),
        ],
        out_specs=[hspec(MLA_QK_PAD), hspec(MLA_QK_PAD),
                   pl.BlockSpec((1, 1, MLA_HEADS, VT_ROWS, tm), lambda b, t: (b, t, 0, 0, 0))],
        out_shape=[jax.ShapeDtypeStruct((bsz, MLA_HEADS, n_pos, MLA_QK_PAD), BF16),
                   jax.ShapeDtypeStruct((bsz, MLA_HEADS, n_pos, MLA_QK_PAD), BF16),
                   jax.ShapeDtypeStruct((bsz, nt, MLA_HEADS, VT_ROWS, tm), BF16)],
        compiler_params=_cparams(("parallel", "parallel")),
        name="mla_prep",
    )(z, z, z, z, cos, sin, qg, kvg, wq1, wq2, wkv)


def _softmax_step_t(st, vt, m_ref, acc_ref, idx, shift=None):
    m_old = m_ref[idx]
    cur = jnp.max(st, axis=0, keepdims=True)
    if shift is not None:
        cur = cur + shift
    m_new = jnp.maximum(m_old, cur)
    p = jnp.exp2(st - (m_new if shift is None else m_new - shift)).astype(BF16)
    corr = jnp.exp2(m_old - m_new)
    acc_ref[idx] = corr * acc_ref[idx] + jnp.dot(vt, p, preferred_element_type=F32)
    m_ref[idx] = m_new


def _causal_mask_t(t):
    k = lax.broadcasted_iota(jnp.int32, (t, t), 0)
    q = lax.broadcasted_iota(jnp.int32, (t, t), 1)
    return q >= k


def _finish_t(acc):
    return jnp.transpose(acc[0:MLA_V] / acc[MLA_V:MLA_V + 1])


def _causal_tiles(nb):
    pairs = [(qi, ki) for qi in range(nb) for ki in range(qi + 1)]
    return (jnp.asarray([p[0] for p in pairs], jnp.int32), jnp.asarray([p[1] for p in pairs], jnp.int32))


def _mla_flash_kernel(qi_ref, ki_ref, q_ref, k_ref, vt_ref, o_ref, m_ref, acc_ref, *, t):
    qi = qi_ref[pl.program_id(1)]
    ki = ki_ref[pl.program_id(1)]

    @pl.when(ki == 0)
    def _():
        m_ref[...] = jnp.full_like(m_ref, NEG_INF)
        acc_ref[...] = jnp.zeros_like(acc_ref)

    def step(masked):
        mask = _causal_mask_t(t) if masked else None
        def scores(h):
            st = lax.dot_general(k_ref[0, h], q_ref[0, h], (((1,), (1,)), ((), ())), preferred_element_type=F32)
            return jnp.where(mask, st, NEG_INF) if masked else st

        sts = [scores(h) for h in range(MLA_HEADS)]
        for h in range(MLA_HEADS):
            _softmax_step_t(sts[h], vt_ref[0, 0, h], m_ref, acc_ref, h)

    @pl.when(ki < qi)
    def _():
        step(False)

    @pl.when(ki == qi)
    def _():
        step(True)
        for h in range(MLA_HEADS):
            o_ref[0, :, h * MLA_V:(h + 1) * MLA_V] = _finish_t(acc_ref[h]).astype(BF16)


def _mla_flash(q, k, vt, *, t=512):
    bsz, nh, n_pos, _ = q.shape
    t = min(t, n_pos)
    nb = n_pos // t
    qi_tab, ki_tab = _causal_tiles(nb)
    return pl.pallas_call(
        functools.partial(_mla_flash_kernel, t=t),
        grid_spec=pltpu.PrefetchScalarGridSpec(
            num_scalar_prefetch=2,
            grid=(bsz, qi_tab.shape[0]),
            in_specs=[
                pl.BlockSpec((1, nh, t, MLA_QK_PAD), lambda b, s, qt, kt: (b, 0, qt[s], 0)),
                pl.BlockSpec((1, nh, t, MLA_QK_PAD), lambda b, s, qt, kt: (b, 0, kt[s], 0)),
                pl.BlockSpec((1, 1, nh, VT_ROWS, t), lambda b, s, qt, kt: (b, kt[s], 0, 0, 0)),
            ],
            out_specs=pl.BlockSpec((1, t, nh * MLA_V), lambda b, s, qt, kt: (b, qt[s], 0)),
            scratch_shapes=[pltpu.VMEM((nh, 1, t), F32), pltpu.VMEM((nh, VT_ROWS, t), F32)],
        ),
        out_shape=jax.ShapeDtypeStruct((bsz, n_pos, nh * MLA_V), BF16),
        compiler_params=_cparams(("parallel", "arbitrary")),
        name="mla_flash",
    )(qi_tab, ki_tab, q, k, vt)


def _diff_prep_kernel(q_ref, k_ref, v_ref, qm_ref, kb_ref, vt_ref):
    hw = 2 * DIFF_QK
    tm = q_ref.shape[0]
    lane = lax.broadcasted_iota(jnp.int32, (tm, hw), 1)
    scale = DIFF_QK ** -0.5 * LOG2E
    ones = jnp.ones((VT_ROWS - DIFF_V, tm), BF16)
    kb_ref[...] = k_ref[...].astype(BF16)
    for h in range(DIFF_HEADS):
        qh = q_ref[:, h * hw:(h + 1) * hw] * scale
        qm_ref[0, 2 * h] = jnp.where(lane < DIFF_QK, qh, 0.0).astype(BF16)
        qm_ref[0, 2 * h + 1] = jnp.where(lane >= DIFF_QK, qh, 0.0).astype(BF16)
        vt_ref[0, 0, h, 0:DIFF_V, :] = jnp.transpose(v_ref[:, h * DIFF_V:(h + 1) * DIFF_V]).astype(BF16)
        vt_ref[0, 0, h, DIFF_V:VT_ROWS, :] = ones


def _diff_prep(z, *, bsz, n_pos, tm=512):
    tm = min(tm, n_pos)
    nt = n_pos // tm
    w = DIFF_HEADS * DIFF_V
    zspec = lambda off: pl.BlockSpec((tm, w), lambda b, t: (b * nt + t, off // w))
    return pl.pallas_call(
        _diff_prep_kernel,
        grid=(bsz, nt),
        in_specs=[zspec(OFF_DQ), zspec(OFF_DK), zspec(OFF_DV)],
        out_specs=[pl.BlockSpec((1, 2 * DIFF_HEADS, tm, 2 * DIFF_QK), lambda b, t: (b, 0, t, 0)),
                   pl.BlockSpec((tm, w), lambda b, t: (b * nt + t, 0)),
                   pl.BlockSpec((1, 1, DIFF_HEADS, VT_ROWS, tm), lambda b, t: (b, t, 0, 0, 0))],
        out_shape=[jax.ShapeDtypeStruct((bsz, 2 * DIFF_HEADS, n_pos, 2 * DIFF_QK), BF16),
                   jax.ShapeDtypeStruct((bsz * n_pos, w), BF16),
                   jax.ShapeDtypeStruct((bsz, nt, DIFF_HEADS, VT_ROWS, tm), BF16)],
        compiler_params=_cparams(("parallel", "parallel")),
        name="diff_prep",
    )(z, z, z)


def _diff_kernel(qi_ref, ki_ref, qm_ref, k_ref, vt_ref, pq_ref, pk_ref, tab_ref, lq1_ref, lk1_ref, lq2_ref, lk2_ref,
                 sg_ref, o_ref, m_ref, acc_ref, *, t, lambda_init):
    qi = qi_ref[pl.program_id(1)]
    ki = ki_ref[pl.program_id(1)]
    nh = DIFF_HEADS
    hw = 2 * DIFF_QK

    @pl.when(ki == 0)
    def _():
        m_ref[...] = jnp.full_like(m_ref, NEG_INF)
        acc_ref[...] = jnp.zeros_like(acc_ref)

    def step(masked):
        pk = pk_ref[...]
        dist = [jnp.clip(pq_ref[0, :, c:c + LANES] - pk, 0, T5_TABLE - 1) for c in range(0, t, LANES)]
        mask = _causal_mask_t(t) if masked else None
        sts = []
        for h in range(nh):
            tab = jnp.broadcast_to(tab_ref[h:h + 1, :], (t, T5_TABLE))
            kh = k_ref[:, h * hw:(h + 1) * hw]
            bias = jnp.concatenate(
                [jnp.take_along_axis(tab, d, axis=1, mode="promise_in_bounds") for d in dist], axis=1)
            if masked:
                bias = jnp.where(mask, bias, NEG_INF)
            for mp in range(2):
                sts.append(lax.dot_general(kh, qm_ref[0, 2 * h + mp], (((1,), (1,)), ((), ())),
                                           preferred_element_type=F32) + bias)
        for j in range(2 * nh):
            _softmax_step_t(sts[j], vt_ref[0, 0, j // 2], m_ref, acc_ref, j)

    def step_far():
        sts = [lax.dot_general(k_ref[:, (j // 2) * hw:(j // 2 + 1) * hw], qm_ref[0, j], (((1,), (1,)), ((), ())),
                               preferred_element_type=F32) for j in range(2 * nh)]
        for j in range(2 * nh):
            shift = tab_ref[j // 2:j // 2 + 1, T5_TABLE - 1:T5_TABLE]
            _softmax_step_t(sts[j], vt_ref[0, 0, j // 2], m_ref, acc_ref, j, shift)

    far = jnp.min(pq_ref[0]) - jnp.max(pk_ref[:, 0:1]) >= T5_TABLE - 1

    @pl.when(jnp.logical_and(ki < qi, far))
    def _():
        step_far()

    @pl.when(jnp.logical_and(ki < qi, jnp.logical_not(far)))
    def _():
        step(False)

    @pl.when(ki == qi)
    def _():
        step(True)
        lam = (jnp.exp(jnp.sum(lq1_ref[...] * lk1_ref[...], axis=-1, keepdims=True))
               - jnp.exp(jnp.sum(lq2_ref[...] * lk2_ref[...], axis=-1, keepdims=True)) + lambda_init)
        for h in range(nh):
            o = _finish_t(acc_ref[2 * h]) - lam * _finish_t(acc_ref[2 * h + 1])
            o = _rms(o, sg_ref[...]) * (1.0 - lambda_init)
            o_ref[:, h * DIFF_V:(h + 1) * DIFF_V] = o.astype(BF16)


def _diff_attn(qm, kb, vt, pos_q, pos_k, table, lq1, lk1, lq2, lk2, sg, *, bsz, n_pos, lambda_init, t=512):
    t = min(t, n_pos)
    nb = n_pos // t
    w = DIFF_HEADS * DIFF_V
    const = lambda b, s, qt, kt: (0, 0)
    qi_tab, ki_tab = _causal_tiles(nb)
    return pl.pallas_call(
        functools.partial(_diff_kernel, t=t, lambda_init=lambda_init),
        grid_spec=pltpu.PrefetchScalarGridSpec(
            num_scalar_prefetch=2,
            grid=(bsz, qi_tab.shape[0]),
            in_specs=[
                pl.BlockSpec((1, 2 * DIFF_HEADS, t, 2 * DIFF_QK), lambda b, s, qt, kt: (b, 0, qt[s], 0)),
                pl.BlockSpec((t, w), lambda b, s, qt, kt: (b * nb + kt[s], 0)),
                pl.BlockSpec((1, 1, DIFF_HEADS, VT_ROWS, t), lambda b, s, qt, kt: (b, kt[s], 0, 0, 0)),
                pl.BlockSpec((1, 1, t), lambda b, s, qt, kt: (b * nb + qt[s], 0, 0)),
                pl.BlockSpec((t, LANES), lambda b, s, qt, kt: (b * nb + kt[s], 0)),
                pl.BlockSpec((DIFF_HEADS, T5_TABLE), const),
                pl.BlockSpec((1, DIFF_QK), const), pl.BlockSpec((1, DIFF_QK), const),
                pl.BlockSpec((1, DIFF_QK), const), pl.BlockSpec((1, DIFF_QK), const),
                pl.BlockSpec((1, DIFF_V), const),
            ],
            out_specs=pl.BlockSpec((t, w), lambda b, s, qt, kt: (b * nb + qt[s], 0)),
            scratch_shapes=[pltpu.VMEM((2 * DIFF_HEADS, 1, t), F32),
                            pltpu.VMEM((2 * DIFF_HEADS, VT_ROWS, t), F32)],
        ),
        out_shape=jax.ShapeDtypeStruct((bsz * n_pos, w), BF16),
        compiler_params=_cparams(("parallel", "arbitrary")),
        name="diff_attn",
    )(qi_tab, ki_tab, qm, kb, vt, pos_q, pos_k, table, lq1, lk1, lq2, lk2, sg)


def _ret_kernel(q_ref, qr_ref, k_ref, kr_ref, v_ref, g_ref, cos_ref, sin_ref, o_ref, st_ref, *, tm):
    t = pl.program_id(1)
    c = RET_CHUNK
    nh = RET_HEADS
    w = nh * RET_QK

    @pl.when(t == 0)
    def _():
        st_ref[...] = jnp.zeros_like(st_ref)

    log_gamma = [math.log(1.0 - 2.0 ** (-5.0 - h)) for h in range(nh)]
    lane = lax.broadcasted_iota(jnp.int32, (1, w), 1)
    lg_lane = jnp.zeros((1, w), F32)
    for h in range(nh):
        lg_lane = jnp.where(lane // RET_QK == h, log_gamma[h], lg_lane)
    tok = lax.broadcasted_iota(jnp.int32, (c, 1), 0).astype(F32)
    q_decay = jnp.exp(lg_lane * (tok + 1.0))
    k_decay = jnp.exp(lg_lane * (c - 1.0 - tok))
    ri = lax.broadcasted_iota(jnp.int32, (c, c), 0)
    ci = lax.broadcasted_iota(jnp.int32, (c, c), 1)
    rel = (ri - ci).astype(F32)
    intra = [jnp.where(rel >= 0, jnp.exp(log_gamma[h] * jnp.maximum(rel, 0.0)), 0.0) for h in range(nh)]
    head_lanes = [(lane // RET_QK == h) for h in range(nh)]

    for j in range(tm // c):
        rows = slice(j * c, (j + 1) * c)
        cos = jnp.concatenate([cos_ref[rows, :]] * (w // LANES), axis=1)
        sin = jnp.concatenate([sin_ref[rows, :]] * (w // LANES), axis=1)
        q = q_ref[rows, :] * cos + qr_ref[rows, :] * sin
        k = (k_ref[rows, :] * cos + kr_ref[rows, :] * sin) * (RET_QK ** -0.5)
        kb = k.astype(BF16)
        qd = q * q_decay
        kdt = jnp.transpose(k * k_decay).astype(BF16)
        for h in range(nh):
            vh = v_ref[rows, h * RET_V:(h + 1) * RET_V].astype(BF16)
            qh = jnp.where(head_lanes[h], q, 0.0).astype(BF16)
            scores = lax.dot_general(qh, kb, (((1,), (1,)), ((), ())), preferred_element_type=F32) * intra[h]
            inner = jnp.dot(scores.astype(BF16), vh, preferred_element_type=F32)
            qdh = jnp.where(head_lanes[h], qd, 0.0).astype(BF16)
            state = st_ref[...]
            cross = jnp.dot(qdh, state.astype(BF16), preferred_element_type=F32)
            o = inner + cross
            mu = jnp.mean(o, axis=-1, keepdims=True)
            oc = o - mu
            var = jnp.mean(oc * oc, axis=-1, keepdims=True)
            o = oc * lax.rsqrt(var + LN_EPS)
            gh = g_ref[rows, h * RET_V:(h + 1) * RET_V]
            o_ref[rows, h * RET_V:(h + 1) * RET_V] = (gh * jax.nn.sigmoid(gh) * o).astype(BF16)
            hs = slice(h * RET_QK, (h + 1) * RET_QK)
            kv = jnp.dot(kdt[hs, :], vh, preferred_element_type=F32)
            st_ref[hs, :] = state[hs, :] * math.exp(log_gamma[h] * c) + kv


def _retention(z, cos, sin, *, bsz, n_pos, tm=512):
    tm = min(tm, n_pos)
    nt = n_pos // tm
    w = RET_HEADS * RET_QK
    wv = RET_HEADS * RET_V
    zspec = lambda width, off: pl.BlockSpec((tm, width), lambda b, t: (b * nt + t, off // width))
    tspec = pl.BlockSpec((tm, LANES), lambda b, t: (b * nt + t, 0))
    return pl.pallas_call(
        functools.partial(_ret_kernel, tm=tm),
        grid=(bsz, nt),
        in_specs=[zspec(w, OFF_RQ), zspec(w, OFF_RQR), zspec(w, OFF_RK), zspec(w, OFF_RKR),
                  zspec(wv, OFF_RV), zspec(wv, OFF_RG), tspec, tspec],
        out_specs=pl.BlockSpec((tm, wv), lambda b, t: (b * nt + t, 0)),
        out_shape=jax.ShapeDtypeStruct((bsz * n_pos, wv), BF16),
        scratch_shapes=[pltpu.VMEM((w, RET_V), F32)],
        compiler_params=_cparams(("parallel", "arbitrary")),
        name="retention",
    )(z, z, z, z, z, z, cos, sin)


def _outproj_ln_kernel(y0_ref, y1_ref, y2_ref, y3_ref, w_ref, x_ref, g_ref, b_ref, o_ref):
    acc = None
    for j, y_ref in enumerate((y0_ref, y1_ref, y2_ref, y3_ref)):
        part = jnp.dot(y_ref[...], w_ref[j * 512:(j + 1) * 512, :], preferred_element_type=F32)
        acc = part if acc is None else acc + part
    o_ref[...] = _layer_norm(ALPHA * x_ref[...] + acc, g_ref[...], b_ref[...])


def _outproj_ln(ys, w, x, g, b, *, tm=512):
    m, d = x.shape
    tm = min(tm, m)
    yspec = pl.BlockSpec((tm, 512), lambda i: (i, 0))
    return pl.pallas_call(
        _outproj_ln_kernel,
        grid=(m // tm,),
        in_specs=[yspec, yspec, yspec, yspec,
                  pl.BlockSpec(w.shape, lambda i: (0, 0)),
                  pl.BlockSpec((tm, d), lambda i: (i, 0)),
                  pl.BlockSpec((1, d), lambda i: (0, 0)),
                  pl.BlockSpec((1, d), lambda i: (0, 0))],
        out_specs=pl.BlockSpec((tm, d), lambda i: (i, 0)),
        out_shape=jax.ShapeDtypeStruct((m, d), F32),
        compiler_params=_cparams(("parallel",)),
        name="outproj_ln",
    )(*ys, w, x, g, b)


def _rot_cols(w, heads, dim):
    k = w.shape[0]
    w = w.reshape(k, heads, 2, dim // 2)
    return jnp.concatenate([-w[:, :, 1], w[:, :, 0]], axis=-1).reshape(k, heads * dim)


def _pad_cols(w, width):
    return jnp.pad(w, ((0, 0), (0, width - w.shape[1])))


def _wide_w_in(w_in):
    sizes = (S5_WIDTH, MLA_Q_RANK, MLA_KV_RANK, MLA_ROPE,
             RET_HEADS * RET_QK, RET_HEADS * RET_QK, RET_HEADS * RET_V, RET_HEADS * RET_V,
             DIFF_HEADS * 2 * DIFF_QK, DIFF_HEADS * 2 * DIFF_QK, DIFF_HEADS * DIFF_V)
    offs = np.concatenate([[0], np.cumsum(sizes)])
    (s5_u, cq, ckv, kr, rq, rk, rv, rg, dq, dk, dv) = [w_in[:, offs[i]:offs[i + 1]] for i in range(len(sizes))]
    cols = [s5_u, cq, rv, rg, dq, dk, dv,
            rq, _rot_cols(rq, RET_HEADS, RET_QK), rk, _rot_cols(rk, RET_HEADS, RET_QK),
            ckv, _pad_cols(kr, LANES), _pad_cols(_rot_cols(kr, 1, MLA_ROPE), LANES),
            jnp.zeros((w_in.shape[0], IN_WIDE - IN_USED), w_in.dtype)]
    return jnp.concatenate(cols, axis=1).astype(BF16)


def _mla_q_weights(w_uq):
    k = w_uq.shape[0]
    w = w_uq.reshape(k, MLA_HEADS, MLA_NOPE + MLA_ROPE)
    rope = w[:, :, MLA_NOPE:]
    pad = MLA_QK_PAD - MLA_NOPE - MLA_ROPE
    w1 = jnp.pad(w, ((0, 0), (0, 0), (0, pad))).reshape(k, MLA_HEADS * MLA_QK_PAD)
    rot = _rot_cols(rope.reshape(k, MLA_HEADS * MLA_ROPE), MLA_HEADS, MLA_ROPE).reshape(k, MLA_HEADS, MLA_ROPE)
    w2 = jnp.pad(rot, ((0, 0), (0, 0), (0, LANES - MLA_ROPE))).reshape(k, MLA_HEADS * LANES)
    return w1.astype(BF16), w2.astype(BF16)


def kernel(x, p, positions, rel_bias, ffn1_w_gate, ffn1_w_up, ffn1_w_down, ln1_g, ln1_b, w_in, w_out, ln2_g, ln2_b, s5_lambda_re, s5_lambda_im, s5_log_dt, s5_b_re, s5_b_im, s5_c_re, s5_c_im, s5_d, s5_w_glu, s5_b_glu, mla_q_norm_g, mla_w_uq, mla_kv_norm_g, mla_w_ukv, diff_lambda_q1, diff_lambda_k1, diff_lambda_q2, diff_lambda_k2, diff_subln_g, ffn2_w_gate, ffn2_w_up, ffn2_w_down, ple_w_gate, ple_b_gate, ple_w_proj, ln3_g, ln3_b):
    bsz, n_pos, d = x.shape
    m = bsz * n_pos
    depth = ffn1_w_gate.shape[0]
    xf = x.reshape(m, d)
    row = lambda v: v.reshape(1, -1)

    cos, sin = _rope_tables(positions.reshape(m, 1))
    table = _bias_table(rel_bias)
    t_diff = min(512, n_pos)
    pos_q = positions.reshape(bsz * (n_pos // t_diff), 1, t_diff)
    pos_k = jnp.broadcast_to(positions.reshape(m, 1), (m, LANES))

    for i in range(depth):
        f1g, f1u, f1d = _cast_layer(i, ffn1_w_gate, ffn1_w_up, ffn1_w_down)
        f2g, f2u, f2d, pwg, wo = _cast_layer(i, ffn2_w_gate, ffn2_w_up, ffn2_w_down, ple_w_gate, w_out)
        xf = _ffn_ln(xf, f1g, f1u, f1d, row(ln1_g[i]), row(ln1_b[i]))
        z = _inproj(xf, _wide_w_in(w_in[i]))

        apr, api, bbr, bbi = _s5_discretise(s5_lambda_re[i], s5_lambda_im[i], s5_log_dt[i], s5_b_re[i], s5_b_im[i])
        bh = jnp.concatenate([_block_diag_in(bbr), _block_diag_in(bbi)], axis=2).astype(BF16)
        ch = jnp.concatenate([_block_diag_out(s5_c_re[i]), -_block_diag_out(s5_c_im[i])], axis=1).astype(BF16)
        apr = jnp.repeat(apr, S5_TM // S5_STEPS, axis=0)
        api = jnp.repeat(api, S5_TM // S5_STEPS, axis=0)
        y_s5 = _s5_mixer(z, bh, ch, apr, api, row(s5_d[i]), s5_w_glu[i].astype(BF16), row(s5_b_glu[i]),
                         bsz=bsz, n_pos=n_pos)

        wq1, wq2 = _mla_q_weights(mla_w_uq[i])
        q, k, vt = _mla_prep(z, cos, sin, row(mla_q_norm_g[i]), row(mla_kv_norm_g[i]), wq1, wq2,
                             mla_w_ukv[i].astype(BF16), bsz=bsz, n_pos=n_pos)
        y_mla = _mla_flash(q, k, vt).reshape(m, MLA_HEADS * MLA_V)

        y_ret = _retention(z, cos, sin, bsz=bsz, n_pos=n_pos)

        lambda_init = 0.8 - 0.6 * math.exp(-0.3 * i)
        dqm, dkb, dvt = _diff_prep(z, bsz=bsz, n_pos=n_pos)
        y_diff = _diff_attn(dqm, dkb, dvt, pos_q, pos_k, table, row(diff_lambda_q1[i]), row(diff_lambda_k1[i]),
                            row(diff_lambda_q2[i]), row(diff_lambda_k2[i]), row(diff_subln_g[i]),
                            bsz=bsz, n_pos=n_pos, lambda_init=lambda_init, t=t_diff)

        xf = _outproj_ln((y_s5, y_mla, y_ret, y_diff), wo, xf, row(ln2_g[i]), row(ln2_b[i]))

        extra = _ple(xf, p[i].reshape(m, PLE_DIM), pwg, row(ple_b_gate[i]), ple_w_proj[i].astype(BF16))
        xf = _ffn_ln(xf, f2g, f2u, f2d, row(ln3_g[i]), row(ln3_b[i]), extra)
    return xf.reshape(bsz, n_pos, d)
```

```python
import functools
import math

import numpy as np
import jax
import jax.numpy as jnp
from jax import lax
from jax.experimental import pallas as pl
from jax.experimental.pallas import tpu as pltpu

F32 = jnp.float32
BF16 = jnp.bfloat16

D_MODEL = 2048
DEPTH = 2
PLE_DIM = 256
D_FF = 5632
ALPHA = (2 * DEPTH) ** 0.25
ROPE_THETA = 10000.0
NEG_INF = -1e30
LN_EPS = 1e-5
RMS_EPS = 1e-6

S5_WIDTH = 512
S5_GROUP = 16
S5_GROUPS = 32
S5_STATE = 64
S5_NSTATE = S5_GROUPS * S5_STATE
S5_HALF = S5_NSTATE // 2
S5_TM = 512
S5_STEPS = S5_TM // 8

MLA_HEADS = 4
MLA_Q_RANK = 512
MLA_KV_RANK = 128
MLA_NOPE = 128
MLA_ROPE = 64
MLA_V = 128
MLA_QK_PAD = 256

RET_HEADS = 4
RET_QK = 64
RET_V = 128
RET_CHUNK = 128

DIFF_HEADS = 4
DIFF_QK = 64
DIFF_V = 128

T5_BUCKETS = 32
T5_MAX_DIST = 128
T5_TABLE = 128

LOG2E = math.log2(math.e)
VT_ROWS = 144

LANES = 128
SUBLANES = 8
VMEM_LIMIT_BYTES = 56 * 1024 * 1024

OFF_S5 = 0
OFF_CQ = 512
OFF_RV = 1024
OFF_RG = 1536
OFF_DQ = 2048
OFF_DK = 2560
OFF_DV = 3072
OFF_RQ = 3584
OFF_RQR = 3840
OFF_RK = 4096
OFF_RKR = 4352
OFF_CKV = 4608
OFF_KR = 4736
OFF_KRR = 4864
IN_USED = 4992
IN_WIDE = 5120
IN_TN = 2560


def _cparams(sem, vmem_limit_bytes=VMEM_LIMIT_BYTES):
    return pltpu.CompilerParams(dimension_semantics=sem, vmem_limit_bytes=vmem_limit_bytes)


def _layer_norm(y, g, b):
    mu = jnp.mean(y, axis=-1, keepdims=True)
    yc = y - mu
    var = jnp.mean(yc * yc, axis=-1, keepdims=True)
    return yc * lax.rsqrt(var + LN_EPS) * g + b


CAST_STEPS = 16


def _cast_kernel(*refs):
    n = len(refs) // 2
    for src, dst in zip(refs[:n], refs[n:]):
        dst[...] = src[0].astype(BF16)


def _cast_layer(layer, *stacked):
    in_specs, out_specs, out_shape = [], [], []
    for w in stacked:
        _, r, c = w.shape
        tr = r // CAST_STEPS
        in_specs.append(pl.BlockSpec((1, tr, c), lambda s: (layer, s, 0)))
        out_specs.append(pl.BlockSpec((tr, c), lambda s: (s, 0)))
        out_shape.append(jax.ShapeDtypeStruct((r, c), BF16))
    return pl.pallas_call(
        _cast_kernel,
        grid=(CAST_STEPS,),
        in_specs=in_specs,
        out_specs=out_specs,
        out_shape=out_shape,
        compiler_params=_cparams(("parallel",)),
        name="cast_weights",
    )(*stacked)


FFN_ROW_CHUNK = 128
FFN_VMEM_LIMIT_BYTES = 60 * 1024 * 1024


def _ffn_ln_kernel(*refs, nf, tm, has_res):
    if has_res:
        xb_ref, res_ref, wg_ref, wu_ref, wd_ref, g_ref, b_ref, o_ref = refs
    else:
        res_ref, wg_ref, wu_ref, wd_ref, g_ref, b_ref, o_ref, xb_ref = refs
    f = pl.program_id(1)
    chunks = tm // FFN_ROW_CHUNK

    def rows(c):
        return pl.ds(pl.multiple_of(c * FFN_ROW_CHUNK, FFN_ROW_CHUNK), FFN_ROW_CHUNK)

    @pl.when(f == 0)
    def _():
        o_ref[...] = jnp.zeros_like(o_ref)
        if not has_res:
            def cast(c, _):
                xb_ref[rows(c), :] = res_ref[rows(c), :].astype(BF16)
                return 0
            lax.fori_loop(0, chunks, cast, 0)

    xb = xb_ref[...]
    gate = jnp.dot(xb, wg_ref[...], preferred_element_type=F32)
    up = jnp.dot(xb, wu_ref[...], preferred_element_type=F32)
    h = gate * jax.nn.sigmoid(gate) * up
    o_ref[...] += jnp.dot(h.astype(BF16), wd_ref[...], preferred_element_type=F32)

    @pl.when(f == nf - 1)
    def _():
        scale = 1.0 if has_res else ALPHA

        def norm(c, _):
            y = scale * res_ref[rows(c), :] + 0.5 * o_ref[rows(c), :]
            o_ref[rows(c), :] = _layer_norm(y, g_ref[...], b_ref[...])
            return 0
        lax.fori_loop(0, chunks, norm, 0)


def _ffn_ln(x, wg, wu, wd, g, b, res=None, *, tm=1024, tf=512):
    m, d = x.shape
    f_dim = wg.shape[1]
    tm = min(tm, m)
    nf = f_dim // tf
    xspec = pl.BlockSpec((tm, d), lambda i, f: (i, 0))
    in_specs = [xspec] * (2 if res is not None else 1) + [
        pl.BlockSpec((d, tf), lambda i, f: (0, f)),
        pl.BlockSpec((d, tf), lambda i, f: (0, f)),
        pl.BlockSpec((tf, d), lambda i, f: (f, 0)),
        pl.BlockSpec((1, d), lambda i, f: (0, 0)),
        pl.BlockSpec((1, d), lambda i, f: (0, 0)),
    ]
    args = ([x, res] if res is not None else [x]) + [wg, wu, wd, g, b]
    return pl.pallas_call(
        functools.partial(_ffn_ln_kernel, nf=nf, tm=tm, has_res=res is not None),
        grid=(m // tm, nf),
        in_specs=in_specs,
        out_specs=pl.BlockSpec((tm, d), lambda i, f: (i, 0)),
        out_shape=jax.ShapeDtypeStruct((m, d), F32),
        scratch_shapes=[] if res is not None else [pltpu.VMEM((tm, d), BF16)],
        compiler_params=_cparams(("parallel", "arbitrary"), FFN_VMEM_LIMIT_BYTES),
        name="ffn_ln",
    )(*args)


def _ple_kernel(x_ref, p_ref, wg_ref, bg_ref, wp_ref, res_ref, xb_ref):
    x = x_ref[...]
    xb = x.astype(BF16)
    xb_ref[...] = xb
    pb = p_ref[...].astype(BF16)
    gate = jax.nn.sigmoid(jnp.dot(xb, wg_ref[...], preferred_element_type=F32) + bg_ref[...])
    res_ref[...] = ALPHA * x + gate * jnp.dot(pb, wp_ref[...], preferred_element_type=F32)


def _ple(x, p, wg, bg, wp, *, tm=512):
    m, d = x.shape
    tm = min(tm, m)
    return pl.pallas_call(
        _ple_kernel,
        grid=(m // tm,),
        in_specs=[
            pl.BlockSpec((tm, d), lambda i: (i, 0)),
            pl.BlockSpec((tm, PLE_DIM), lambda i: (i, 0)),
            pl.BlockSpec((d, d), lambda i: (0, 0)),
            pl.BlockSpec((1, d), lambda i: (0, 0)),
            pl.BlockSpec((PLE_DIM, d), lambda i: (0, 0)),
        ],
        out_specs=[pl.BlockSpec((tm, d), lambda i: (i, 0))] * 2,
        out_shape=[jax.ShapeDtypeStruct((m, d), F32), jax.ShapeDtypeStruct((m, d), BF16)],
        compiler_params=_cparams(("parallel",)),
        name="ple",
    )(x, p, wg, bg, wp)


def _inproj_kernel(x_ref, w_ref, o_ref):
    o_ref[...] = jnp.dot(x_ref[...].astype(BF16), w_ref[...], preferred_element_type=F32)


def _inproj(x, w, *, tm=512):
    m, d = x.shape
    n = w.shape[1]
    tm = min(tm, m)
    return pl.pallas_call(
        _inproj_kernel,
        grid=(n // IN_TN, m // tm),
        in_specs=[
            pl.BlockSpec((tm, d), lambda j, i: (i, 0)),
            pl.BlockSpec((d, IN_TN), lambda j, i: (0, j)),
        ],
        out_specs=pl.BlockSpec((tm, IN_TN), lambda j, i: (i, j)),
        out_shape=jax.ShapeDtypeStruct((m, n), F32),
        compiler_params=_cparams(("parallel", "parallel")),
        name="inproj",
    )(x, w)


def _rope_kernel(pos_ref, freq_ref, cos_ref, sin_ref):
    ang = pos_ref[...].astype(F32) * freq_ref[...]
    cos_ref[...] = jnp.cos(ang)
    sin_ref[...] = jnp.sin(ang)


def _rope_tables(pos_col, *, tm=1024):
    m = pos_col.shape[0]
    tm = min(tm, m)
    half = MLA_ROPE // 2
    inv = 1.0 / (ROPE_THETA ** (np.arange(0, MLA_ROPE, 2, dtype=np.float64) / MLA_ROPE))
    freq = jnp.asarray(np.tile(inv, LANES // half)[None, :], F32)
    return pl.pallas_call(
        _rope_kernel,
        grid=(m // tm,),
        in_specs=[pl.BlockSpec((tm, 1), lambda i: (i, 0)), pl.BlockSpec((1, LANES), lambda i: (0, 0))],
        out_specs=[pl.BlockSpec((tm, LANES), lambda i: (i, 0))] * 2,
        out_shape=[jax.ShapeDtypeStruct((m, LANES), F32)] * 2,
        compiler_params=_cparams(("parallel",)),
        name="rope_tables",
    )(pos_col, freq)


def _t5_bucket_static():
    n = np.arange(T5_TABLE)
    max_exact = T5_BUCKETS // 2
    nf = np.maximum(n, 1).astype(np.float64)
    large = max_exact + (np.log(nf / max_exact) / math.log(T5_MAX_DIST / max_exact)
                         * (T5_BUCKETS - max_exact)).astype(np.int64)
    large = np.minimum(large, T5_BUCKETS - 1)
    return np.where(n < max_exact, n, large)


def _bias_table_kernel(rb_ref, onehot_ref, o_ref):
    rb = rb_ref[...]
    oh = onehot_ref[...]
    rows = [jnp.sum(oh * rb[:, h:h + 1], axis=0, keepdims=True) for h in range(DIFF_HEADS)]
    o_ref[...] = jnp.concatenate(rows, axis=0) * LOG2E


def _bias_table(rel_bias):
    bucket = _t5_bucket_static()
    onehot = jnp.asarray((np.arange(T5_BUCKETS)[:, None] == bucket[None, :]).astype(np.float32))
    return pl.pallas_call(
        _bias_table_kernel,
        out_shape=jax.ShapeDtypeStruct((DIFF_HEADS, T5_TABLE), F32),
        name="t5_bias_table",
    )(rel_bias, onehot)


def _s5_disc_kernel(lr_ref, li_ref, ldt_ref, br_ref, bi_ref, apr_ref, api_ref, bbr_ref, bbi_ref):
    lr = lr_ref[...]
    li = li_ref[...]
    dt = jnp.exp(ldt_ref[...])
    k = (lax.broadcasted_iota(jnp.int32, (S5_STEPS, 1), 0) + 1).astype(F32)
    mag = jnp.exp(lr * dt * k)
    apr_ref[...] = mag * jnp.cos(li * dt * k)
    api_ref[...] = mag * jnp.sin(li * dt * k)
    mag1 = jnp.exp(lr * dt)
    ar = mag1 * jnp.cos(li * dt)
    ai = mag1 * jnp.sin(li * dt)
    den = lr * lr + li * li
    fr = ((ar - 1.0) * lr + ai * li) / den
    fi = (ai * lr - (ar - 1.0) * li) / den
    br = br_ref[...]
    bi = bi_ref[...]
    bbr_ref[...] = fr * br - fi * bi
    bbi_ref[...] = fr * bi + fi * br


def _s5_discretise(lam_re, lam_im, log_dt, b_re, b_im):
    n = S5_NSTATE
    lr = lam_re.reshape(1, n)
    li = lam_im.reshape(1, n)
    ldt = jnp.repeat(log_dt, S5_STATE).reshape(1, n)
    br = jnp.transpose(b_re, (2, 0, 1)).reshape(S5_GROUP, n)
    bi = jnp.transpose(b_im, (2, 0, 1)).reshape(S5_GROUP, n)
    return pl.pallas_call(
        _s5_disc_kernel,
        out_shape=[jax.ShapeDtypeStruct((S5_STEPS, n), F32)] * 2 + [jax.ShapeDtypeStruct((S5_GROUP, n), F32)] * 2,
        name="s5_discretise",
    )(lr, li, ldt, br, bi)


def _block_diag_in(bb):
    gh = S5_GROUPS // 2
    eye = jnp.eye(gh, dtype=bb.dtype)
    v = bb.reshape(S5_GROUP, 2, gh, S5_STATE)
    out = eye[None, :, None, :, None] * jnp.transpose(v, (1, 0, 2, 3))[:, None, :, :, :]
    return out.reshape(2, gh * S5_GROUP, S5_HALF)


def _block_diag_out(c):
    gh = S5_GROUPS // 2
    eye = jnp.eye(gh, dtype=c.dtype)
    v = jnp.transpose(c, (0, 2, 1)).reshape(2, gh, S5_STATE, S5_GROUP)
    out = v[:, :, :, None, :] * eye[None, :, None, :, None]
    return out.reshape(2, S5_HALF, gh * S5_GROUP)


def _s5_kernel(u_ref, bh_ref, ch_ref, pwr_ref, pwi_ref, d_ref, wglu_ref, bglu_ref, o_ref,
               h_ref, hb_ref, cr_ref, ci_ref, cinr_ref, cini_ref):
    t = pl.program_id(1)
    nh = S5_HALF
    cw = S5_WIDTH // 2

    @pl.when(t == 0)
    def _():
        cr_ref[...] = jnp.zeros_like(cr_ref)
        ci_ref[...] = jnp.zeros_like(ci_ref)

    u = u_ref[...]
    ub = u.astype(BF16)
    ys = []
    for half in range(2):
        st = pl.ds(half * nh, nh)
        re = pl.ds(2 * half * nh, nh)
        im = pl.ds((2 * half + 1) * nh, nh)
        h_ref[:, pl.ds(2 * half * nh, 2 * nh)] = jnp.dot(ub[:, half * cw:(half + 1) * cw], bh_ref[half],
                                                          preferred_element_type=F32)
        ar = pwr_ref[0:SUBLANES, st]
        ai = pwi_ref[0:SUBLANES, st]

        def scan(j, carry, re=re, im=im, ar=ar, ai=ai):
            hr, hi = carry
            r0 = pl.multiple_of(j * SUBLANES, SUBLANES)
            nr = ar * hr - ai * hi + h_ref[pl.ds(r0, SUBLANES), re]
            ni = ar * hi + ai * hr + h_ref[pl.ds(r0, SUBLANES), im]
            h_ref[pl.ds(r0, SUBLANES), re] = nr
            h_ref[pl.ds(r0, SUBLANES), im] = ni
            return nr, ni

        zero = jnp.zeros((SUBLANES, nh), F32)
        er, ei = lax.fori_loop(0, S5_STEPS, scan, (zero, zero))

        a_seg_r = pwr_ref[S5_TM - 1:S5_TM, st]
        a_seg_i = pwi_ref[S5_TM - 1:S5_TM, st]
        cr = cr_ref[:, st]
        ci = ci_ref[:, st]
        for s in range(SUBLANES):
            cinr_ref[s:s + 1, :] = cr
            cini_ref[s:s + 1, :] = ci
            cr, ci = (er[s:s + 1] + a_seg_r * cr - a_seg_i * ci, ei[s:s + 1] + a_seg_r * ci + a_seg_i * cr)
        cr_ref[:, st] = cr
        ci_ref[:, st] = ci

        def fix(jj, _, re=re, im=im, st=st):
            r0 = pl.multiple_of(jj * 2 * SUBLANES, 2 * SUBLANES)
            out_r, out_i = [], []
            cin_r = cinr_ref[...]
            cin_i = cini_ref[...]
            for k in range(2):
                rows = pl.ds(r0 + k * SUBLANES, SUBLANES)
                pr = pwr_ref[rows, st]
                pi = pwi_ref[rows, st]
                out_r.append(h_ref[rows, re] + pr * cin_r - pi * cin_i)
                out_i.append(h_ref[rows, im] + pr * cin_i + pi * cin_r)
            hb_ref[pl.ds(r0, 2 * SUBLANES), re] = jnp.concatenate(out_r, axis=0).astype(BF16)
            hb_ref[pl.ds(r0, 2 * SUBLANES), im] = jnp.concatenate(out_i, axis=0).astype(BF16)
            return 0

        lax.fori_loop(0, S5_STEPS // 2, fix, 0)
        ys.append(jnp.dot(hb_ref[:, pl.ds(2 * half * nh, 2 * nh)], ch_ref[half], preferred_element_type=F32))

    y = jnp.concatenate(ys, axis=1) + d_ref[...] * u
    c0 = math.sqrt(2.0 / math.pi)
    y = 0.5 * y * (1.0 + jnp.tanh(c0 * (y + 0.044715 * (y * y * y))))
    gate = jax.nn.sigmoid(jnp.dot(y.astype(BF16), wglu_ref[...], preferred_element_type=F32) + bglu_ref[...])
    o_ref[...] = (y * gate).astype(BF16)


def _s5_segment_order(a, n_pos, inverse=False):
    m, w = a.shape
    inner = (S5_STEPS, SUBLANES) if inverse else (SUBLANES, S5_STEPS)
    return jnp.transpose(a.reshape(m // S5_TM, *inner, w), (0, 2, 1, 3)).reshape(m, w)


def _s5_mixer(z, bh, ch, pwr, pwi, d, wglu, bglu, *, bsz, n_pos):
    tm = S5_TM
    nt = n_pos // tm
    m = bsz * n_pos
    u = _s5_segment_order(z[:, OFF_S5:OFF_S5 + S5_WIDTH], n_pos)
    const = lambda b, t: (0, 0)
    const3 = lambda b, t: (0, 0, 0)
    y = pl.pallas_call(
        _s5_kernel,
        grid=(bsz, nt),
        in_specs=[
            pl.BlockSpec((tm, S5_WIDTH), lambda b, t: (b * nt + t, 0)),
            pl.BlockSpec(bh.shape, const3),
            pl.BlockSpec(ch.shape, const3),
            pl.BlockSpec((S5_TM, S5_NSTATE), const),
            pl.BlockSpec((S5_TM, S5_NSTATE), const),
            pl.BlockSpec((1, S5_WIDTH), const),
            pl.BlockSpec((S5_WIDTH, S5_WIDTH), const),
            pl.BlockSpec((1, S5_WIDTH), const),
        ],
        out_specs=pl.BlockSpec((tm, S5_WIDTH), lambda b, t: (b * nt + t, 0)),
        out_shape=jax.ShapeDtypeStruct((m, S5_WIDTH), BF16),
        scratch_shapes=[pltpu.VMEM((tm, 2 * S5_NSTATE), F32), pltpu.VMEM((tm, 2 * S5_NSTATE), BF16),
                        pltpu.VMEM((1, S5_NSTATE), F32), pltpu.VMEM((1, S5_NSTATE), F32),
                        pltpu.VMEM((8, S5_HALF), F32), pltpu.VMEM((8, S5_HALF), F32)],
        compiler_params=_cparams(("parallel", "arbitrary")),
        name="s5_mixer",
    )(u, bh, ch, pwr, pwi, d, wglu, bglu)
    return _s5_segment_order(y, n_pos, inverse=True)


def _rms(x, g):
    return x * lax.rsqrt(jnp.mean(x * x, axis=-1, keepdims=True) + RMS_EPS) * g


def _mla_prep_kernel(cq_ref, ckv_ref, kr_ref, krr_ref, cos_ref, sin_ref, qg_ref, kvg_ref,
                     wq1_ref, wq2_ref, wkv_ref, q_ref, k_ref, vt_ref):
    cos = cos_ref[...]
    sin = sin_ref[...]
    scale = (MLA_NOPE + MLA_ROPE) ** -0.5 * LOG2E
    ones = jnp.ones((VT_ROWS - MLA_V, cos.shape[0]), BF16)
    cqn = _rms(cq_ref[...], qg_ref[...]).astype(BF16)
    z1 = jnp.dot(cqn, wq1_ref[...], preferred_element_type=F32)
    z2 = jnp.dot(cqn, wq2_ref[...], preferred_element_type=F32)
    ckvn = _rms(ckv_ref[...], kvg_ref[...]).astype(BF16)
    zkv = jnp.dot(ckvn, wkv_ref[...], preferred_element_type=F32)
    k_rope = (kr_ref[...] * cos + krr_ref[...] * sin).astype(BF16)
    for h in range(MLA_HEADS):
        a = h * MLA_QK_PAD
        q_nope = z1[:, a:a + MLA_NOPE]
        q_rope = z1[:, a + MLA_NOPE:a + MLA_QK_PAD] * cos + z2[:, h * LANES:(h + 1) * LANES] * sin
        q_ref[0, h, :, 0:MLA_NOPE] = (q_nope * scale).astype(BF16)
        q_ref[0, h, :, MLA_NOPE:MLA_QK_PAD] = (q_rope * scale).astype(BF16)
        k_ref[0, h, :, 0:MLA_NOPE] = zkv[:, a:a + MLA_NOPE].astype(BF16)
        k_ref[0, h, :, MLA_NOPE:MLA_QK_PAD] = k_rope
        vt_ref[0, 0, h, 0:MLA_V, :] = jnp.transpose(zkv[:, a + MLA_NOPE:a + MLA_NOPE + MLA_V]).astype(BF16)
        vt_ref[0, 0, h, MLA_V:VT_ROWS, :] = ones


def _mla_prep(z, cos, sin, qg, kvg, wq1, wq2, wkv, *, bsz, n_pos, tm=512):
    tm = min(tm, n_pos)
    nt = n_pos // tm
    const = lambda b, t: (0, 0)
    zspec = lambda width, off: pl.BlockSpec((tm, width), lambda b, t: (b * nt + t, off // width))
    hspec = lambda width: pl.BlockSpec((1, MLA_HEADS, tm, width), lambda b, t: (b, 0, t, 0))
    return pl.pallas_call(
        _mla_prep_kernel,
        grid=(bsz, nt),
        in_specs=[
            zspec(MLA_Q_RANK, OFF_CQ), zspec(MLA_KV_RANK, OFF_CKV), zspec(LANES, OFF_KR), zspec(LANES, OFF_KRR),
            pl.BlockSpec((tm, LANES), lambda b, t: (b * nt + t, 0)),
            pl.BlockSpec((tm, LANES), lambda b, t: (b * nt + t, 0)),
            pl.BlockSpec((1, MLA_Q_RANK), const), pl.BlockSpec((1, MLA_KV_RANK), const),
            pl.BlockSpec(wq1.shape, const), pl.BlockSpec(wq2.shape, const), pl.BlockSpec(wkv.shape, const),
        ],
        out_specs=[hspec(MLA_QK_PAD), hspec(MLA_QK_PAD),
                   pl.BlockSpec((1, 1, MLA_HEADS, VT_ROWS, tm), lambda b, t: (b, t, 0, 0, 0))],
        out_shape=[jax.ShapeDtypeStruct((bsz, MLA_HEADS, n_pos, MLA_QK_PAD), BF16),
                   jax.ShapeDtypeStruct((bsz, MLA_HEADS, n_pos, MLA_QK_PAD), BF16),
                   jax.ShapeDtypeStruct((bsz, nt, MLA_HEADS, VT_ROWS, tm), BF16)],
        compiler_params=_cparams(("parallel", "parallel")),
        name="mla_prep",
    )(z, z, z, z, cos, sin, qg, kvg, wq1, wq2, wkv)


def _softmax_step_t(st, vt, m_ref, acc_ref, idx, shift=None):
    m_old = m_ref[idx]
    cur = jnp.max(st, axis=0, keepdims=True)
    if shift is not None:
        cur = cur + shift
    m_new = jnp.maximum(m_old, cur)
    p = jnp.exp2(st - (m_new if shift is None else m_new - shift)).astype(BF16)
    corr = jnp.exp2(m_old - m_new)
    acc_ref[idx] = corr * acc_ref[idx] + jnp.dot(vt, p, preferred_element_type=F32)
    m_ref[idx] = m_new


def _causal_mask_t(t):
    k = lax.broadcasted_iota(jnp.int32, (t, t), 0)
    q = lax.broadcasted_iota(jnp.int32, (t, t), 1)
    return q >= k


def _finish_t(acc):
    return jnp.transpose(acc[0:MLA_V] / acc[MLA_V:MLA_V + 1])


def _causal_tiles(nb):
    pairs = [(qi, ki) for qi in range(nb) for ki in range(qi + 1)]
    return (jnp.asarray([p[0] for p in pairs], jnp.int32), jnp.asarray([p[1] for p in pairs], jnp.int32))


def _mla_flash_kernel(qi_ref, ki_ref, q_ref, k_ref, vt_ref, o_ref, m_ref, acc_ref, *, t):
    qi = qi_ref[pl.program_id(1)]
    ki = ki_ref[pl.program_id(1)]

    @pl.when(ki == 0)
    def _():
        m_ref[...] = jnp.full_like(m_ref, NEG_INF)
        acc_ref[...] = jnp.zeros_like(acc_ref)

    def step(masked):
        mask = _causal_mask_t(t) if masked else None
        def scores(h):
            st = lax.dot_general(k_ref[0, h], q_ref[0, h], (((1,), (1,)), ((), ())), preferred_element_type=F32)
            return jnp.where(mask, st, NEG_INF) if masked else st

        sts = [scores(h) for h in range(MLA_HEADS)]
        for h in range(MLA_HEADS):
            _softmax_step_t(sts[h], vt_ref[0, 0, h], m_ref, acc_ref, h)

    @pl.when(ki < qi)
    def _():
        step(False)

    @pl.when(ki == qi)
    def _():
        step(True)
        for h in range(MLA_HEADS):
            o_ref[0, :, h * MLA_V:(h + 1) * MLA_V] = _finish_t(acc_ref[h]).astype(BF16)


def _mla_flash(q, k, vt, *, t=512):
    bsz, nh, n_pos, _ = q.shape
    t = min(t, n_pos)
    nb = n_pos // t
    qi_tab, ki_tab = _causal_tiles(nb)
    return pl.pallas_call(
        functools.partial(_mla_flash_kernel, t=t),
        grid_spec=pltpu.PrefetchScalarGridSpec(
            num_scalar_prefetch=2,
            grid=(bsz, qi_tab.shape[0]),
            in_specs=[
                pl.BlockSpec((1, nh, t, MLA_QK_PAD), lambda b, s, qt, kt: (b, 0, qt[s], 0)),
                pl.BlockSpec((1, nh, t, MLA_QK_PAD), lambda b, s, qt, kt: (b, 0, kt[s], 0)),
                pl.BlockSpec((1, 1, nh, VT_ROWS, t), lambda b, s, qt, kt: (b, kt[s], 0, 0, 0)),
            ],
            out_specs=pl.BlockSpec((1, t, nh * MLA_V), lambda b, s, qt, kt: (b, qt[s], 0)),
            scratch_shapes=[pltpu.VMEM((nh, 1, t), F32), pltpu.VMEM((nh, VT_ROWS, t), F32)],
        ),
        out_shape=jax.ShapeDtypeStruct((bsz, n_pos, nh * MLA_V), BF16),
        compiler_params=_cparams(("parallel", "arbitrary")),
        name="mla_flash",
    )(qi_tab, ki_tab, q, k, vt)


def _diff_prep_kernel(q_ref, k_ref, v_ref, qm_ref, kb_ref, vt_ref):
    hw = 2 * DIFF_QK
    tm = q_ref.shape[0]
    lane = lax.broadcasted_iota(jnp.int32, (tm, hw), 1)
    scale = DIFF_QK ** -0.5 * LOG2E
    ones = jnp.ones((VT_ROWS - DIFF_V, tm), BF16)
    kb_ref[...] = k_ref[...].astype(BF16)
    for h in range(DIFF_HEADS):
        qh = q_ref[:, h * hw:(h + 1) * hw] * scale
        qm_ref[0, 2 * h] = jnp.where(lane < DIFF_QK, qh, 0.0).astype(BF16)
        qm_ref[0, 2 * h + 1] = jnp.where(lane >= DIFF_QK, qh, 0.0).astype(BF16)
        vt_ref[0, 0, h, 0:DIFF_V, :] = jnp.transpose(v_ref[:, h * DIFF_V:(h + 1) * DIFF_V]).astype(BF16)
        vt_ref[0, 0, h, DIFF_V:VT_ROWS, :] = ones


def _diff_prep(z, *, bsz, n_pos, tm=512):
    tm = min(tm, n_pos)
    nt = n_pos // tm
    w = DIFF_HEADS * DIFF_V
    zspec = lambda off: pl.BlockSpec((tm, w), lambda b, t: (b * nt + t, off // w))
    return pl.pallas_call(
        _diff_prep_kernel,
        grid=(bsz, nt),
        in_specs=[zspec(OFF_DQ), zspec(OFF_DK), zspec(OFF_DV)],
        out_specs=[pl.BlockSpec((1, 2 * DIFF_HEADS, tm, 2 * DIFF_QK), lambda b, t: (b, 0, t, 0)),
                   pl.BlockSpec((tm, w), lambda b, t: (b * nt + t, 0)),
                   pl.BlockSpec((1, 1, DIFF_HEADS, VT_ROWS, tm), lambda b, t: (b, t, 0, 0, 0))],
        out_shape=[jax.ShapeDtypeStruct((bsz, 2 * DIFF_HEADS, n_pos, 2 * DIFF_QK), BF16),
                   jax.ShapeDtypeStruct((bsz * n_pos, w), BF16),
                   jax.ShapeDtypeStruct((bsz, nt, DIFF_HEADS, VT_ROWS, tm), BF16)],
        compiler_params=_cparams(("parallel", "parallel")),
        name="diff_prep",
    )(z, z, z)


def _diff_kernel(qi_ref, ki_ref, qm_ref, k_ref, vt_ref, pq_ref, pk_ref, tab_ref, lq1_ref, lk1_ref, lq2_ref, lk2_ref,
                 sg_ref, o_ref, m_ref, acc_ref, *, t, lambda_init):
    qi = qi_ref[pl.program_id(1)]
    ki = ki_ref[pl.program_id(1)]
    nh = DIFF_HEADS
    hw = 2 * DIFF_QK

    @pl.when(ki == 0)
    def _():
        m_ref[...] = jnp.full_like(m_ref, NEG_INF)
        acc_ref[...] = jnp.zeros_like(acc_ref)

    def step(masked):
        pk = pk_ref[...]
        dist = [jnp.clip(pq_ref[0, :, c:c + LANES] - pk, 0, T5_TABLE - 1) for c in range(0, t, LANES)]
        mask = _causal_mask_t(t) if masked else None
        sts = []
        for h in range(nh):
            tab = jnp.broadcast_to(tab_ref[h:h + 1, :], (t, T5_TABLE))
            kh = k_ref[:, h * hw:(h + 1) * hw]
            bias = jnp.concatenate(
                [jnp.take_along_axis(tab, d, axis=1, mode="promise_in_bounds") for d in dist], axis=1)
            if masked:
                bias = jnp.where(mask, bias, NEG_INF)
            for mp in range(2):
                sts.append(lax.dot_general(kh, qm_ref[0, 2 * h + mp], (((1,), (1,)), ((), ())),
                                           preferred_element_type=F32) + bias)
        for j in range(2 * nh):
            _softmax_step_t(sts[j], vt_ref[0, 0, j // 2], m_ref, acc_ref, j)

    def step_far():
        sts = [lax.dot_general(k_ref[:, (j // 2) * hw:(j // 2 + 1) * hw], qm_ref[0, j], (((1,), (1,)), ((), ())),
                               preferred_element_type=F32) for j in range(2 * nh)]
        for j in range(2 * nh):
            shift = tab_ref[j // 2:j // 2 + 1, T5_TABLE - 1:T5_TABLE]
            _softmax_step_t(sts[j], vt_ref[0, 0, j // 2], m_ref, acc_ref, j, shift)

    far = jnp.min(pq_ref[0]) - jnp.max(pk_ref[:, 0:1]) >= T5_TABLE - 1

    @pl.when(jnp.logical_and(ki < qi, far))
    def _():
        step_far()

    @pl.when(jnp.logical_and(ki < qi, jnp.logical_not(far)))
    def _():
        step(False)

    @pl.when(ki == qi)
    def _():
        step(True)
        lam = (jnp.exp(jnp.sum(lq1_ref[...] * lk1_ref[...], axis=-1, keepdims=True))
               - jnp.exp(jnp.sum(lq2_ref[...] * lk2_ref[...], axis=-1, keepdims=True)) + lambda_init)
        for h in range(nh):
            o = _finish_t(acc_ref[2 * h]) - lam * _finish_t(acc_ref[2 * h + 1])
            o = _rms(o, sg_ref[...]) * (1.0 - lambda_init)
            o_ref[:, h * DIFF_V:(h + 1) * DIFF_V] = o.astype(BF16)


def _diff_attn(qm, kb, vt, pos_q, pos_k, table, lq1, lk1, lq2, lk2, sg, *, bsz, n_pos, lambda_init, t=512):
    t = min(t, n_pos)
    nb = n_pos // t
    w = DIFF_HEADS * DIFF_V
    const = lambda b, s, qt, kt: (0, 0)
    qi_tab, ki_tab = _causal_tiles(nb)
    return pl.pallas_call(
        functools.partial(_diff_kernel, t=t, lambda_init=lambda_init),
        grid_spec=pltpu.PrefetchScalarGridSpec(
            num_scalar_prefetch=2,
            grid=(bsz, qi_tab.shape[0]),
            in_specs=[
                pl.BlockSpec((1, 2 * DIFF_HEADS, t, 2 * DIFF_QK), lambda b, s, qt, kt: (b, 0, qt[s], 0)),
                pl.BlockSpec((t, w), lambda b, s, qt, kt: (b * nb + kt[s], 0)),
                pl.BlockSpec((1, 1, DIFF_HEADS, VT_ROWS, t), lambda b, s, qt, kt: (b, kt[s], 0, 0, 0)),
                pl.BlockSpec((1, 1, t), lambda b, s, qt, kt: (b * nb + qt[s], 0, 0)),
                pl.BlockSpec((t, LANES), lambda b, s, qt, kt: (b * nb + kt[s], 0)),
                pl.BlockSpec((DIFF_HEADS, T5_TABLE), const),
                pl.BlockSpec((1, DIFF_QK), const), pl.BlockSpec((1, DIFF_QK), const),
                pl.BlockSpec((1, DIFF_QK), const), pl.BlockSpec((1, DIFF_QK), const),
                pl.BlockSpec((1, DIFF_V), const),
            ],
            out_specs=pl.BlockSpec((t, w), lambda b, s, qt, kt: (b * nb + qt[s], 0)),
            scratch_shapes=[pltpu.VMEM((2 * DIFF_HEADS, 1, t), F32),
                            pltpu.VMEM((2 * DIFF_HEADS, VT_ROWS, t), F32)],
        ),
        out_shape=jax.ShapeDtypeStruct((bsz * n_pos, w), BF16),
        compiler_params=_cparams(("parallel", "arbitrary")),
        name="diff_attn",
    )(qi_tab, ki_tab, qm, kb, vt, pos_q, pos_k, table, lq1, lk1, lq2, lk2, sg)


def _ret_kernel(q_ref, qr_ref, k_ref, kr_ref, v_ref, g_ref, cos_ref, sin_ref, o_ref, st_ref, *, tm):
    t = pl.program_id(1)
    c = RET_CHUNK
    nh = RET_HEADS
    w = nh * RET_QK

    @pl.when(t == 0)
    def _():
        st_ref[...] = jnp.zeros_like(st_ref)

    log_gamma = [math.log(1.0 - 2.0 ** (-5.0 - h)) for h in range(nh)]
    lane = lax.broadcasted_iota(jnp.int32, (1, w), 1)
    lg_lane = jnp.zeros((1, w), F32)
    for h in range(nh):
        lg_lane = jnp.where(lane // RET_QK == h, log_gamma[h], lg_lane)
    tok = lax.broadcasted_iota(jnp.int32, (c, 1), 0).astype(F32)
    q_decay = jnp.exp(lg_lane * (tok + 1.0))
    k_decay = jnp.exp(lg_lane * (c - 1.0 - tok))
    ri = lax.broadcasted_iota(jnp.int32, (c, c), 0)
    ci = lax.broadcasted_iota(jnp.int32, (c, c), 1)
    rel = (ri - ci).astype(F32)
    intra = [jnp.where(rel >= 0, jnp.exp(log_gamma[h] * jnp.maximum(rel, 0.0)), 0.0) for h in range(nh)]
    head_lanes = [(lane // RET_QK == h) for h in range(nh)]

    for j in range(tm // c):
        rows = slice(j * c, (j + 1) * c)
        cos = jnp.concatenate([cos_ref[rows, :]] * (w // LANES), axis=1)
        sin = jnp.concatenate([sin_ref[rows, :]] * (w // LANES), axis=1)
        q = q_ref[rows, :] * cos + qr_ref[rows, :] * sin
        k = (k_ref[rows, :] * cos + kr_ref[rows, :] * sin) * (RET_QK ** -0.5)
        kb = k.astype(BF16)
        qd = q * q_decay
        kdt = jnp.transpose(k * k_decay).astype(BF16)
        for h in range(nh):
            vh = v_ref[rows, h * RET_V:(h + 1) * RET_V].astype(BF16)
            qh = jnp.where(head_lanes[h], q, 0.0).astype(BF16)
            scores = lax.dot_general(qh, kb, (((1,), (1,)), ((), ())), preferred_element_type=F32) * intra[h]
            inner = jnp.dot(scores.astype(BF16), vh, preferred_element_type=F32)
            qdh = jnp.where(head_lanes[h], qd, 0.0).astype(BF16)
            state = st_ref[...]
            cross = jnp.dot(qdh, state.astype(BF16), preferred_element_type=F32)
            o = inner + cross
            mu = jnp.mean(o, axis=-1, keepdims=True)
            oc = o - mu
            var = jnp.mean(oc * oc, axis=-1, keepdims=True)
            o = oc * lax.rsqrt(var + LN_EPS)
            gh = g_ref[rows, h * RET_V:(h + 1) * RET_V]
            o_ref[rows, h * RET_V:(h + 1) * RET_V] = (gh * jax.nn.sigmoid(gh) * o).astype(BF16)
            hs = slice(h * RET_QK, (h + 1) * RET_QK)
            kv = jnp.dot(kdt[hs, :], vh, preferred_element_type=F32)
            st_ref[hs, :] = state[hs, :] * math.exp(log_gamma[h] * c) + kv


def _retention(z, cos, sin, *, bsz, n_pos, tm=512):
    tm = min(tm, n_pos)
    nt = n_pos // tm
    w = RET_HEADS * RET_QK
    wv = RET_HEADS * RET_V
    zspec = lambda width, off: pl.BlockSpec((tm, width), lambda b, t: (b * nt + t, off // width))
    tspec = pl.BlockSpec((tm, LANES), lambda b, t: (b * nt + t, 0))
    return pl.pallas_call(
        functools.partial(_ret_kernel, tm=tm),
        grid=(bsz, nt),
        in_specs=[zspec(w, OFF_RQ), zspec(w, OFF_RQR), zspec(w, OFF_RK), zspec(w, OFF_RKR),
                  zspec(wv, OFF_RV), zspec(wv, OFF_RG), tspec, tspec],
        out_specs=pl.BlockSpec((tm, wv), lambda b, t: (b * nt + t, 0)),
        out_shape=jax.ShapeDtypeStruct((bsz * n_pos, wv), BF16),
        scratch_shapes=[pltpu.VMEM((w, RET_V), F32)],
        compiler_params=_cparams(("parallel", "arbitrary")),
        name="retention",
    )(z, z, z, z, z, z, cos, sin)


def _outproj_ln_kernel(y0_ref, y1_ref, y2_ref, y3_ref, w_ref, x_ref, g_ref, b_ref, o_ref):
    acc = None
    for j, y_ref in enumerate((y0_ref, y1_ref, y2_ref, y3_ref)):
        part = jnp.dot(y_ref[...], w_ref[j * 512:(j + 1) * 512, :], preferred_element_type=F32)
        acc = part if acc is None else acc + part
    o_ref[...] = _layer_norm(ALPHA * x_ref[...] + acc, g_ref[...], b_ref[...])


def _outproj_ln(ys, w, x, g, b, *, tm=512):
    m, d = x.shape
    tm = min(tm, m)
    yspec = pl.BlockSpec((tm, 512), lambda i: (i, 0))
    return pl.pallas_call(
        _outproj_ln_kernel,
        grid=(m // tm,),
        in_specs=[yspec, yspec, yspec, yspec,
                  pl.BlockSpec(w.shape, lambda i: (0, 0)),
                  pl.BlockSpec((tm, d), lambda i: (i, 0)),
                  pl.BlockSpec((1, d), lambda i: (0, 0)),
                  pl.BlockSpec((1, d), lambda i: (0, 0))],
        out_specs=pl.BlockSpec((tm, d), lambda i: (i, 0)),
        out_shape=jax.ShapeDtypeStruct((m, d), F32),
        compiler_params=_cparams(("parallel",)),
        name="outproj_ln",
    )(*ys, w, x, g, b)


def _rot_cols(w, heads, dim):
    k = w.shape[0]
    w = w.reshape(k, heads, 2, dim // 2)
    return jnp.concatenate([-w[:, :, 1], w[:, :, 0]], axis=-1).reshape(k, heads * dim)


def _pad_cols(w, width):
    return jnp.pad(w, ((0, 0), (0, width - w.shape[1])))


def _wide_w_in(w_in):
    sizes = (S5_WIDTH, MLA_Q_RANK, MLA_KV_RANK, MLA_ROPE,
             RET_HEADS * RET_QK, RET_HEADS * RET_QK, RET_HEADS * RET_V, RET_HEADS * RET_V,
             DIFF_HEADS * 2 * DIFF_QK, DIFF_HEADS * 2 * DIFF_QK, DIFF_HEADS * DIFF_V)
    offs = np.concatenate([[0], np.cumsum(sizes)])
    (s5_u, cq, ckv, kr, rq, rk, rv, rg, dq, dk, dv) = [w_in[:, offs[i]:offs[i + 1]] for i in range(len(sizes))]
    cols = [s5_u, cq, rv, rg, dq, dk, dv,
            rq, _rot_cols(rq, RET_HEADS, RET_QK), rk, _rot_cols(rk, RET_HEADS, RET_QK),
            ckv, _pad_cols(kr, LANES), _pad_cols(_rot_cols(kr, 1, MLA_ROPE), LANES),
            jnp.zeros((w_in.shape[0], IN_WIDE - IN_USED), w_in.dtype)]
    return jnp.concatenate(cols, axis=1).astype(BF16)


def _mla_q_weights(w_uq):
    k = w_uq.shape[0]
    w = w_uq.reshape(k, MLA_HEADS, MLA_NOPE + MLA_ROPE)
    rope = w[:, :, MLA_NOPE:]
    pad = MLA_QK_PAD - MLA_NOPE - MLA_ROPE
    w1 = jnp.pad(w, ((0, 0), (0, 0), (0, pad))).reshape(k, MLA_HEADS * MLA_QK_PAD)
    rot = _rot_cols(rope.reshape(k, MLA_HEADS * MLA_ROPE), MLA_HEADS, MLA_ROPE).reshape(k, MLA_HEADS, MLA_ROPE)
    w2 = jnp.pad(rot, ((0, 0), (0, 0), (0, LANES - MLA_ROPE))).reshape(k, MLA_HEADS * LANES)
    return w1.astype(BF16), w2.astype(BF16)


def kernel(x, p, positions, rel_bias, ffn1_w_gate, ffn1_w_up, ffn1_w_down, ln1_g, ln1_b, w_in, w_out, ln2_g, ln2_b, s5_lambda_re, s5_lambda_im, s5_log_dt, s5_b_re, s5_b_im, s5_c_re, s5_c_im, s5_d, s5_w_glu, s5_b_glu, mla_q_norm_g, mla_w_uq, mla_kv_norm_g, mla_w_ukv, diff_lambda_q1, diff_lambda_k1, diff_lambda_q2, diff_lambda_k2, diff_subln_g, ffn2_w_gate, ffn2_w_up, ffn2_w_down, ple_w_gate, ple_b_gate, ple_w_proj, ln3_g, ln3_b):
    bsz, n_pos, d = x.shape
    m = bsz * n_pos
    depth = ffn1_w_gate.shape[0]
    xf = x.reshape(m, d)
    row = lambda v: v.reshape(1, -1)

    cos, sin = _rope_tables(positions.reshape(m, 1))
    table = _bias_table(rel_bias)
    t_diff = min(512, n_pos)
    pos_q = positions.reshape(bsz * (n_pos // t_diff), 1, t_diff)
    pos_k = jnp.broadcast_to(positions.reshape(m, 1), (m, LANES))

    for i in range(depth):
        f1g, f1u, f1d = _cast_layer(i, ffn1_w_gate, ffn1_w_up, ffn1_w_down)
        f2g, f2u, f2d, pwg, wo = _cast_layer(i, ffn2_w_gate, ffn2_w_up, ffn2_w_down, ple_w_gate, w_out)
        xf = _ffn_ln(xf, f1g, f1u, f1d, row(ln1_g[i]), row(ln1_b[i]))
        z = _inproj(xf, _wide_w_in(w_in[i]))

        apr, api, bbr, bbi = _s5_discretise(s5_lambda_re[i], s5_lambda_im[i], s5_log_dt[i], s5_b_re[i], s5_b_im[i])
        bh = jnp.concatenate([_block_diag_in(bbr), _block_diag_in(bbi)], axis=2).astype(BF16)
        ch = jnp.concatenate([_block_diag_out(s5_c_re[i]), -_block_diag_out(s5_c_im[i])], axis=1).astype(BF16)
        apr = jnp.repeat(apr, S5_TM // S5_STEPS, axis=0)
        api = jnp.repeat(api, S5_TM // S5_STEPS, axis=0)
        y_s5 = _s5_mixer(z, bh, ch, apr, api, row(s5_d[i]), s5_w_glu[i].astype(BF16), row(s5_b_glu[i]),
                         bsz=bsz, n_pos=n_pos)

        wq1, wq2 = _mla_q_weights(mla_w_uq[i])
        q, k, vt = _mla_prep(z, cos, sin, row(mla_q_norm_g[i]), row(mla_kv_norm_g[i]), wq1, wq2,
                             mla_w_ukv[i].astype(BF16), bsz=bsz, n_pos=n_pos)
        y_mla = _mla_flash(q, k, vt).reshape(m, MLA_HEADS * MLA_V)

        y_ret = _retention(z, cos, sin, bsz=bsz, n_pos=n_pos)

        lambda_init = 0.8 - 0.6 * math.exp(-0.3 * i)
        dqm, dkb, dvt = _diff_prep(z, bsz=bsz, n_pos=n_pos)
        y_diff = _diff_attn(dqm, dkb, dvt, pos_q, pos_k, table, row(diff_lambda_q1[i]), row(diff_lambda_k1[i]),
                            row(diff_lambda_q2[i]), row(diff_lambda_k2[i]), row(diff_subln_g[i]),
                            bsz=bsz, n_pos=n_pos, lambda_init=lambda_init, t=t_diff)

        xf = _outproj_ln((y_s5, y_mla, y_ret, y_diff), wo, xf, row(ln2_g[i]), row(ln2_b[i]))

        res, xb = _ple(xf, p[i].reshape(m, PLE_DIM), pwg, row(ple_b_gate[i]), ple_w_proj[i].astype(BF16))
        xf = _ffn_ln(xb, f2g, f2u, f2d, row(ln3_g[i]), row(ln3_b[i]), res)
    return xf.reshape(bsz, n_pos, d)
```

```python
import functools
import math

import numpy as np
import jax
import jax.numpy as jnp
from jax import lax
from jax.experimental import pallas as pl
from jax.experimental.pallas import tpu as pltpu

F32 = jnp.float32
BF16 = jnp.bfloat16

D_MODEL = 2048
DEPTH = 2
PLE_DIM = 256
D_FF = 5632
ALPHA = (2 * DEPTH) ** 0.25
ROPE_THETA = 10000.0
NEG_INF = -1e30
LN_EPS = 1e-5
RMS_EPS = 1e-6

S5_WIDTH = 512
S5_GROUP = 16
S5_GROUPS = 32
S5_STATE = 64
S5_NSTATE = S5_GROUPS * S5_STATE
S5_HALF = S5_NSTATE // 2
S5_TM = 512
S5_STEPS = S5_TM // 8

MLA_HEADS = 4
MLA_Q_RANK = 512
MLA_KV_RANK = 128
MLA_NOPE = 128
MLA_ROPE = 64
MLA_V = 128
MLA_QK_PAD = 256

RET_HEADS = 4
RET_QK = 64
RET_V = 128
RET_CHUNK = 128

DIFF_HEADS = 4
DIFF_QK = 64
DIFF_V = 128

T5_BUCKETS = 32
T5_MAX_DIST = 128
T5_TABLE = 128

LOG2E = math.log2(math.e)
VT_ROWS = 144

LANES = 128
SUBLANES = 8
VMEM_LIMIT_BYTES = 56 * 1024 * 1024

OFF_S5 = 0
OFF_CQ = 512
OFF_RV = 1024
OFF_RG = 1536
OFF_DQ = 2048
OFF_DK = 2560
OFF_DV = 3072
OFF_RQ = 3584
OFF_RQR = 3840
OFF_RK = 4096
OFF_RKR = 4352
OFF_CKV = 4608
OFF_KR = 4736
OFF_KRR = 4864
IN_USED = 4992
IN_WIDE = 5120
IN_TN = 2560


def _cparams(sem, vmem_limit_bytes=VMEM_LIMIT_BYTES):
    return pltpu.CompilerParams(dimension_semantics=sem, vmem_limit_bytes=vmem_limit_bytes)


def _layer_norm(y, g, b):
    mu = jnp.mean(y, axis=-1, keepdims=True)
    yc = y - mu
    var = jnp.mean(yc * yc, axis=-1, keepdims=True)
    return yc * lax.rsqrt(var + LN_EPS) * g + b


CAST_STEPS = 16


def _cast_kernel(*refs):
    n = len(refs) // 2
    for src, dst in zip(refs[:n], refs[n:]):
        dst[...] = src[0].astype(BF16)


def _cast_layer(layer, *stacked):
    in_specs, out_specs, out_shape = [], [], []
    for w in stacked:
        _, r, c = w.shape
        tr = r // CAST_STEPS
        in_specs.append(pl.BlockSpec((1, tr, c), lambda s: (layer, s, 0)))
        out_specs.append(pl.BlockSpec((tr, c), lambda s: (s, 0)))
        out_shape.append(jax.ShapeDtypeStruct((r, c), BF16))
    return pl.pallas_call(
        _cast_kernel,
        grid=(CAST_STEPS,),
        in_specs=in_specs,
        out_specs=out_specs,
        out_shape=out_shape,
        compiler_params=_cparams(("parallel",)),
        name="cast_weights",
    )(*stacked)


FFN_ROW_CHUNK = 128
FFN_VMEM_LIMIT_BYTES = 60 * 1024 * 1024


def _ffn_ln_kernel(*refs, nf, tm, has_res):
    if has_res:
        xb_ref, res_ref, wg_ref, wu_ref, wd_ref, g_ref, b_ref, o_ref = refs
    else:
        res_ref, wg_ref, wu_ref, wd_ref, g_ref, b_ref, o_ref, xb_ref = refs
    f = pl.program_id(1)
    chunks = tm // FFN_ROW_CHUNK

    def rows(c):
        return pl.ds(pl.multiple_of(c * FFN_ROW_CHUNK, FFN_ROW_CHUNK), FFN_ROW_CHUNK)

    @pl.when(f == 0)
    def _():
        o_ref[...] = jnp.zeros_like(o_ref)
        if not has_res:
            def cast(c, _):
                xb_ref[rows(c), :] = res_ref[rows(c), :].astype(BF16)
                return 0
            lax.fori_loop(0, chunks, cast, 0)

    xb = xb_ref[...]
    gate = jnp.dot(xb, wg_ref[...], preferred_element_type=F32)
    up = jnp.dot(xb, wu_ref[...], preferred_element_type=F32)
    h = gate * jax.nn.sigmoid(gate) * up
    o_ref[...] += jnp.dot(h.astype(BF16), wd_ref[...], preferred_element_type=F32)

    @pl.when(f == nf - 1)
    def _():
        scale = 1.0 if has_res else ALPHA

        def norm(c, _):
            y = scale * res_ref[rows(c), :] + 0.5 * o_ref[rows(c), :]
            o_ref[rows(c), :] = _layer_norm(y, g_ref[...], b_ref[...])
            return 0
        lax.fori_loop(0, chunks, norm, 0)


def _ffn_ln(x, wg, wu, wd, g, b, res=None, *, tm=1024, tf=512):
    m, d = x.shape
    f_dim = wg.shape[1]
    tm = min(tm, m)
    nf = f_dim // tf
    xspec = pl.BlockSpec((tm, d), lambda i, f: (i, 0))
    in_specs = [xspec] * (2 if res is not None else 1) + [
        pl.BlockSpec((d, tf), lambda i, f: (0, f)),
        pl.BlockSpec((d, tf), lambda i, f: (0, f)),
        pl.BlockSpec((tf, d), lambda i, f: (f, 0)),
        pl.BlockSpec((1, d), lambda i, f: (0, 0)),
        pl.BlockSpec((1, d), lambda i, f: (0, 0)),
    ]
    args = ([x, res] if res is not None else [x]) + [wg, wu, wd, g, b]
    return pl.pallas_call(
        functools.partial(_ffn_ln_kernel, nf=nf, tm=tm, has_res=res is not None),
        grid=(m // tm, nf),
        in_specs=in_specs,
        out_specs=pl.BlockSpec((tm, d), lambda i, f: (i, 0)),
        out_shape=jax.ShapeDtypeStruct((m, d), F32),
        scratch_shapes=[] if res is not None else [pltpu.VMEM((tm, d), BF16)],
        compiler_params=_cparams(("parallel", "arbitrary"), FFN_VMEM_LIMIT_BYTES),
        name="ffn_ln",
    )(*args)


def _ple_kernel(x_ref, p_ref, wg_ref, bg_ref, wp_ref, res_ref, xb_ref):
    x = x_ref[...]
    xb = x.astype(BF16)
    xb_ref[...] = xb
    pb = p_ref[...].astype(BF16)
    gate = jax.nn.sigmoid(jnp.dot(xb, wg_ref[...], preferred_element_type=F32) + bg_ref[...])
    res_ref[...] = ALPHA * x + gate * jnp.dot(pb, wp_ref[...], preferred_element_type=F32)


def _ple(x, p, layer, wg, bg, wp, *, tm=512):
    m, d = x.shape
    tm = min(tm, m)
    return pl.pallas_call(
        _ple_kernel,
        grid=(m // tm,),
        in_specs=[
            pl.BlockSpec((tm, d), lambda i: (i, 0)),
            pl.BlockSpec((None, tm, PLE_DIM), lambda i: (layer, i, 0)),
            pl.BlockSpec((d, d), lambda i: (0, 0)),
            pl.BlockSpec((1, d), lambda i: (0, 0)),
            pl.BlockSpec((PLE_DIM, d), lambda i: (0, 0)),
        ],
        out_specs=[pl.BlockSpec((tm, d), lambda i: (i, 0))] * 2,
        out_shape=[jax.ShapeDtypeStruct((m, d), F32), jax.ShapeDtypeStruct((m, d), BF16)],
        compiler_params=_cparams(("parallel",)),
        name="ple",
    )(x, p, wg, bg, wp)


def _inproj_kernel(x_ref, w_ref, o_ref):
    o_ref[...] = jnp.dot(x_ref[...].astype(BF16), w_ref[...], preferred_element_type=F32)


def _inproj(x, w, *, tm=512):
    m, d = x.shape
    n = w.shape[1]
    tm = min(tm, m)
    return pl.pallas_call(
        _inproj_kernel,
        grid=(n // IN_TN, m // tm),
        in_specs=[
            pl.BlockSpec((tm, d), lambda j, i: (i, 0)),
            pl.BlockSpec((d, IN_TN), lambda j, i: (0, j)),
        ],
        out_specs=pl.BlockSpec((tm, IN_TN), lambda j, i: (i, j)),
        out_shape=jax.ShapeDtypeStruct((m, n), F32),
        compiler_params=_cparams(("parallel", "parallel")),
        name="inproj",
    )(x, w)


def _rope_kernel(pos_ref, freq_ref, cos_ref, sin_ref):
    ang = pos_ref[...].astype(F32) * freq_ref[...]
    cos_ref[...] = jnp.cos(ang)
    sin_ref[...] = jnp.sin(ang)


def _rope_tables(pos_col, *, tm=1024):
    m = pos_col.shape[0]
    tm = min(tm, m)
    half = MLA_ROPE // 2
    inv = 1.0 / (ROPE_THETA ** (np.arange(0, MLA_ROPE, 2, dtype=np.float64) / MLA_ROPE))
    freq = jnp.asarray(np.tile(inv, LANES // half)[None, :], F32)
    return pl.pallas_call(
        _rope_kernel,
        grid=(m // tm,),
        in_specs=[pl.BlockSpec((tm, 1), lambda i: (i, 0)), pl.BlockSpec((1, LANES), lambda i: (0, 0))],
        out_specs=[pl.BlockSpec((tm, LANES), lambda i: (i, 0))] * 2,
        out_shape=[jax.ShapeDtypeStruct((m, LANES), F32)] * 2,
        compiler_params=_cparams(("parallel",)),
        name="rope_tables",
    )(pos_col, freq)


def _t5_bucket_static():
    n = np.arange(T5_TABLE)
    max_exact = T5_BUCKETS // 2
    nf = np.maximum(n, 1).astype(np.float64)
    large = max_exact + (np.log(nf / max_exact) / math.log(T5_MAX_DIST / max_exact)
                         * (T5_BUCKETS - max_exact)).astype(np.int64)
    large = np.minimum(large, T5_BUCKETS - 1)
    return np.where(n < max_exact, n, large)


def _bias_table_kernel(rb_ref, onehot_ref, o_ref):
    rb = rb_ref[...]
    oh = onehot_ref[...]
    rows = [jnp.sum(oh * rb[:, h:h + 1], axis=0, keepdims=True) for h in range(DIFF_HEADS)]
    o_ref[...] = jnp.concatenate(rows, axis=0) * LOG2E


def _bias_table(rel_bias):
    bucket = _t5_bucket_static()
    onehot = jnp.asarray((np.arange(T5_BUCKETS)[:, None] == bucket[None, :]).astype(np.float32))
    return pl.pallas_call(
        _bias_table_kernel,
        out_shape=jax.ShapeDtypeStruct((DIFF_HEADS, T5_TABLE), F32),
        name="t5_bias_table",
    )(rel_bias, onehot)


def _s5_disc_kernel(lr_ref, li_ref, ldt_ref, br_ref, bi_ref, apr_ref, api_ref, bbr_ref, bbi_ref):
    lr = lr_ref[...]
    li = li_ref[...]
    dt = jnp.exp(ldt_ref[...])
    k = (lax.broadcasted_iota(jnp.int32, (S5_STEPS, 1), 0) + 1).astype(F32)
    mag = jnp.exp(lr * dt * k)
    apr_ref[...] = mag * jnp.cos(li * dt * k)
    api_ref[...] = mag * jnp.sin(li * dt * k)
    mag1 = jnp.exp(lr * dt)
    ar = mag1 * jnp.cos(li * dt)
    ai = mag1 * jnp.sin(li * dt)
    den = lr * lr + li * li
    fr = ((ar - 1.0) * lr + ai * li) / den
    fi = (ai * lr - (ar - 1.0) * li) / den
    br = br_ref[...]
    bi = bi_ref[...]
    bbr_ref[...] = fr * br - fi * bi
    bbi_ref[...] = fr * bi + fi * br


def _s5_discretise(lam_re, lam_im, log_dt, b_re, b_im):
    n = S5_NSTATE
    lr = lam_re.reshape(1, n)
    li = lam_im.reshape(1, n)
    ldt = jnp.repeat(log_dt, S5_STATE).reshape(1, n)
    br = jnp.transpose(b_re, (2, 0, 1)).reshape(S5_GROUP, n)
    bi = jnp.transpose(b_im, (2, 0, 1)).reshape(S5_GROUP, n)
    return pl.pallas_call(
        _s5_disc_kernel,
        out_shape=[jax.ShapeDtypeStruct((S5_STEPS, n), F32)] * 2 + [jax.ShapeDtypeStruct((S5_GROUP, n), F32)] * 2,
        name="s5_discretise",
    )(lr, li, ldt, br, bi)


def _block_diag_in(bb):
    gh = S5_GROUPS // 2
    eye = jnp.eye(gh, dtype=bb.dtype)
    v = bb.reshape(S5_GROUP, 2, gh, S5_STATE)
    out = eye[None, :, None, :, None] * jnp.transpose(v, (1, 0, 2, 3))[:, None, :, :, :]
    return out.reshape(2, gh * S5_GROUP, S5_HALF)


def _block_diag_out(c):
    gh = S5_GROUPS // 2
    eye = jnp.eye(gh, dtype=c.dtype)
    v = jnp.transpose(c, (0, 2, 1)).reshape(2, gh, S5_STATE, S5_GROUP)
    out = v[:, :, :, None, :] * eye[None, :, None, :, None]
    return out.reshape(2, S5_HALF, gh * S5_GROUP)


def _s5_kernel(u0_ref, u1_ref, u2_ref, u3_ref, bh_ref, ch_ref, pwr_ref, pwi_ref, d_ref, wglu_ref, bglu_ref, o_ref,
               h_ref, hb_ref, cr_ref, ci_ref, cinr_ref, cini_ref, yo_ref):
    u_refs = (u0_ref, u1_ref, u2_ref, u3_ref)
    t = pl.program_id(1)
    nh = S5_HALF
    cw = S5_WIDTH // 2

    @pl.when(t == 0)
    def _():
        cr_ref[...] = jnp.zeros_like(cr_ref)
        ci_ref[...] = jnp.zeros_like(ci_ref)

    u = jnp.concatenate(
        [jnp.concatenate([r[pl.ds(j, SUBLANES, stride=S5_STEPS), :] for j in range(S5_STEPS)], axis=0)
         for r in u_refs], axis=1)
    ub = u.astype(BF16)
    ys = []
    for half in range(2):
        st = pl.ds(half * nh, nh)
        re = pl.ds(2 * half * nh, nh)
        im = pl.ds((2 * half + 1) * nh, nh)
        h_ref[:, pl.ds(2 * half * nh, 2 * nh)] = jnp.dot(ub[:, half * cw:(half + 1) * cw], bh_ref[half],
                                                          preferred_element_type=F32)
        ar = pwr_ref[0:SUBLANES, st]
        ai = pwi_ref[0:SUBLANES, st]

        def scan(j, carry, re=re, im=im, ar=ar, ai=ai):
            hr, hi = carry
            r0 = pl.multiple_of(j * SUBLANES, SUBLANES)
            nr = ar * hr - ai * hi + h_ref[pl.ds(r0, SUBLANES), re]
            ni = ar * hi + ai * hr + h_ref[pl.ds(r0, SUBLANES), im]
            h_ref[pl.ds(r0, SUBLANES), re] = nr
            h_ref[pl.ds(r0, SUBLANES), im] = ni
            return nr, ni

        zero = jnp.zeros((SUBLANES, nh), F32)
        er, ei = lax.fori_loop(0, S5_STEPS, scan, (zero, zero))

        a_seg_r = pwr_ref[S5_TM - 1:S5_TM, st]
        a_seg_i = pwi_ref[S5_TM - 1:S5_TM, st]
        cr = cr_ref[:, st]
        ci = ci_ref[:, st]
        for s in range(SUBLANES):
            cinr_ref[s:s + 1, :] = cr
            cini_ref[s:s + 1, :] = ci
            cr, ci = (er[s:s + 1] + a_seg_r * cr - a_seg_i * ci, ei[s:s + 1] + a_seg_r * ci + a_seg_i * cr)
        cr_ref[:, st] = cr
        ci_ref[:, st] = ci

        def fix(jj, _, re=re, im=im, st=st):
            r0 = pl.multiple_of(jj * 2 * SUBLANES, 2 * SUBLANES)
            out_r, out_i = [], []
            cin_r = cinr_ref[...]
            cin_i = cini_ref[...]
            for k in range(2):
                rows = pl.ds(r0 + k * SUBLANES, SUBLANES)
                pr = pwr_ref[rows, st]
                pi = pwi_ref[rows, st]
                out_r.append(h_ref[rows, re] + pr * cin_r - pi * cin_i)
                out_i.append(h_ref[rows, im] + pr * cin_i + pi * cin_r)
            hb_ref[pl.ds(r0, 2 * SUBLANES), re] = jnp.concatenate(out_r, axis=0).astype(BF16)
            hb_ref[pl.ds(r0, 2 * SUBLANES), im] = jnp.concatenate(out_i, axis=0).astype(BF16)
            return 0

        lax.fori_loop(0, S5_STEPS // 2, fix, 0)
        ys.append(jnp.dot(hb_ref[:, pl.ds(2 * half * nh, 2 * nh)], ch_ref[half], preferred_element_type=F32))

    y = jnp.concatenate(ys, axis=1) + d_ref[...] * u
    c0 = math.sqrt(2.0 / math.pi)
    y = 0.5 * y * (1.0 + jnp.tanh(c0 * (y + 0.044715 * (y * y * y))))
    gate = jax.nn.sigmoid(jnp.dot(y.astype(BF16), wglu_ref[...], preferred_element_type=F32) + bglu_ref[...])
    out = y * gate
    tm = out.shape[0]
    for c in range(S5_WIDTH // LANES):
        yo_ref[pl.ds(c * tm, tm), :] = out[:, c * LANES:(c + 1) * LANES]
    for c in range(S5_WIDTH // LANES):
        for s in range(SUBLANES):
            o_ref[s * S5_STEPS:(s + 1) * S5_STEPS, c * LANES:(c + 1) * LANES] = (
                yo_ref[pl.ds(c * tm + s, S5_STEPS, stride=SUBLANES), :].astype(BF16))


def _s5_mixer(z, bh, ch, pwr, pwi, d, wglu, bglu, *, bsz, n_pos):
    tm = S5_TM
    nt = n_pos // tm
    m = bsz * n_pos
    const = lambda b, t: (0, 0)
    const3 = lambda b, t: (0, 0, 0)
    return pl.pallas_call(
        _s5_kernel,
        grid=(bsz, nt),
        in_specs=[
            *[pl.BlockSpec((tm, LANES), lambda b, t, c=c: (b * nt + t, OFF_S5 // LANES + c))
              for c in range(S5_WIDTH // LANES)],
            pl.BlockSpec(bh.shape, const3),
            pl.BlockSpec(ch.shape, const3),
            pl.BlockSpec((S5_TM, S5_NSTATE), const),
            pl.BlockSpec((S5_TM, S5_NSTATE), const),
            pl.BlockSpec((1, S5_WIDTH), const),
            pl.BlockSpec((S5_WIDTH, S5_WIDTH), const),
            pl.BlockSpec((1, S5_WIDTH), const),
        ],
        out_specs=pl.BlockSpec((tm, S5_WIDTH), lambda b, t: (b * nt + t, 0)),
        out_shape=jax.ShapeDtypeStruct((m, S5_WIDTH), BF16),
        scratch_shapes=[pltpu.VMEM((tm, 2 * S5_NSTATE), F32), pltpu.VMEM((tm, 2 * S5_NSTATE), BF16),
                        pltpu.VMEM((1, S5_NSTATE), F32), pltpu.VMEM((1, S5_NSTATE), F32),
                        pltpu.VMEM((8, S5_HALF), F32), pltpu.VMEM((8, S5_HALF), F32),
                        pltpu.VMEM((S5_WIDTH // LANES * tm, LANES), F32)],
        compiler_params=_cparams(("parallel", "arbitrary")),
        name="s5_mixer",
    )(z, z, z, z, bh, ch, pwr, pwi, d, wglu, bglu)


def _rms(x, g):
    return x * lax.rsqrt(jnp.mean(x * x, axis=-1, keepdims=True) + RMS_EPS) * g


def _mla_prep_kernel(cq_ref, ckv_ref, kr_ref, krr_ref, cos_ref, sin_ref, qg_ref, kvg_ref,
                     wq1_ref, wq2_ref, wkv_ref, q_ref, k_ref, vt_ref):
    cos = cos_ref[...]
    sin = sin_ref[...]
    scale = (MLA_NOPE + MLA_ROPE) ** -0.5 * LOG2E
    ones = jnp.ones((VT_ROWS - MLA_V, cos.shape[0]), BF16)
    cqn = _rms(cq_ref[...], qg_ref[...]).astype(BF16)
    z1 = jnp.dot(cqn, wq1_ref[...], preferred_element_type=F32)
    z2 = jnp.dot(cqn, wq2_ref[...], preferred_element_type=F32)
    ckvn = _rms(ckv_ref[...], kvg_ref[...]).astype(BF16)
    zkv = jnp.dot(ckvn, wkv_ref[...], preferred_element_type=F32)
    k_rope = (kr_ref[...] * cos + krr_ref[...] * sin).astype(BF16)
    for h in range(MLA_HEADS):
        a = h * MLA_QK_PAD
        q_nope = z1[:, a:a + MLA_NOPE]
        q_rope = z1[:, a + MLA_NOPE:a + MLA_QK_PAD] * cos + z2[:, h * LANES:(h + 1) * LANES] * sin
        q_ref[0, h, :, 0:MLA_NOPE] = (q_nope * scale).astype(BF16)
        q_ref[0, h, :, MLA_NOPE:MLA_QK_PAD] = (q_rope * scale).astype(BF16)
        k_ref[0, h, :, 0:MLA_NOPE] = zkv[:, a:a + MLA_NOPE].astype(BF16)
        k_ref[0, h, :, MLA_NOPE:MLA_QK_PAD] = k_rope
        vt_ref[0, 0, h, 0:MLA_V, :] = jnp.transpose(zkv[:, a + MLA_NOPE:a + MLA_NOPE + MLA_V]).astype(BF16)
        vt_ref[0, 0, h, MLA_V:VT_ROWS, :] = ones


def _softmax_step_t(st, vt, m_ref, acc_ref, idx, shift=None):
    m_old = m_ref[idx]
    cur = jnp.max(st, axis=0, keepdims=True)
    if shift is not None:
        cur = cur + shift
    m_new = jnp.maximum(m_old, cur)
    p = jnp.exp2(st - (m_new if shift is None else m_new - shift)).astype(BF16)
    corr = jnp.exp2(m_old - m_new)
    acc_ref[idx] = corr * acc_ref[idx] + jnp.dot(vt, p, preferred_element_type=F32)
    m_ref[idx] = m_new


def _causal_mask_t(t):
    k = lax.broadcasted_iota(jnp.int32, (t, t), 0)
    q = lax.broadcasted_iota(jnp.int32, (t, t), 1)
    return q >= k


def _finish_t(acc):
    return jnp.transpose(acc[0:MLA_V] / acc[MLA_V:MLA_V + 1])


def _causal_tiles(nb):
    pairs = [(qi, ki) for qi in range(nb) for ki in range(qi + 1)]
    return (jnp.asarray([p[0] for p in pairs], jnp.int32), jnp.asarray([p[1] for p in pairs], jnp.int32))


def _mla_flash_kernel(qi_ref, ki_ref, q_ref, k_ref, vt_ref, o_ref, m_ref, acc_ref, *, t):
    qi = qi_ref[pl.program_id(1)]
    ki = ki_ref[pl.program_id(1)]

    @pl.when(ki == 0)
    def _():
        m_ref[...] = jnp.full_like(m_ref, NEG_INF)
        acc_ref[...] = jnp.zeros_like(acc_ref)

    def step(masked):
        mask = _causal_mask_t(t) if masked else None
        def scores(h):
            st = lax.dot_general(k_ref[0, h], q_ref[0, h], (((1,), (1,)), ((), ())), preferred_element_type=F32)
            return jnp.where(mask, st, NEG_INF) if masked else st

        sts = [scores(h) for h in range(MLA_HEADS)]
        for h in range(MLA_HEADS):
            _softmax_step_t(sts[h], vt_ref[0, 0, h], m_ref, acc_ref, h)

    @pl.when(ki < qi)
    def _():
        step(False)

    @pl.when(ki == qi)
    def _():
        step(True)
        for h in range(MLA_HEADS):
            o_ref[0, :, h * MLA_V:(h + 1) * MLA_V] = _finish_t(acc_ref[h]).astype(BF16)


def _mla_flash(q, k, vt, *, t=512):
    bsz, nh, n_pos, _ = q.shape
    t = min(t, n_pos)
    nb = n_pos // t
    qi_tab, ki_tab = _causal_tiles(nb)
    return pl.pallas_call(
        functools.partial(_mla_flash_kernel, t=t),
        grid_spec=pltpu.PrefetchScalarGridSpec(
            num_scalar_prefetch=2,
            grid=(bsz, qi_tab.shape[0]),
            in_specs=[
                pl.BlockSpec((1, nh, t, MLA_QK_PAD), lambda b, s, qt, kt: (b, 0, qt[s], 0)),
                pl.BlockSpec((1, nh, t, MLA_QK_PAD), lambda b, s, qt, kt: (b, 0, kt[s], 0)),
                pl.BlockSpec((1, 1, nh, VT_ROWS, t), lambda b, s, qt, kt: (b, kt[s], 0, 0, 0)),
            ],
            out_specs=pl.BlockSpec((1, t, nh * MLA_V), lambda b, s, qt, kt: (b, qt[s], 0)),
            scratch_shapes=[pltpu.VMEM((nh, 1, t), F32), pltpu.VMEM((nh, VT_ROWS, t), F32)],
        ),
        out_shape=jax.ShapeDtypeStruct((bsz, n_pos, nh * MLA_V), BF16),
        compiler_params=_cparams(("parallel", "arbitrary")),
        name="mla_flash",
    )(qi_tab, ki_tab, q, k, vt)


def _diff_prep_kernel(q_ref, k_ref, v_ref, qm_ref, kb_ref, vt_ref):
    hw = 2 * DIFF_QK
    tm = q_ref.shape[0]
    lane = lax.broadcasted_iota(jnp.int32, (tm, hw), 1)
    scale = DIFF_QK ** -0.5 * LOG2E
    ones = jnp.ones((VT_ROWS - DIFF_V, tm), BF16)
    kb_ref[...] = k_ref[...].astype(BF16)
    for h in range(DIFF_HEADS):
        qh = q_ref[:, h * hw:(h + 1) * hw] * scale
        qm_ref[0, 2 * h] = jnp.where(lane < DIFF_QK, qh, 0.0).astype(BF16)
        qm_ref[0, 2 * h + 1] = jnp.where(lane >= DIFF_QK, qh, 0.0).astype(BF16)
        vt_ref[0, 0, h, 0:DIFF_V, :] = jnp.transpose(v_ref[:, h * DIFF_V:(h + 1) * DIFF_V]).astype(BF16)
        vt_ref[0, 0, h, DIFF_V:VT_ROWS, :] = ones


N_MLA_PREP_IN = 11
N_DIFF_PREP_IN = 3


def _attn_prep_kernel(*refs):
    n_in = N_MLA_PREP_IN + N_DIFF_PREP_IN
    ins, outs = refs[:n_in], refs[n_in:]
    _mla_prep_kernel(*ins[:N_MLA_PREP_IN], *outs[:3])
    _diff_prep_kernel(*ins[N_MLA_PREP_IN:], *outs[3:])


def _attn_prep(z, cos, sin, qg, kvg, wq1, wq2, wkv, *, bsz, n_pos, tm=512):
    tm = min(tm, n_pos)
    nt = n_pos // tm
    w = DIFF_HEADS * DIFF_V
    const = lambda b, t: (0, 0)
    zspec = lambda width, off: pl.BlockSpec((tm, width), lambda b, t: (b * nt + t, off // width))
    hspec = lambda width: pl.BlockSpec((1, MLA_HEADS, tm, width), lambda b, t: (b, 0, t, 0))
    slab = lambda nh: pl.BlockSpec((1, 1, nh, VT_ROWS, tm), lambda b, t: (b, t, 0, 0, 0))
    return pl.pallas_call(
        _attn_prep_kernel,
        grid=(bsz, nt),
        in_specs=[
            zspec(MLA_Q_RANK, OFF_CQ), zspec(MLA_KV_RANK, OFF_CKV), zspec(LANES, OFF_KR), zspec(LANES, OFF_KRR),
            pl.BlockSpec((tm, LANES), lambda b, t: (b * nt + t, 0)),
            pl.BlockSpec((tm, LANES), lambda b, t: (b * nt + t, 0)),
            pl.BlockSpec((1, MLA_Q_RANK), const), pl.BlockSpec((1, MLA_KV_RANK), const),
            pl.BlockSpec(wq1.shape, const), pl.BlockSpec(wq2.shape, const), pl.BlockSpec(wkv.shape, const),
            zspec(w, OFF_DQ), zspec(w, OFF_DK), zspec(w, OFF_DV),
        ],
        out_specs=[hspec(MLA_QK_PAD), hspec(MLA_QK_PAD), slab(MLA_HEADS),
                   pl.BlockSpec((1, 2 * DIFF_HEADS, tm, 2 * DIFF_QK), lambda b, t: (b, 0, t, 0)),
                   pl.BlockSpec((tm, w), lambda b, t: (b * nt + t, 0)),
                   slab(DIFF_HEADS)],
        out_shape=[jax.ShapeDtypeStruct((bsz, MLA_HEADS, n_pos, MLA_QK_PAD), BF16),
                   jax.ShapeDtypeStruct((bsz, MLA_HEADS, n_pos, MLA_QK_PAD), BF16),
                   jax.ShapeDtypeStruct((bsz, nt, MLA_HEADS, VT_ROWS, tm), BF16),
                   jax.ShapeDtypeStruct((bsz, 2 * DIFF_HEADS, n_pos, 2 * DIFF_QK), BF16),
                   jax.ShapeDtypeStruct((bsz * n_pos, w), BF16),
                   jax.ShapeDtypeStruct((bsz, nt, DIFF_HEADS, VT_ROWS, tm), BF16)],
        compiler_params=_cparams(("parallel", "parallel")),
        name="attn_prep",
    )(z, z, z, z, cos, sin, qg, kvg, wq1, wq2, wkv, z, z, z)


def _diff_kernel(qi_ref, ki_ref, qm_ref, k_ref, vt_ref, pq_ref, pk_ref, tab_ref, lq1_ref, lk1_ref, lq2_ref, lk2_ref,
                 sg_ref, o_ref, m_ref, acc_ref, *, t, lambda_init):
    qi = qi_ref[pl.program_id(1)]
    ki = ki_ref[pl.program_id(1)]
    nh = DIFF_HEADS
    hw = 2 * DIFF_QK

    @pl.when(ki == 0)
    def _():
        m_ref[...] = jnp.full_like(m_ref, NEG_INF)
        acc_ref[...] = jnp.zeros_like(acc_ref)

    def step(masked):
        pk = pk_ref[...]
        dist = [jnp.clip(pq_ref[0, :, c:c + LANES] - pk, 0, T5_TABLE - 1) for c in range(0, t, LANES)]
        mask = _causal_mask_t(t) if masked else None
        sts = []
        for h in range(nh):
            tab = jnp.broadcast_to(tab_ref[h:h + 1, :], (t, T5_TABLE))
            kh = k_ref[:, h * hw:(h + 1) * hw]
            bias = jnp.concatenate(
                [jnp.take_along_axis(tab, d, axis=1, mode="promise_in_bounds") for d in dist], axis=1)
            if masked:
                bias = jnp.where(mask, bias, NEG_INF)
            for mp in range(2):
                sts.append(lax.dot_general(kh, qm_ref[0, 2 * h + mp], (((1,), (1,)), ((), ())),
                                           preferred_element_type=F32) + bias)
        for j in range(2 * nh):
            _softmax_step_t(sts[j], vt_ref[0, 0, j // 2], m_ref, acc_ref, j)

    def step_far():
        sts = [lax.dot_general(k_ref[:, (j // 2) * hw:(j // 2 + 1) * hw], qm_ref[0, j], (((1,), (1,)), ((), ())),
                               preferred_element_type=F32) for j in range(2 * nh)]
        for j in range(2 * nh):
            shift = tab_ref[j // 2:j // 2 + 1, T5_TABLE - 1:T5_TABLE]
            _softmax_step_t(sts[j], vt_ref[0, 0, j // 2], m_ref, acc_ref, j, shift)

    far = jnp.min(pq_ref[0]) - jnp.max(pk_ref[:, 0:1]) >= T5_TABLE - 1

    @pl.when(jnp.logical_and(ki < qi, far))
    def _():
        step_far()

    @pl.when(jnp.logical_and(ki < qi, jnp.logical_not(far)))
    def _():
        step(False)

    @pl.when(ki == qi)
    def _():
        step(True)
        lam = (jnp.exp(jnp.sum(lq1_ref[...] * lk1_ref[...], axis=-1, keepdims=True))
               - jnp.exp(jnp.sum(lq2_ref[...] * lk2_ref[...], axis=-1, keepdims=True)) + lambda_init)
        for h in range(nh):
            o = _finish_t(acc_ref[2 * h]) - lam * _finish_t(acc_ref[2 * h + 1])
            o = _rms(o, sg_ref[...]) * (1.0 - lambda_init)
            o_ref[:, h * DIFF_V:(h + 1) * DIFF_V] = o.astype(BF16)


def _diff_attn(qm, kb, vt, pos_q, pos_k, table, lq1, lk1, lq2, lk2, sg, *, bsz, n_pos, lambda_init, t=512):
    t = min(t, n_pos)
    nb = n_pos // t
    w = DIFF_HEADS * DIFF_V
    const = lambda b, s, qt, kt: (0, 0)
    qi_tab, ki_tab = _causal_tiles(nb)
    return pl.pallas_call(
        functools.partial(_diff_kernel, t=t, lambda_init=lambda_init),
        grid_spec=pltpu.PrefetchScalarGridSpec(
            num_scalar_prefetch=2,
            grid=(bsz, qi_tab.shape[0]),
            in_specs=[
                pl.BlockSpec((1, 2 * DIFF_HEADS, t, 2 * DIFF_QK), lambda b, s, qt, kt: (b, 0, qt[s], 0)),
                pl.BlockSpec((t, w), lambda b, s, qt, kt: (b * nb + kt[s], 0)),
                pl.BlockSpec((1, 1, DIFF_HEADS, VT_ROWS, t), lambda b, s, qt, kt: (b, kt[s], 0, 0, 0)),
                pl.BlockSpec((1, 1, t), lambda b, s, qt, kt: (b * nb + qt[s], 0, 0)),
                pl.BlockSpec((t, LANES), lambda b, s, qt, kt: (b * nb + kt[s], 0)),
                pl.BlockSpec((DIFF_HEADS, T5_TABLE), const),
                pl.BlockSpec((1, DIFF_QK), const), pl.BlockSpec((1, DIFF_QK), const),
                pl.BlockSpec((1, DIFF_QK), const), pl.BlockSpec((1, DIFF_QK), const),
                pl.BlockSpec((1, DIFF_V), const),
            ],
            out_specs=pl.BlockSpec((t, w), lambda b, s, qt, kt: (b * nb + qt[s], 0)),
            scratch_shapes=[pltpu.VMEM((2 * DIFF_HEADS, 1, t), F32),
                            pltpu.VMEM((2 * DIFF_HEADS, VT_ROWS, t), F32)],
        ),
        out_shape=jax.ShapeDtypeStruct((bsz * n_pos, w), BF16),
        compiler_params=_cparams(("parallel", "arbitrary")),
        name="diff_attn",
    )(qi_tab, ki_tab, qm, kb, vt, pos_q, pos_k, table, lq1, lk1, lq2, lk2, sg)


def _ret_kernel(q_ref, qr_ref, k_ref, kr_ref, v_ref, g_ref, cos_ref, sin_ref, o_ref, st_ref, *, tm):
    t = pl.program_id(1)
    c = RET_CHUNK
    nh = RET_HEADS
    w = nh * RET_QK

    @pl.when(t == 0)
    def _():
        st_ref[...] = jnp.zeros_like(st_ref)

    log_gamma = [math.log(1.0 - 2.0 ** (-5.0 - h)) for h in range(nh)]
    lane = lax.broadcasted_iota(jnp.int32, (1, w), 1)
    lg_lane = jnp.zeros((1, w), F32)
    for h in range(nh):
        lg_lane = jnp.where(lane // RET_QK == h, log_gamma[h], lg_lane)
    tok = lax.broadcasted_iota(jnp.int32, (c, 1), 0).astype(F32)
    q_decay = jnp.exp(lg_lane * (tok + 1.0))
    k_decay = jnp.exp(lg_lane * (c - 1.0 - tok))
    ri = lax.broadcasted_iota(jnp.int32, (c, c), 0)
    ci = lax.broadcasted_iota(jnp.int32, (c, c), 1)
    rel = (ri - ci).astype(F32)
    intra = [jnp.where(rel >= 0, jnp.exp(log_gamma[h] * jnp.maximum(rel, 0.0)), 0.0) for h in range(nh)]
    head_lanes = [(lane // RET_QK == h) for h in range(nh)]

    for j in range(tm // c):
        rows = slice(j * c, (j + 1) * c)
        cos = jnp.concatenate([cos_ref[rows, :]] * (w // LANES), axis=1)
        sin = jnp.concatenate([sin_ref[rows, :]] * (w // LANES), axis=1)
        q = q_ref[rows, :] * cos + qr_ref[rows, :] * sin
        k = (k_ref[rows, :] * cos + kr_ref[rows, :] * sin) * (RET_QK ** -0.5)
        kb = k.astype(BF16)
        qd = q * q_decay
        kdt = jnp.transpose(k * k_decay).astype(BF16)
        for h in range(nh):
            vh = v_ref[rows, h * RET_V:(h + 1) * RET_V].astype(BF16)
            qh = jnp.where(head_lanes[h], q, 0.0).astype(BF16)
            scores = lax.dot_general(qh, kb, (((1,), (1,)), ((), ())), preferred_element_type=F32) * intra[h]
            inner = jnp.dot(scores.astype(BF16), vh, preferred_element_type=F32)
            qdh = jnp.where(head_lanes[h], qd, 0.0).astype(BF16)
            state = st_ref[...]
            cross = jnp.dot(qdh, state.astype(BF16), preferred_element_type=F32)
            o = inner + cross
            mu = jnp.mean(o, axis=-1, keepdims=True)
            oc = o - mu
            var = jnp.mean(oc * oc, axis=-1, keepdims=True)
            o = oc * lax.rsqrt(var + LN_EPS)
            gh = g_ref[rows, h * RET_V:(h + 1) * RET_V]
            o_ref[rows, h * RET_V:(h + 1) * RET_V] = (gh * jax.nn.sigmoid(gh) * o).astype(BF16)
            hs = slice(h * RET_QK, (h + 1) * RET_QK)
            kv = jnp.dot(kdt[hs, :], vh, preferred_element_type=F32)
            st_ref[hs, :] = state[hs, :] * math.exp(log_gamma[h] * c) + kv


def _retention(z, cos, sin, *, bsz, n_pos, tm=512):
    tm = min(tm, n_pos)
    nt = n_pos // tm
    w = RET_HEADS * RET_QK
    wv = RET_HEADS * RET_V
    zspec = lambda width, off: pl.BlockSpec((tm, width), lambda b, t: (b * nt + t, off // width))
    tspec = pl.BlockSpec((tm, LANES), lambda b, t: (b * nt + t, 0))
    return pl.pallas_call(
        functools.partial(_ret_kernel, tm=tm),
        grid=(bsz, nt),
        in_specs=[zspec(w, OFF_RQ), zspec(w, OFF_RQR), zspec(w, OFF_RK), zspec(w, OFF_RKR),
                  zspec(wv, OFF_RV), zspec(wv, OFF_RG), tspec, tspec],
        out_specs=pl.BlockSpec((tm, wv), lambda b, t: (b * nt + t, 0)),
        out_shape=jax.ShapeDtypeStruct((bsz * n_pos, wv), BF16),
        scratch_shapes=[pltpu.VMEM((w, RET_V), F32)],
        compiler_params=_cparams(("parallel", "arbitrary")),
        name="retention",
    )(z, z, z, z, z, z, cos, sin)


def _outproj_ln_kernel(y0_ref, y1_ref, y2_ref, y3_ref, w_ref, x_ref, g_ref, b_ref, o_ref):
    acc = None
    for j, y_ref in enumerate((y0_ref, y1_ref, y2_ref, y3_ref)):
        part = jnp.dot(y_ref[...], w_ref[j * 512:(j + 1) * 512, :], preferred_element_type=F32)
        acc = part if acc is None else acc + part
    o_ref[...] = _layer_norm(ALPHA * x_ref[...] + acc, g_ref[...], b_ref[...])


def _outproj_ln(ys, w, x, g, b, *, tm=512):
    m, d = x.shape
    tm = min(tm, m)
    yspec = pl.BlockSpec((tm, 512), lambda i: (i, 0))
    return pl.pallas_call(
        _outproj_ln_kernel,
        grid=(m // tm,),
        in_specs=[yspec, yspec, yspec, yspec,
                  pl.BlockSpec(w.shape, lambda i: (0, 0)),
                  pl.BlockSpec((tm, d), lambda i: (i, 0)),
                  pl.BlockSpec((1, d), lambda i: (0, 0)),
                  pl.BlockSpec((1, d), lambda i: (0, 0))],
        out_specs=pl.BlockSpec((tm, d), lambda i: (i, 0)),
        out_shape=jax.ShapeDtypeStruct((m, d), F32),
        compiler_params=_cparams(("parallel",)),
        name="outproj_ln",
    )(*ys, w, x, g, b)


def _rot_cols(w, heads, dim):
    k = w.shape[0]
    w = w.reshape(k, heads, 2, dim // 2)
    return jnp.concatenate([-w[:, :, 1], w[:, :, 0]], axis=-1).reshape(k, heads * dim)


def _pad_cols(w, width):
    return jnp.pad(w, ((0, 0), (0, width - w.shape[1])))


def _wide_w_in(w_in):
    sizes = (S5_WIDTH, MLA_Q_RANK, MLA_KV_RANK, MLA_ROPE,
             RET_HEADS * RET_QK, RET_HEADS * RET_QK, RET_HEADS * RET_V, RET_HEADS * RET_V,
             DIFF_HEADS * 2 * DIFF_QK, DIFF_HEADS * 2 * DIFF_QK, DIFF_HEADS * DIFF_V)
    offs = np.concatenate([[0], np.cumsum(sizes)])
    (s5_u, cq, ckv, kr, rq, rk, rv, rg, dq, dk, dv) = [w_in[:, offs[i]:offs[i + 1]] for i in range(len(sizes))]
    cols = [s5_u, cq, rv, rg, dq, dk, dv,
            rq, _rot_cols(rq, RET_HEADS, RET_QK), rk, _rot_cols(rk, RET_HEADS, RET_QK),
            ckv, _pad_cols(kr, LANES), _pad_cols(_rot_cols(kr, 1, MLA_ROPE), LANES),
            jnp.zeros((w_in.shape[0], IN_WIDE - IN_USED), w_in.dtype)]
    return jnp.concatenate(cols, axis=1).astype(BF16)


def _mla_q_weights(w_uq):
    k = w_uq.shape[0]
    w = w_uq.reshape(k, MLA_HEADS, MLA_NOPE + MLA_ROPE)
    rope = w[:, :, MLA_NOPE:]
    pad = MLA_QK_PAD - MLA_NOPE - MLA_ROPE
    w1 = jnp.pad(w, ((0, 0), (0, 0), (0, pad))).reshape(k, MLA_HEADS * MLA_QK_PAD)
    rot = _rot_cols(rope.reshape(k, MLA_HEADS * MLA_ROPE), MLA_HEADS, MLA_ROPE).reshape(k, MLA_HEADS, MLA_ROPE)
    w2 = jnp.pad(rot, ((0, 0), (0, 0), (0, LANES - MLA_ROPE))).reshape(k, MLA_HEADS * LANES)
    return w1.astype(BF16), w2.astype(BF16)


def kernel(x, p, positions, rel_bias, ffn1_w_gate, ffn1_w_up, ffn1_w_down, ln1_g, ln1_b, w_in, w_out, ln2_g, ln2_b, s5_lambda_re, s5_lambda_im, s5_log_dt, s5_b_re, s5_b_im, s5_c_re, s5_c_im, s5_d, s5_w_glu, s5_b_glu, mla_q_norm_g, mla_w_uq, mla_kv_norm_g, mla_w_ukv, diff_lambda_q1, diff_lambda_k1, diff_lambda_q2, diff_lambda_k2, diff_subln_g, ffn2_w_gate, ffn2_w_up, ffn2_w_down, ple_w_gate, ple_b_gate, ple_w_proj, ln3_g, ln3_b):
    bsz, n_pos, d = x.shape
    m = bsz * n_pos
    depth = ffn1_w_gate.shape[0]
    xf = x.reshape(m, d)
    row = lambda v: v.reshape(1, -1)

    cos, sin = _rope_tables(positions.reshape(m, 1))
    table = _bias_table(rel_bias)
    t_diff = min(512, n_pos)
    pos_q = positions.reshape(bsz * (n_pos // t_diff), 1, t_diff)
    pos_k = jnp.broadcast_to(positions.reshape(m, 1), (m, LANES))

    for i in range(depth):
        f1g, f1u, f1d = _cast_layer(i, ffn1_w_gate, ffn1_w_up, ffn1_w_down)
        f2g, f2u, f2d, pwg, wo = _cast_layer(i, ffn2_w_gate, ffn2_w_up, ffn2_w_down, ple_w_gate, w_out)
        xf = _ffn_ln(xf, f1g, f1u, f1d, row(ln1_g[i]), row(ln1_b[i]))
        z = _inproj(xf, _wide_w_in(w_in[i]))

        apr, api, bbr, bbi = _s5_discretise(s5_lambda_re[i], s5_lambda_im[i], s5_log_dt[i], s5_b_re[i], s5_b_im[i])
        bh = jnp.concatenate([_block_diag_in(bbr), _block_diag_in(bbi)], axis=2).astype(BF16)
        ch = jnp.concatenate([_block_diag_out(s5_c_re[i]), -_block_diag_out(s5_c_im[i])], axis=1).astype(BF16)
        apr = jnp.repeat(apr, S5_TM // S5_STEPS, axis=0)
        api = jnp.repeat(api, S5_TM // S5_STEPS, axis=0)
        y_s5 = _s5_mixer(z, bh, ch, apr, api, row(s5_d[i]), s5_w_glu[i].astype(BF16), row(s5_b_glu[i]),
                         bsz=bsz, n_pos=n_pos)

        wq1, wq2 = _mla_q_weights(mla_w_uq[i])
        q, k, vt, dqm, dkb, dvt = _attn_prep(z, cos, sin, row(mla_q_norm_g[i]), row(mla_kv_norm_g[i]), wq1, wq2,
                                             mla_w_ukv[i].astype(BF16), bsz=bsz, n_pos=n_pos)
        y_mla = _mla_flash(q, k, vt).reshape(m, MLA_HEADS * MLA_V)

        y_ret = _retention(z, cos, sin, bsz=bsz, n_pos=n_pos)

        lambda_init = 0.8 - 0.6 * math.exp(-0.3 * i)
        y_diff = _diff_attn(dqm, dkb, dvt, pos_q, pos_k, table, row(diff_lambda_q1[i]), row(diff_lambda_k1[i]),
                            row(diff_lambda_q2[i]), row(diff_lambda_k2[i]), row(diff_subln_g[i]),
                            bsz=bsz, n_pos=n_pos, lambda_init=lambda_init, t=t_diff)

        xf = _outproj_ln((y_s5, y_mla, y_ret, y_diff), wo, xf, row(ln2_g[i]), row(ln2_b[i]))

        res, xb = _ple(xf, p.reshape(depth, m, PLE_DIM), i, pwg, row(ple_b_gate[i]), ple_w_proj[i].astype(BF16))
        xf = _ffn_ln(xb, f2g, f2u, f2d, row(ln3_g[i]), row(ln3_b[i]), res)
    return xf.reshape(bsz, n_pos, d)
```

```python
import functools
import math

import numpy as np
import jax
import jax.numpy as jnp
from jax import lax
from jax.experimental import pallas as pl
from jax.experimental.pallas import tpu as pltpu

F32 = jnp.float32
BF16 = jnp.bfloat16

D_MODEL = 2048
DEPTH = 2
PLE_DIM = 256
D_FF = 5632
ALPHA = (2 * DEPTH) ** 0.25
ROPE_THETA = 10000.0
NEG_INF = -1e30
LN_EPS = 1e-5
RMS_EPS = 1e-6

S5_WIDTH = 512
S5_GROUP = 16
S5_GROUPS = 32
S5_STATE = 64
S5_NSTATE = S5_GROUPS * S5_STATE
S5_HALF = S5_NSTATE // 2
S5_TM = 512
S5_STEPS = S5_TM // 8

MLA_HEADS = 4
MLA_Q_RANK = 512
MLA_KV_RANK = 128
MLA_NOPE = 128
MLA_ROPE = 64
MLA_V = 128
MLA_QK_PAD = 256

RET_HEADS = 4
RET_QK = 64
RET_V = 128
RET_CHUNK = 128

DIFF_HEADS = 4
DIFF_QK = 64
DIFF_V = 128

T5_BUCKETS = 32
T5_MAX_DIST = 128
T5_TABLE = 128

LOG2E = math.log2(math.e)
VT_ROWS = 144

LANES = 128
SUBLANES = 8
VMEM_LIMIT_BYTES = 56 * 1024 * 1024

OFF_S5 = 0
OFF_CQ = 512
OFF_RV = 1024
OFF_RG = 1536
OFF_DQ = 2048
OFF_DK = 2560
OFF_DV = 3072
OFF_RQ = 3584
OFF_RQR = 3840
OFF_RK = 4096
OFF_RKR = 4352
OFF_CKV = 4608
OFF_KR = 4736
OFF_KRR = 4864
IN_USED = 4992
IN_WIDE = 5120
IN_TN = 2560


def _cparams(sem, vmem_limit_bytes=VMEM_LIMIT_BYTES):
    return pltpu.CompilerParams(dimension_semantics=sem, vmem_limit_bytes=vmem_limit_bytes)


def _layer_norm(y, g, b):
    mu = jnp.mean(y, axis=-1, keepdims=True)
    yc = y - mu
    var = jnp.mean(yc * yc, axis=-1, keepdims=True)
    return yc * lax.rsqrt(var + LN_EPS) * g + b


CAST_STEPS = 16


def _cast_kernel(*refs):
    n = len(refs) // 2
    for src, dst in zip(refs[:n], refs[n:]):
        dst[...] = src[0].astype(BF16)


def _cast_layer(layer, *stacked):
    in_specs, out_specs, out_shape = [], [], []
    for w in stacked:
        _, r, c = w.shape
        tr = r // CAST_STEPS
        in_specs.append(pl.BlockSpec((1, tr, c), lambda s: (layer, s, 0)))
        out_specs.append(pl.BlockSpec((tr, c), lambda s: (s, 0)))
        out_shape.append(jax.ShapeDtypeStruct((r, c), BF16))
    return pl.pallas_call(
        _cast_kernel,
        grid=(CAST_STEPS,),
        in_specs=in_specs,
        out_specs=out_specs,
        out_shape=out_shape,
        compiler_params=_cparams(("parallel",)),
        name="cast_weights",
    )(*stacked)


FFN_ROW_CHUNK = 128
FFN_VMEM_LIMIT_BYTES = 60 * 1024 * 1024


def _ffn_ln_kernel(*refs, nf, tm, has_res):
    if has_res:
        xb_ref, res_ref, wg_ref, wu_ref, wd_ref, g_ref, b_ref, o_ref = refs
    else:
        res_ref, wg_ref, wu_ref, wd_ref, g_ref, b_ref, o_ref, xb_ref = refs
    f = pl.program_id(1)
    chunks = tm // FFN_ROW_CHUNK

    def rows(c):
        return pl.ds(pl.multiple_of(c * FFN_ROW_CHUNK, FFN_ROW_CHUNK), FFN_ROW_CHUNK)

    @pl.when(f == 0)
    def _():
        o_ref[...] = jnp.zeros_like(o_ref)
        if not has_res:
            def cast(c, _):
                xb_ref[rows(c), :] = res_ref[rows(c), :].astype(BF16)
                return 0
            lax.fori_loop(0, chunks, cast, 0)

    xb = xb_ref[...]
    gate = jnp.dot(xb, wg_ref[...], preferred_element_type=F32)
    up = jnp.dot(xb, wu_ref[...], preferred_element_type=F32)
    h = gate * jax.nn.sigmoid(gate) * up
    o_ref[...] += jnp.dot(h.astype(BF16), wd_ref[...], preferred_element_type=F32)

    @pl.when(f == nf - 1)
    def _():
        scale = 1.0 if has_res else ALPHA

        def norm(c, _):
            y = scale * res_ref[rows(c), :] + 0.5 * o_ref[rows(c), :]
            o_ref[rows(c), :] = _layer_norm(y, g_ref[...], b_ref[...])
            return 0
        lax.fori_loop(0, chunks, norm, 0)


def _ffn_ln(x, wg, wu, wd, g, b, res=None, *, tm=1024, tf=512):
    m, d = x.shape
    f_dim = wg.shape[1]
    tm = min(tm, m)
    nf = f_dim // tf
    xspec = pl.BlockSpec((tm, d), lambda i, f: (i, 0))
    in_specs = [xspec] * (2 if res is not None else 1) + [
        pl.BlockSpec((d, tf), lambda i, f: (0, f)),
        pl.BlockSpec((d, tf), lambda i, f: (0, f)),
        pl.BlockSpec((tf, d), lambda i, f: (f, 0)),
        pl.BlockSpec((1, d), lambda i, f: (0, 0)),
        pl.BlockSpec((1, d), lambda i, f: (0, 0)),
    ]
    args = ([x, res] if res is not None else [x]) + [wg, wu, wd, g, b]
    return pl.pallas_call(
        functools.partial(_ffn_ln_kernel, nf=nf, tm=tm, has_res=res is not None),
        grid=(m // tm, nf),
        in_specs=in_specs,
        out_specs=pl.BlockSpec((tm, d), lambda i, f: (i, 0)),
        out_shape=jax.ShapeDtypeStruct((m, d), F32),
        scratch_shapes=[] if res is not None else [pltpu.VMEM((tm, d), BF16)],
        compiler_params=_cparams(("parallel", "arbitrary"), FFN_VMEM_LIMIT_BYTES),
        name="ffn_ln",
    )(*args)


def _ple_kernel(x_ref, p_ref, wg_ref, bg_ref, wp_ref, res_ref, xb_ref):
    x = x_ref[...]
    xb = x.astype(BF16)
    xb_ref[...] = xb
    pb = p_ref[...].astype(BF16)
    gate = jax.nn.sigmoid(jnp.dot(xb, wg_ref[...], preferred_element_type=F32) + bg_ref[...])
    res_ref[...] = ALPHA * x + gate * jnp.dot(pb, wp_ref[...], preferred_element_type=F32)


def _ple(x, p, layer, wg, bg, wp, *, tm=512):
    m, d = x.shape
    tm = min(tm, m)
    return pl.pallas_call(
        _ple_kernel,
        grid=(m // tm,),
        in_specs=[
            pl.BlockSpec((tm, d), lambda i: (i, 0)),
            pl.BlockSpec((None, tm, PLE_DIM), lambda i: (layer, i, 0)),
            pl.BlockSpec((d, d), lambda i: (0, 0)),
            pl.BlockSpec((1, d), lambda i: (0, 0)),
            pl.BlockSpec((PLE_DIM, d), lambda i: (0, 0)),
        ],
        out_specs=[pl.BlockSpec((tm, d), lambda i: (i, 0))] * 2,
        out_shape=[jax.ShapeDtypeStruct((m, d), F32), jax.ShapeDtypeStruct((m, d), BF16)],
        compiler_params=_cparams(("parallel",)),
        name="ple",
    )(x, p, wg, bg, wp)


def _inproj_kernel(x_ref, w_ref, o_ref):
    o_ref[...] = jnp.dot(x_ref[...].astype(BF16), w_ref[...], preferred_element_type=F32)


def _inproj(x, w, *, tm=512):
    m, d = x.shape
    n = w.shape[1]
    tm = min(tm, m)
    return pl.pallas_call(
        _inproj_kernel,
        grid=(n // IN_TN, m // tm),
        in_specs=[
            pl.BlockSpec((tm, d), lambda j, i: (i, 0)),
            pl.BlockSpec((d, IN_TN), lambda j, i: (0, j)),
        ],
        out_specs=pl.BlockSpec((tm, IN_TN), lambda j, i: (i, j)),
        out_shape=jax.ShapeDtypeStruct((m, n), F32),
        compiler_params=_cparams(("parallel", "parallel")),
        name="inproj",
    )(x, w)


def _rope_kernel(pos_ref, freq_ref, cos_ref, sin_ref):
    ang = pos_ref[...].astype(F32) * freq_ref[...]
    cos_ref[...] = jnp.cos(ang)
    sin_ref[...] = jnp.sin(ang)


def _rope_tables(pos_col, *, tm=1024):
    m = pos_col.shape[0]
    tm = min(tm, m)
    half = MLA_ROPE // 2
    inv = 1.0 / (ROPE_THETA ** (np.arange(0, MLA_ROPE, 2, dtype=np.float64) / MLA_ROPE))
    freq = jnp.asarray(np.tile(inv, LANES // half)[None, :], F32)
    return pl.pallas_call(
        _rope_kernel,
        grid=(m // tm,),
        in_specs=[pl.BlockSpec((tm, 1), lambda i: (i, 0)), pl.BlockSpec((1, LANES), lambda i: (0, 0))],
        out_specs=[pl.BlockSpec((tm, LANES), lambda i: (i, 0))] * 2,
        out_shape=[jax.ShapeDtypeStruct((m, LANES), F32)] * 2,
        compiler_params=_cparams(("parallel",)),
        name="rope_tables",
    )(pos_col, freq)


def _t5_bucket_static():
    n = np.arange(T5_TABLE)
    max_exact = T5_BUCKETS // 2
    nf = np.maximum(n, 1).astype(np.float64)
    large = max_exact + (np.log(nf / max_exact) / math.log(T5_MAX_DIST / max_exact)
                         * (T5_BUCKETS - max_exact)).astype(np.int64)
    large = np.minimum(large, T5_BUCKETS - 1)
    return np.where(n < max_exact, n, large)


def _bias_table_kernel(rb_ref, onehot_ref, o_ref):
    rb = rb_ref[...]
    oh = onehot_ref[...]
    rows = [jnp.sum(oh * rb[:, h:h + 1], axis=0, keepdims=True) for h in range(DIFF_HEADS)]
    o_ref[...] = jnp.concatenate(rows, axis=0) * LOG2E


def _bias_table(rel_bias):
    bucket = _t5_bucket_static()
    onehot = jnp.asarray((np.arange(T5_BUCKETS)[:, None] == bucket[None, :]).astype(np.float32))
    return pl.pallas_call(
        _bias_table_kernel,
        out_shape=jax.ShapeDtypeStruct((DIFF_HEADS, T5_TABLE), F32),
        name="t5_bias_table",
    )(rel_bias, onehot)


def _s5_disc_kernel(lr_ref, li_ref, ldt_ref, br_ref, bi_ref, apr_ref, api_ref, bbr_ref, bbi_ref):
    lr = lr_ref[...]
    li = li_ref[...]
    dt = jnp.exp(ldt_ref[...])
    k = (lax.broadcasted_iota(jnp.int32, (S5_STEPS, 1), 0) + 1).astype(F32)
    mag = jnp.exp(lr * dt * k)
    apr_ref[...] = mag * jnp.cos(li * dt * k)
    api_ref[...] = mag * jnp.sin(li * dt * k)
    mag1 = jnp.exp(lr * dt)
    ar = mag1 * jnp.cos(li * dt)
    ai = mag1 * jnp.sin(li * dt)
    den = lr * lr + li * li
    fr = ((ar - 1.0) * lr + ai * li) / den
    fi = (ai * lr - (ar - 1.0) * li) / den
    br = br_ref[...]
    bi = bi_ref[...]
    bbr_ref[...] = fr * br - fi * bi
    bbi_ref[...] = fr * bi + fi * br


def _s5_discretise(lam_re, lam_im, log_dt, b_re, b_im):
    n = S5_NSTATE
    lr = lam_re.reshape(1, n)
    li = lam_im.reshape(1, n)
    ldt = jnp.repeat(log_dt, S5_STATE).reshape(1, n)
    br = jnp.transpose(b_re, (2, 0, 1)).reshape(S5_GROUP, n)
    bi = jnp.transpose(b_im, (2, 0, 1)).reshape(S5_GROUP, n)
    return pl.pallas_call(
        _s5_disc_kernel,
        out_shape=[jax.ShapeDtypeStruct((S5_STEPS, n), F32)] * 2 + [jax.ShapeDtypeStruct((S5_GROUP, n), F32)] * 2,
        name="s5_discretise",
    )(lr, li, ldt, br, bi)


def _block_diag_in(bb):
    gh = S5_GROUPS // 2
    eye = jnp.eye(gh, dtype=bb.dtype)
    v = bb.reshape(S5_GROUP, 2, gh, S5_STATE)
    out = eye[None, :, None, :, None] * jnp.transpose(v, (1, 0, 2, 3))[:, None, :, :, :]
    return out.reshape(2, gh * S5_GROUP, S5_HALF)


def _block_diag_out(c):
    gh = S5_GROUPS // 2
    eye = jnp.eye(gh, dtype=c.dtype)
    v = jnp.transpose(c, (0, 2, 1)).reshape(2, gh, S5_STATE, S5_GROUP)
    out = v[:, :, :, None, :] * eye[None, :, None, :, None]
    return out.reshape(2, S5_HALF, gh * S5_GROUP)


def _s5_kernel(u0_ref, u1_ref, u2_ref, u3_ref, bh_ref, ch_ref, pwr_ref, pwi_ref, d_ref, wglu_ref, bglu_ref, o_ref,
               h_ref, hb_ref, cr_ref, ci_ref, cinr_ref, cini_ref, yo_ref):
    u_refs = (u0_ref, u1_ref, u2_ref, u3_ref)
    t = pl.program_id(1)
    nh = S5_HALF
    cw = S5_WIDTH // 2

    @pl.when(t == 0)
    def _():
        cr_ref[...] = jnp.zeros_like(cr_ref)
        ci_ref[...] = jnp.zeros_like(ci_ref)

    u = jnp.concatenate(
        [jnp.concatenate([r[pl.ds(j, SUBLANES, stride=S5_STEPS), :] for j in range(S5_STEPS)], axis=0)
         for r in u_refs], axis=1)
    ub = u.astype(BF16)
    ys = []
    for half in range(2):
        st = pl.ds(half * nh, nh)
        re = pl.ds(2 * half * nh, nh)
        im = pl.ds((2 * half + 1) * nh, nh)
        h_ref[:, pl.ds(2 * half * nh, 2 * nh)] = jnp.dot(ub[:, half * cw:(half + 1) * cw], bh_ref[half],
                                                          preferred_element_type=F32)
        ar = pwr_ref[0:SUBLANES, st]
        ai = pwi_ref[0:SUBLANES, st]

        def scan(j, carry, re=re, im=im, ar=ar, ai=ai):
            hr, hi = carry
            r0 = pl.multiple_of(j * SUBLANES, SUBLANES)
            nr = ar * hr - ai * hi + h_ref[pl.ds(r0, SUBLANES), re]
            ni = ar * hi + ai * hr + h_ref[pl.ds(r0, SUBLANES), im]
            h_ref[pl.ds(r0, SUBLANES), re] = nr
            h_ref[pl.ds(r0, SUBLANES), im] = ni
            return nr, ni

        zero = jnp.zeros((SUBLANES, nh), F32)
        er, ei = lax.fori_loop(0, S5_STEPS, scan, (zero, zero))

        a_seg_r = pwr_ref[S5_TM - 1:S5_TM, st]
        a_seg_i = pwi_ref[S5_TM - 1:S5_TM, st]
        cr = cr_ref[:, st]
        ci = ci_ref[:, st]
        for s in range(SUBLANES):
            cinr_ref[s:s + 1, :] = cr
            cini_ref[s:s + 1, :] = ci
            cr, ci = (er[s:s + 1] + a_seg_r * cr - a_seg_i * ci, ei[s:s + 1] + a_seg_r * ci + a_seg_i * cr)
        cr_ref[:, st] = cr
        ci_ref[:, st] = ci

        def fix(jj, _, re=re, im=im, st=st):
            r0 = pl.multiple_of(jj * 2 * SUBLANES, 2 * SUBLANES)
            out_r, out_i = [], []
            cin_r = cinr_ref[...]
            cin_i = cini_ref[...]
            for k in range(2):
                rows = pl.ds(r0 + k * SUBLANES, SUBLANES)
                pr = pwr_ref[rows, st]
                pi = pwi_ref[rows, st]
                out_r.append(h_ref[rows, re] + pr * cin_r - pi * cin_i)
                out_i.append(h_ref[rows, im] + pr * cin_i + pi * cin_r)
            hb_ref[pl.ds(r0, 2 * SUBLANES), re] = jnp.concatenate(out_r, axis=0).astype(BF16)
            hb_ref[pl.ds(r0, 2 * SUBLANES), im] = jnp.concatenate(out_i, axis=0).astype(BF16)
            return 0

        lax.fori_loop(0, S5_STEPS // 2, fix, 0)
        ys.append(jnp.dot(hb_ref[:, pl.ds(2 * half * nh, 2 * nh)], ch_ref[half], preferred_element_type=F32))

    y = jnp.concatenate(ys, axis=1) + d_ref[...] * u
    c0 = math.sqrt(2.0 / math.pi)
    y = 0.5 * y * (1.0 + jnp.tanh(c0 * (y + 0.044715 * (y * y * y))))
    gate = jax.nn.sigmoid(jnp.dot(y.astype(BF16), wglu_ref[...], preferred_element_type=F32) + bglu_ref[...])
    out = y * gate
    tm = out.shape[0]
    for c in range(S5_WIDTH // LANES):
        yo_ref[pl.ds(c * tm, tm), :] = out[:, c * LANES:(c + 1) * LANES]
    for c in range(S5_WIDTH // LANES):
        for s in range(SUBLANES):
            o_ref[s * S5_STEPS:(s + 1) * S5_STEPS, c * LANES:(c + 1) * LANES] = (
                yo_ref[pl.ds(c * tm + s, S5_STEPS, stride=SUBLANES), :].astype(BF16))


def _s5_mixer(z, bh, ch, pwr, pwi, d, wglu, bglu, *, bsz, n_pos):
    tm = S5_TM
    nt = n_pos // tm
    m = bsz * n_pos
    const = lambda b, t: (0, 0)
    const3 = lambda b, t: (0, 0, 0)
    return pl.pallas_call(
        _s5_kernel,
        grid=(bsz, nt),
        in_specs=[
            *[pl.BlockSpec((tm, LANES), lambda b, t, c=c: (b * nt + t, OFF_S5 // LANES + c))
              for c in range(S5_WIDTH // LANES)],
            pl.BlockSpec(bh.shape, const3),
            pl.BlockSpec(ch.shape, const3),
            pl.BlockSpec((S5_TM, S5_NSTATE), const),
            pl.BlockSpec((S5_TM, S5_NSTATE), const),
            pl.BlockSpec((1, S5_WIDTH), const),
            pl.BlockSpec((S5_WIDTH, S5_WIDTH), const),
            pl.BlockSpec((1, S5_WIDTH), const),
        ],
        out_specs=pl.BlockSpec((tm, S5_WIDTH), lambda b, t: (b * nt + t, 0)),
        out_shape=jax.ShapeDtypeStruct((m, S5_WIDTH), BF16),
        scratch_shapes=[pltpu.VMEM((tm, 2 * S5_NSTATE), F32), pltpu.VMEM((tm, 2 * S5_NSTATE), BF16),
                        pltpu.VMEM((1, S5_NSTATE), F32), pltpu.VMEM((1, S5_NSTATE), F32),
                        pltpu.VMEM((8, S5_HALF), F32), pltpu.VMEM((8, S5_HALF), F32),
                        pltpu.VMEM((S5_WIDTH // LANES * tm, LANES), F32)],
        compiler_params=_cparams(("parallel", "arbitrary")),
        name="s5_mixer",
    )(z, z, z, z, bh, ch, pwr, pwi, d, wglu, bglu)


def _rms(x, g):
    return x * lax.rsqrt(jnp.mean(x * x, axis=-1, keepdims=True) + RMS_EPS) * g


def _mla_prep_kernel(cq_ref, ckv_ref, kr_ref, krr_ref, cos_ref, sin_ref, qg_ref, kvg_ref,
                     wq1_ref, wq2_ref, wkv_ref, q_ref, k_ref, vt_ref):
    cos = cos_ref[...]
    sin = sin_ref[...]
    scale = (MLA_NOPE + MLA_ROPE) ** -0.5 * LOG2E
    ones = jnp.ones((VT_ROWS - MLA_V, cos.shape[0]), BF16)
    cqn = _rms(cq_ref[...], qg_ref[...]).astype(BF16)
    z1 = jnp.dot(cqn, wq1_ref[...], preferred_element_type=F32)
    z2 = jnp.dot(cqn, wq2_ref[...], preferred_element_type=F32)
    ckvn = _rms(ckv_ref[...], kvg_ref[...]).astype(BF16)
    zkv = jnp.dot(ckvn, wkv_ref[...], preferred_element_type=F32)
    k_rope = (kr_ref[...] * cos + krr_ref[...] * sin).astype(BF16)
    for h in range(MLA_HEADS):
        a = h * MLA_QK_PAD
        q_nope = z1[:, a:a + MLA_NOPE]
        q_rope = z1[:, a + MLA_NOPE:a + MLA_QK_PAD] * cos + z2[:, h * LANES:(h + 1) * LANES] * sin
        q_ref[0, h, :, 0:MLA_NOPE] = (q_nope * scale).astype(BF16)
        q_ref[0, h, :, MLA_NOPE:MLA_QK_PAD] = (q_rope * scale).astype(BF16)
        k_ref[0, h, :, 0:MLA_NOPE] = zkv[:, a:a + MLA_NOPE].astype(BF16)
        k_ref[0, h, :, MLA_NOPE:MLA_QK_PAD] = k_rope
        vt_ref[0, 0, h, 0:MLA_V, :] = jnp.transpose(zkv[:, a + MLA_NOPE:a + MLA_NOPE + MLA_V]).astype(BF16)
        vt_ref[0, 0, h, MLA_V:VT_ROWS, :] = ones


def _softmax_step_t(st, vt, m_ref, acc_ref, idx, shift=None):
    m_old = m_ref[idx]
    cur = jnp.max(st, axis=0, keepdims=True)
    if shift is not None:
        cur = cur + shift
    m_new = jnp.maximum(m_old, cur)
    p = jnp.exp2(st - (m_new if shift is None else m_new - shift)).astype(BF16)
    corr = jnp.exp2(m_old - m_new)
    acc_ref[idx] = corr * acc_ref[idx] + jnp.dot(vt, p, preferred_element_type=F32)
    m_ref[idx] = m_new


def _causal_mask_t(t):
    k = lax.broadcasted_iota(jnp.int32, (t, t), 0)
    q = lax.broadcasted_iota(jnp.int32, (t, t), 1)
    return q >= k


def _finish_t(acc):
    return jnp.transpose(acc[0:MLA_V] / acc[MLA_V:MLA_V + 1])


def _causal_tiles(nb):
    pairs = [(qi, ki) for qi in range(nb) for ki in range(qi + 1)]
    return (jnp.asarray([p[0] for p in pairs], jnp.int32), jnp.asarray([p[1] for p in pairs], jnp.int32))


def _mla_flash_kernel(qi_ref, ki_ref, q_ref, k_ref, vt_ref, o_ref, m_ref, acc_ref, *, t):
    qi = qi_ref[pl.program_id(1)]
    ki = ki_ref[pl.program_id(1)]

    @pl.when(ki == 0)
    def _():
        m_ref[...] = jnp.full_like(m_ref, NEG_INF)
        acc_ref[...] = jnp.zeros_like(acc_ref)

    def step(masked):
        mask = _causal_mask_t(t) if masked else None
        def scores(h):
            st = lax.dot_general(k_ref[0, h], q_ref[0, h], (((1,), (1,)), ((), ())), preferred_element_type=F32)
            return jnp.where(mask, st, NEG_INF) if masked else st

        sts = [scores(h) for h in range(MLA_HEADS)]
        for h in range(MLA_HEADS):
            _softmax_step_t(sts[h], vt_ref[0, 0, h], m_ref, acc_ref, h)

    @pl.when(ki < qi)
    def _():
        step(False)

    @pl.when(ki == qi)
    def _():
        step(True)
        for h in range(MLA_HEADS):
            o_ref[0, :, h * MLA_V:(h + 1) * MLA_V] = _finish_t(acc_ref[h]).astype(BF16)


def _mla_flash(q, k, vt, *, t=512):
    bsz, nh, n_pos, _ = q.shape
    t = min(t, n_pos)
    nb = n_pos // t
    qi_tab, ki_tab = _causal_tiles(nb)
    return pl.pallas_call(
        functools.partial(_mla_flash_kernel, t=t),
        grid_spec=pltpu.PrefetchScalarGridSpec(
            num_scalar_prefetch=2,
            grid=(bsz, qi_tab.shape[0]),
            in_specs=[
                pl.BlockSpec((1, nh, t, MLA_QK_PAD), lambda b, s, qt, kt: (b, 0, qt[s], 0)),
                pl.BlockSpec((1, nh, t, MLA_QK_PAD), lambda b, s, qt, kt: (b, 0, kt[s], 0)),
                pl.BlockSpec((1, 1, nh, VT_ROWS, t), lambda b, s, qt, kt: (b, kt[s], 0, 0, 0)),
            ],
            out_specs=pl.BlockSpec((1, t, nh * MLA_V), lambda b, s, qt, kt: (b, qt[s], 0)),
            scratch_shapes=[pltpu.VMEM((nh, 1, t), F32), pltpu.VMEM((nh, VT_ROWS, t), F32)],
        ),
        out_shape=jax.ShapeDtypeStruct((bsz, n_pos, nh * MLA_V), BF16),
        compiler_params=_cparams(("parallel", "arbitrary")),
        name="mla_flash",
    )(qi_tab, ki_tab, q, k, vt)


def _diff_prep_kernel(q_ref, k_ref, v_ref, qm_ref, kb_ref, vt_ref):
    hw = 2 * DIFF_QK
    tm = q_ref.shape[0]
    lane = lax.broadcasted_iota(jnp.int32, (tm, hw), 1)
    scale = DIFF_QK ** -0.5 * LOG2E
    ones = jnp.ones((VT_ROWS - DIFF_V, tm), BF16)
    kb_ref[...] = k_ref[...].astype(BF16)
    for h in range(DIFF_HEADS):
        qh = q_ref[:, h * hw:(h + 1) * hw] * scale
        qm_ref[0, 2 * h] = jnp.where(lane < DIFF_QK, qh, 0.0).astype(BF16)
        qm_ref[0, 2 * h + 1] = jnp.where(lane >= DIFF_QK, qh, 0.0).astype(BF16)
        vt_ref[0, 0, h, 0:DIFF_V, :] = jnp.transpose(v_ref[:, h * DIFF_V:(h + 1) * DIFF_V]).astype(BF16)
        vt_ref[0, 0, h, DIFF_V:VT_ROWS, :] = ones


N_MLA_PREP_IN = 11
N_DIFF_PREP_IN = 3


def _attn_prep_kernel(*refs):
    n_in = N_MLA_PREP_IN + N_DIFF_PREP_IN
    ins, outs = refs[:n_in], refs[n_in:]
    _mla_prep_kernel(*ins[:N_MLA_PREP_IN], *outs[:3])
    _diff_prep_kernel(*ins[N_MLA_PREP_IN:], *outs[3:])


def _attn_prep(z, cos, sin, qg, kvg, wq1, wq2, wkv, *, bsz, n_pos, tm=512):
    tm = min(tm, n_pos)
    nt = n_pos // tm
    w = DIFF_HEADS * DIFF_V
    const = lambda b, t: (0, 0)
    zspec = lambda width, off: pl.BlockSpec((tm, width), lambda b, t: (b * nt + t, off // width))
    hspec = lambda width: pl.BlockSpec((1, MLA_HEADS, tm, width), lambda b, t: (b, 0, t, 0))
    slab = lambda nh: pl.BlockSpec((1, 1, nh, VT_ROWS, tm), lambda b, t: (b, t, 0, 0, 0))
    return pl.pallas_call(
        _attn_prep_kernel,
        grid=(bsz, nt),
        in_specs=[
            zspec(MLA_Q_RANK, OFF_CQ), zspec(MLA_KV_RANK, OFF_CKV), zspec(LANES, OFF_KR), zspec(LANES, OFF_KRR),
            pl.BlockSpec((tm, LANES), lambda b, t: (b * nt + t, 0)),
            pl.BlockSpec((tm, LANES), lambda b, t: (b * nt + t, 0)),
            pl.BlockSpec((1, MLA_Q_RANK), const), pl.BlockSpec((1, MLA_KV_RANK), const),
            pl.BlockSpec(wq1.shape, const), pl.BlockSpec(wq2.shape, const), pl.BlockSpec(wkv.shape, const),
            zspec(w, OFF_DQ), zspec(w, OFF_DK), zspec(w, OFF_DV),
        ],
        out_specs=[hspec(MLA_QK_PAD), hspec(MLA_QK_PAD), slab(MLA_HEADS),
                   pl.BlockSpec((1, 2 * DIFF_HEADS, tm, 2 * DIFF_QK), lambda b, t: (b, 0, t, 0)),
                   pl.BlockSpec((tm, w), lambda b, t: (b * nt + t, 0)),
                   slab(DIFF_HEADS)],
        out_shape=[jax.ShapeDtypeStruct((bsz, MLA_HEADS, n_pos, MLA_QK_PAD), BF16),
                   jax.ShapeDtypeStruct((bsz, MLA_HEADS, n_pos, MLA_QK_PAD), BF16),
                   jax.ShapeDtypeStruct((bsz, nt, MLA_HEADS, VT_ROWS, tm), BF16),
                   jax.ShapeDtypeStruct((bsz, 2 * DIFF_HEADS, n_pos, 2 * DIFF_QK), BF16),
                   jax.ShapeDtypeStruct((bsz * n_pos, w), BF16),
                   jax.ShapeDtypeStruct((bsz, nt, DIFF_HEADS, VT_ROWS, tm), BF16)],
        compiler_params=_cparams(("parallel", "parallel")),
        name="attn_prep",
    )(z, z, z, z, cos, sin, qg, kvg, wq1, wq2, wkv, z, z, z)


def _diff_kernel(qi_ref, ki_ref, qmin_ref, kmax_ref, qm_ref, k_ref, vt_ref, pq_ref, pk_ref, tab_ref,
                 lq1_ref, lk1_ref, lq2_ref, lk2_ref, sg_ref, o_ref, m_ref, acc_ref, *, t, tiles_per_row, lambda_init):
    qi = qi_ref[pl.program_id(1)]
    ki = ki_ref[pl.program_id(1)]
    nh = DIFF_HEADS
    hw = 2 * DIFF_QK

    @pl.when(ki == 0)
    def _():
        m_ref[...] = jnp.full_like(m_ref, NEG_INF)
        acc_ref[...] = jnp.zeros_like(acc_ref)

    nc = t // LANES
    qbase = (pl.program_id(0) * tiles_per_row + qi) * nc
    kbase = (pl.program_id(0) * tiles_per_row + ki) * nc

    def is_far(r, c):
        return qmin_ref[qbase + c] - kmax_ref[kbase + r] >= T5_TABLE - 1

    def all_of(pairs):
        out = None
        for r, c in pairs:
            out = is_far(r, c) if out is None else jnp.logical_and(out, is_far(r, c))
        return out

    def step(plan):
        sub = lambda r, c: (slice(r * LANES, (r + 1) * LANES), slice(c * LANES, (c + 1) * LANES))
        dist, local_mask = {}, None
        for r in range(nc):
            for c in range(nc):
                if plan[r][c] in ("gather", "diag"):
                    rows, cols = sub(r, c)
                    dist[r, c] = jnp.clip(pq_ref[0, :, cols] - pk_ref[rows, :], 0, T5_TABLE - 1)
                if plan[r][c] == "diag":
                    local_mask = _causal_mask_t(LANES)
        sts = []
        for h in range(nh):
            kh = k_ref[:, h * hw:(h + 1) * hw]
            shift = tab_ref[h:h + 1, T5_TABLE - 1:T5_TABLE]
            tab = jnp.broadcast_to(tab_ref[h:h + 1, :], (LANES, T5_TABLE))
            delta = {}
            for (r, c), d in dist.items():
                dl = jnp.take_along_axis(tab, d, axis=1, mode="promise_in_bounds") - shift
                delta[r, c] = jnp.where(local_mask, dl, NEG_INF) if plan[r][c] == "diag" else dl
            for mp in range(2):
                raw = lax.dot_general(kh, qm_ref[0, 2 * h + mp], (((1,), (1,)), ((), ())),
                                      preferred_element_type=F32)
                if delta or any("neg" in row for row in plan):
                    blocks = []
                    for r in range(nc):
                        row = []
                        for c in range(nc):
                            rows, cols = sub(r, c)
                            if plan[r][c] == "neg":
                                row.append(jnp.full((LANES, LANES), NEG_INF, F32))
                            elif (r, c) in delta:
                                row.append(raw[rows, cols] + delta[r, c])
                            else:
                                row.append(raw[rows, cols])
                        blocks.append(jnp.concatenate(row, axis=1))
                    raw = jnp.concatenate(blocks, axis=0)
                sts.append(raw)
        for j in range(2 * nh):
            shift = tab_ref[j // 2:j // 2 + 1, T5_TABLE - 1:T5_TABLE]
            _softmax_step_t(sts[j], vt_ref[0, 0, j // 2], m_ref, acc_ref, j, shift)

    every = [(r, c) for r in range(nc) for c in range(nc)]
    corner = (nc - 1, 0)
    plan_far = [["const"] * nc for _ in range(nc)]
    plan_corner = [["gather" if (r, c) == corner else "const" for c in range(nc)] for r in range(nc)]
    plan_full = [["gather"] * nc for _ in range(nc)]
    diag_kind = lambda r, c, beyond: "neg" if c < r else "diag" if c == r else "gather" if c == r + 1 else beyond
    plan_band = [[diag_kind(r, c, "const") for c in range(nc)] for r in range(nc)]
    plan_diag = [[diag_kind(r, c, "gather") for c in range(nc)] for r in range(nc)]

    below = ki < qi
    all_far = all_of(every)
    corner_far = all_of([p for p in every if p != corner])
    band_far = all_of([(r, c) for r, c in every if c >= r + 2])

    @pl.when(jnp.logical_and(below, all_far))
    def _():
        step(plan_far)

    @pl.when(jnp.logical_and(below, jnp.logical_and(corner_far, jnp.logical_not(all_far))))
    def _():
        step(plan_corner)

    @pl.when(jnp.logical_and(below, jnp.logical_not(corner_far)))
    def _():
        step(plan_full)

    @pl.when(jnp.logical_and(ki == qi, band_far))
    def _():
        step(plan_band)

    @pl.when(jnp.logical_and(ki == qi, jnp.logical_not(band_far)))
    def _():
        step(plan_diag)

    @pl.when(ki == qi)
    def _():
        lam =(jnp.exp(jnp.sum(lq1_ref[...] * lk1_ref[...], axis=-1, keepdims=True))
               - jnp.exp(jnp.sum(lq2_ref[...] * lk2_ref[...], axis=-1, keepdims=True)) + lambda_init)
        for h in range(nh):
            o = _finish_t(acc_ref[2 * h]) - lam * _finish_t(acc_ref[2 * h + 1])
            o = _rms(o, sg_ref[...]) * (1.0 - lambda_init)
            o_ref[:, h * DIFF_V:(h + 1) * DIFF_V] = o.astype(BF16)


def _diff_attn(qm, kb, vt, pos_q, pos_k, table, lq1, lk1, lq2, lk2, sg, *, bsz, n_pos, lambda_init, t=512):
    t = min(t, n_pos)
    nb = n_pos // t
    w = DIFF_HEADS * DIFF_V
    const = lambda b, s, qt, kt, qmin, kmax: (0, 0)
    qi_tab, ki_tab = _causal_tiles(nb)
    chunks = pos_q.reshape(bsz * n_pos // LANES, LANES)
    qmin = jnp.min(chunks, axis=1)
    kmax = jnp.max(chunks, axis=1)
    return pl.pallas_call(
        functools.partial(_diff_kernel, t=t, tiles_per_row=nb, lambda_init=lambda_init),
        grid_spec=pltpu.PrefetchScalarGridSpec(
            num_scalar_prefetch=4,
            grid=(bsz, qi_tab.shape[0]),
            in_specs=[
                pl.BlockSpec((1, 2 * DIFF_HEADS, t, 2 * DIFF_QK), lambda b, s, qt, kt, qmin, kmax: (b, 0, qt[s], 0)),
                pl.BlockSpec((t, w), lambda b, s, qt, kt, qmin, kmax: (b * nb + kt[s], 0)),
                pl.BlockSpec((1, 1, DIFF_HEADS, VT_ROWS, t), lambda b, s, qt, kt, qmin, kmax: (b, kt[s], 0, 0, 0)),
                pl.BlockSpec((1, 1, t), lambda b, s, qt, kt, qmin, kmax: (b * nb + qt[s], 0, 0)),
                pl.BlockSpec((t, LANES), lambda b, s, qt, kt, qmin, kmax: (b * nb + kt[s], 0)),
                pl.BlockSpec((DIFF_HEADS, T5_TABLE), const),
                pl.BlockSpec((1, DIFF_QK), const), pl.BlockSpec((1, DIFF_QK), const),
                pl.BlockSpec((1, DIFF_QK), const), pl.BlockSpec((1, DIFF_QK), const),
                pl.BlockSpec((1, DIFF_V), const),
            ],
            out_specs=pl.BlockSpec((t, w), lambda b, s, qt, kt, qmin, kmax: (b * nb + qt[s], 0)),
            scratch_shapes=[pltpu.VMEM((2 * DIFF_HEADS, 1, t), F32),
                            pltpu.VMEM((2 * DIFF_HEADS, VT_ROWS, t), F32)],
        ),
        out_shape=jax.ShapeDtypeStruct((bsz * n_pos, w), BF16),
        compiler_params=_cparams(("parallel", "arbitrary")),
        name="diff_attn",
    )(qi_tab, ki_tab, qmin, kmax, qm, kb, vt, pos_q, pos_k, table, lq1, lk1, lq2, lk2, sg)


def _ret_kernel(q_ref, qr_ref, k_ref, kr_ref, v_ref, g_ref, cos_ref, sin_ref, o_ref, st_ref, *, tm):
    t = pl.program_id(1)
    c = RET_CHUNK
    nh = RET_HEADS
    w = nh * RET_QK

    @pl.when(t == 0)
    def _():
        st_ref[...] = jnp.zeros_like(st_ref)

    log_gamma = [math.log(1.0 - 2.0 ** (-5.0 - h)) for h in range(nh)]
    lane = lax.broadcasted_iota(jnp.int32, (1, w), 1)
    lg_lane = jnp.zeros((1, w), F32)
    for h in range(nh):
        lg_lane = jnp.where(lane // RET_QK == h, log_gamma[h], lg_lane)
    tok = lax.broadcasted_iota(jnp.int32, (c, 1), 0).astype(F32)
    q_decay = jnp.exp(lg_lane * (tok + 1.0))
    k_decay = jnp.exp(lg_lane * (c - 1.0 - tok))
    ri = lax.broadcasted_iota(jnp.int32, (c, c), 0)
    ci = lax.broadcasted_iota(jnp.int32, (c, c), 1)
    rel = (ri - ci).astype(F32)
    intra = [jnp.where(rel >= 0, jnp.exp(log_gamma[h] * jnp.maximum(rel, 0.0)), 0.0) for h in range(nh)]
    head_lanes = [(lane // RET_QK == h) for h in range(nh)]

    for j in range(tm // c):
        rows = slice(j * c, (j + 1) * c)
        cos = jnp.concatenate([cos_ref[rows, :]] * (w // LANES), axis=1)
        sin = jnp.concatenate([sin_ref[rows, :]] * (w // LANES), axis=1)
        q = q_ref[rows, :] * cos + qr_ref[rows, :] * sin
        k = (k_ref[rows, :] * cos + kr_ref[rows, :] * sin) * (RET_QK ** -0.5)
        kb = k.astype(BF16)
        qd = q * q_decay
        kdt = jnp.transpose(k * k_decay).astype(BF16)
        for h in range(nh):
            vh = v_ref[rows, h * RET_V:(h + 1) * RET_V].astype(BF16)
            qh = jnp.where(head_lanes[h], q, 0.0).astype(BF16)
            scores = lax.dot_general(qh, kb, (((1,), (1,)), ((), ())), preferred_element_type=F32) * intra[h]
            inner = jnp.dot(scores.astype(BF16), vh, preferred_element_type=F32)
            qdh = jnp.where(head_lanes[h], qd, 0.0).astype(BF16)
            state = st_ref[...]
            cross = jnp.dot(qdh, state.astype(BF16), preferred_element_type=F32)
            o = inner + cross
            mu = jnp.mean(o, axis=-1, keepdims=True)
            oc = o - mu
            var = jnp.mean(oc * oc, axis=-1, keepdims=True)
            o = oc * lax.rsqrt(var + LN_EPS)
            gh = g_ref[rows, h * RET_V:(h + 1) * RET_V]
            o_ref[rows, h * RET_V:(h + 1) * RET_V] = (gh * jax.nn.sigmoid(gh) * o).astype(BF16)
            hs = slice(h * RET_QK, (h + 1) * RET_QK)
            kv = jnp.dot(kdt[hs, :], vh, preferred_element_type=F32)
            st_ref[hs, :] = state[hs, :] * math.exp(log_gamma[h] * c) + kv


def _retention(z, cos, sin, *, bsz, n_pos, tm=512):
    tm = min(tm, n_pos)
    nt = n_pos // tm
    w = RET_HEADS * RET_QK
    wv = RET_HEADS * RET_V
    zspec = lambda width, off: pl.BlockSpec((tm, width), lambda b, t: (b * nt + t, off // width))
    tspec = pl.BlockSpec((tm, LANES), lambda b, t: (b * nt + t, 0))
    return pl.pallas_call(
        functools.partial(_ret_kernel, tm=tm),
        grid=(bsz, nt),
        in_specs=[zspec(w, OFF_RQ), zspec(w, OFF_RQR), zspec(w, OFF_RK), zspec(w, OFF_RKR),
                  zspec(wv, OFF_RV), zspec(wv, OFF_RG), tspec, tspec],
        out_specs=pl.BlockSpec((tm, wv), lambda b, t: (b * nt + t, 0)),
        out_shape=jax.ShapeDtypeStruct((bsz * n_pos, wv), BF16),
        scratch_shapes=[pltpu.VMEM((w, RET_V), F32)],
        compiler_params=_cparams(("parallel", "arbitrary")),
        name="retention",
    )(z, z, z, z, z, z, cos, sin)


def _outproj_ln_kernel(y0_ref, y1_ref, y2_ref, y3_ref, w_ref, x_ref, g_ref, b_ref, o_ref):
    acc = None
    for j, y_ref in enumerate((y0_ref, y1_ref, y2_ref, y3_ref)):
        part = jnp.dot(y_ref[...], w_ref[j * 512:(j + 1) * 512, :], preferred_element_type=F32)
        acc = part if acc is None else acc + part
    o_ref[...] = _layer_norm(ALPHA * x_ref[...] + acc, g_ref[...], b_ref[...])


def _outproj_ln(ys, w, x, g, b, *, tm=512):
    m, d = x.shape
    tm = min(tm, m)
    yspec = pl.BlockSpec((tm, 512), lambda i: (i, 0))
    return pl.pallas_call(
        _outproj_ln_kernel,
        grid=(m // tm,),
        in_specs=[yspec, yspec, yspec, yspec,
                  pl.BlockSpec(w.shape, lambda i: (0, 0)),
                  pl.BlockSpec((tm, d), lambda i: (i, 0)),
                  pl.BlockSpec((1, d), lambda i: (0, 0)),
                  pl.BlockSpec((1, d), lambda i: (0, 0))],
        out_specs=pl.BlockSpec((tm, d), lambda i: (i, 0)),
        out_shape=jax.ShapeDtypeStruct((m, d), F32),
        compiler_params=_cparams(("parallel",)),
        name="outproj_ln",
    )(*ys, w, x, g, b)


def _rot_cols(w, heads, dim):
    k = w.shape[0]
    w = w.reshape(k, heads, 2, dim // 2)
    return jnp.concatenate([-w[:, :, 1], w[:, :, 0]], axis=-1).reshape(k, heads * dim)


def _pad_cols(w, width):
    return jnp.pad(w, ((0, 0), (0, width - w.shape[1])))


def _wide_w_in(w_in):
    sizes = (S5_WIDTH, MLA_Q_RANK, MLA_KV_RANK, MLA_ROPE,
             RET_HEADS * RET_QK, RET_HEADS * RET_QK, RET_HEADS * RET_V, RET_HEADS * RET_V,
             DIFF_HEADS * 2 * DIFF_QK, DIFF_HEADS * 2 * DIFF_QK, DIFF_HEADS * DIFF_V)
    offs = np.concatenate([[0], np.cumsum(sizes)])
    (s5_u, cq, ckv, kr, rq, rk, rv, rg, dq, dk, dv) = [w_in[:, offs[i]:offs[i + 1]] for i in range(len(sizes))]
    cols = [s5_u, cq, rv, rg, dq, dk, dv,
            rq, _rot_cols(rq, RET_HEADS, RET_QK), rk, _rot_cols(rk, RET_HEADS, RET_QK),
            ckv, _pad_cols(kr, LANES), _pad_cols(_rot_cols(kr, 1, MLA_ROPE), LANES),
            jnp.zeros((w_in.shape[0], IN_WIDE - IN_USED), w_in.dtype)]
    return jnp.concatenate(cols, axis=1).astype(BF16)


def _mla_q_weights(w_uq):
    k = w_uq.shape[0]
    w = w_uq.reshape(k, MLA_HEADS, MLA_NOPE + MLA_ROPE)
    rope = w[:, :, MLA_NOPE:]
    pad = MLA_QK_PAD - MLA_NOPE - MLA_ROPE
    w1 = jnp.pad(w, ((0, 0), (0, 0), (0, pad))).reshape(k, MLA_HEADS * MLA_QK_PAD)
    rot = _rot_cols(rope.reshape(k, MLA_HEADS * MLA_ROPE), MLA_HEADS, MLA_ROPE).reshape(k, MLA_HEADS, MLA_ROPE)
    w2 = jnp.pad(rot, ((0, 0), (0, 0), (0, LANES - MLA_ROPE))).reshape(k, MLA_HEADS * LANES)
    return w1.astype(BF16), w2.astype(BF16)


def kernel(x, p, positions, rel_bias, ffn1_w_gate, ffn1_w_up, ffn1_w_down, ln1_g, ln1_b, w_in, w_out, ln2_g, ln2_b, s5_lambda_re, s5_lambda_im, s5_log_dt, s5_b_re, s5_b_im, s5_c_re, s5_c_im, s5_d, s5_w_glu, s5_b_glu, mla_q_norm_g, mla_w_uq, mla_kv_norm_g, mla_w_ukv, diff_lambda_q1, diff_lambda_k1, diff_lambda_q2, diff_lambda_k2, diff_subln_g, ffn2_w_gate, ffn2_w_up, ffn2_w_down, ple_w_gate, ple_b_gate, ple_w_proj, ln3_g, ln3_b):
    bsz, n_pos, d = x.shape
    m = bsz * n_pos
    depth = ffn1_w_gate.shape[0]
    xf = x.reshape(m, d)
    row = lambda v: v.reshape(1, -1)

    cos, sin = _rope_tables(positions.reshape(m, 1))
    table = _bias_table(rel_bias)
    t_diff = min(512, n_pos)
    pos_q = positions.reshape(bsz * (n_pos // t_diff), 1, t_diff)
    pos_k = jnp.broadcast_to(positions.reshape(m, 1), (m, LANES))

    for i in range(depth):
        f1g, f1u, f1d = _cast_layer(i, ffn1_w_gate, ffn1_w_up, ffn1_w_down)
        f2g, f2u, f2d, pwg, wo = _cast_layer(i, ffn2_w_gate, ffn2_w_up, ffn2_w_down, ple_w_gate, w_out)
        xf = _ffn_ln(xf, f1g, f1u, f1d, row(ln1_g[i]), row(ln1_b[i]))
        z = _inproj(xf, _wide_w_in(w_in[i]))

        apr, api, bbr, bbi = _s5_discretise(s5_lambda_re[i], s5_lambda_im[i], s5_log_dt[i], s5_b_re[i], s5_b_im[i])
        bh = jnp.concatenate([_block_diag_in(bbr), _block_diag_in(bbi)], axis=2).astype(BF16)
        ch = jnp.concatenate([_block_diag_out(s5_c_re[i]), -_block_diag_out(s5_c_im[i])], axis=1).astype(BF16)
        apr = jnp.repeat(apr, S5_TM // S5_STEPS, axis=0)
        api = jnp.repeat(api, S5_TM // S5_STEPS, axis=0)
        y_s5 = _s5_mixer(z, bh, ch, apr, api, row(s5_d[i]), s5_w_glu[i].astype(BF16), row(s5_b_glu[i]),
                         bsz=bsz, n_pos=n_pos)

        wq1, wq2 = _mla_q_weights(mla_w_uq[i])
        q, k, vt, dqm, dkb, dvt = _attn_prep(z, cos, sin, row(mla_q_norm_g[i]), row(mla_kv_norm_g[i]), wq1, wq2,
                                             mla_w_ukv[i].astype(BF16), bsz=bsz, n_pos=n_pos)
        y_mla = _mla_flash(q, k, vt).reshape(m, MLA_HEADS * MLA_V)

        y_ret = _retention(z, cos, sin, bsz=bsz, n_pos=n_pos)

        lambda_init = 0.8 - 0.6 * math.exp(-0.3 * i)
        y_diff = _diff_attn(dqm, dkb, dvt, pos_q, pos_k, table, row(diff_lambda_q1[i]), row(diff_lambda_k1[i]),
                            row(diff_lambda_q2[i]), row(diff_lambda_k2[i]), row(diff_subln_g[i]),
                            bsz=bsz, n_pos=n_pos, lambda_init=lambda_init, t=t_diff)

        xf = _outproj_ln((y_s5, y_mla, y_ret, y_diff), wo, xf, row(ln2_g[i]), row(ln2_b[i]))

        res, xb = _ple(xf, p.reshape(depth, m, PLE_DIM), i, pwg, row(ple_b_gate[i]), ple_w_proj[i].astype(BF16))
        xf = _ffn_ln(xb, f2g, f2u, f2d, row(ln3_g[i]), row(ln3_b[i]), res)
    return xf.reshape(bsz, n_pos, d)
```

```python
import functools
import math

import numpy as np
import jax
import jax.numpy as jnp
from jax import lax
from jax.experimental import pallas as pl
from jax.experimental.pallas import tpu as pltpu

F32 = jnp.float32
BF16 = jnp.bfloat16

LANES = 128
SUBLANES = 8
VMEM_BYTES = 64 * 1024 * 1024
VMEM_LIMIT_BYTES = VMEM_BYTES - 8 * 1024 * 1024
FFN_VMEM_LIMIT_BYTES = VMEM_BYTES - 4 * 1024 * 1024

D_MODEL = 2048
DEPTH = 2
PLE_DIM = 256
D_FF = 5632
ALPHA = (2 * DEPTH) ** 0.25
ROPE_THETA = 10000.0
NEG_INF = -1e30
LN_EPS = 1e-5
RMS_EPS = 1e-6

S5_WIDTH = 512
S5_GROUP = 16
S5_GROUPS = 32
S5_STATE = 64
S5_NSTATE = S5_GROUPS * S5_STATE
S5_HALF = S5_NSTATE // 2
S5_TM = 512
S5_STEPS = S5_TM // SUBLANES

MLA_HEADS = 4
MLA_Q_RANK = 512
MLA_KV_RANK = 128
MLA_NOPE = 128
MLA_ROPE = 64
MLA_V = 128
MLA_QK_PAD = 256

RET_HEADS = 4
RET_QK = 64
RET_V = 128
RET_CHUNK = 128

DIFF_HEADS = 4
DIFF_QK = 64
DIFF_V = 128

T5_BUCKETS = 32
T5_MAX_DIST = 128
T5_TABLE = 128

LOG2E = math.log2(math.e)
VT_ROWS = 144
ATTN_TILE = 512
FFN_TM = 1024

OFF_S5 = 0
OFF_CQ = 512
OFF_RV = 1024
OFF_RG = 1536
OFF_DQ = 2048
OFF_DK = 2560
OFF_DV = 3072
OFF_RQ = 3584
OFF_RQR = 3840
OFF_RK = 4096
OFF_RKR = 4352
OFF_CKV = 4608
OFF_KR = 4736
OFF_KRR = 4864
IN_USED = 4992
IN_WIDE = 5120
IN_TN = 2560


def _cparams(sem, vmem_limit_bytes=VMEM_LIMIT_BYTES):
    return pltpu.CompilerParams(dimension_semantics=sem, vmem_limit_bytes=vmem_limit_bytes)


def _layer_norm(y, g, b):
    mu = jnp.mean(y, axis=-1, keepdims=True)
    yc = y - mu
    var = jnp.mean(yc * yc, axis=-1, keepdims=True)
    return yc * lax.rsqrt(var + LN_EPS) * g + b


CAST_STEPS = 16


def _cast_kernel(*refs):
    n = len(refs) // 2
    for src, dst in zip(refs[:n], refs[n:]):
        dst[...] = src[0].astype(BF16)


def _cast_layer(layer, *stacked):
    in_specs, out_specs, out_shape = [], [], []
    for w in stacked:
        _, r, c = w.shape
        tr = r // CAST_STEPS
        in_specs.append(pl.BlockSpec((1, tr, c), lambda s: (layer, s, 0)))
        out_specs.append(pl.BlockSpec((tr, c), lambda s: (s, 0)))
        out_shape.append(jax.ShapeDtypeStruct((r, c), BF16))
    return pl.pallas_call(
        _cast_kernel,
        grid=(CAST_STEPS,),
        in_specs=in_specs,
        out_specs=out_specs,
        out_shape=out_shape,
        compiler_params=_cparams(("parallel",)),
        name="cast_weights",
    )(*stacked)


FFN_ROW_CHUNK = 128


def _ffn_ln_kernel(*refs, nf, tm, has_res):
    if has_res:
        xb_ref, res_ref, wg_ref, wu_ref, wd_ref, g_ref, b_ref, o_ref = refs
    else:
        res_ref, wg_ref, wu_ref, wd_ref, g_ref, b_ref, o_ref, xb_ref = refs
    f = pl.program_id(1)
    chunks = tm // FFN_ROW_CHUNK

    def rows(c):
        return pl.ds(pl.multiple_of(c * FFN_ROW_CHUNK, FFN_ROW_CHUNK), FFN_ROW_CHUNK)

    @pl.when(f == 0)
    def _():
        o_ref[...] = jnp.zeros_like(o_ref)
        if not has_res:
            def cast(c, _):
                xb_ref[rows(c), :] = res_ref[rows(c), :].astype(BF16)
                return 0
            lax.fori_loop(0, chunks, cast, 0)

    xb = xb_ref[...]
    gate = jnp.dot(xb, wg_ref[...], preferred_element_type=F32)
    up = jnp.dot(xb, wu_ref[...], preferred_element_type=F32)
    h = gate * jax.nn.sigmoid(gate) * up
    o_ref[...] += jnp.dot(h.astype(BF16), wd_ref[...], preferred_element_type=F32)

    @pl.when(f == nf - 1)
    def _():
        def norm(c, _):
            res = res_ref[rows(c), :]
            y = (res if has_res else ALPHA * res) + 0.5 * o_ref[rows(c), :]
            o_ref[rows(c), :] = _layer_norm(y, g_ref[...], b_ref[...])
            return 0
        lax.fori_loop(0, chunks, norm, 0)


def _ffn_ln(x, wg, wu, wd, g, b, res=None, *, tm=FFN_TM, tf=512):
    m, d = x.shape
    f_dim = wg.shape[1]
    tm = min(tm, m)
    nf = f_dim // tf
    xspec = pl.BlockSpec((tm, d), lambda i, f: (i, 0))
    in_specs = [xspec] * (2 if res is not None else 1) + [
        pl.BlockSpec((d, tf), lambda i, f: (0, f)),
        pl.BlockSpec((d, tf), lambda i, f: (0, f)),
        pl.BlockSpec((tf, d), lambda i, f: (f, 0)),
        pl.BlockSpec((1, d), lambda i, f: (0, 0)),
        pl.BlockSpec((1, d), lambda i, f: (0, 0)),
    ]
    args = ([x, res] if res is not None else [x]) + [wg, wu, wd, g, b]
    return pl.pallas_call(
        functools.partial(_ffn_ln_kernel, nf=nf, tm=tm, has_res=res is not None),
        grid=(m // tm, nf),
        in_specs=in_specs,
        out_specs=pl.BlockSpec((tm, d), lambda i, f: (i, 0)),
        out_shape=jax.ShapeDtypeStruct((m, d), F32),
        scratch_shapes=[] if res is not None else [pltpu.VMEM((tm, d), BF16)],
        compiler_params=_cparams(("parallel", "arbitrary"), FFN_VMEM_LIMIT_BYTES),
        name="ffn_ln",
    )(*args)


def _ple_kernel(x_ref, p_ref, wg_ref, bg_ref, wp_ref, res_ref, xb_ref):
    x = x_ref[...]
    xb = x.astype(BF16)
    xb_ref[...] = xb
    pb = p_ref[...].astype(BF16)
    gate = jax.nn.sigmoid(jnp.dot(xb, wg_ref[...], preferred_element_type=F32) + bg_ref[...])
    res_ref[...] = ALPHA * x + gate * jnp.dot(pb, wp_ref[...], preferred_element_type=F32)


def _ple(x, p, layer, wg, bg, wp, *, tm=512):
    m, d = x.shape
    tm = min(tm, m)
    return pl.pallas_call(
        _ple_kernel,
        grid=(m // tm,),
        in_specs=[
            pl.BlockSpec((tm, d), lambda i: (i, 0)),
            pl.BlockSpec((None, tm, PLE_DIM), lambda i: (layer, i, 0)),
            pl.BlockSpec((d, d), lambda i: (0, 0)),
            pl.BlockSpec((1, d), lambda i: (0, 0)),
            pl.BlockSpec((PLE_DIM, d), lambda i: (0, 0)),
        ],
        out_specs=[pl.BlockSpec((tm, d), lambda i: (i, 0))] * 2,
        out_shape=[jax.ShapeDtypeStruct((m, d), F32), jax.ShapeDtypeStruct((m, d), BF16)],
        compiler_params=_cparams(("parallel",)),
        name="ple",
    )(x, p, wg, bg, wp)


def _inproj_kernel(x_ref, w_ref, o_ref):
    o_ref[...] = jnp.dot(x_ref[...].astype(BF16), w_ref[...], preferred_element_type=F32)


def _inproj(x, w, *, tm=512):
    m, d = x.shape
    n = w.shape[1]
    tm = min(tm, m)
    return pl.pallas_call(
        _inproj_kernel,
        grid=(n // IN_TN, m // tm),
        in_specs=[
            pl.BlockSpec((tm, d), lambda j, i: (i, 0)),
            pl.BlockSpec((d, IN_TN), lambda j, i: (0, j)),
        ],
        out_specs=pl.BlockSpec((tm, IN_TN), lambda j, i: (i, j)),
        out_shape=jax.ShapeDtypeStruct((m, n), F32),
        compiler_params=_cparams(("parallel", "parallel")),
        name="inproj",
    )(x, w)


def _rope_kernel(pos_ref, freq_ref, cos_ref, sin_ref):
    ang = pos_ref[...].astype(F32) * freq_ref[...]
    cos_ref[...] = jnp.cos(ang)
    sin_ref[...] = jnp.sin(ang)


def _rope_tables(pos_col, *, tm=1024):
    m = pos_col.shape[0]
    tm = min(tm, m)
    half = MLA_ROPE // 2
    inv = 1.0 / (ROPE_THETA ** (np.arange(0, MLA_ROPE, 2, dtype=np.float64) / MLA_ROPE))
    freq = jnp.asarray(np.tile(inv, LANES // half)[None, :], F32)
    return pl.pallas_call(
        _rope_kernel,
        grid=(m // tm,),
        in_specs=[pl.BlockSpec((tm, 1), lambda i: (i, 0)), pl.BlockSpec((1, LANES), lambda i: (0, 0))],
        out_specs=[pl.BlockSpec((tm, LANES), lambda i: (i, 0))] * 2,
        out_shape=[jax.ShapeDtypeStruct((m, LANES), F32)] * 2,
        compiler_params=_cparams(("parallel",)),
        name="rope_tables",
    )(pos_col, freq)


def _t5_bucket_static():
    n = np.arange(T5_TABLE)
    max_exact = T5_BUCKETS // 2
    nf = np.maximum(n, 1).astype(np.float64)
    large = max_exact + (np.log(nf / max_exact) / math.log(T5_MAX_DIST / max_exact)
                         * (T5_BUCKETS - max_exact)).astype(np.int64)
    large = np.minimum(large, T5_BUCKETS - 1)
    return np.where(n < max_exact, n, large)


def _bias_table_kernel(rb_ref, onehot_ref, o_ref):
    rb = rb_ref[...]
    oh = onehot_ref[...]
    rows = [jnp.sum(oh * rb[:, h:h + 1], axis=0, keepdims=True) for h in range(DIFF_HEADS)]
    o_ref[...] = jnp.concatenate(rows, axis=0) * LOG2E


def _bias_table(rel_bias):
    bucket = _t5_bucket_static()
    onehot = jnp.asarray((np.arange(T5_BUCKETS)[:, None] == bucket[None, :]).astype(np.float32))
    return pl.pallas_call(
        _bias_table_kernel,
        out_shape=jax.ShapeDtypeStruct((DIFF_HEADS, T5_TABLE), F32),
        name="t5_bias_table",
    )(rel_bias, onehot)


def _s5_disc_kernel(lr_ref, li_ref, ldt_ref, br_ref, bi_ref, apr_ref, api_ref, bbr_ref, bbi_ref):
    lr = lr_ref[...]
    li = li_ref[...]
    dt = jnp.exp(ldt_ref[...])
    k = (lax.broadcasted_iota(jnp.int32, (S5_STEPS, 1), 0) + 1).astype(F32)
    mag = jnp.exp(lr * dt * k)
    apr_ref[...] = mag * jnp.cos(li * dt * k)
    api_ref[...] = mag * jnp.sin(li * dt * k)
    mag1 = jnp.exp(lr * dt)
    ar = mag1 * jnp.cos(li * dt)
    ai = mag1 * jnp.sin(li * dt)
    den = lr * lr + li * li
    fr = ((ar - 1.0) * lr + ai * li) / den
    fi = (ai * lr - (ar - 1.0) * li) / den
    br = br_ref[...]
    bi = bi_ref[...]
    bbr_ref[...] = fr * br - fi * bi
    bbi_ref[...] = fr * bi + fi * br


def _s5_discretise(lam_re, lam_im, log_dt, b_re, b_im):
    n = S5_NSTATE
    lr = lam_re.reshape(1, n)
    li = lam_im.reshape(1, n)
    ldt = jnp.repeat(log_dt, S5_STATE).reshape(1, n)
    br = jnp.transpose(b_re, (2, 0, 1)).reshape(S5_GROUP, n)
    bi = jnp.transpose(b_im, (2, 0, 1)).reshape(S5_GROUP, n)
    return pl.pallas_call(
        _s5_disc_kernel,
        out_shape=[jax.ShapeDtypeStruct((S5_STEPS, n), F32)] * 2 + [jax.ShapeDtypeStruct((S5_GROUP, n), F32)] * 2,
        name="s5_discretise",
    )(lr, li, ldt, br, bi)


def _block_diag_in(bb):
    gh = S5_GROUPS // 2
    eye = jnp.eye(gh, dtype=bb.dtype)
    v = bb.reshape(S5_GROUP, 2, gh, S5_STATE)
    out = eye[None, :, None, :, None] * jnp.transpose(v, (1, 0, 2, 3))[:, None, :, :, :]
    return out.reshape(2, gh * S5_GROUP, S5_HALF)


def _block_diag_out(c):
    gh = S5_GROUPS // 2
    eye = jnp.eye(gh, dtype=c.dtype)
    v = jnp.transpose(c, (0, 2, 1)).reshape(2, gh, S5_STATE, S5_GROUP)
    out = v[:, :, :, None, :] * eye[None, :, None, :, None]
    return out.reshape(2, S5_HALF, gh * S5_GROUP)


def _s5_kernel(u0_ref, u1_ref, u2_ref, u3_ref, bh_ref, ch_ref, pwr_ref, pwi_ref, d_ref, wglu_ref, bglu_ref, o_ref,
               h_ref, hb_ref, cr_ref, ci_ref, cinr_ref, cini_ref, yo_ref):
    u_refs = (u0_ref, u1_ref, u2_ref, u3_ref)
    t = pl.program_id(1)
    nh = S5_HALF
    cw = S5_WIDTH // 2

    @pl.when(t == 0)
    def _():
        cr_ref[...] = jnp.zeros_like(cr_ref)
        ci_ref[...] = jnp.zeros_like(ci_ref)

    u = jnp.concatenate(
        [jnp.concatenate([r[pl.ds(j, SUBLANES, stride=S5_STEPS), :] for j in range(S5_STEPS)], axis=0)
         for r in u_refs], axis=1)
    ub = u.astype(BF16)
    ys = []
    for half in range(2):
        st = pl.ds(half * nh, nh)
        re = pl.ds(2 * half * nh, nh)
        im = pl.ds((2 * half + 1) * nh, nh)
        h_ref[:, pl.ds(2 * half * nh, 2 * nh)] = jnp.dot(ub[:, half * cw:(half + 1) * cw], bh_ref[half],
                                                          preferred_element_type=F32)
        ar = pwr_ref[0:SUBLANES, st]
        ai = pwi_ref[0:SUBLANES, st]

        def scan(j, carry, re=re, im=im, ar=ar, ai=ai):
            hr, hi = carry
            r0 = pl.multiple_of(j * SUBLANES, SUBLANES)
            nr = ar * hr - ai * hi + h_ref[pl.ds(r0, SUBLANES), re]
            ni = ar * hi + ai * hr + h_ref[pl.ds(r0, SUBLANES), im]
            h_ref[pl.ds(r0, SUBLANES), re] = nr
            h_ref[pl.ds(r0, SUBLANES), im] = ni
            return nr, ni

        zero = jnp.zeros((SUBLANES, nh), F32)
        er, ei = lax.fori_loop(0, S5_STEPS, scan, (zero, zero))

        a_seg_r = pwr_ref[S5_TM - 1:S5_TM, st]
        a_seg_i = pwi_ref[S5_TM - 1:S5_TM, st]
        cr = cr_ref[:, st]
        ci = ci_ref[:, st]
        for s in range(SUBLANES):
            cinr_ref[s:s + 1, :] = cr
            cini_ref[s:s + 1, :] = ci
            cr, ci = (er[s:s + 1] + a_seg_r * cr - a_seg_i * ci, ei[s:s + 1] + a_seg_r * ci + a_seg_i * cr)
        cr_ref[:, st] = cr
        ci_ref[:, st] = ci

        cin_r = cinr_ref[...]
        cin_i = cini_ref[...]

        def fix(jj, _, re=re, im=im, st=st, cin_r=cin_r, cin_i=cin_i):
            r0 = pl.multiple_of(jj * 2 * SUBLANES, 2 * SUBLANES)
            out_r, out_i = [], []
            for k in range(2):
                rows = pl.ds(r0 + k * SUBLANES, SUBLANES)
                pr = pwr_ref[rows, st]
                pi = pwi_ref[rows, st]
                out_r.append(h_ref[rows, re] + pr * cin_r - pi * cin_i)
                out_i.append(h_ref[rows, im] + pr * cin_i + pi * cin_r)
            hb_ref[pl.ds(r0, 2 * SUBLANES), re] = jnp.concatenate(out_r, axis=0).astype(BF16)
            hb_ref[pl.ds(r0, 2 * SUBLANES), im] = jnp.concatenate(out_i, axis=0).astype(BF16)
            return 0

        lax.fori_loop(0, S5_STEPS // 2, fix, 0)
        ys.append(jnp.dot(hb_ref[:, pl.ds(2 * half * nh, 2 * nh)], ch_ref[half], preferred_element_type=F32))

    y = jnp.concatenate(ys, axis=1) + d_ref[...] * u
    c0 = math.sqrt(2.0 / math.pi)
    y = 0.5 * y * (1.0 + jnp.tanh(c0 * (y + 0.044715 * (y * y * y))))
    gate = jax.nn.sigmoid(jnp.dot(y.astype(BF16), wglu_ref[...], preferred_element_type=F32) + bglu_ref[...])
    out = y * gate
    tm = out.shape[0]
    for c in range(S5_WIDTH // LANES):
        yo_ref[pl.ds(c * tm, tm), :] = out[:, c * LANES:(c + 1) * LANES]
    for c in range(S5_WIDTH // LANES):
        for s in range(SUBLANES):
            o_ref[s * S5_STEPS:(s + 1) * S5_STEPS, c * LANES:(c + 1) * LANES] = (
                yo_ref[pl.ds(c * tm + s, S5_STEPS, stride=SUBLANES), :].astype(BF16))


def _s5_mixer(z, bh, ch, pwr, pwi, d, wglu, bglu, *, bsz, n_pos):
    tm = S5_TM
    nt = n_pos // tm
    m = bsz * n_pos
    const = lambda b, t: (0, 0)
    const3 = lambda b, t: (0, 0, 0)
    return pl.pallas_call(
        _s5_kernel,
        grid=(bsz, nt),
        in_specs=[
            *[pl.BlockSpec((tm, LANES), lambda b, t, c=c: (b * nt + t, OFF_S5 // LANES + c))
              for c in range(S5_WIDTH // LANES)],
            pl.BlockSpec(bh.shape, const3),
            pl.BlockSpec(ch.shape, const3),
            pl.BlockSpec((S5_TM, S5_NSTATE), const),
            pl.BlockSpec((S5_TM, S5_NSTATE), const),
            pl.BlockSpec((1, S5_WIDTH), const),
            pl.BlockSpec((S5_WIDTH, S5_WIDTH), const),
            pl.BlockSpec((1, S5_WIDTH), const),
        ],
        out_specs=pl.BlockSpec((tm, S5_WIDTH), lambda b, t: (b * nt + t, 0)),
        out_shape=jax.ShapeDtypeStruct((m, S5_WIDTH), BF16),
        scratch_shapes=[pltpu.VMEM((tm, 2 * S5_NSTATE), F32), pltpu.VMEM((tm, 2 * S5_NSTATE), BF16),
                        pltpu.VMEM((1, S5_NSTATE), F32), pltpu.VMEM((1, S5_NSTATE), F32),
                        pltpu.VMEM((8, S5_HALF), F32), pltpu.VMEM((8, S5_HALF), F32),
                        pltpu.VMEM((S5_WIDTH // LANES * tm, LANES), F32)],
        compiler_params=_cparams(("parallel", "arbitrary")),
        name="s5_mixer",
    )(z, z, z, z, bh, ch, pwr, pwi, d, wglu, bglu)


def _rms(x, g):
    return x * lax.rsqrt(jnp.mean(x * x, axis=-1, keepdims=True) + RMS_EPS) * g


def _mla_prep_kernel(cq_ref, ckv_ref, kr_ref, krr_ref, cos_ref, sin_ref, qg_ref, kvg_ref,
                     wq1_ref, wq2_ref, wkv_ref, q_ref, k_ref, vt_ref):
    cos = cos_ref[...]
    sin = sin_ref[...]
    scale = (MLA_NOPE + MLA_ROPE) ** -0.5 * LOG2E
    ones = jnp.ones((VT_ROWS - MLA_V, cos.shape[0]), BF16)
    cqn = _rms(cq_ref[...], qg_ref[...]).astype(BF16)
    z1 = jnp.dot(cqn, wq1_ref[...], preferred_element_type=F32)
    z2 = jnp.dot(cqn, wq2_ref[...], preferred_element_type=F32)
    ckvn = _rms(ckv_ref[...], kvg_ref[...]).astype(BF16)
    zkv = jnp.dot(ckvn, wkv_ref[...], preferred_element_type=F32)
    k_rope = (kr_ref[...] * cos + krr_ref[...] * sin).astype(BF16)
    for h in range(MLA_HEADS):
        a = h * MLA_QK_PAD
        q_nope = z1[:, a:a + MLA_NOPE]
        q_rope = z1[:, a + MLA_NOPE:a + MLA_QK_PAD] * cos + z2[:, h * LANES:(h + 1) * LANES] * sin
        q_ref[0, h, :, 0:MLA_NOPE] = (q_nope * scale).astype(BF16)
        q_ref[0, h, :, MLA_NOPE:MLA_QK_PAD] = (q_rope * scale).astype(BF16)
        k_ref[0, h, :, 0:MLA_NOPE] = zkv[:, a:a + MLA_NOPE].astype(BF16)
        k_ref[0, h, :, MLA_NOPE:MLA_QK_PAD] = k_rope
        vt_ref[0, 0, h, 0:MLA_V, :] = jnp.transpose(zkv[:, a + MLA_NOPE:a + MLA_NOPE + MLA_V]).astype(BF16)
        vt_ref[0, 0, h, MLA_V:VT_ROWS, :] = ones


def _softmax_step_t(st, vt, m_ref, acc_ref, idx, shift=None):
    m_old = m_ref[idx]
    cur = jnp.max(st, axis=0, keepdims=True)
    if shift is not None:
        cur = cur + shift
    m_new = jnp.maximum(m_old, cur)
    p = jnp.exp2(st - (m_new if shift is None else m_new - shift)).astype(BF16)
    corr = jnp.exp2(m_old - m_new)
    acc_ref[idx] = corr * acc_ref[idx] + jnp.dot(vt, p, preferred_element_type=F32)
    m_ref[idx] = m_new


def _causal_mask_t(t):
    k = lax.broadcasted_iota(jnp.int32, (t, t), 0)
    q = lax.broadcasted_iota(jnp.int32, (t, t), 1)
    return q >= k


def _finish_t(acc):
    return jnp.transpose(acc[0:MLA_V] / acc[MLA_V:MLA_V + 1])


def _causal_tiles(nb):
    pairs = [(qi, ki) for qi in range(nb) for ki in range(qi + 1)]
    return (jnp.asarray([p[0] for p in pairs], jnp.int32), jnp.asarray([p[1] for p in pairs], jnp.int32))


def _mla_flash_kernel(qi_ref, ki_ref, q_ref, k_ref, vt_ref, o_ref, m_ref, acc_ref, *, t):
    qi = qi_ref[pl.program_id(1)]
    ki = ki_ref[pl.program_id(1)]

    @pl.when(ki == 0)
    def _():
        m_ref[...] = jnp.full_like(m_ref, NEG_INF)
        acc_ref[...] = jnp.zeros_like(acc_ref)

    def step(masked):
        mask = _causal_mask_t(t) if masked else None
        def scores(h):
            st = lax.dot_general(k_ref[0, h], q_ref[0, h], (((1,), (1,)), ((), ())), preferred_element_type=F32)
            return jnp.where(mask, st, NEG_INF) if masked else st

        sts = [scores(h) for h in range(MLA_HEADS)]
        for h in range(MLA_HEADS):
            _softmax_step_t(sts[h], vt_ref[0, 0, h], m_ref, acc_ref, h)

    @pl.when(ki < qi)
    def _():
        step(False)

    @pl.when(ki == qi)
    def _():
        step(True)
        for h in range(MLA_HEADS):
            o_ref[0, :, h * MLA_V:(h + 1) * MLA_V] = _finish_t(acc_ref[h]).astype(BF16)


def _mla_flash(q, k, vt, *, t=ATTN_TILE):
    bsz, nh, n_pos, _ = q.shape
    t = min(t, n_pos)
    nb = n_pos // t
    qi_tab, ki_tab = _causal_tiles(nb)
    return pl.pallas_call(
        functools.partial(_mla_flash_kernel, t=t),
        grid_spec=pltpu.PrefetchScalarGridSpec(
            num_scalar_prefetch=2,
            grid=(bsz, qi_tab.shape[0]),
            in_specs=[
                pl.BlockSpec((1, nh, t, MLA_QK_PAD), lambda b, s, qt, kt: (b, 0, qt[s], 0)),
                pl.BlockSpec((1, nh, t, MLA_QK_PAD), lambda b, s, qt, kt: (b, 0, kt[s], 0)),
                pl.BlockSpec((1, 1, nh, VT_ROWS, t), lambda b, s, qt, kt: (b, kt[s], 0, 0, 0)),
            ],
            out_specs=pl.BlockSpec((1, t, nh * MLA_V), lambda b, s, qt, kt: (b, qt[s], 0)),
            scratch_shapes=[pltpu.VMEM((nh, 1, t), F32), pltpu.VMEM((nh, VT_ROWS, t), F32)],
        ),
        out_shape=jax.ShapeDtypeStruct((bsz, n_pos, nh * MLA_V), BF16),
        compiler_params=_cparams(("parallel", "arbitrary")),
        name="mla_flash",
    )(qi_tab, ki_tab, q, k, vt)


def _diff_prep_kernel(q_ref, k_ref, v_ref, qm_ref, kb_ref, vt_ref):
    hw = 2 * DIFF_QK
    tm = q_ref.shape[0]
    lane = lax.broadcasted_iota(jnp.int32, (tm, hw), 1)
    scale = DIFF_QK ** -0.5 * LOG2E
    ones = jnp.ones((VT_ROWS - DIFF_V, tm), BF16)
    kb_ref[...] = k_ref[...].astype(BF16)
    for h in range(DIFF_HEADS):
        qh = q_ref[:, h * hw:(h + 1) * hw] * scale
        qm_ref[0, 2 * h] = jnp.where(lane < DIFF_QK, qh, 0.0).astype(BF16)
        qm_ref[0, 2 * h + 1] = jnp.where(lane >= DIFF_QK, qh, 0.0).astype(BF16)
        vt_ref[0, 0, h, 0:DIFF_V, :] = jnp.transpose(v_ref[:, h * DIFF_V:(h + 1) * DIFF_V]).astype(BF16)
        vt_ref[0, 0, h, DIFF_V:VT_ROWS, :] = ones


N_MLA_PREP_IN = 11
N_DIFF_PREP_IN = 3


def _attn_prep_kernel(*refs):
    n_in = N_MLA_PREP_IN + N_DIFF_PREP_IN
    ins, outs = refs[:n_in], refs[n_in:]
    _mla_prep_kernel(*ins[:N_MLA_PREP_IN], *outs[:3])
    _diff_prep_kernel(*ins[N_MLA_PREP_IN:], *outs[3:])


def _attn_prep(z, cos, sin, qg, kvg, wq1, wq2, wkv, *, bsz, n_pos, tm=ATTN_TILE):
    tm = min(tm, n_pos)
    nt = n_pos // tm
    w = DIFF_HEADS * DIFF_V
    const = lambda b, t: (0, 0)
    zspec = lambda width, off: pl.BlockSpec((tm, width), lambda b, t: (b * nt + t, off // width))
    hspec = lambda width: pl.BlockSpec((1, MLA_HEADS, tm, width), lambda b, t: (b, 0, t, 0))
    slab = lambda nh: pl.BlockSpec((1, 1, nh, VT_ROWS, tm), lambda b, t: (b, t, 0, 0, 0))
    return pl.pallas_call(
        _attn_prep_kernel,
        grid=(bsz, nt),
        in_specs=[
            zspec(MLA_Q_RANK, OFF_CQ), zspec(MLA_KV_RANK, OFF_CKV), zspec(LANES, OFF_KR), zspec(LANES, OFF_KRR),
            pl.BlockSpec((tm, LANES), lambda b, t: (b * nt + t, 0)),
            pl.BlockSpec((tm, LANES), lambda b, t: (b * nt + t, 0)),
            pl.BlockSpec((1, MLA_Q_RANK), const), pl.BlockSpec((1, MLA_KV_RANK), const),
            pl.BlockSpec(wq1.shape, const), pl.BlockSpec(wq2.shape, const), pl.BlockSpec(wkv.shape, const),
            zspec(w, OFF_DQ), zspec(w, OFF_DK), zspec(w, OFF_DV),
        ],
        out_specs=[hspec(MLA_QK_PAD), hspec(MLA_QK_PAD), slab(MLA_HEADS),
                   pl.BlockSpec((1, 2 * DIFF_HEADS, tm, 2 * DIFF_QK), lambda b, t: (b, 0, t, 0)),
                   pl.BlockSpec((tm, w), lambda b, t: (b * nt + t, 0)),
                   slab(DIFF_HEADS)],
        out_shape=[jax.ShapeDtypeStruct((bsz, MLA_HEADS, n_pos, MLA_QK_PAD), BF16),
                   jax.ShapeDtypeStruct((bsz, MLA_HEADS, n_pos, MLA_QK_PAD), BF16),
                   jax.ShapeDtypeStruct((bsz, nt, MLA_HEADS, VT_ROWS, tm), BF16),
                   jax.ShapeDtypeStruct((bsz, 2 * DIFF_HEADS, n_pos, 2 * DIFF_QK), BF16),
                   jax.ShapeDtypeStruct((bsz * n_pos, w), BF16),
                   jax.ShapeDtypeStruct((bsz, nt, DIFF_HEADS, VT_ROWS, tm), BF16)],
        compiler_params=_cparams(("parallel", "parallel")),
        name="attn_prep",
    )(z, z, z, z, cos, sin, qg, kvg, wq1, wq2, wkv, z, z, z)


def _diff_kernel(qi_ref, ki_ref, qmin_ref, kmax_ref, qm_ref, k_ref, vt_ref, pq_ref, pk_ref, tab_ref,
                 lq1_ref, lk1_ref, lq2_ref, lk2_ref, sg_ref, o_ref, m_ref, acc_ref, *, t, tiles_per_row, lambda_init):
    qi = qi_ref[pl.program_id(1)]
    ki = ki_ref[pl.program_id(1)]
    nh = DIFF_HEADS
    hw = 2 * DIFF_QK

    @pl.when(ki == 0)
    def _():
        m_ref[...] = jnp.full_like(m_ref, NEG_INF)
        acc_ref[...] = jnp.zeros_like(acc_ref)

    nc = t // LANES
    qbase = (pl.program_id(0) * tiles_per_row + qi) * nc
    kbase = (pl.program_id(0) * tiles_per_row + ki) * nc

    def is_far(r, c):
        return qmin_ref[qbase + c] - kmax_ref[kbase + r] >= T5_TABLE - 1

    def all_of(pairs):
        out = None
        for r, c in pairs:
            out = is_far(r, c) if out is None else jnp.logical_and(out, is_far(r, c))
        return out

    def step(plan):
        sub = lambda r, c: (slice(r * LANES, (r + 1) * LANES), slice(c * LANES, (c + 1) * LANES))
        dist, local_mask = {}, None
        for r in range(nc):
            for c in range(nc):
                if plan[r][c] in ("gather", "diag"):
                    rows, cols = sub(r, c)
                    dist[r, c] = jnp.clip(pq_ref[0, :, cols] - pk_ref[rows, :], 0, T5_TABLE - 1)
                if plan[r][c] == "diag":
                    local_mask = _causal_mask_t(LANES)
        sts = []
        for h in range(nh):
            kh = k_ref[:, h * hw:(h + 1) * hw]
            shift = tab_ref[h:h + 1, T5_TABLE - 1:T5_TABLE]
            tab = jnp.broadcast_to(tab_ref[h:h + 1, :], (LANES, T5_TABLE))
            delta = {}
            for (r, c), d in dist.items():
                dl = jnp.take_along_axis(tab, d, axis=1, mode="promise_in_bounds") - shift
                delta[r, c] = jnp.where(local_mask, dl, NEG_INF) if plan[r][c] == "diag" else dl
            for mp in range(2):
                raw = lax.dot_general(kh, qm_ref[0, 2 * h + mp], (((1,), (1,)), ((), ())),
                                      preferred_element_type=F32)
                if delta or any("neg" in row for row in plan):
                    blocks = []
                    for r in range(nc):
                        row = []
                        for c in range(nc):
                            rows, cols = sub(r, c)
                            if plan[r][c] == "neg":
                                row.append(jnp.full((LANES, LANES), NEG_INF, F32))
                            elif (r, c) in delta:
                                row.append(raw[rows, cols] + delta[r, c])
                            else:
                                row.append(raw[rows, cols])
                        blocks.append(jnp.concatenate(row, axis=1))
                    raw = jnp.concatenate(blocks, axis=0)
                sts.append(raw)
        for j in range(2 * nh):
            shift = tab_ref[j // 2:j // 2 + 1, T5_TABLE - 1:T5_TABLE]
            _softmax_step_t(sts[j], vt_ref[0, 0, j // 2], m_ref, acc_ref, j, shift)

    every = [(r, c) for r in range(nc) for c in range(nc)]
    corner = (nc - 1, 0)
    plan_far = [["const"] * nc for _ in range(nc)]
    plan_corner = [["gather" if (r, c) == corner else "const" for c in range(nc)] for r in range(nc)]
    plan_full = [["gather"] * nc for _ in range(nc)]
    diag_kind = lambda r, c, beyond: "neg" if c < r else "diag" if c == r else "gather" if c == r + 1 else beyond
    plan_band = [[diag_kind(r, c, "const") for c in range(nc)] for r in range(nc)]
    plan_diag = [[diag_kind(r, c, "gather") for c in range(nc)] for r in range(nc)]

    below = ki < qi
    all_far = all_of(every)
    corner_far = all_of([p for p in every if p != corner])
    band_far = all_of([(r, c) for r, c in every if c >= r + 2])

    @pl.when(jnp.logical_and(below, all_far))
    def _():
        step(plan_far)

    @pl.when(jnp.logical_and(below, jnp.logical_and(corner_far, jnp.logical_not(all_far))))
    def _():
        step(plan_corner)

    @pl.when(jnp.logical_and(below, jnp.logical_not(corner_far)))
    def _():
        step(plan_full)

    @pl.when(jnp.logical_and(ki == qi, band_far))
    def _():
        step(plan_band)

    @pl.when(jnp.logical_and(ki == qi, jnp.logical_not(band_far)))
    def _():
        step(plan_diag)

    @pl.when(ki == qi)
    def _():
        lam =(jnp.exp(jnp.sum(lq1_ref[...] * lk1_ref[...], axis=-1, keepdims=True))
               - jnp.exp(jnp.sum(lq2_ref[...] * lk2_ref[...], axis=-1, keepdims=True)) + lambda_init)
        for h in range(nh):
            o = _finish_t(acc_ref[2 * h]) - lam * _finish_t(acc_ref[2 * h + 1])
            o = _rms(o, sg_ref[...]) * (1.0 - lambda_init)
            o_ref[:, h * DIFF_V:(h + 1) * DIFF_V] = o.astype(BF16)


def _diff_attn(qm, kb, vt, pos_q, pos_k, table, lq1, lk1, lq2, lk2, sg, *, bsz, n_pos, lambda_init, t=ATTN_TILE):
    t = min(t, n_pos)
    nb = n_pos // t
    w = DIFF_HEADS * DIFF_V
    const = lambda b, s, qt, kt, qmin, kmax: (0, 0)
    qi_tab, ki_tab = _causal_tiles(nb)
    chunks = pos_q.reshape(bsz * n_pos // LANES, LANES)
    qmin = jnp.min(chunks, axis=1)
    kmax = jnp.max(chunks, axis=1)
    return pl.pallas_call(
        functools.partial(_diff_kernel, t=t, tiles_per_row=nb, lambda_init=lambda_init),
        grid_spec=pltpu.PrefetchScalarGridSpec(
            num_scalar_prefetch=4,
            grid=(bsz, qi_tab.shape[0]),
            in_specs=[
                pl.BlockSpec((1, 2 * DIFF_HEADS, t, 2 * DIFF_QK), lambda b, s, qt, kt, qmin, kmax: (b, 0, qt[s], 0)),
                pl.BlockSpec((t, w), lambda b, s, qt, kt, qmin, kmax: (b * nb + kt[s], 0)),
                pl.BlockSpec((1, 1, DIFF_HEADS, VT_ROWS, t), lambda b, s, qt, kt, qmin, kmax: (b, kt[s], 0, 0, 0)),
                pl.BlockSpec((1, 1, t), lambda b, s, qt, kt, qmin, kmax: (b * nb + qt[s], 0, 0)),
                pl.BlockSpec((t, LANES), lambda b, s, qt, kt, qmin, kmax: (b * nb + kt[s], 0)),
                pl.BlockSpec((DIFF_HEADS, T5_TABLE), const),
                pl.BlockSpec((1, DIFF_QK), const), pl.BlockSpec((1, DIFF_QK), const),
                pl.BlockSpec((1, DIFF_QK), const), pl.BlockSpec((1, DIFF_QK), const),
                pl.BlockSpec((1, DIFF_V), const),
            ],
            out_specs=pl.BlockSpec((t, w), lambda b, s, qt, kt, qmin, kmax: (b * nb + qt[s], 0)),
            scratch_shapes=[pltpu.VMEM((2 * DIFF_HEADS, 1, t), F32),
                            pltpu.VMEM((2 * DIFF_HEADS, VT_ROWS, t), F32)],
        ),
        out_shape=jax.ShapeDtypeStruct((bsz * n_pos, w), BF16),
        compiler_params=_cparams(("parallel", "arbitrary")),
        name="diff_attn",
    )(qi_tab, ki_tab, qmin, kmax, qm, kb, vt, pos_q, pos_k, table, lq1, lk1, lq2, lk2, sg)


def _ret_kernel(q_ref, qr_ref, k_ref, kr_ref, v_ref, g_ref, cos_ref, sin_ref, o_ref, st_ref, *, tm):
    t = pl.program_id(1)
    c = RET_CHUNK
    nh = RET_HEADS
    w = nh * RET_QK

    @pl.when(t == 0)
    def _():
        st_ref[...] = jnp.zeros_like(st_ref)

    log_gamma = [math.log(1.0 - 2.0 ** (-5.0 - h)) for h in range(nh)]
    lane = lax.broadcasted_iota(jnp.int32, (1, w), 1)
    lg_lane = jnp.zeros((1, w), F32)
    for h in range(nh):
        lg_lane = jnp.where(lane // RET_QK == h, log_gamma[h], lg_lane)
    tok = lax.broadcasted_iota(jnp.int32, (c, 1), 0).astype(F32)
    q_decay = jnp.exp(lg_lane * (tok + 1.0))
    k_decay = jnp.exp(lg_lane * (c - 1.0 - tok))
    ri = lax.broadcasted_iota(jnp.int32, (c, c), 0)
    ci = lax.broadcasted_iota(jnp.int32, (c, c), 1)
    rel = (ri - ci).astype(F32)
    intra = [jnp.where(rel >= 0, jnp.exp(log_gamma[h] * jnp.maximum(rel, 0.0)), 0.0) for h in range(nh)]
    head_lanes = [(lane // RET_QK == h) for h in range(nh)]

    for j in range(tm // c):
        rows = slice(j * c, (j + 1) * c)
        cos = jnp.concatenate([cos_ref[rows, :]] * (w // LANES), axis=1)
        sin = jnp.concatenate([sin_ref[rows, :]] * (w // LANES), axis=1)
        q = q_ref[rows, :] * cos + qr_ref[rows, :] * sin
        k = (k_ref[rows, :] * cos + kr_ref[rows, :] * sin) * (RET_QK ** -0.5)
        kb = k.astype(BF16)
        qd = q * q_decay
        kdt = jnp.transpose(k * k_decay).astype(BF16)
        for h in range(nh):
            vh = v_ref[rows, h * RET_V:(h + 1) * RET_V].astype(BF16)
            qh = jnp.where(head_lanes[h], q, 0.0).astype(BF16)
            scores = lax.dot_general(qh, kb, (((1,), (1,)), ((), ())), preferred_element_type=F32) * intra[h]
            inner = jnp.dot(scores.astype(BF16), vh, preferred_element_type=F32)
            qdh = jnp.where(head_lanes[h], qd, 0.0).astype(BF16)
            state = st_ref[...]
            cross = jnp.dot(qdh, state.astype(BF16), preferred_element_type=F32)
            o = inner + cross
            mu = jnp.mean(o, axis=-1, keepdims=True)
            oc = o - mu
            var = jnp.mean(oc * oc, axis=-1, keepdims=True)
            o = oc * lax.rsqrt(var + LN_EPS)
            gh = g_ref[rows, h * RET_V:(h + 1) * RET_V]
            o_ref[rows, h * RET_V:(h + 1) * RET_V] = (gh * jax.nn.sigmoid(gh) * o).astype(BF16)
            hs = slice(h * RET_QK, (h + 1) * RET_QK)
            kv = jnp.dot(kdt[hs, :], vh, preferred_element_type=F32)
            st_ref[hs, :] = state[hs, :] * math.exp(log_gamma[h] * c) + kv


def _retention(z, cos, sin, *, bsz, n_pos, tm=512):
    tm = min(tm, n_pos)
    nt = n_pos // tm
    w = RET_HEADS * RET_QK
    wv = RET_HEADS * RET_V
    zspec = lambda width, off: pl.BlockSpec((tm, width), lambda b, t: (b * nt + t, off // width))
    tspec = pl.BlockSpec((tm, LANES), lambda b, t: (b * nt + t, 0))
    return pl.pallas_call(
        functools.partial(_ret_kernel, tm=tm),
        grid=(bsz, nt),
        in_specs=[zspec(w, OFF_RQ), zspec(w, OFF_RQR), zspec(w, OFF_RK), zspec(w, OFF_RKR),
                  zspec(wv, OFF_RV), zspec(wv, OFF_RG), tspec, tspec],
        out_specs=pl.BlockSpec((tm, wv), lambda b, t: (b * nt + t, 0)),
        out_shape=jax.ShapeDtypeStruct((bsz * n_pos, wv), BF16),
        scratch_shapes=[pltpu.VMEM((w, RET_V), F32)],
        compiler_params=_cparams(("parallel", "arbitrary")),
        name="retention",
    )(z, z, z, z, z, z, cos, sin)


def _outproj_ln_kernel(y0_ref, y1_ref, y2_ref, y3_ref, w_ref, x_ref, g_ref, b_ref, o_ref):
    acc = None
    for j, y_ref in enumerate((y0_ref, y1_ref, y2_ref, y3_ref)):
        part = jnp.dot(y_ref[...], w_ref[j * 512:(j + 1) * 512, :], preferred_element_type=F32)
        acc = part if acc is None else acc + part
    o_ref[...] = _layer_norm(ALPHA * x_ref[...] + acc, g_ref[...], b_ref[...])


def _outproj_ln(ys, w, x, g, b, *, tm=512):
    m, d = x.shape
    tm = min(tm, m)
    yspec = pl.BlockSpec((tm, 512), lambda i: (i, 0))
    return pl.pallas_call(
        _outproj_ln_kernel,
        grid=(m // tm,),
        in_specs=[yspec, yspec, yspec, yspec,
                  pl.BlockSpec(w.shape, lambda i: (0, 0)),
                  pl.BlockSpec((tm, d), lambda i: (i, 0)),
                  pl.BlockSpec((1, d), lambda i: (0, 0)),
                  pl.BlockSpec((1, d), lambda i: (0, 0))],
        out_specs=pl.BlockSpec((tm, d), lambda i: (i, 0)),
        out_shape=jax.ShapeDtypeStruct((m, d), F32),
        compiler_params=_cparams(("parallel",)),
        name="outproj_ln",
    )(*ys, w, x, g, b)


def _rot_cols(w, heads, dim):
    k = w.shape[0]
    w = w.reshape(k, heads, 2, dim // 2)
    return jnp.concatenate([-w[:, :, 1], w[:, :, 0]], axis=-1).reshape(k, heads * dim)


def _pad_cols(w, width):
    return jnp.pad(w, ((0, 0), (0, width - w.shape[1])))


def _wide_w_in(w_in):
    sizes = (S5_WIDTH, MLA_Q_RANK, MLA_KV_RANK, MLA_ROPE,
             RET_HEADS * RET_QK, RET_HEADS * RET_QK, RET_HEADS * RET_V, RET_HEADS * RET_V,
             DIFF_HEADS * 2 * DIFF_QK, DIFF_HEADS * 2 * DIFF_QK, DIFF_HEADS * DIFF_V)
    offs = np.concatenate([[0], np.cumsum(sizes)])
    (s5_u, cq, ckv, kr, rq, rk, rv, rg, dq, dk, dv) = [w_in[:, offs[i]:offs[i + 1]] for i in range(len(sizes))]
    cols = [s5_u, cq, rv, rg, dq, dk, dv,
            rq, _rot_cols(rq, RET_HEADS, RET_QK), rk, _rot_cols(rk, RET_HEADS, RET_QK),
            ckv, _pad_cols(kr, LANES), _pad_cols(_rot_cols(kr, 1, MLA_ROPE), LANES),
            jnp.zeros((w_in.shape[0], IN_WIDE - IN_USED), w_in.dtype)]
    return jnp.concatenate(cols, axis=1).astype(BF16)


def _mla_q_weights(w_uq):
    k = w_uq.shape[0]
    w = w_uq.reshape(k, MLA_HEADS, MLA_NOPE + MLA_ROPE)
    rope = w[:, :, MLA_NOPE:]
    pad = MLA_QK_PAD - MLA_NOPE - MLA_ROPE
    w1 = jnp.pad(w, ((0, 0), (0, 0), (0, pad))).reshape(k, MLA_HEADS * MLA_QK_PAD)
    rot = _rot_cols(rope.reshape(k, MLA_HEADS * MLA_ROPE), MLA_HEADS, MLA_ROPE).reshape(k, MLA_HEADS, MLA_ROPE)
    w2 = jnp.pad(rot, ((0, 0), (0, 0), (0, LANES - MLA_ROPE))).reshape(k, MLA_HEADS * LANES)
    return w1.astype(BF16), w2.astype(BF16)


def kernel(x, p, positions, rel_bias, ffn1_w_gate, ffn1_w_up, ffn1_w_down, ln1_g, ln1_b, w_in, w_out, ln2_g, ln2_b, s5_lambda_re, s5_lambda_im, s5_log_dt, s5_b_re, s5_b_im, s5_c_re, s5_c_im, s5_d, s5_w_glu, s5_b_glu, mla_q_norm_g, mla_w_uq, mla_kv_norm_g, mla_w_ukv, diff_lambda_q1, diff_lambda_k1, diff_lambda_q2, diff_lambda_k2, diff_subln_g, ffn2_w_gate, ffn2_w_up, ffn2_w_down, ple_w_gate, ple_b_gate, ple_w_proj, ln3_g, ln3_b):
    bsz, n_pos, d = x.shape
    m = bsz * n_pos
    depth = ffn1_w_gate.shape[0]
    assert d == D_MODEL and n_pos % S5_TM == 0 and n_pos % ATTN_TILE == 0 and m % FFN_TM == 0, (x.shape,)
    xf = x.reshape(m, d)
    row = lambda v: v.reshape(1, -1)

    cos, sin = _rope_tables(positions.reshape(m, 1))
    table = _bias_table(rel_bias)
    t_diff = ATTN_TILE
    pos_q = positions.reshape(bsz * (n_pos // t_diff), 1, t_diff)
    pos_k = jnp.broadcast_to(positions.reshape(m, 1), (m, LANES))

    for i in range(depth):
        f1g, f1u, f1d = _cast_layer(i, ffn1_w_gate, ffn1_w_up, ffn1_w_down)
        f2g, f2u, f2d, pwg, wo = _cast_layer(i, ffn2_w_gate, ffn2_w_up, ffn2_w_down, ple_w_gate, w_out)
        xf = _ffn_ln(xf, f1g, f1u, f1d, row(ln1_g[i]), row(ln1_b[i]))
        z = _inproj(xf, _wide_w_in(w_in[i]))

        apr, api, bbr, bbi = _s5_discretise(s5_lambda_re[i], s5_lambda_im[i], s5_log_dt[i], s5_b_re[i], s5_b_im[i])
        bh = jnp.concatenate([_block_diag_in(bbr), _block_diag_in(bbi)], axis=2).astype(BF16)
        ch = jnp.concatenate([_block_diag_out(s5_c_re[i]), -_block_diag_out(s5_c_im[i])], axis=1).astype(BF16)
        apr = jnp.repeat(apr, S5_TM // S5_STEPS, axis=0)
        api = jnp.repeat(api, S5_TM // S5_STEPS, axis=0)
        y_s5 = _s5_mixer(z, bh, ch, apr, api, row(s5_d[i]), s5_w_glu[i].astype(BF16), row(s5_b_glu[i]),
                         bsz=bsz, n_pos=n_pos)

        wq1, wq2 = _mla_q_weights(mla_w_uq[i])
        q, k, vt, dqm, dkb, dvt = _attn_prep(z, cos, sin, row(mla_q_norm_g[i]), row(mla_kv_norm_g[i]), wq1, wq2,
                                             mla_w_ukv[i].astype(BF16), bsz=bsz, n_pos=n_pos)
        y_mla = _mla_flash(q, k, vt).reshape(m, MLA_HEADS * MLA_V)

        y_ret = _retention(z, cos, sin, bsz=bsz, n_pos=n_pos)

        lambda_init = 0.8 - 0.6 * math.exp(-0.3 * i)
        y_diff = _diff_attn(dqm, dkb, dvt, pos_q, pos_k, table, row(diff_lambda_q1[i]), row(diff_lambda_k1[i]),
                            row(diff_lambda_q2[i]), row(diff_lambda_k2[i]), row(diff_subln_g[i]),
                            bsz=bsz, n_pos=n_pos, lambda_init=lambda_init, t=t_diff)

        xf = _outproj_ln((y_s5, y_mla, y_ret, y_diff), wo, xf, row(ln2_g[i]), row(ln2_b[i]))

        res, xb = _ple(xf, p.reshape(depth, m, PLE_DIM), i, pwg, row(ple_b_gate[i]), ple_w_proj[i].astype(BF16))
        xf = _ffn_ln(xb, f2g, f2u, f2d, row(ln3_g[i]), row(ln3_b[i]), res)
    return xf.reshape(bsz, n_pos, d)
```

```python
import functools
import math

import numpy as np
import jax
import jax.numpy as jnp
from jax import lax
from jax.experimental import pallas as pl
from jax.experimental.pallas import tpu as pltpu

F32 = jnp.float32
BF16 = jnp.bfloat16

LANES = 128
SUBLANES = 8
VMEM_BYTES = 64 * 1024 * 1024
VMEM_LIMIT_BYTES = VMEM_BYTES - 8 * 1024 * 1024
FFN_VMEM_LIMIT_BYTES = VMEM_BYTES - 4 * 1024 * 1024

D_MODEL = 2048
DEPTH = 2
PLE_DIM = 256
D_FF = 5632
ALPHA = (2 * DEPTH) ** 0.25
ROPE_THETA = 10000.0
NEG_INF = -1e30
LN_EPS = 1e-5
RMS_EPS = 1e-6

S5_WIDTH = 512
S5_GROUP = 16
S5_GROUPS = 32
S5_STATE = 64
S5_NSTATE = S5_GROUPS * S5_STATE
S5_HALF = S5_NSTATE // 2
S5_TM = 512
S5_STEPS = S5_TM // SUBLANES

MLA_HEADS = 4
MLA_Q_RANK = 512
MLA_KV_RANK = 128
MLA_NOPE = 128
MLA_ROPE = 64
MLA_V = 128
MLA_QK_PAD = 256

RET_HEADS = 4
RET_QK = 64
RET_V = 128
RET_CHUNK = 128

DIFF_HEADS = 4
DIFF_QK = 64
DIFF_V = 128

T5_BUCKETS = 32
T5_MAX_DIST = 128
T5_TABLE = 128

LOG2E = math.log2(math.e)
VT_ROWS = 144
ATTN_TILE = 512
FFN_TM = 1024

OFF_S5 = 0
OFF_CQ = 512
OFF_RV = 1024
OFF_RG = 1536
OFF_DQ = 2048
OFF_DK = 2560
OFF_DV = 3072
OFF_RQ = 3584
OFF_RQR = 3840
OFF_RK = 4096
OFF_RKR = 4352
OFF_CKV = 4608
OFF_KR = 4736
OFF_KRR = 4864
IN_USED = 4992
IN_WIDE = 5120
IN_TN = 2560


def _cparams(sem, vmem_limit_bytes=VMEM_LIMIT_BYTES):
    return pltpu.CompilerParams(dimension_semantics=sem, vmem_limit_bytes=vmem_limit_bytes)


def _layer_norm(y, g, b):
    mu = jnp.mean(y, axis=-1, keepdims=True)
    yc = y - mu
    var = jnp.mean(yc * yc, axis=-1, keepdims=True)
    return yc * lax.rsqrt(var + LN_EPS) * g + b


CAST_STEPS = 16


def _cast_kernel(*refs):
    n = len(refs) // 2
    for src, dst in zip(refs[:n], refs[n:]):
        dst[...] = src[0].astype(BF16)


def _cast_layer(layer, *stacked):
    in_specs, out_specs, out_shape = [], [], []
    for w in stacked:
        _, r, c = w.shape
        tr = r // CAST_STEPS
        in_specs.append(pl.BlockSpec((1, tr, c), lambda s: (layer, s, 0)))
        out_specs.append(pl.BlockSpec((tr, c), lambda s: (s, 0)))
        out_shape.append(jax.ShapeDtypeStruct((r, c), BF16))
    return pl.pallas_call(
        _cast_kernel,
        grid=(CAST_STEPS,),
        in_specs=in_specs,
        out_specs=out_specs,
        out_shape=out_shape,
        compiler_params=_cparams(("parallel",)),
        name="cast_weights",
    )(*stacked)


FFN_ROW_CHUNK = 128


def _ffn_ln_kernel(*refs, nf, tm, has_res):
    if has_res:
        xb_ref, res_ref, wg_ref, wu_ref, wd_ref, g_ref, b_ref, o_ref = refs
    else:
        res_ref, wg_ref, wu_ref, wd_ref, g_ref, b_ref, o_ref, xb_ref = refs
    f = pl.program_id(1)
    chunks = tm // FFN_ROW_CHUNK

    def rows(c):
        return pl.ds(pl.multiple_of(c * FFN_ROW_CHUNK, FFN_ROW_CHUNK), FFN_ROW_CHUNK)

    @pl.when(f == 0)
    def _():
        o_ref[...] = jnp.zeros_like(o_ref)
        if not has_res:
            def cast(c, _):
                xb_ref[rows(c), :] = res_ref[rows(c), :].astype(BF16)
                return 0
            lax.fori_loop(0, chunks, cast, 0)

    xb = xb_ref[...]
    gate = jnp.dot(xb, wg_ref[...], preferred_element_type=F32)
    up = jnp.dot(xb, wu_ref[...], preferred_element_type=F32)
    h = gate * jax.nn.sigmoid(gate) * up
    o_ref[...] += jnp.dot(h.astype(BF16), wd_ref[...], preferred_element_type=F32)

    @pl.when(f == nf - 1)
    def _():
        def norm(c, _):
            res = res_ref[rows(c), :]
            y = (res if has_res else ALPHA * res) + 0.5 * o_ref[rows(c), :]
            o_ref[rows(c), :] = _layer_norm(y, g_ref[...], b_ref[...])
            return 0
        lax.fori_loop(0, chunks, norm, 0)


def _ffn_ln(x, wg, wu, wd, g, b, res=None, *, tm=FFN_TM, tf=512):
    m, d = x.shape
    f_dim = wg.shape[1]
    tm = min(tm, m)
    nf = f_dim // tf
    xspec = pl.BlockSpec((tm, d), lambda i, f: (i, 0))
    in_specs = [xspec] * (2 if res is not None else 1) + [
        pl.BlockSpec((d, tf), lambda i, f: (0, f)),
        pl.BlockSpec((d, tf), lambda i, f: (0, f)),
        pl.BlockSpec((tf, d), lambda i, f: (f, 0)),
        pl.BlockSpec((1, d), lambda i, f: (0, 0)),
        pl.BlockSpec((1, d), lambda i, f: (0, 0)),
    ]
    args = ([x, res] if res is not None else [x]) + [wg, wu, wd, g, b]
    return pl.pallas_call(
        functools.partial(_ffn_ln_kernel, nf=nf, tm=tm, has_res=res is not None),
        grid=(m // tm, nf),
        in_specs=in_specs,
        out_specs=pl.BlockSpec((tm, d), lambda i, f: (i, 0)),
        out_shape=jax.ShapeDtypeStruct((m, d), F32),
        scratch_shapes=[] if res is not None else [pltpu.VMEM((tm, d), BF16)],
        compiler_params=_cparams(("parallel", "arbitrary"), FFN_VMEM_LIMIT_BYTES),
        name="ffn_ln",
    )(*args)


def _ple_kernel(x_ref, p_ref, wg_ref, bg_ref, wp_ref, res_ref, xb_ref):
    x = x_ref[...]
    xb = x.astype(BF16)
    xb_ref[...] = xb
    pb = p_ref[...].astype(BF16)
    gate = jax.nn.sigmoid(jnp.dot(xb, wg_ref[...], preferred_element_type=F32) + bg_ref[...])
    res_ref[...] = ALPHA * x + gate * jnp.dot(pb, wp_ref[...], preferred_element_type=F32)


def _ple(x, p, layer, wg, bg, wp, *, tm=512):
    m, d = x.shape
    tm = min(tm, m)
    return pl.pallas_call(
        _ple_kernel,
        grid=(m // tm,),
        in_specs=[
            pl.BlockSpec((tm, d), lambda i: (i, 0)),
            pl.BlockSpec((None, tm, PLE_DIM), lambda i: (layer, i, 0)),
            pl.BlockSpec((d, d), lambda i: (0, 0)),
            pl.BlockSpec((1, d), lambda i: (0, 0)),
            pl.BlockSpec((PLE_DIM, d), lambda i: (0, 0)),
        ],
        out_specs=[pl.BlockSpec((tm, d), lambda i: (i, 0))] * 2,
        out_shape=[jax.ShapeDtypeStruct((m, d), F32), jax.ShapeDtypeStruct((m, d), BF16)],
        compiler_params=_cparams(("parallel",)),
        name="ple",
    )(x, p, wg, bg, wp)


def _inproj_kernel(x_ref, w_ref, o_ref):
    o_ref[...] = jnp.dot(x_ref[...].astype(BF16), w_ref[...], preferred_element_type=F32)


def _inproj(x, w, *, tm=512):
    m, d = x.shape
    n = w.shape[1]
    tm = min(tm, m)
    return pl.pallas_call(
        _inproj_kernel,
        grid=(n // IN_TN, m // tm),
        in_specs=[
            pl.BlockSpec((tm, d), lambda j, i: (i, 0)),
            pl.BlockSpec((d, IN_TN), lambda j, i: (0, j)),
        ],
        out_specs=pl.BlockSpec((tm, IN_TN), lambda j, i: (i, j)),
        out_shape=jax.ShapeDtypeStruct((m, n), F32),
        compiler_params=_cparams(("parallel", "parallel")),
        name="inproj",
    )(x, w)


def _rope_kernel(pos_ref, freq_ref, cos_ref, sin_ref):
    ang = pos_ref[...].astype(F32) * freq_ref[...]
    cos_ref[...] = jnp.cos(ang)
    sin_ref[...] = jnp.sin(ang)


def _rope_tables(pos_col, *, tm=1024):
    m = pos_col.shape[0]
    tm = min(tm, m)
    half = MLA_ROPE // 2
    inv = 1.0 / (ROPE_THETA ** (np.arange(0, MLA_ROPE, 2, dtype=np.float64) / MLA_ROPE))
    freq = jnp.asarray(np.tile(inv, LANES // half)[None, :], F32)
    return pl.pallas_call(
        _rope_kernel,
        grid=(m // tm,),
        in_specs=[pl.BlockSpec((tm, 1), lambda i: (i, 0)), pl.BlockSpec((1, LANES), lambda i: (0, 0))],
        out_specs=[pl.BlockSpec((tm, LANES), lambda i: (i, 0))] * 2,
        out_shape=[jax.ShapeDtypeStruct((m, LANES), F32)] * 2,
        compiler_params=_cparams(("parallel",)),
        name="rope_tables",
    )(pos_col, freq)


def _t5_bucket_static():
    n = np.arange(T5_TABLE)
    max_exact = T5_BUCKETS // 2
    nf = np.maximum(n, 1).astype(np.float64)
    large = max_exact + (np.log(nf / max_exact) / math.log(T5_MAX_DIST / max_exact)
                         * (T5_BUCKETS - max_exact)).astype(np.int64)
    large = np.minimum(large, T5_BUCKETS - 1)
    return np.where(n < max_exact, n, large)


def _bias_table_kernel(rb_ref, onehot_ref, o_ref):
    rb = rb_ref[...]
    oh = onehot_ref[...]
    rows = [jnp.sum(oh * rb[:, h:h + 1], axis=0, keepdims=True) for h in range(DIFF_HEADS)]
    o_ref[...] = jnp.concatenate(rows, axis=0) * LOG2E


def _bias_table(rel_bias):
    bucket = _t5_bucket_static()
    onehot = jnp.asarray((np.arange(T5_BUCKETS)[:, None] == bucket[None, :]).astype(np.float32))
    return pl.pallas_call(
        _bias_table_kernel,
        out_shape=jax.ShapeDtypeStruct((DIFF_HEADS, T5_TABLE), F32),
        name="t5_bias_table",
    )(rel_bias, onehot)


def _s5_disc_kernel(lr_ref, li_ref, ldt_ref, br_ref, bi_ref, apr_ref, api_ref, bbr_ref, bbi_ref):
    lr = lr_ref[...]
    li = li_ref[...]
    dt = jnp.exp(ldt_ref[...])
    k = (lax.broadcasted_iota(jnp.int32, (S5_STEPS, 1), 0) + 1).astype(F32)
    mag = jnp.exp(lr * dt * k)
    apr_ref[...] = mag * jnp.cos(li * dt * k)
    api_ref[...] = mag * jnp.sin(li * dt * k)
    mag1 = jnp.exp(lr * dt)
    ar = mag1 * jnp.cos(li * dt)
    ai = mag1 * jnp.sin(li * dt)
    den = lr * lr + li * li
    fr = ((ar - 1.0) * lr + ai * li) / den
    fi = (ai * lr - (ar - 1.0) * li) / den
    br = br_ref[...]
    bi = bi_ref[...]
    bbr_ref[...] = fr * br - fi * bi
    bbi_ref[...] = fr * bi + fi * br


def _s5_discretise(lam_re, lam_im, log_dt, b_re, b_im):
    n = S5_NSTATE
    lr = lam_re.reshape(1, n)
    li = lam_im.reshape(1, n)
    ldt = jnp.repeat(log_dt, S5_STATE).reshape(1, n)
    br = jnp.transpose(b_re, (2, 0, 1)).reshape(S5_GROUP, n)
    bi = jnp.transpose(b_im, (2, 0, 1)).reshape(S5_GROUP, n)
    return pl.pallas_call(
        _s5_disc_kernel,
        out_shape=[jax.ShapeDtypeStruct((S5_STEPS, n), F32)] * 2 + [jax.ShapeDtypeStruct((S5_GROUP, n), F32)] * 2,
        name="s5_discretise",
    )(lr, li, ldt, br, bi)


def _block_diag_in(bb):
    gh = S5_GROUPS // 2
    eye = jnp.eye(gh, dtype=bb.dtype)
    v = bb.reshape(S5_GROUP, 2, gh, S5_STATE)
    out = eye[None, :, None, :, None] * jnp.transpose(v, (1, 0, 2, 3))[:, None, :, :, :]
    return out.reshape(2, gh * S5_GROUP, S5_HALF)


def _block_diag_out(c):
    gh = S5_GROUPS // 2
    eye = jnp.eye(gh, dtype=c.dtype)
    v = jnp.transpose(c, (0, 2, 1)).reshape(2, gh, S5_STATE, S5_GROUP)
    out = v[:, :, :, None, :] * eye[None, :, None, :, None]
    return out.reshape(2, S5_HALF, gh * S5_GROUP)


def _s5_kernel(u0_ref, u1_ref, u2_ref, u3_ref, bh_ref, ch_ref, pwr_ref, pwi_ref, d_ref, wglu_ref, bglu_ref, o_ref,
               h_ref, hb_ref, cr_ref, ci_ref, cinr_ref, cini_ref, yo_ref):
    u_refs = (u0_ref, u1_ref, u2_ref, u3_ref)
    t = pl.program_id(1)
    nh = S5_HALF
    cw = S5_WIDTH // 2

    @pl.when(t == 0)
    def _():
        cr_ref[...] = jnp.zeros_like(cr_ref)
        ci_ref[...] = jnp.zeros_like(ci_ref)

    u = jnp.concatenate(
        [jnp.concatenate([r[pl.ds(j, SUBLANES, stride=S5_STEPS), :] for j in range(S5_STEPS)], axis=0)
         for r in u_refs], axis=1)
    ub = u.astype(BF16)
    ys = []
    for half in range(2):
        h_ref[:, pl.ds(2 * half * nh, 2 * nh)] = jnp.dot(ub[:, half * cw:(half + 1) * cw], bh_ref[half],
                                                          preferred_element_type=F32)
    for half in range(2):
        st = pl.ds(half * nh, nh)
        re = pl.ds(2 * half * nh, nh)
        im = pl.ds((2 * half + 1) * nh, nh)
        ar = pwr_ref[0:SUBLANES, st]
        ai = pwi_ref[0:SUBLANES, st]

        def scan(j, carry, re=re, im=im, ar=ar, ai=ai):
            hr, hi = carry
            r0 = pl.multiple_of(j * SUBLANES, SUBLANES)
            nr = ar * hr - ai * hi + h_ref[pl.ds(r0, SUBLANES), re]
            ni = ar * hi + ai * hr + h_ref[pl.ds(r0, SUBLANES), im]
            h_ref[pl.ds(r0, SUBLANES), re] = nr
            h_ref[pl.ds(r0, SUBLANES), im] = ni
            return nr, ni

        zero = jnp.zeros((SUBLANES, nh), F32)
        er, ei = lax.fori_loop(0, S5_STEPS, scan, (zero, zero), unroll=True)

        a_seg_r = pwr_ref[S5_TM - 1:S5_TM, st]
        a_seg_i = pwi_ref[S5_TM - 1:S5_TM, st]
        cr = cr_ref[:, st]
        ci = ci_ref[:, st]
        for s in range(SUBLANES):
            cinr_ref[s:s + 1, :] = cr
            cini_ref[s:s + 1, :] = ci
            cr, ci = (er[s:s + 1] + a_seg_r * cr - a_seg_i * ci, ei[s:s + 1] + a_seg_r * ci + a_seg_i * cr)
        cr_ref[:, st] = cr
        ci_ref[:, st] = ci

        cin_r = cinr_ref[...]
        cin_i = cini_ref[...]

        def fix(jj, _, re=re, im=im, st=st, cin_r=cin_r, cin_i=cin_i):
            r0 = pl.multiple_of(jj * 2 * SUBLANES, 2 * SUBLANES)
            out_r, out_i = [], []
            for k in range(2):
                rows = pl.ds(r0 + k * SUBLANES, SUBLANES)
                pr = pwr_ref[rows, st]
                pi = pwi_ref[rows, st]
                out_r.append(h_ref[rows, re] + pr * cin_r - pi * cin_i)
                out_i.append(h_ref[rows, im] + pr * cin_i + pi * cin_r)
            hb_ref[pl.ds(r0, 2 * SUBLANES), re] = jnp.concatenate(out_r, axis=0).astype(BF16)
            hb_ref[pl.ds(r0, 2 * SUBLANES), im] = jnp.concatenate(out_i, axis=0).astype(BF16)
            return 0

        lax.fori_loop(0, S5_STEPS // 2, fix, 0, unroll=True)
        ys.append(jnp.dot(hb_ref[:, pl.ds(2 * half * nh, 2 * nh)], ch_ref[half], preferred_element_type=F32))

    y = jnp.concatenate(ys, axis=1) + d_ref[...] * u
    c0 = math.sqrt(2.0 / math.pi)
    y = 0.5 * y * (1.0 + jnp.tanh(c0 * (y + 0.044715 * (y * y * y))))
    gate = jax.nn.sigmoid(jnp.dot(y.astype(BF16), wglu_ref[...], preferred_element_type=F32) + bglu_ref[...])
    out = y * gate
    tm = out.shape[0]
    for c in range(S5_WIDTH // LANES):
        yo_ref[pl.ds(c * tm, tm), :] = out[:, c * LANES:(c + 1) * LANES]
    for c in range(S5_WIDTH // LANES):
        for s in range(SUBLANES):
            o_ref[s * S5_STEPS:(s + 1) * S5_STEPS, c * LANES:(c + 1) * LANES] = (
                yo_ref[pl.ds(c * tm + s, S5_STEPS, stride=SUBLANES), :].astype(BF16))


def _s5_mixer(z, bh, ch, pwr, pwi, d, wglu, bglu, *, bsz, n_pos):
    tm = S5_TM
    nt = n_pos // tm
    m = bsz * n_pos
    const = lambda b, t: (0, 0)
    const3 = lambda b, t: (0, 0, 0)
    return pl.pallas_call(
        _s5_kernel,
        grid=(bsz, nt),
        in_specs=[
            *[pl.BlockSpec((tm, LANES), lambda b, t, c=c: (b * nt + t, OFF_S5 // LANES + c))
              for c in range(S5_WIDTH // LANES)],
            pl.BlockSpec(bh.shape, const3),
            pl.BlockSpec(ch.shape, const3),
            pl.BlockSpec((S5_TM, S5_NSTATE), const),
            pl.BlockSpec((S5_TM, S5_NSTATE), const),
            pl.BlockSpec((1, S5_WIDTH), const),
            pl.BlockSpec((S5_WIDTH, S5_WIDTH), const),
            pl.BlockSpec((1, S5_WIDTH), const),
        ],
        out_specs=pl.BlockSpec((tm, S5_WIDTH), lambda b, t: (b * nt + t, 0)),
        out_shape=jax.ShapeDtypeStruct((m, S5_WIDTH), BF16),
        scratch_shapes=[pltpu.VMEM((tm, 2 * S5_NSTATE), F32), pltpu.VMEM((tm, 2 * S5_NSTATE), BF16),
                        pltpu.VMEM((1, S5_NSTATE), F32), pltpu.VMEM((1, S5_NSTATE), F32),
                        pltpu.VMEM((8, S5_HALF), F32), pltpu.VMEM((8, S5_HALF), F32),
                        pltpu.VMEM((S5_WIDTH // LANES * tm, LANES), F32)],
        compiler_params=_cparams(("parallel", "arbitrary")),
        name="s5_mixer",
    )(z, z, z, z, bh, ch, pwr, pwi, d, wglu, bglu)


def _rms(x, g):
    return x * lax.rsqrt(jnp.mean(x * x, axis=-1, keepdims=True) + RMS_EPS) * g


def _mla_prep_kernel(cq_ref, ckv_ref, kr_ref, krr_ref, cos_ref, sin_ref, qg_ref, kvg_ref,
                     wq1_ref, wq2_ref, wkv_ref, q_ref, k_ref, vt_ref):
    cos = cos_ref[...]
    sin = sin_ref[...]
    scale = (MLA_NOPE + MLA_ROPE) ** -0.5 * LOG2E
    ones = jnp.ones((VT_ROWS - MLA_V, cos.shape[0]), BF16)
    cqn = _rms(cq_ref[...], qg_ref[...]).astype(BF16)
    z1 = jnp.dot(cqn, wq1_ref[...], preferred_element_type=F32)
    z2 = jnp.dot(cqn, wq2_ref[...], preferred_element_type=F32)
    ckvn = _rms(ckv_ref[...], kvg_ref[...]).astype(BF16)
    zkv = jnp.dot(ckvn, wkv_ref[...], preferred_element_type=F32)
    k_rope = (kr_ref[...] * cos + krr_ref[...] * sin).astype(BF16)
    for h in range(MLA_HEADS):
        a = h * MLA_QK_PAD
        q_nope = z1[:, a:a + MLA_NOPE]
        q_rope = z1[:, a + MLA_NOPE:a + MLA_QK_PAD] * cos + z2[:, h * LANES:(h + 1) * LANES] * sin
        q_ref[0, h, :, 0:MLA_NOPE] = (q_nope * scale).astype(BF16)
        q_ref[0, h, :, MLA_NOPE:MLA_QK_PAD] = (q_rope * scale).astype(BF16)
        k_ref[0, h, :, 0:MLA_NOPE] = zkv[:, a:a + MLA_NOPE].astype(BF16)
        k_ref[0, h, :, MLA_NOPE:MLA_QK_PAD] = k_rope
        vt_ref[0, 0, h, 0:MLA_V, :] = jnp.transpose(zkv[:, a + MLA_NOPE:a + MLA_NOPE + MLA_V]).astype(BF16)
        vt_ref[0, 0, h, MLA_V:VT_ROWS, :] = ones


def _softmax_step_t(st, vt, m_ref, acc_ref, idx, shift=None):
    m_old = m_ref[idx]
    cur = jnp.max(st, axis=0, keepdims=True)
    if shift is not None:
        cur = cur + shift
    m_new = jnp.maximum(m_old, cur)
    p = jnp.exp2(st - (m_new if shift is None else m_new - shift)).astype(BF16)
    corr = jnp.exp2(m_old - m_new)
    acc_ref[idx] = corr * acc_ref[idx] + jnp.dot(vt, p, preferred_element_type=F32)
    m_ref[idx] = m_new


def _causal_mask_t(t):
    k = lax.broadcasted_iota(jnp.int32, (t, t), 0)
    q = lax.broadcasted_iota(jnp.int32, (t, t), 1)
    return q >= k


def _finish_t(acc):
    return jnp.transpose(acc[0:MLA_V] / acc[MLA_V:MLA_V + 1])


def _causal_tiles(nb):
    pairs = [(qi, ki) for qi in range(nb) for ki in range(qi + 1)]
    return (jnp.asarray([p[0] for p in pairs], jnp.int32), jnp.asarray([p[1] for p in pairs], jnp.int32))


def _mla_flash_kernel(qi_ref, ki_ref, q_ref, k_ref, vt_ref, o_ref, m_ref, acc_ref, *, t):
    qi = qi_ref[pl.program_id(1)]
    ki = ki_ref[pl.program_id(1)]

    @pl.when(ki == 0)
    def _():
        m_ref[...] = jnp.full_like(m_ref, NEG_INF)
        acc_ref[...] = jnp.zeros_like(acc_ref)

    def step(masked):
        mask = _causal_mask_t(t) if masked else None
        def scores(h):
            st = lax.dot_general(k_ref[0, h], q_ref[0, h], (((1,), (1,)), ((), ())), preferred_element_type=F32)
            return jnp.where(mask, st, NEG_INF) if masked else st

        sts = [scores(h) for h in range(MLA_HEADS)]
        for h in range(MLA_HEADS):
            _softmax_step_t(sts[h], vt_ref[0, 0, h], m_ref, acc_ref, h)

    @pl.when(ki < qi)
    def _():
        step(False)

    @pl.when(ki == qi)
    def _():
        step(True)
        for h in range(MLA_HEADS):
            o_ref[0, :, h * MLA_V:(h + 1) * MLA_V] = _finish_t(acc_ref[h]).astype(BF16)


def _mla_flash(q, k, vt, *, t=ATTN_TILE):
    bsz, nh, n_pos, _ = q.shape
    t = min(t, n_pos)
    nb = n_pos // t
    qi_tab, ki_tab = _causal_tiles(nb)
    return pl.pallas_call(
        functools.partial(_mla_flash_kernel, t=t),
        grid_spec=pltpu.PrefetchScalarGridSpec(
            num_scalar_prefetch=2,
            grid=(bsz, qi_tab.shape[0]),
            in_specs=[
                pl.BlockSpec((1, nh, t, MLA_QK_PAD), lambda b, s, qt, kt: (b, 0, qt[s], 0)),
                pl.BlockSpec((1, nh, t, MLA_QK_PAD), lambda b, s, qt, kt: (b, 0, kt[s], 0)),
                pl.BlockSpec((1, 1, nh, VT_ROWS, t), lambda b, s, qt, kt: (b, kt[s], 0, 0, 0)),
            ],
            out_specs=pl.BlockSpec((1, t, nh * MLA_V), lambda b, s, qt, kt: (b, qt[s], 0)),
            scratch_shapes=[pltpu.VMEM((nh, 1, t), F32), pltpu.VMEM((nh, VT_ROWS, t), F32)],
        ),
        out_shape=jax.ShapeDtypeStruct((bsz, n_pos, nh * MLA_V), BF16),
        compiler_params=_cparams(("parallel", "arbitrary")),
        name="mla_flash",
    )(qi_tab, ki_tab, q, k, vt)


def _diff_prep_kernel(q_ref, k_ref, v_ref, qm_ref, kb_ref, vt_ref):
    hw = 2 * DIFF_QK
    tm = q_ref.shape[0]
    lane = lax.broadcasted_iota(jnp.int32, (tm, hw), 1)
    scale = DIFF_QK ** -0.5 * LOG2E
    ones = jnp.ones((VT_ROWS - DIFF_V, tm), BF16)
    kb_ref[...] = k_ref[...].astype(BF16)
    for h in range(DIFF_HEADS):
        qh = q_ref[:, h * hw:(h + 1) * hw] * scale
        qm_ref[0, 2 * h] = jnp.where(lane < DIFF_QK, qh, 0.0).astype(BF16)
        qm_ref[0, 2 * h + 1] = jnp.where(lane >= DIFF_QK, qh, 0.0).astype(BF16)
        vt_ref[0, 0, h, 0:DIFF_V, :] = jnp.transpose(v_ref[:, h * DIFF_V:(h + 1) * DIFF_V]).astype(BF16)
        vt_ref[0, 0, h, DIFF_V:VT_ROWS, :] = ones


N_MLA_PREP_IN = 11
N_DIFF_PREP_IN = 3


def _attn_prep_kernel(*refs):
    n_in = N_MLA_PREP_IN + N_DIFF_PREP_IN
    ins, outs = refs[:n_in], refs[n_in:]
    _mla_prep_kernel(*ins[:N_MLA_PREP_IN], *outs[:3])
    _diff_prep_kernel(*ins[N_MLA_PREP_IN:], *outs[3:])


def _attn_prep(z, cos, sin, qg, kvg, wq1, wq2, wkv, *, bsz, n_pos, tm=ATTN_TILE):
    tm = min(tm, n_pos)
    nt = n_pos // tm
    w = DIFF_HEADS * DIFF_V
    const = lambda b, t: (0, 0)
    zspec = lambda width, off: pl.BlockSpec((tm, width), lambda b, t: (b * nt + t, off // width))
    hspec = lambda width: pl.BlockSpec((1, MLA_HEADS, tm, width), lambda b, t: (b, 0, t, 0))
    slab = lambda nh: pl.BlockSpec((1, 1, nh, VT_ROWS, tm), lambda b, t: (b, t, 0, 0, 0))
    return pl.pallas_call(
        _attn_prep_kernel,
        grid=(bsz, nt),
        in_specs=[
            zspec(MLA_Q_RANK, OFF_CQ), zspec(MLA_KV_RANK, OFF_CKV), zspec(LANES, OFF_KR), zspec(LANES, OFF_KRR),
            pl.BlockSpec((tm, LANES), lambda b, t: (b * nt + t, 0)),
            pl.BlockSpec((tm, LANES), lambda b, t: (b * nt + t, 0)),
            pl.BlockSpec((1, MLA_Q_RANK), const), pl.BlockSpec((1, MLA_KV_RANK), const),
            pl.BlockSpec(wq1.shape, const), pl.BlockSpec(wq2.shape, const), pl.BlockSpec(wkv.shape, const),
            zspec(w, OFF_DQ), zspec(w, OFF_DK), zspec(w, OFF_DV),
        ],
        out_specs=[hspec(MLA_QK_PAD), hspec(MLA_QK_PAD), slab(MLA_HEADS),
                   pl.BlockSpec((1, 2 * DIFF_HEADS, tm, 2 * DIFF_QK), lambda b, t: (b, 0, t, 0)),
                   pl.BlockSpec((tm, w), lambda b, t: (b * nt + t, 0)),
                   slab(DIFF_HEADS)],
        out_shape=[jax.ShapeDtypeStruct((bsz, MLA_HEADS, n_pos, MLA_QK_PAD), BF16),
                   jax.ShapeDtypeStruct((bsz, MLA_HEADS, n_pos, MLA_QK_PAD), BF16),
                   jax.ShapeDtypeStruct((bsz, nt, MLA_HEADS, VT_ROWS, tm), BF16),
                   jax.ShapeDtypeStruct((bsz, 2 * DIFF_HEADS, n_pos, 2 * DIFF_QK), BF16),
                   jax.ShapeDtypeStruct((bsz * n_pos, w), BF16),
                   jax.ShapeDtypeStruct((bsz, nt, DIFF_HEADS, VT_ROWS, tm), BF16)],
        compiler_params=_cparams(("parallel", "parallel")),
        name="attn_prep",
    )(z, z, z, z, cos, sin, qg, kvg, wq1, wq2, wkv, z, z, z)


def _diff_kernel(qi_ref, ki_ref, qmin_ref, kmax_ref, qm_ref, k_ref, vt_ref, pq_ref, pk_ref, tab_ref,
                 lq1_ref, lk1_ref, lq2_ref, lk2_ref, sg_ref, o_ref, m_ref, acc_ref, *, t, tiles_per_row, lambda_init):
    qi = qi_ref[pl.program_id(1)]
    ki = ki_ref[pl.program_id(1)]
    nh = DIFF_HEADS
    hw = 2 * DIFF_QK

    @pl.when(ki == 0)
    def _():
        m_ref[...] = jnp.full_like(m_ref, NEG_INF)
        acc_ref[...] = jnp.zeros_like(acc_ref)

    nc = t // LANES
    qbase = (pl.program_id(0) * tiles_per_row + qi) * nc
    kbase = (pl.program_id(0) * tiles_per_row + ki) * nc

    def is_far(r, c):
        return qmin_ref[qbase + c] - kmax_ref[kbase + r] >= T5_TABLE - 1

    def all_of(pairs):
        out = None
        for r, c in pairs:
            out = is_far(r, c) if out is None else jnp.logical_and(out, is_far(r, c))
        return out

    def step(plan):
        sub = lambda r, c: (slice(r * LANES, (r + 1) * LANES), slice(c * LANES, (c + 1) * LANES))
        dist, local_mask = {}, None
        for r in range(nc):
            for c in range(nc):
                if plan[r][c] in ("gather", "diag"):
                    rows, cols = sub(r, c)
                    dist[r, c] = jnp.clip(pq_ref[0, :, cols] - pk_ref[rows, :], 0, T5_TABLE - 1)
                if plan[r][c] == "diag":
                    local_mask = _causal_mask_t(LANES)
        sts = []
        for h in range(nh):
            kh = k_ref[:, h * hw:(h + 1) * hw]
            shift = tab_ref[h:h + 1, T5_TABLE - 1:T5_TABLE]
            tab = jnp.broadcast_to(tab_ref[h:h + 1, :], (LANES, T5_TABLE))
            delta = {}
            for (r, c), d in dist.items():
                dl = jnp.take_along_axis(tab, d, axis=1, mode="promise_in_bounds") - shift
                delta[r, c] = jnp.where(local_mask, dl, NEG_INF) if plan[r][c] == "diag" else dl
            for mp in range(2):
                raw = lax.dot_general(kh, qm_ref[0, 2 * h + mp], (((1,), (1,)), ((), ())),
                                      preferred_element_type=F32)
                if delta or any("neg" in row for row in plan):
                    blocks = []
                    for r in range(nc):
                        row = []
                        for c in range(nc):
                            rows, cols = sub(r, c)
                            if plan[r][c] == "neg":
                                row.append(jnp.full((LANES, LANES), NEG_INF, F32))
                            elif (r, c) in delta:
                                row.append(raw[rows, cols] + delta[r, c])
                            else:
                                row.append(raw[rows, cols])
                        blocks.append(jnp.concatenate(row, axis=1))
                    raw = jnp.concatenate(blocks, axis=0)
                sts.append(raw)
        for j in range(2 * nh):
            shift = tab_ref[j // 2:j // 2 + 1, T5_TABLE - 1:T5_TABLE]
            _softmax_step_t(sts[j], vt_ref[0, 0, j // 2], m_ref, acc_ref, j, shift)

    every = [(r, c) for r in range(nc) for c in range(nc)]
    corner = (nc - 1, 0)
    plan_far = [["const"] * nc for _ in range(nc)]
    plan_corner = [["gather" if (r, c) == corner else "const" for c in range(nc)] for r in range(nc)]
    plan_full = [["gather"] * nc for _ in range(nc)]
    diag_kind = lambda r, c, beyond: "neg" if c < r else "diag" if c == r else "gather" if c == r + 1 else beyond
    plan_band = [[diag_kind(r, c, "const") for c in range(nc)] for r in range(nc)]
    plan_diag = [[diag_kind(r, c, "gather") for c in range(nc)] for r in range(nc)]

    below = ki < qi
    all_far = all_of(every)
    corner_far = all_of([p for p in every if p != corner])
    band_far = all_of([(r, c) for r, c in every if c >= r + 2])

    @pl.when(jnp.logical_and(below, all_far))
    def _():
        step(plan_far)

    @pl.when(jnp.logical_and(below, jnp.logical_and(corner_far, jnp.logical_not(all_far))))
    def _():
        step(plan_corner)

    @pl.when(jnp.logical_and(below, jnp.logical_not(corner_far)))
    def _():
        step(plan_full)

    @pl.when(jnp.logical_and(ki == qi, band_far))
    def _():
        step(plan_band)

    @pl.when(jnp.logical_and(ki == qi, jnp.logical_not(band_far)))
    def _():
        step(plan_diag)

    @pl.when(ki == qi)
    def _():
        lam =(jnp.exp(jnp.sum(lq1_ref[...] * lk1_ref[...], axis=-1, keepdims=True))
               - jnp.exp(jnp.sum(lq2_ref[...] * lk2_ref[...], axis=-1, keepdims=True)) + lambda_init)
        for h in range(nh):
            o = _finish_t(acc_ref[2 * h]) - lam * _finish_t(acc_ref[2 * h + 1])
            o = _rms(o, sg_ref[...]) * (1.0 - lambda_init)
            o_ref[:, h * DIFF_V:(h + 1) * DIFF_V] = o.astype(BF16)


def _diff_attn(qm, kb, vt, pos_q, pos_k, table, lq1, lk1, lq2, lk2, sg, *, bsz, n_pos, lambda_init, t=ATTN_TILE):
    t = min(t, n_pos)
    nb = n_pos // t
    w = DIFF_HEADS * DIFF_V
    const = lambda b, s, qt, kt, qmin, kmax: (0, 0)
    qi_tab, ki_tab = _causal_tiles(nb)
    chunks = pos_q.reshape(bsz * n_pos // LANES, LANES)
    qmin = jnp.min(chunks, axis=1)
    kmax = jnp.max(chunks, axis=1)
    return pl.pallas_call(
        functools.partial(_diff_kernel, t=t, tiles_per_row=nb, lambda_init=lambda_init),
        grid_spec=pltpu.PrefetchScalarGridSpec(
            num_scalar_prefetch=4,
            grid=(bsz, qi_tab.shape[0]),
            in_specs=[
                pl.BlockSpec((1, 2 * DIFF_HEADS, t, 2 * DIFF_QK), lambda b, s, qt, kt, qmin, kmax: (b, 0, qt[s], 0)),
                pl.BlockSpec((t, w), lambda b, s, qt, kt, qmin, kmax: (b * nb + kt[s], 0)),
                pl.BlockSpec((1, 1, DIFF_HEADS, VT_ROWS, t), lambda b, s, qt, kt, qmin, kmax: (b, kt[s], 0, 0, 0)),
                pl.BlockSpec((1, 1, t), lambda b, s, qt, kt, qmin, kmax: (b * nb + qt[s], 0, 0)),
                pl.BlockSpec((t, LANES), lambda b, s, qt, kt, qmin, kmax: (b * nb + kt[s], 0)),
                pl.BlockSpec((DIFF_HEADS, T5_TABLE), const),
                pl.BlockSpec((1, DIFF_QK), const), pl.BlockSpec((1, DIFF_QK), const),
                pl.BlockSpec((1, DIFF_QK), const), pl.BlockSpec((1, DIFF_QK), const),
                pl.BlockSpec((1, DIFF_V), const),
            ],
            out_specs=pl.BlockSpec((t, w), lambda b, s, qt, kt, qmin, kmax: (b * nb + qt[s], 0)),
            scratch_shapes=[pltpu.VMEM((2 * DIFF_HEADS, 1, t), F32),
                            pltpu.VMEM((2 * DIFF_HEADS, VT_ROWS, t), F32)],
        ),
        out_shape=jax.ShapeDtypeStruct((bsz * n_pos, w), BF16),
        compiler_params=_cparams(("parallel", "arbitrary")),
        name="diff_attn",
    )(qi_tab, ki_tab, qmin, kmax, qm, kb, vt, pos_q, pos_k, table, lq1, lk1, lq2, lk2, sg)


def _ret_kernel(q_ref, qr_ref, k_ref, kr_ref, v_ref, g_ref, cos_ref, sin_ref, o_ref, st_ref, *, tm):
    t = pl.program_id(1)
    c = RET_CHUNK
    nh = RET_HEADS
    w = nh * RET_QK

    @pl.when(t == 0)
    def _():
        st_ref[...] = jnp.zeros_like(st_ref)

    log_gamma = [math.log(1.0 - 2.0 ** (-5.0 - h)) for h in range(nh)]
    lane = lax.broadcasted_iota(jnp.int32, (1, w), 1)
    lg_lane = jnp.zeros((1, w), F32)
    for h in range(nh):
        lg_lane = jnp.where(lane // RET_QK == h, log_gamma[h], lg_lane)
    tok = lax.broadcasted_iota(jnp.int32, (c, 1), 0).astype(F32)
    q_decay = jnp.exp(lg_lane * (tok + 1.0))
    k_decay = jnp.exp(lg_lane * (c - 1.0 - tok))
    ri = lax.broadcasted_iota(jnp.int32, (c, c), 0)
    ci = lax.broadcasted_iota(jnp.int32, (c, c), 1)
    rel = (ri - ci).astype(F32)
    intra = [jnp.where(rel >= 0, jnp.exp(log_gamma[h] * jnp.maximum(rel, 0.0)), 0.0) for h in range(nh)]
    head_lanes = [(lane // RET_QK == h) for h in range(nh)]

    states = [st_ref[h * RET_QK:(h + 1) * RET_QK, :] for h in range(nh)]
    n_chunks = tm // c
    chunk_rows = [slice(j * c, (j + 1) * c) for j in range(n_chunks)]
    head_cols = [slice(h * RET_V, (h + 1) * RET_V) for h in range(nh)]

    vb, qdh, scores, kvs = {}, {}, {}, {}
    for j, rows in enumerate(chunk_rows):
        cos = jnp.concatenate([cos_ref[rows, :]] * (w // LANES), axis=1)
        sin = jnp.concatenate([sin_ref[rows, :]] * (w // LANES), axis=1)
        q = q_ref[rows, :] * cos + qr_ref[rows, :] * sin
        k = (k_ref[rows, :] * cos + kr_ref[rows, :] * sin) * (RET_QK ** -0.5)
        kb = k.astype(BF16)
        qd = q * q_decay
        kdt = jnp.transpose(k * k_decay).astype(BF16)
        for h in range(nh):
            vb[j, h] = v_ref[rows, head_cols[h]].astype(BF16)
            qh = jnp.where(head_lanes[h], q, 0.0).astype(BF16)
            scores[j, h] = lax.dot_general(qh, kb, (((1,), (1,)), ((), ())), preferred_element_type=F32) * intra[h]
            qdh[j, h] = jnp.where(head_lanes[h], qd, 0.0).astype(BF16)
            kvs[j, h] = jnp.dot(kdt[h * RET_QK:(h + 1) * RET_QK, :], vb[j, h], preferred_element_type=F32)

    state_b = []
    for j in range(n_chunks):
        state_b.append(jnp.concatenate(states, axis=0).astype(BF16))
        states = [states[h] * math.exp(log_gamma[h] * c) + kvs[j, h] for h in range(nh)]
    for h in range(nh):
        st_ref[h * RET_QK:(h + 1) * RET_QK, :] = states[h]

    for j, rows in enumerate(chunk_rows):
        for h in range(nh):
            inner = jnp.dot(scores[j, h].astype(BF16), vb[j, h], preferred_element_type=F32)
            cross = jnp.dot(qdh[j, h], state_b[j], preferred_element_type=F32)
            o = inner + cross
            mu = jnp.mean(o, axis=-1, keepdims=True)
            oc = o - mu
            var = jnp.mean(oc * oc, axis=-1, keepdims=True)
            o = oc * lax.rsqrt(var + LN_EPS)
            gh = g_ref[rows, head_cols[h]]
            o_ref[rows, head_cols[h]] = (gh * jax.nn.sigmoid(gh) * o).astype(BF16)


def _retention(z, cos, sin, *, bsz, n_pos, tm=512):
    tm = min(tm, n_pos)
    nt = n_pos // tm
    w = RET_HEADS * RET_QK
    wv = RET_HEADS * RET_V
    zspec = lambda width, off: pl.BlockSpec((tm, width), lambda b, t: (b * nt + t, off // width))
    tspec = pl.BlockSpec((tm, LANES), lambda b, t: (b * nt + t, 0))
    return pl.pallas_call(
        functools.partial(_ret_kernel, tm=tm),
        grid=(bsz, nt),
        in_specs=[zspec(w, OFF_RQ), zspec(w, OFF_RQR), zspec(w, OFF_RK), zspec(w, OFF_RKR),
                  zspec(wv, OFF_RV), zspec(wv, OFF_RG), tspec, tspec],
        out_specs=pl.BlockSpec((tm, wv), lambda b, t: (b * nt + t, 0)),
        out_shape=jax.ShapeDtypeStruct((bsz * n_pos, wv), BF16),
        scratch_shapes=[pltpu.VMEM((w, RET_V), F32)],
        compiler_params=_cparams(("parallel", "arbitrary")),
        name="retention",
    )(z, z, z, z, z, z, cos, sin)


def _outproj_ln_kernel(y0_ref, y1_ref, y2_ref, y3_ref, w_ref, x_ref, g_ref, b_ref, o_ref):
    acc = None
    for j, y_ref in enumerate((y0_ref, y1_ref, y2_ref, y3_ref)):
        part = jnp.dot(y_ref[...], w_ref[j * 512:(j + 1) * 512, :], preferred_element_type=F32)
        acc = part if acc is None else acc + part
    o_ref[...] = _layer_norm(ALPHA * x_ref[...] + acc, g_ref[...], b_ref[...])


def _outproj_ln(ys, w, x, g, b, *, tm=512):
    m, d = x.shape
    tm = min(tm, m)
    yspec = pl.BlockSpec((tm, 512), lambda i: (i, 0))
    return pl.pallas_call(
        _outproj_ln_kernel,
        grid=(m // tm,),
        in_specs=[yspec, yspec, yspec, yspec,
                  pl.BlockSpec(w.shape, lambda i: (0, 0)),
                  pl.BlockSpec((tm, d), lambda i: (i, 0)),
                  pl.BlockSpec((1, d), lambda i: (0, 0)),
                  pl.BlockSpec((1, d), lambda i: (0, 0))],
        out_specs=pl.BlockSpec((tm, d), lambda i: (i, 0)),
        out_shape=jax.ShapeDtypeStruct((m, d), F32),
        compiler_params=_cparams(("parallel",)),
        name="outproj_ln",
    )(*ys, w, x, g, b)


def _rot_cols(w, heads, dim):
    k = w.shape[0]
    w = w.reshape(k, heads, 2, dim // 2)
    return jnp.concatenate([-w[:, :, 1], w[:, :, 0]], axis=-1).reshape(k, heads * dim)


def _pad_cols(w, width):
    return jnp.pad(w, ((0, 0), (0, width - w.shape[1])))


def _wide_w_in(w_in):
    sizes = (S5_WIDTH, MLA_Q_RANK, MLA_KV_RANK, MLA_ROPE,
             RET_HEADS * RET_QK, RET_HEADS * RET_QK, RET_HEADS * RET_V, RET_HEADS * RET_V,
             DIFF_HEADS * 2 * DIFF_QK, DIFF_HEADS * 2 * DIFF_QK, DIFF_HEADS * DIFF_V)
    offs = np.concatenate([[0], np.cumsum(sizes)])
    (s5_u, cq, ckv, kr, rq, rk, rv, rg, dq, dk, dv) = [w_in[:, offs[i]:offs[i + 1]] for i in range(len(sizes))]
    cols = [s5_u, cq, rv, rg, dq, dk, dv,
            rq, _rot_cols(rq, RET_HEADS, RET_QK), rk, _rot_cols(rk, RET_HEADS, RET_QK),
            ckv, _pad_cols(kr, LANES), _pad_cols(_rot_cols(kr, 1, MLA_ROPE), LANES),
            jnp.zeros((w_in.shape[0], IN_WIDE - IN_USED), w_in.dtype)]
    return jnp.concatenate(cols, axis=1).astype(BF16)


def _mla_q_weights(w_uq):
    k = w_uq.shape[0]
    w = w_uq.reshape(k, MLA_HEADS, MLA_NOPE + MLA_ROPE)
    rope = w[:, :, MLA_NOPE:]
    pad = MLA_QK_PAD - MLA_NOPE - MLA_ROPE
    w1 = jnp.pad(w, ((0, 0), (0, 0), (0, pad))).reshape(k, MLA_HEADS * MLA_QK_PAD)
    rot = _rot_cols(rope.reshape(k, MLA_HEADS * MLA_ROPE), MLA_HEADS, MLA_ROPE).reshape(k, MLA_HEADS, MLA_ROPE)
    w2 = jnp.pad(rot, ((0, 0), (0, 0), (0, LANES - MLA_ROPE))).reshape(k, MLA_HEADS * LANES)
    return w1.astype(BF16), w2.astype(BF16)


def kernel(x, p, positions, rel_bias, ffn1_w_gate, ffn1_w_up, ffn1_w_down, ln1_g, ln1_b, w_in, w_out, ln2_g, ln2_b, s5_lambda_re, s5_lambda_im, s5_log_dt, s5_b_re, s5_b_im, s5_c_re, s5_c_im, s5_d, s5_w_glu, s5_b_glu, mla_q_norm_g, mla_w_uq, mla_kv_norm_g, mla_w_ukv, diff_lambda_q1, diff_lambda_k1, diff_lambda_q2, diff_lambda_k2, diff_subln_g, ffn2_w_gate, ffn2_w_up, ffn2_w_down, ple_w_gate, ple_b_gate, ple_w_proj, ln3_g, ln3_b):
    bsz, n_pos, d = x.shape
    m = bsz * n_pos
    depth = ffn1_w_gate.shape[0]
    assert d == D_MODEL and n_pos % S5_TM == 0 and n_pos % ATTN_TILE == 0 and m % FFN_TM == 0, (x.shape,)
    xf = x.reshape(m, d)
    row = lambda v: v.reshape(1, -1)

    cos, sin = _rope_tables(positions.reshape(m, 1))
    table = _bias_table(rel_bias)
    t_diff = ATTN_TILE
    pos_q = positions.reshape(bsz * (n_pos // t_diff), 1, t_diff)
    pos_k = jnp.broadcast_to(positions.reshape(m, 1), (m, LANES))

    for i in range(depth):
        f1g, f1u, f1d = _cast_layer(i, ffn1_w_gate, ffn1_w_up, ffn1_w_down)
        f2g, f2u, f2d, pwg, wo = _cast_layer(i, ffn2_w_gate, ffn2_w_up, ffn2_w_down, ple_w_gate, w_out)
        xf = _ffn_ln(xf, f1g, f1u, f1d, row(ln1_g[i]), row(ln1_b[i]))
        z = _inproj(xf, _wide_w_in(w_in[i]))

        apr, api, bbr, bbi = _s5_discretise(s5_lambda_re[i], s5_lambda_im[i], s5_log_dt[i], s5_b_re[i], s5_b_im[i])
        bh = jnp.concatenate([_block_diag_in(bbr), _block_diag_in(bbi)], axis=2).astype(BF16)
        ch = jnp.concatenate([_block_diag_out(s5_c_re[i]), -_block_diag_out(s5_c_im[i])], axis=1).astype(BF16)
        apr = jnp.repeat(apr, S5_TM // S5_STEPS, axis=0)
        api = jnp.repeat(api, S5_TM // S5_STEPS, axis=0)
        y_s5 = _s5_mixer(z, bh, ch, apr, api, row(s5_d[i]), s5_w_glu[i].astype(BF16), row(s5_b_glu[i]),
                         bsz=bsz, n_pos=n_pos)

        wq1, wq2 = _mla_q_weights(mla_w_uq[i])
        q, k, vt, dqm, dkb, dvt = _attn_prep(z, cos, sin, row(mla_q_norm_g[i]), row(mla_kv_norm_g[i]), wq1, wq2,
                                             mla_w_ukv[i].astype(BF16), bsz=bsz, n_pos=n_pos)
        y_mla = _mla_flash(q, k, vt).reshape(m, MLA_HEADS * MLA_V)

        y_ret = _retention(z, cos, sin, bsz=bsz, n_pos=n_pos)

        lambda_init = 0.8 - 0.6 * math.exp(-0.3 * i)
        y_diff = _diff_attn(dqm, dkb, dvt, pos_q, pos_k, table, row(diff_lambda_q1[i]), row(diff_lambda_k1[i]),
                            row(diff_lambda_q2[i]), row(diff_lambda_k2[i]), row(diff_subln_g[i]),
                            bsz=bsz, n_pos=n_pos, lambda_init=lambda_init, t=t_diff)

        xf = _outproj_ln((y_s5, y_mla, y_ret, y_diff), wo, xf, row(ln2_g[i]), row(ln2_b[i]))

        res, xb = _ple(xf, p.reshape(depth, m, PLE_DIM), i, pwg, row(ple_b_gate[i]), ple_w_proj[i].astype(BF16))
        xf = _ffn_ln(xb, f2g, f2u, f2d, row(ln3_g[i]), row(ln3_b[i]), res)
    return xf.reshape(bsz, n_pos, d)
```

```python
import functools
import math

import numpy as np
import jax
import jax.numpy as jnp
from jax import lax
from jax.experimental import pallas as pl
from jax.experimental.pallas import tpu as pltpu

F32 = jnp.float32
BF16 = jnp.bfloat16

LANES = 128
SUBLANES = 8
VMEM_BYTES = 64 * 1024 * 1024
VMEM_LIMIT_BYTES = VMEM_BYTES - 8 * 1024 * 1024
FFN_VMEM_LIMIT_BYTES = VMEM_BYTES - 4 * 1024 * 1024

D_MODEL = 2048
DEPTH = 2
PLE_DIM = 256
D_FF = 5632
ALPHA = (2 * DEPTH) ** 0.25
ROPE_THETA = 10000.0
NEG_INF = -1e30
LN_EPS = 1e-5
RMS_EPS = 1e-6

S5_WIDTH = 512
S5_GROUP = 16
S5_GROUPS = 32
S5_STATE = 64
S5_NSTATE = S5_GROUPS * S5_STATE
S5_HALF = S5_NSTATE // 2
S5_TM = 512
S5_STEPS = S5_TM // SUBLANES

MLA_HEADS = 4
MLA_Q_RANK = 512
MLA_KV_RANK = 128
MLA_NOPE = 128
MLA_ROPE = 64
MLA_V = 128
MLA_QK_PAD = 256

RET_HEADS = 4
RET_QK = 64
RET_V = 128
RET_CHUNK = 128

DIFF_HEADS = 4
DIFF_QK = 64
DIFF_V = 128

T5_BUCKETS = 32
T5_MAX_DIST = 128
T5_TABLE = 128

LOG2E = math.log2(math.e)
VT_ROWS = 144
ATTN_TILE = 512
FFN_TM = 1024

OFF_S5 = 0
OFF_CQ = 512
OFF_RV = 1024
OFF_RG = 1536
OFF_DQ = 2048
OFF_DK = 2560
OFF_DV = 3072
OFF_RQ = 3584
OFF_RQR = 3840
OFF_RK = 4096
OFF_RKR = 4352
OFF_CKV = 4608
OFF_KR = 4736
OFF_KRR = 4864
IN_USED = 4992
IN_WIDE = 5120
IN_TN = 2560


def _cparams(sem, vmem_limit_bytes=VMEM_LIMIT_BYTES):
    return pltpu.CompilerParams(dimension_semantics=sem, vmem_limit_bytes=vmem_limit_bytes)


def _layer_norm(y, g, b):
    mu = jnp.mean(y, axis=-1, keepdims=True)
    yc = y - mu
    var = jnp.mean(yc * yc, axis=-1, keepdims=True)
    return yc * lax.rsqrt(var + LN_EPS) * g + b


CAST_STEPS = 16


def _cast_kernel(*refs):
    n = len(refs) // 2
    for src, dst in zip(refs[:n], refs[n:]):
        dst[...] = src[0].astype(BF16)


def _cast_layer(layer, *stacked):
    in_specs, out_specs, out_shape = [], [], []
    for w in stacked:
        _, r, c = w.shape
        tr = r // CAST_STEPS
        in_specs.append(pl.BlockSpec((1, tr, c), lambda s: (layer, s, 0)))
        out_specs.append(pl.BlockSpec((tr, c), lambda s: (s, 0)))
        out_shape.append(jax.ShapeDtypeStruct((r, c), BF16))
    return pl.pallas_call(
        _cast_kernel,
        grid=(CAST_STEPS,),
        in_specs=in_specs,
        out_specs=out_specs,
        out_shape=out_shape,
        compiler_params=_cparams(("parallel",)),
        name="cast_weights",
    )(*stacked)


FFN_ROW_CHUNK = 128


def _ffn_ln_kernel(*refs, nf, tm, has_res):
    if has_res:
        xb_ref, res_ref, wg_ref, wu_ref, wd_ref, g_ref, b_ref, o_ref = refs
    else:
        res_ref, wg_ref, wu_ref, wd_ref, g_ref, b_ref, o_ref, xb_ref = refs
    f = pl.program_id(1)
    chunks = tm // FFN_ROW_CHUNK

    def rows(c):
        return pl.ds(pl.multiple_of(c * FFN_ROW_CHUNK, FFN_ROW_CHUNK), FFN_ROW_CHUNK)

    if not has_res:
        @pl.when(f == 0)
        def _():
            def cast(c, _):
                xb_ref[rows(c), :] = res_ref[rows(c), :].astype(BF16)
                return 0
            lax.fori_loop(0, chunks, cast, 0)

    def hidden():
        xb = xb_ref[...]
        gate = jnp.dot(xb, wg_ref[...], preferred_element_type=F32)
        up = jnp.dot(xb, wu_ref[...], preferred_element_type=F32)
        return (gate * jax.nn.sigmoid(gate) * up).astype(BF16)

    @pl.when(f == 0)
    def _():
        o_ref[...] = jnp.dot(hidden(), wd_ref[...], preferred_element_type=F32)

    @pl.when(jnp.logical_and(f > 0, f < nf - 1))
    def _():
        o_ref[...] += jnp.dot(hidden(), wd_ref[...], preferred_element_type=F32)

    @pl.when(f == nf - 1)
    def _():
        h = hidden()
        groups = [slice(c * FFN_ROW_CHUNK, (c + 1) * FFN_ROW_CHUNK) for c in range(chunks)]
        parts = [jnp.dot(h[r, :], wd_ref[...], preferred_element_type=F32) for r in groups]
        for r, part in zip(groups, parts):
            res = res_ref[r, :]
            y = (res if has_res else ALPHA * res) + 0.5 * (o_ref[r, :] + part)
            o_ref[r, :] = _layer_norm(y, g_ref[...], b_ref[...])


def _ffn_ln(x, wg, wu, wd, g, b, res=None, *, tm=FFN_TM, tf=512):
    m, d = x.shape
    f_dim = wg.shape[1]
    tm = min(tm, m)
    nf = f_dim // tf
    assert f_dim % tf == 0 and nf >= 2, (f_dim, tf)
    xspec = pl.BlockSpec((tm, d), lambda i, f: (i, 0))
    in_specs = [xspec] * (2 if res is not None else 1) + [
        pl.BlockSpec((d, tf), lambda i, f: (0, f)),
        pl.BlockSpec((d, tf), lambda i, f: (0, f)),
        pl.BlockSpec((tf, d), lambda i, f: (f, 0)),
        pl.BlockSpec((1, d), lambda i, f: (0, 0)),
        pl.BlockSpec((1, d), lambda i, f: (0, 0)),
    ]
    args = ([x, res] if res is not None else [x]) + [wg, wu, wd, g, b]
    return pl.pallas_call(
        functools.partial(_ffn_ln_kernel, nf=nf, tm=tm, has_res=res is not None),
        grid=(m // tm, nf),
        in_specs=in_specs,
        out_specs=pl.BlockSpec((tm, d), lambda i, f: (i, 0)),
        out_shape=jax.ShapeDtypeStruct((m, d), F32),
        scratch_shapes=[] if res is not None else [pltpu.VMEM((tm, d), BF16)],
        compiler_params=_cparams(("parallel", "arbitrary"), FFN_VMEM_LIMIT_BYTES),
        name="ffn_ln",
    )(*args)


def _ple_kernel(x_ref, p_ref, wg_ref, bg_ref, wp_ref, res_ref, xb_ref):
    x = x_ref[...]
    xb = x.astype(BF16)
    xb_ref[...] = xb
    pb = p_ref[...].astype(BF16)
    gate = jax.nn.sigmoid(jnp.dot(xb, wg_ref[...], preferred_element_type=F32) + bg_ref[...])
    res_ref[...] = ALPHA * x + gate * jnp.dot(pb, wp_ref[...], preferred_element_type=F32)


def _ple(x, p, layer, wg, bg, wp, *, tm=512):
    m, d = x.shape
    tm = min(tm, m)
    return pl.pallas_call(
        _ple_kernel,
        grid=(m // tm,),
        in_specs=[
            pl.BlockSpec((tm, d), lambda i: (i, 0)),
            pl.BlockSpec((None, tm, PLE_DIM), lambda i: (layer, i, 0)),
            pl.BlockSpec((d, d), lambda i: (0, 0)),
            pl.BlockSpec((1, d), lambda i: (0, 0)),
            pl.BlockSpec((PLE_DIM, d), lambda i: (0, 0)),
        ],
        out_specs=[pl.BlockSpec((tm, d), lambda i: (i, 0))] * 2,
        out_shape=[jax.ShapeDtypeStruct((m, d), F32), jax.ShapeDtypeStruct((m, d), BF16)],
        compiler_params=_cparams(("parallel",)),
        name="ple",
    )(x, p, wg, bg, wp)


def _inproj_kernel(x_ref, w_ref, o_ref):
    o_ref[...] = jnp.dot(x_ref[...].astype(BF16), w_ref[...], preferred_element_type=F32)


def _inproj(x, w, *, tm=512):
    m, d = x.shape
    n = w.shape[1]
    tm = min(tm, m)
    return pl.pallas_call(
        _inproj_kernel,
        grid=(n // IN_TN, m // tm),
        in_specs=[
            pl.BlockSpec((tm, d), lambda j, i: (i, 0)),
            pl.BlockSpec((d, IN_TN), lambda j, i: (0, j)),
        ],
        out_specs=pl.BlockSpec((tm, IN_TN), lambda j, i: (i, j)),
        out_shape=jax.ShapeDtypeStruct((m, n), F32),
        compiler_params=_cparams(("parallel", "parallel")),
        name="inproj",
    )(x, w)


def _rope_kernel(pos_ref, freq_ref, cos_ref, sin_ref):
    ang = pos_ref[...].astype(F32) * freq_ref[...]
    cos_ref[...] = jnp.cos(ang)
    sin_ref[...] = jnp.sin(ang)


def _rope_tables(pos_col, *, tm=1024):
    m = pos_col.shape[0]
    tm = min(tm, m)
    half = MLA_ROPE // 2
    inv = 1.0 / (ROPE_THETA ** (np.arange(0, MLA_ROPE, 2, dtype=np.float64) / MLA_ROPE))
    freq = jnp.asarray(np.tile(inv, LANES // half)[None, :], F32)
    return pl.pallas_call(
        _rope_kernel,
        grid=(m // tm,),
        in_specs=[pl.BlockSpec((tm, 1), lambda i: (i, 0)), pl.BlockSpec((1, LANES), lambda i: (0, 0))],
        out_specs=[pl.BlockSpec((tm, LANES), lambda i: (i, 0))] * 2,
        out_shape=[jax.ShapeDtypeStruct((m, LANES), F32)] * 2,
        compiler_params=_cparams(("parallel",)),
        name="rope_tables",
    )(pos_col, freq)


def _t5_bucket_static():
    n = np.arange(T5_TABLE)
    max_exact = T5_BUCKETS // 2
    nf = np.maximum(n, 1).astype(np.float64)
    large = max_exact + (np.log(nf / max_exact) / math.log(T5_MAX_DIST / max_exact)
                         * (T5_BUCKETS - max_exact)).astype(np.int64)
    large = np.minimum(large, T5_BUCKETS - 1)
    return np.where(n < max_exact, n, large)


def _bias_table_kernel(rb_ref, onehot_ref, o_ref):
    rb = rb_ref[...]
    oh = onehot_ref[...]
    rows = [jnp.sum(oh * rb[:, h:h + 1], axis=0, keepdims=True) for h in range(DIFF_HEADS)]
    o_ref[...] = jnp.concatenate(rows, axis=0) * LOG2E


def _bias_table(rel_bias):
    bucket = _t5_bucket_static()
    onehot = jnp.asarray((np.arange(T5_BUCKETS)[:, None] == bucket[None, :]).astype(np.float32))
    return pl.pallas_call(
        _bias_table_kernel,
        out_shape=jax.ShapeDtypeStruct((DIFF_HEADS, T5_TABLE), F32),
        name="t5_bias_table",
    )(rel_bias, onehot)


def _s5_disc_kernel(lr_ref, li_ref, ldt_ref, br_ref, bi_ref, apr_ref, api_ref, bbr_ref, bbi_ref):
    lr = lr_ref[...]
    li = li_ref[...]
    dt = jnp.exp(ldt_ref[...])
    k = (lax.broadcasted_iota(jnp.int32, (S5_STEPS, 1), 0) + 1).astype(F32)
    mag = jnp.exp(lr * dt * k)
    apr_ref[...] = mag * jnp.cos(li * dt * k)
    api_ref[...] = mag * jnp.sin(li * dt * k)
    mag1 = jnp.exp(lr * dt)
    ar = mag1 * jnp.cos(li * dt)
    ai = mag1 * jnp.sin(li * dt)
    den = lr * lr + li * li
    fr = ((ar - 1.0) * lr + ai * li) / den
    fi = (ai * lr - (ar - 1.0) * li) / den
    br = br_ref[...]
    bi = bi_ref[...]
    bbr_ref[...] = fr * br - fi * bi
    bbi_ref[...] = fr * bi + fi * br


def _s5_discretise(lam_re, lam_im, log_dt, b_re, b_im):
    n = S5_NSTATE
    lr = lam_re.reshape(1, n)
    li = lam_im.reshape(1, n)
    ldt = jnp.repeat(log_dt, S5_STATE).reshape(1, n)
    br = jnp.transpose(b_re, (2, 0, 1)).reshape(S5_GROUP, n)
    bi = jnp.transpose(b_im, (2, 0, 1)).reshape(S5_GROUP, n)
    return pl.pallas_call(
        _s5_disc_kernel,
        out_shape=[jax.ShapeDtypeStruct((S5_STEPS, n), F32)] * 2 + [jax.ShapeDtypeStruct((S5_GROUP, n), F32)] * 2,
        name="s5_discretise",
    )(lr, li, ldt, br, bi)


def _block_diag_in(bb):
    gh = S5_GROUPS // 2
    eye = jnp.eye(gh, dtype=bb.dtype)
    v = bb.reshape(S5_GROUP, 2, gh, S5_STATE)
    out = eye[None, :, None, :, None] * jnp.transpose(v, (1, 0, 2, 3))[:, None, :, :, :]
    return out.reshape(2, gh * S5_GROUP, S5_HALF)


def _block_diag_out(c):
    gh = S5_GROUPS // 2
    eye = jnp.eye(gh, dtype=c.dtype)
    v = jnp.transpose(c, (0, 2, 1)).reshape(2, gh, S5_STATE, S5_GROUP)
    out = v[:, :, :, None, :] * eye[None, :, None, :, None]
    return out.reshape(2, S5_HALF, gh * S5_GROUP)


def _s5_kernel(u0_ref, u1_ref, u2_ref, u3_ref, bh_ref, ch_ref, pwr_ref, pwi_ref, d_ref, wglu_ref, bglu_ref, o_ref,
               h_ref, hb_ref, cr_ref, ci_ref, cinr_ref, cini_ref, yo_ref):
    u_refs = (u0_ref, u1_ref, u2_ref, u3_ref)
    t = pl.program_id(1)
    nh = S5_HALF
    cw = S5_WIDTH // 2

    @pl.when(t == 0)
    def _():
        cr_ref[...] = jnp.zeros_like(cr_ref)
        ci_ref[...] = jnp.zeros_like(ci_ref)

    u = jnp.concatenate(
        [jnp.concatenate([r[pl.ds(j, SUBLANES, stride=S5_STEPS), :] for j in range(S5_STEPS)], axis=0)
         for r in u_refs], axis=1)
    ub = u.astype(BF16)
    ys = []
    for half in range(2):
        h_ref[:, pl.ds(2 * half * nh, 2 * nh)] = jnp.dot(ub[:, half * cw:(half + 1) * cw], bh_ref[half],
                                                          preferred_element_type=F32)
    for half in range(2):
        st = pl.ds(half * nh, nh)
        re = pl.ds(2 * half * nh, nh)
        im = pl.ds((2 * half + 1) * nh, nh)
        ar = pwr_ref[0:SUBLANES, st]
        ai = pwi_ref[0:SUBLANES, st]

        def scan(j, carry, re=re, im=im, ar=ar, ai=ai):
            hr, hi = carry
            r0 = pl.multiple_of(j * SUBLANES, SUBLANES)
            nr = ar * hr - ai * hi + h_ref[pl.ds(r0, SUBLANES), re]
            ni = ar * hi + ai * hr + h_ref[pl.ds(r0, SUBLANES), im]
            h_ref[pl.ds(r0, SUBLANES), re] = nr
            h_ref[pl.ds(r0, SUBLANES), im] = ni
            return nr, ni

        zero = jnp.zeros((SUBLANES, nh), F32)
        er, ei = lax.fori_loop(0, S5_STEPS, scan, (zero, zero), unroll=True)

        a_seg_r = pwr_ref[S5_TM - 1:S5_TM, st]
        a_seg_i = pwi_ref[S5_TM - 1:S5_TM, st]
        cr = cr_ref[:, st]
        ci = ci_ref[:, st]
        for s in range(SUBLANES):
            cinr_ref[s:s + 1, :] = cr
            cini_ref[s:s + 1, :] = ci
            cr, ci = (er[s:s + 1] + a_seg_r * cr - a_seg_i * ci, ei[s:s + 1] + a_seg_r * ci + a_seg_i * cr)
        cr_ref[:, st] = cr
        ci_ref[:, st] = ci

        cin_r = cinr_ref[...]
        cin_i = cini_ref[...]

        def fix(jj, _, re=re, im=im, st=st, cin_r=cin_r, cin_i=cin_i):
            r0 = pl.multiple_of(jj * 2 * SUBLANES, 2 * SUBLANES)
            out_r, out_i = [], []
            for k in range(2):
                rows = pl.ds(r0 + k * SUBLANES, SUBLANES)
                pr = pwr_ref[rows, st]
                pi = pwi_ref[rows, st]
                out_r.append(h_ref[rows, re] + pr * cin_r - pi * cin_i)
                out_i.append(h_ref[rows, im] + pr * cin_i + pi * cin_r)
            hb_ref[pl.ds(r0, 2 * SUBLANES), re] = jnp.concatenate(out_r, axis=0).astype(BF16)
            hb_ref[pl.ds(r0, 2 * SUBLANES), im] = jnp.concatenate(out_i, axis=0).astype(BF16)
            return 0

        lax.fori_loop(0, S5_STEPS // 2, fix, 0, unroll=True)
        ys.append(jnp.dot(hb_ref[:, pl.ds(2 * half * nh, 2 * nh)], ch_ref[half], preferred_element_type=F32))

    y = jnp.concatenate(ys, axis=1) + d_ref[...] * u
    c0 = math.sqrt(2.0 / math.pi)
    y = 0.5 * y * (1.0 + jnp.tanh(c0 * (y + 0.044715 * (y * y * y))))
    gate = jax.nn.sigmoid(jnp.dot(y.astype(BF16), wglu_ref[...], preferred_element_type=F32) + bglu_ref[...])
    out = y * gate
    tm = out.shape[0]
    for c in range(S5_WIDTH // LANES):
        yo_ref[pl.ds(c * tm, tm), :] = out[:, c * LANES:(c + 1) * LANES]
    for c in range(S5_WIDTH // LANES):
        for s in range(SUBLANES):
            o_ref[s * S5_STEPS:(s + 1) * S5_STEPS, c * LANES:(c + 1) * LANES] = (
                yo_ref[pl.ds(c * tm + s, S5_STEPS, stride=SUBLANES), :].astype(BF16))


def _s5_mixer(z, bh, ch, pwr, pwi, d, wglu, bglu, *, bsz, n_pos):
    tm = S5_TM
    nt = n_pos // tm
    m = bsz * n_pos
    const = lambda b, t: (0, 0)
    const3 = lambda b, t: (0, 0, 0)
    return pl.pallas_call(
        _s5_kernel,
        grid=(bsz, nt),
        in_specs=[
            *[pl.BlockSpec((tm, LANES), lambda b, t, c=c: (b * nt + t, OFF_S5 // LANES + c))
              for c in range(S5_WIDTH // LANES)],
            pl.BlockSpec(bh.shape, const3),
            pl.BlockSpec(ch.shape, const3),
            pl.BlockSpec((S5_TM, S5_NSTATE), const),
            pl.BlockSpec((S5_TM, S5_NSTATE), const),
            pl.BlockSpec((1, S5_WIDTH), const),
            pl.BlockSpec((S5_WIDTH, S5_WIDTH), const),
            pl.BlockSpec((1, S5_WIDTH), const),
        ],
        out_specs=pl.BlockSpec((tm, S5_WIDTH), lambda b, t: (b * nt + t, 0)),
        out_shape=jax.ShapeDtypeStruct((m, S5_WIDTH), BF16),
        scratch_shapes=[pltpu.VMEM((tm, 2 * S5_NSTATE), F32), pltpu.VMEM((tm, 2 * S5_NSTATE), BF16),
                        pltpu.VMEM((1, S5_NSTATE), F32), pltpu.VMEM((1, S5_NSTATE), F32),
                        pltpu.VMEM((8, S5_HALF), F32), pltpu.VMEM((8, S5_HALF), F32),
                        pltpu.VMEM((S5_WIDTH // LANES * tm, LANES), F32)],
        compiler_params=_cparams(("parallel", "arbitrary")),
        name="s5_mixer",
    )(z, z, z, z, bh, ch, pwr, pwi, d, wglu, bglu)


def _rms(x, g):
    return x * lax.rsqrt(jnp.mean(x * x, axis=-1, keepdims=True) + RMS_EPS) * g


def _mla_prep_kernel(cq_ref, ckv_ref, kr_ref, krr_ref, cos_ref, sin_ref, qg_ref, kvg_ref,
                     wq1_ref, wq2_ref, wkv_ref, q_ref, k_ref, vt_ref):
    cos = cos_ref[...]
    sin = sin_ref[...]
    scale = (MLA_NOPE + MLA_ROPE) ** -0.5 * LOG2E
    ones = jnp.ones((VT_ROWS - MLA_V, cos.shape[0]), BF16)
    cqn = _rms(cq_ref[...], qg_ref[...]).astype(BF16)
    z1 = jnp.dot(cqn, wq1_ref[...], preferred_element_type=F32)
    z2 = jnp.dot(cqn, wq2_ref[...], preferred_element_type=F32)
    ckvn = _rms(ckv_ref[...], kvg_ref[...]).astype(BF16)
    zkv = jnp.dot(ckvn, wkv_ref[...], preferred_element_type=F32)
    k_rope = (kr_ref[...] * cos + krr_ref[...] * sin).astype(BF16)
    for h in range(MLA_HEADS):
        a = h * MLA_QK_PAD
        q_nope = z1[:, a:a + MLA_NOPE]
        q_rope = z1[:, a + MLA_NOPE:a + MLA_QK_PAD] * cos + z2[:, h * LANES:(h + 1) * LANES] * sin
        q_ref[0, h, :, 0:MLA_NOPE] = (q_nope * scale).astype(BF16)
        q_ref[0, h, :, MLA_NOPE:MLA_QK_PAD] = (q_rope * scale).astype(BF16)
        k_ref[0, h, :, 0:MLA_NOPE] = zkv[:, a:a + MLA_NOPE].astype(BF16)
        k_ref[0, h, :, MLA_NOPE:MLA_QK_PAD] = k_rope
        vt_ref[0, 0, h, 0:MLA_V, :] = jnp.transpose(zkv[:, a + MLA_NOPE:a + MLA_NOPE + MLA_V]).astype(BF16)
        vt_ref[0, 0, h, MLA_V:VT_ROWS, :] = ones


def _softmax_step_t(st, vt, m_ref, acc_ref, idx, shift=None):
    m_old = m_ref[idx]
    cur = jnp.max(st, axis=0, keepdims=True)
    if shift is not None:
        cur = cur + shift
    m_new = jnp.maximum(m_old, cur)
    p = jnp.exp2(st - (m_new if shift is None else m_new - shift)).astype(BF16)
    corr = jnp.exp2(m_old - m_new)
    acc_ref[idx] = corr * acc_ref[idx] + jnp.dot(vt, p, preferred_element_type=F32)
    m_ref[idx] = m_new


def _causal_mask_t(t):
    k = lax.broadcasted_iota(jnp.int32, (t, t), 0)
    q = lax.broadcasted_iota(jnp.int32, (t, t), 1)
    return q >= k


def _finish_t(acc):
    return jnp.transpose(acc[0:MLA_V] / acc[MLA_V:MLA_V + 1])


def _causal_tiles(nb):
    pairs = [(qi, ki) for qi in range(nb) for ki in range(qi + 1)]
    return (jnp.asarray([p[0] for p in pairs], jnp.int32), jnp.asarray([p[1] for p in pairs], jnp.int32))


def _mla_flash_kernel(qi_ref, ki_ref, q_ref, k_ref, vt_ref, o_ref, m_ref, acc_ref, *, t):
    qi = qi_ref[pl.program_id(1)]
    ki = ki_ref[pl.program_id(1)]

    @pl.when(ki == 0)
    def _():
        m_ref[...] = jnp.full_like(m_ref, NEG_INF)
        acc_ref[...] = jnp.zeros_like(acc_ref)

    def step(masked):
        mask = _causal_mask_t(t) if masked else None
        def scores(h):
            st = lax.dot_general(k_ref[0, h], q_ref[0, h], (((1,), (1,)), ((), ())), preferred_element_type=F32)
            return jnp.where(mask, st, NEG_INF) if masked else st

        sts = [scores(h) for h in range(MLA_HEADS)]
        for h in range(MLA_HEADS):
            _softmax_step_t(sts[h], vt_ref[0, 0, h], m_ref, acc_ref, h)

    @pl.when(ki < qi)
    def _():
        step(False)

    @pl.when(ki == qi)
    def _():
        step(True)
        for h in range(MLA_HEADS):
            o_ref[0, :, h * MLA_V:(h + 1) * MLA_V] = _finish_t(acc_ref[h]).astype(BF16)


def _mla_flash(q, k, vt, *, t=ATTN_TILE):
    bsz, nh, n_pos, _ = q.shape
    t = min(t, n_pos)
    nb = n_pos // t
    qi_tab, ki_tab = _causal_tiles(nb)
    return pl.pallas_call(
        functools.partial(_mla_flash_kernel, t=t),
        grid_spec=pltpu.PrefetchScalarGridSpec(
            num_scalar_prefetch=2,
            grid=(bsz, qi_tab.shape[0]),
            in_specs=[
                pl.BlockSpec((1, nh, t, MLA_QK_PAD), lambda b, s, qt, kt: (b, 0, qt[s], 0)),
                pl.BlockSpec((1, nh, t, MLA_QK_PAD), lambda b, s, qt, kt: (b, 0, kt[s], 0)),
                pl.BlockSpec((1, 1, nh, VT_ROWS, t), lambda b, s, qt, kt: (b, kt[s], 0, 0, 0)),
            ],
            out_specs=pl.BlockSpec((1, t, nh * MLA_V), lambda b, s, qt, kt: (b, qt[s], 0)),
            scratch_shapes=[pltpu.VMEM((nh, 1, t), F32), pltpu.VMEM((nh, VT_ROWS, t), F32)],
        ),
        out_shape=jax.ShapeDtypeStruct((bsz, n_pos, nh * MLA_V), BF16),
        compiler_params=_cparams(("parallel", "arbitrary")),
        name="mla_flash",
    )(qi_tab, ki_tab, q, k, vt)


def _diff_prep_kernel(q_ref, k_ref, v_ref, qm_ref, kb_ref, vt_ref):
    hw = 2 * DIFF_QK
    tm = q_ref.shape[0]
    lane = lax.broadcasted_iota(jnp.int32, (tm, hw), 1)
    scale = DIFF_QK ** -0.5 * LOG2E
    ones = jnp.ones((VT_ROWS - DIFF_V, tm), BF16)
    kb_ref[...] = k_ref[...].astype(BF16)
    for h in range(DIFF_HEADS):
        qh = q_ref[:, h * hw:(h + 1) * hw] * scale
        qm_ref[0, 2 * h] = jnp.where(lane < DIFF_QK, qh, 0.0).astype(BF16)
        qm_ref[0, 2 * h + 1] = jnp.where(lane >= DIFF_QK, qh, 0.0).astype(BF16)
        vt_ref[0, 0, h, 0:DIFF_V, :] = jnp.transpose(v_ref[:, h * DIFF_V:(h + 1) * DIFF_V]).astype(BF16)
        vt_ref[0, 0, h, DIFF_V:VT_ROWS, :] = ones


N_MLA_PREP_IN = 11
N_DIFF_PREP_IN = 3


def _attn_prep_kernel(*refs):
    n_in = N_MLA_PREP_IN + N_DIFF_PREP_IN
    ins, outs = refs[:n_in], refs[n_in:]
    _mla_prep_kernel(*ins[:N_MLA_PREP_IN], *outs[:3])
    _diff_prep_kernel(*ins[N_MLA_PREP_IN:], *outs[3:])


def _attn_prep(z, cos, sin, qg, kvg, wq1, wq2, wkv, *, bsz, n_pos, tm=ATTN_TILE):
    tm = min(tm, n_pos)
    nt = n_pos // tm
    w = DIFF_HEADS * DIFF_V
    const = lambda b, t: (0, 0)
    zspec = lambda width, off: pl.BlockSpec((tm, width), lambda b, t: (b * nt + t, off // width))
    hspec = lambda width: pl.BlockSpec((1, MLA_HEADS, tm, width), lambda b, t: (b, 0, t, 0))
    slab = lambda nh: pl.BlockSpec((1, 1, nh, VT_ROWS, tm), lambda b, t: (b, t, 0, 0, 0))
    return pl.pallas_call(
        _attn_prep_kernel,
        grid=(bsz, nt),
        in_specs=[
            zspec(MLA_Q_RANK, OFF_CQ), zspec(MLA_KV_RANK, OFF_CKV), zspec(LANES, OFF_KR), zspec(LANES, OFF_KRR),
            pl.BlockSpec((tm, LANES), lambda b, t: (b * nt + t, 0)),
            pl.BlockSpec((tm, LANES), lambda b, t: (b * nt + t, 0)),
            pl.BlockSpec((1, MLA_Q_RANK), const), pl.BlockSpec((1, MLA_KV_RANK), const),
            pl.BlockSpec(wq1.shape, const), pl.BlockSpec(wq2.shape, const), pl.BlockSpec(wkv.shape, const),
            zspec(w, OFF_DQ), zspec(w, OFF_DK), zspec(w, OFF_DV),
        ],
        out_specs=[hspec(MLA_QK_PAD), hspec(MLA_QK_PAD), slab(MLA_HEADS),
                   pl.BlockSpec((1, 2 * DIFF_HEADS, tm, 2 * DIFF_QK), lambda b, t: (b, 0, t, 0)),
                   pl.BlockSpec((tm, w), lambda b, t: (b * nt + t, 0)),
                   slab(DIFF_HEADS)],
        out_shape=[jax.ShapeDtypeStruct((bsz, MLA_HEADS, n_pos, MLA_QK_PAD), BF16),
                   jax.ShapeDtypeStruct((bsz, MLA_HEADS, n_pos, MLA_QK_PAD), BF16),
                   jax.ShapeDtypeStruct((bsz, nt, MLA_HEADS, VT_ROWS, tm), BF16),
                   jax.ShapeDtypeStruct((bsz, 2 * DIFF_HEADS, n_pos, 2 * DIFF_QK), BF16),
                   jax.ShapeDtypeStruct((bsz * n_pos, w), BF16),
                   jax.ShapeDtypeStruct((bsz, nt, DIFF_HEADS, VT_ROWS, tm), BF16)],
        compiler_params=_cparams(("parallel", "parallel")),
        name="attn_prep",
    )(z, z, z, z, cos, sin, qg, kvg, wq1, wq2, wkv, z, z, z)


def _diff_kernel(qi_ref, ki_ref, qmin_ref, kmax_ref, qm_ref, k_ref, vt_ref, pq_ref, pk_ref, tab_ref,
                 lq1_ref, lk1_ref, lq2_ref, lk2_ref, sg_ref, o_ref, m_ref, acc_ref, *, t, tiles_per_row, lambda_init):
    qi = qi_ref[pl.program_id(1)]
    ki = ki_ref[pl.program_id(1)]
    nh = DIFF_HEADS
    hw = 2 * DIFF_QK

    @pl.when(ki == 0)
    def _():
        m_ref[...] = jnp.full_like(m_ref, NEG_INF)
        acc_ref[...] = jnp.zeros_like(acc_ref)

    nc = t // LANES
    qbase = (pl.program_id(0) * tiles_per_row + qi) * nc
    kbase = (pl.program_id(0) * tiles_per_row + ki) * nc

    def is_far(r, c):
        return qmin_ref[qbase + c] - kmax_ref[kbase + r] >= T5_TABLE - 1

    def all_of(pairs):
        out = None
        for r, c in pairs:
            out = is_far(r, c) if out is None else jnp.logical_and(out, is_far(r, c))
        return out

    def step(plan):
        sub = lambda r, c: (slice(r * LANES, (r + 1) * LANES), slice(c * LANES, (c + 1) * LANES))
        dist, local_mask = {}, None
        for r in range(nc):
            for c in range(nc):
                if plan[r][c] in ("gather", "diag"):
                    rows, cols = sub(r, c)
                    dist[r, c] = jnp.clip(pq_ref[0, :, cols] - pk_ref[rows, :], 0, T5_TABLE - 1)
                if plan[r][c] == "diag":
                    local_mask = _causal_mask_t(LANES)
        sts = []
        for h in range(nh):
            kh = k_ref[:, h * hw:(h + 1) * hw]
            shift = tab_ref[h:h + 1, T5_TABLE - 1:T5_TABLE]
            tab = jnp.broadcast_to(tab_ref[h:h + 1, :], (LANES, T5_TABLE))
            delta = {}
            for (r, c), d in dist.items():
                dl = jnp.take_along_axis(tab, d, axis=1, mode="promise_in_bounds") - shift
                delta[r, c] = jnp.where(local_mask, dl, NEG_INF) if plan[r][c] == "diag" else dl
            for mp in range(2):
                raw = lax.dot_general(kh, qm_ref[0, 2 * h + mp], (((1,), (1,)), ((), ())),
                                      preferred_element_type=F32)
                if delta or any("neg" in row for row in plan):
                    blocks = []
                    for r in range(nc):
                        row = []
                        for c in range(nc):
                            rows, cols = sub(r, c)
                            if plan[r][c] == "neg":
                                row.append(jnp.full((LANES, LANES), NEG_INF, F32))
                            elif (r, c) in delta:
                                row.append(raw[rows, cols] + delta[r, c])
                            else:
                                row.append(raw[rows, cols])
                        blocks.append(jnp.concatenate(row, axis=1))
                    raw = jnp.concatenate(blocks, axis=0)
                sts.append(raw)
        for j in range(2 * nh):
            shift = tab_ref[j // 2:j // 2 + 1, T5_TABLE - 1:T5_TABLE]
            _softmax_step_t(sts[j], vt_ref[0, 0, j // 2], m_ref, acc_ref, j, shift)

    every = [(r, c) for r in range(nc) for c in range(nc)]
    corner = (nc - 1, 0)
    plan_far = [["const"] * nc for _ in range(nc)]
    plan_corner = [["gather" if (r, c) == corner else "const" for c in range(nc)] for r in range(nc)]
    plan_full = [["gather"] * nc for _ in range(nc)]
    diag_kind = lambda r, c, beyond: "neg" if c < r else "diag" if c == r else "gather" if c == r + 1 else beyond
    plan_band = [[diag_kind(r, c, "const") for c in range(nc)] for r in range(nc)]
    plan_diag = [[diag_kind(r, c, "gather") for c in range(nc)] for r in range(nc)]

    below = ki < qi
    all_far = all_of(every)
    corner_far = all_of([p for p in every if p != corner])
    band_far = all_of([(r, c) for r, c in every if c >= r + 2])

    @pl.when(jnp.logical_and(below, all_far))
    def _():
        step(plan_far)

    @pl.when(jnp.logical_and(below, jnp.logical_and(corner_far, jnp.logical_not(all_far))))
    def _():
        step(plan_corner)

    @pl.when(jnp.logical_and(below, jnp.logical_not(corner_far)))
    def _():
        step(plan_full)

    @pl.when(jnp.logical_and(ki == qi, band_far))
    def _():
        step(plan_band)

    @pl.when(jnp.logical_and(ki == qi, jnp.logical_not(band_far)))
    def _():
        step(plan_diag)

    @pl.when(ki == qi)
    def _():
        lam =(jnp.exp(jnp.sum(lq1_ref[...] * lk1_ref[...], axis=-1, keepdims=True))
               - jnp.exp(jnp.sum(lq2_ref[...] * lk2_ref[...], axis=-1, keepdims=True)) + lambda_init)
        for h in range(nh):
            o = _finish_t(acc_ref[2 * h]) - lam * _finish_t(acc_ref[2 * h + 1])
            o = _rms(o, sg_ref[...]) * (1.0 - lambda_init)
            o_ref[:, h * DIFF_V:(h + 1) * DIFF_V] = o.astype(BF16)


def _diff_attn(qm, kb, vt, pos_q, pos_k, table, lq1, lk1, lq2, lk2, sg, *, bsz, n_pos, lambda_init, t=ATTN_TILE):
    t = min(t, n_pos)
    nb = n_pos // t
    w = DIFF_HEADS * DIFF_V
    const = lambda b, s, qt, kt, qmin, kmax: (0, 0)
    qi_tab, ki_tab = _causal_tiles(nb)
    chunks = pos_q.reshape(bsz * n_pos // LANES, LANES)
    qmin = jnp.min(chunks, axis=1)
    kmax = jnp.max(chunks, axis=1)
    return pl.pallas_call(
        functools.partial(_diff_kernel, t=t, tiles_per_row=nb, lambda_init=lambda_init),
        grid_spec=pltpu.PrefetchScalarGridSpec(
            num_scalar_prefetch=4,
            grid=(bsz, qi_tab.shape[0]),
            in_specs=[
                pl.BlockSpec((1, 2 * DIFF_HEADS, t, 2 * DIFF_QK), lambda b, s, qt, kt, qmin, kmax: (b, 0, qt[s], 0)),
                pl.BlockSpec((t, w), lambda b, s, qt, kt, qmin, kmax: (b * nb + kt[s], 0)),
                pl.BlockSpec((1, 1, DIFF_HEADS, VT_ROWS, t), lambda b, s, qt, kt, qmin, kmax: (b, kt[s], 0, 0, 0)),
                pl.BlockSpec((1, 1, t), lambda b, s, qt, kt, qmin, kmax: (b * nb + qt[s], 0, 0)),
                pl.BlockSpec((t, LANES), lambda b, s, qt, kt, qmin, kmax: (b * nb + kt[s], 0)),
                pl.BlockSpec((DIFF_HEADS, T5_TABLE), const),
                pl.BlockSpec((1, DIFF_QK), const), pl.BlockSpec((1, DIFF_QK), const),
                pl.BlockSpec((1, DIFF_QK), const), pl.BlockSpec((1, DIFF_QK), const),
                pl.BlockSpec((1, DIFF_V), const),
            ],
            out_specs=pl.BlockSpec((t, w), lambda b, s, qt, kt, qmin, kmax: (b * nb + qt[s], 0)),
            scratch_shapes=[pltpu.VMEM((2 * DIFF_HEADS, 1, t), F32),
                            pltpu.VMEM((2 * DIFF_HEADS, VT_ROWS, t), F32)],
        ),
        out_shape=jax.ShapeDtypeStruct((bsz * n_pos, w), BF16),
        compiler_params=_cparams(("parallel", "arbitrary")),
        name="diff_attn",
    )(qi_tab, ki_tab, qmin, kmax, qm, kb, vt, pos_q, pos_k, table, lq1, lk1, lq2, lk2, sg)


def _ret_kernel(q_ref, qr_ref, k_ref, kr_ref, v_ref, g_ref, cos_ref, sin_ref, o_ref, st_ref, *, tm):
    t = pl.program_id(1)
    c = RET_CHUNK
    nh = RET_HEADS
    w = nh * RET_QK

    @pl.when(t == 0)
    def _():
        st_ref[...] = jnp.zeros_like(st_ref)

    log_gamma = [math.log(1.0 - 2.0 ** (-5.0 - h)) for h in range(nh)]
    lane = lax.broadcasted_iota(jnp.int32, (1, w), 1)
    lg_lane = jnp.zeros((1, w), F32)
    for h in range(nh):
        lg_lane = jnp.where(lane // RET_QK == h, log_gamma[h], lg_lane)
    tok = lax.broadcasted_iota(jnp.int32, (c, 1), 0).astype(F32)
    q_decay = jnp.exp(lg_lane * (tok + 1.0))
    k_decay = jnp.exp(lg_lane * (c - 1.0 - tok))
    ri = lax.broadcasted_iota(jnp.int32, (c, c), 0)
    ci = lax.broadcasted_iota(jnp.int32, (c, c), 1)
    rel = (ri - ci).astype(F32)
    intra = [jnp.where(rel >= 0, jnp.exp(log_gamma[h] * jnp.maximum(rel, 0.0)), 0.0) for h in range(nh)]
    head_lanes = [(lane // RET_QK == h) for h in range(nh)]

    states = [st_ref[h * RET_QK:(h + 1) * RET_QK, :] for h in range(nh)]
    n_chunks = tm // c
    chunk_rows = [slice(j * c, (j + 1) * c) for j in range(n_chunks)]
    head_cols = [slice(h * RET_V, (h + 1) * RET_V) for h in range(nh)]

    vb, qdh, scores, kvs = {}, {}, {}, {}
    for j, rows in enumerate(chunk_rows):
        cos = jnp.concatenate([cos_ref[rows, :]] * (w // LANES), axis=1)
        sin = jnp.concatenate([sin_ref[rows, :]] * (w // LANES), axis=1)
        q = q_ref[rows, :] * cos + qr_ref[rows, :] * sin
        k = (k_ref[rows, :] * cos + kr_ref[rows, :] * sin) * (RET_QK ** -0.5)
        kb = k.astype(BF16)
        qd = q * q_decay
        kdt = jnp.transpose(k * k_decay).astype(BF16)
        for h in range(nh):
            vb[j, h] = v_ref[rows, head_cols[h]].astype(BF16)
            qh = jnp.where(head_lanes[h], q, 0.0).astype(BF16)
            scores[j, h] = lax.dot_general(qh, kb, (((1,), (1,)), ((), ())), preferred_element_type=F32) * intra[h]
            qdh[j, h] = jnp.where(head_lanes[h], qd, 0.0).astype(BF16)
            kvs[j, h] = jnp.dot(kdt[h * RET_QK:(h + 1) * RET_QK, :], vb[j, h], preferred_element_type=F32)

    state_b = []
    for j in range(n_chunks):
        state_b.append(jnp.concatenate(states, axis=0).astype(BF16))
        states = [states[h] * math.exp(log_gamma[h] * c) + kvs[j, h] for h in range(nh)]
    for h in range(nh):
        st_ref[h * RET_QK:(h + 1) * RET_QK, :] = states[h]

    for j, rows in enumerate(chunk_rows):
        for h in range(nh):
            inner = jnp.dot(scores[j, h].astype(BF16), vb[j, h], preferred_element_type=F32)
            cross = jnp.dot(qdh[j, h], state_b[j], preferred_element_type=F32)
            o = inner + cross
            mu = jnp.mean(o, axis=-1, keepdims=True)
            oc = o - mu
            var = jnp.mean(oc * oc, axis=-1, keepdims=True)
            o = oc * lax.rsqrt(var + LN_EPS)
            gh = g_ref[rows, head_cols[h]]
            o_ref[rows, head_cols[h]] = (gh * jax.nn.sigmoid(gh) * o).astype(BF16)


def _retention(z, cos, sin, *, bsz, n_pos, tm=512):
    tm = min(tm, n_pos)
    nt = n_pos // tm
    w = RET_HEADS * RET_QK
    wv = RET_HEADS * RET_V
    zspec = lambda width, off: pl.BlockSpec((tm, width), lambda b, t: (b * nt + t, off // width))
    tspec = pl.BlockSpec((tm, LANES), lambda b, t: (b * nt + t, 0))
    return pl.pallas_call(
        functools.partial(_ret_kernel, tm=tm),
        grid=(bsz, nt),
        in_specs=[zspec(w, OFF_RQ), zspec(w, OFF_RQR), zspec(w, OFF_RK), zspec(w, OFF_RKR),
                  zspec(wv, OFF_RV), zspec(wv, OFF_RG), tspec, tspec],
        out_specs=pl.BlockSpec((tm, wv), lambda b, t: (b * nt + t, 0)),
        out_shape=jax.ShapeDtypeStruct((bsz * n_pos, wv), BF16),
        scratch_shapes=[pltpu.VMEM((w, RET_V), F32)],
        compiler_params=_cparams(("parallel", "arbitrary")),
        name="retention",
    )(z, z, z, z, z, z, cos, sin)


def _outproj_ln_kernel(y0_ref, y1_ref, y2_ref, y3_ref, w_ref, x_ref, g_ref, b_ref, o_ref):
    parts = 4
    half = o_ref.shape[0] // parts
    accs = []
    for r in range(parts):
        rows = slice(r * half, (r + 1) * half)
        acc = None
        for j, y_ref in enumerate((y0_ref, y1_ref, y2_ref, y3_ref)):
            part = jnp.dot(y_ref[rows, :], w_ref[j * 512:(j + 1) * 512, :], preferred_element_type=F32)
            acc = part if acc is None else acc + part
        accs.append(acc)
    for r in range(parts):
        rows = slice(r * half, (r + 1) * half)
        o_ref[rows, :] = _layer_norm(ALPHA * x_ref[rows, :] + accs[r], g_ref[...], b_ref[...])


def _outproj_ln(ys, w, x, g, b, *, tm=512):
    m, d = x.shape
    tm = min(tm, m)
    yspec = pl.BlockSpec((tm, 512), lambda i: (i, 0))
    return pl.pallas_call(
        _outproj_ln_kernel,
        grid=(m // tm,),
        in_specs=[yspec, yspec, yspec, yspec,
                  pl.BlockSpec(w.shape, lambda i: (0, 0)),
                  pl.BlockSpec((tm, d), lambda i: (i, 0)),
                  pl.BlockSpec((1, d), lambda i: (0, 0)),
                  pl.BlockSpec((1, d), lambda i: (0, 0))],
        out_specs=pl.BlockSpec((tm, d), lambda i: (i, 0)),
        out_shape=jax.ShapeDtypeStruct((m, d), F32),
        compiler_params=_cparams(("parallel",)),
        name="outproj_ln",
    )(*ys, w, x, g, b)


def _rot_cols(w, heads, dim):
    k = w.shape[0]
    w = w.reshape(k, heads, 2, dim // 2)
    return jnp.concatenate([-w[:, :, 1], w[:, :, 0]], axis=-1).reshape(k, heads * dim)


def _pad_cols(w, width):
    return jnp.pad(w, ((0, 0), (0, width - w.shape[1])))


def _wide_w_in(w_in):
    sizes = (S5_WIDTH, MLA_Q_RANK, MLA_KV_RANK, MLA_ROPE,
             RET_HEADS * RET_QK, RET_HEADS * RET_QK, RET_HEADS * RET_V, RET_HEADS * RET_V,
             DIFF_HEADS * 2 * DIFF_QK, DIFF_HEADS * 2 * DIFF_QK, DIFF_HEADS * DIFF_V)
    offs = np.concatenate([[0], np.cumsum(sizes)])
    (s5_u, cq, ckv, kr, rq, rk, rv, rg, dq, dk, dv) = [w_in[:, offs[i]:offs[i + 1]] for i in range(len(sizes))]
    cols = [s5_u, cq, rv, rg, dq, dk, dv,
            rq, _rot_cols(rq, RET_HEADS, RET_QK), rk, _rot_cols(rk, RET_HEADS, RET_QK),
            ckv, _pad_cols(kr, LANES), _pad_cols(_rot_cols(kr, 1, MLA_ROPE), LANES),
            jnp.zeros((w_in.shape[0], IN_WIDE - IN_USED), w_in.dtype)]
    return jnp.concatenate(cols, axis=1).astype(BF16)


def _mla_q_weights(w_uq):
    k = w_uq.shape[0]
    w = w_uq.reshape(k, MLA_HEADS, MLA_NOPE + MLA_ROPE)
    rope = w[:, :, MLA_NOPE:]
    pad = MLA_QK_PAD - MLA_NOPE - MLA_ROPE
    w1 = jnp.pad(w, ((0, 0), (0, 0), (0, pad))).reshape(k, MLA_HEADS * MLA_QK_PAD)
    rot = _rot_cols(rope.reshape(k, MLA_HEADS * MLA_ROPE), MLA_HEADS, MLA_ROPE).reshape(k, MLA_HEADS, MLA_ROPE)
    w2 = jnp.pad(rot, ((0, 0), (0, 0), (0, LANES - MLA_ROPE))).reshape(k, MLA_HEADS * LANES)
    return w1.astype(BF16), w2.astype(BF16)


def kernel(x, p, positions, rel_bias, ffn1_w_gate, ffn1_w_up, ffn1_w_down, ln1_g, ln1_b, w_in, w_out, ln2_g, ln2_b, s5_lambda_re, s5_lambda_im, s5_log_dt, s5_b_re, s5_b_im, s5_c_re, s5_c_im, s5_d, s5_w_glu, s5_b_glu, mla_q_norm_g, mla_w_uq, mla_kv_norm_g, mla_w_ukv, diff_lambda_q1, diff_lambda_k1, diff_lambda_q2, diff_lambda_k2, diff_subln_g, ffn2_w_gate, ffn2_w_up, ffn2_w_down, ple_w_gate, ple_b_gate, ple_w_proj, ln3_g, ln3_b):
    bsz, n_pos, d = x.shape
    m = bsz * n_pos
    depth = ffn1_w_gate.shape[0]
    assert d == D_MODEL and n_pos % S5_TM == 0 and n_pos % ATTN_TILE == 0 and m % FFN_TM == 0, (x.shape,)
    xf = x.reshape(m, d)
    row = lambda v: v.reshape(1, -1)

    cos, sin = _rope_tables(positions.reshape(m, 1))
    table = _bias_table(rel_bias)
    t_diff = ATTN_TILE
    pos_q = positions.reshape(bsz * (n_pos // t_diff), 1, t_diff)
    pos_k = jnp.broadcast_to(positions.reshape(m, 1), (m, LANES))

    for i in range(depth):
        f1g, f1u, f1d = _cast_layer(i, ffn1_w_gate, ffn1_w_up, ffn1_w_down)
        f2g, f2u, f2d, pwg, wo = _cast_layer(i, ffn2_w_gate, ffn2_w_up, ffn2_w_down, ple_w_gate, w_out)
        xf = _ffn_ln(xf, f1g, f1u, f1d, row(ln1_g[i]), row(ln1_b[i]))
        z = _inproj(xf, _wide_w_in(w_in[i]))

        apr, api, bbr, bbi = _s5_discretise(s5_lambda_re[i], s5_lambda_im[i], s5_log_dt[i], s5_b_re[i], s5_b_im[i])
        bh = jnp.concatenate([_block_diag_in(bbr), _block_diag_in(bbi)], axis=2).astype(BF16)
        ch = jnp.concatenate([_block_diag_out(s5_c_re[i]), -_block_diag_out(s5_c_im[i])], axis=1).astype(BF16)
        apr = jnp.repeat(apr, S5_TM // S5_STEPS, axis=0)
        api = jnp.repeat(api, S5_TM // S5_STEPS, axis=0)
        y_s5 = _s5_mixer(z, bh, ch, apr, api, row(s5_d[i]), s5_w_glu[i].astype(BF16), row(s5_b_glu[i]),
                         bsz=bsz, n_pos=n_pos)

        wq1, wq2 = _mla_q_weights(mla_w_uq[i])
        q, k, vt, dqm, dkb, dvt = _attn_prep(z, cos, sin, row(mla_q_norm_g[i]), row(mla_kv_norm_g[i]), wq1, wq2,
                                             mla_w_ukv[i].astype(BF16), bsz=bsz, n_pos=n_pos)
        y_mla = _mla_flash(q, k, vt).reshape(m, MLA_HEADS * MLA_V)

        y_ret = _retention(z, cos, sin, bsz=bsz, n_pos=n_pos)

        lambda_init = 0.8 - 0.6 * math.exp(-0.3 * i)
        y_diff = _diff_attn(dqm, dkb, dvt, pos_q, pos_k, table, row(diff_lambda_q1[i]), row(diff_lambda_k1[i]),
                            row(diff_lambda_q2[i]), row(diff_lambda_k2[i]), row(diff_subln_g[i]),
                            bsz=bsz, n_pos=n_pos, lambda_init=lambda_init, t=t_diff)

        xf = _outproj_ln((y_s5, y_mla, y_ret, y_diff), wo, xf, row(ln2_g[i]), row(ln2_b[i]))

        res, xb = _ple(xf, p.reshape(depth, m, PLE_DIM), i, pwg, row(ple_b_gate[i]), ple_w_proj[i].astype(BF16))
        xf = _ffn_ln(xb, f2g, f2u, f2d, row(ln3_g[i]), row(ln3_b[i]), res)
    return xf.reshape(bsz, n_pos, d)
```

```python
import functools
import math

import numpy as np
import jax
import jax.numpy as jnp
from jax import lax
from jax.experimental import pallas as pl
from jax.experimental.pallas import tpu as pltpu

F32 = jnp.float32
BF16 = jnp.bfloat16

LANES = 128
SUBLANES = 8
VMEM_BYTES = 64 * 1024 * 1024
VMEM_LIMIT_BYTES = VMEM_BYTES - 8 * 1024 * 1024
FFN_VMEM_LIMIT_BYTES = VMEM_BYTES - 4 * 1024 * 1024

D_MODEL = 2048
DEPTH = 2
PLE_DIM = 256
D_FF = 5632
ALPHA = (2 * DEPTH) ** 0.25
ROPE_THETA = 10000.0
NEG_INF = -1e30
LN_EPS = 1e-5
RMS_EPS = 1e-6

S5_WIDTH = 512
S5_GROUP = 16
S5_GROUPS = 32
S5_STATE = 64
S5_NSTATE = S5_GROUPS * S5_STATE
S5_HALF = S5_NSTATE // 2
S5_TM = 512
S5_STEPS = S5_TM // SUBLANES

MLA_HEADS = 4
MLA_Q_RANK = 512
MLA_KV_RANK = 128
MLA_NOPE = 128
MLA_ROPE = 64
MLA_V = 128
MLA_QK_PAD = 256

RET_HEADS = 4
RET_QK = 64
RET_V = 128
RET_CHUNK = 128

DIFF_HEADS = 4
DIFF_QK = 64
DIFF_V = 128

T5_BUCKETS = 32
T5_MAX_DIST = 128
T5_TABLE = 128

LOG2E = math.log2(math.e)
VT_ROWS = 144
ATTN_TILE = 512
FFN_TM = 1024
FFN_TF = 512

OFF_S5 = 0
OFF_CQ = 512
OFF_RV = 1024
OFF_RG = 1536
OFF_DQ = 2048
OFF_DK = 2560
OFF_DV = 3072
OFF_RQ = 3584
OFF_RQR = 3840
OFF_RK = 4096
OFF_RKR = 4352
OFF_CKV = 4608
OFF_KR = 4736
OFF_KRR = 4864
IN_USED = 4992
IN_WIDE = 5120
IN_TN = 2560


def _cparams(sem, vmem_limit_bytes=VMEM_LIMIT_BYTES):
    return pltpu.CompilerParams(dimension_semantics=sem, vmem_limit_bytes=vmem_limit_bytes)


def _layer_norm(y, g, b):
    mu = jnp.mean(y, axis=-1, keepdims=True)
    yc = y - mu
    var = jnp.mean(yc * yc, axis=-1, keepdims=True)
    return yc * lax.rsqrt(var + LN_EPS) * g + b


CAST_STEPS = 16


def _cast_kernel(*refs):
    n = len(refs) // 2
    for src, dst in zip(refs[:n], refs[n:]):
        if len(dst.shape) == 3:
            tile = dst.shape[2]
            for j in range(dst.shape[0]):
                dst[j] = src[0, :, j * tile:(j + 1) * tile].astype(BF16)
        else:
            dst[...] = src[0].astype(BF16)


def _cast_layer(layer, *stacked, col_tile=()):
    in_specs, out_specs, out_shape = [], [], []
    for k, w in enumerate(stacked):
        _, r, c = w.shape
        tr = r // CAST_STEPS
        in_specs.append(pl.BlockSpec((1, tr, c), lambda s: (layer, s, 0)))
        t = col_tile[k] if k < len(col_tile) else None
        if t:
            out_specs.append(pl.BlockSpec((c // t, tr, t), lambda s: (0, s, 0)))
            out_shape.append(jax.ShapeDtypeStruct((c // t, r, t), BF16))
        else:
            out_specs.append(pl.BlockSpec((tr, c), lambda s: (s, 0)))
            out_shape.append(jax.ShapeDtypeStruct((r, c), BF16))
    return pl.pallas_call(
        _cast_kernel,
        grid=(CAST_STEPS,),
        in_specs=in_specs,
        out_specs=out_specs,
        out_shape=out_shape,
        compiler_params=_cparams(("parallel",)),
        name="cast_weights",
    )(*stacked)


FFN_ROW_CHUNK = 128


def _ffn_ln_kernel(*refs, nf, tm, has_res):
    if has_res:
        xb_ref, res_ref, wg_ref, wu_ref, wd_ref, g_ref, b_ref, o_ref = refs
    else:
        res_ref, wg_ref, wu_ref, wd_ref, g_ref, b_ref, o_ref, xb_ref = refs
    f = pl.program_id(1)
    chunks = tm // FFN_ROW_CHUNK

    def rows(c):
        return pl.ds(pl.multiple_of(c * FFN_ROW_CHUNK, FFN_ROW_CHUNK), FFN_ROW_CHUNK)

    if not has_res:
        @pl.when(f == 0)
        def _():
            def cast(c, _):
                xb_ref[rows(c), :] = res_ref[rows(c), :].astype(BF16)
                return 0
            lax.fori_loop(0, chunks, cast, 0)

    def hidden():
        xb = xb_ref[...]
        gate = jnp.dot(xb, wg_ref[...], preferred_element_type=F32)
        up = jnp.dot(xb, wu_ref[...], preferred_element_type=F32)
        return (gate * jax.nn.sigmoid(gate) * up).astype(BF16)

    @pl.when(f == 0)
    def _():
        o_ref[...] = jnp.dot(hidden(), wd_ref[...], preferred_element_type=F32)

    @pl.when(jnp.logical_and(f > 0, f < nf - 1))
    def _():
        o_ref[...] += jnp.dot(hidden(), wd_ref[...], preferred_element_type=F32)

    @pl.when(f == nf - 1)
    def _():
        h = hidden()
        groups = [slice(c * FFN_ROW_CHUNK, (c + 1) * FFN_ROW_CHUNK) for c in range(chunks)]
        parts = [jnp.dot(h[r, :], wd_ref[...], preferred_element_type=F32) for r in groups]
        for r, part in zip(groups, parts):
            res = res_ref[r, :]
            y = (res if has_res else ALPHA * res) + 0.5 * (o_ref[r, :] + part)
            o_ref[r, :] = _layer_norm(y, g_ref[...], b_ref[...])


def _ffn_ln(x, wg, wu, wd, g, b, res=None, *, tm=FFN_TM):
    m, d = x.shape
    nf, _, tf = wg.shape
    tm = min(tm, m)
    assert nf >= 2 and wd.shape[0] == nf * tf, (wg.shape, wd.shape)
    xspec = pl.BlockSpec((tm, d), lambda i, f: (i, 0))
    in_specs = [xspec] * (2 if res is not None else 1) + [
        pl.BlockSpec((None, d, tf), lambda i, f: (f, 0, 0)),
        pl.BlockSpec((None, d, tf), lambda i, f: (f, 0, 0)),
        pl.BlockSpec((tf, d), lambda i, f: (f, 0)),
        pl.BlockSpec((1, d), lambda i, f: (0, 0)),
        pl.BlockSpec((1, d), lambda i, f: (0, 0)),
    ]
    args = ([x, res] if res is not None else [x]) + [wg, wu, wd, g, b]
    return pl.pallas_call(
        functools.partial(_ffn_ln_kernel, nf=nf, tm=tm, has_res=res is not None),
        grid=(m // tm, nf),
        in_specs=in_specs,
        out_specs=pl.BlockSpec((tm, d), lambda i, f: (i, 0)),
        out_shape=jax.ShapeDtypeStruct((m, d), F32),
        scratch_shapes=[] if res is not None else [pltpu.VMEM((tm, d), BF16)],
        compiler_params=_cparams(("parallel", "arbitrary"), FFN_VMEM_LIMIT_BYTES),
        name="ffn_ln",
    )(*args)


def _ple_kernel(x_ref, p_ref, wg_ref, bg_ref, wp_ref, res_ref, xb_ref):
    x = x_ref[...]
    xb = x.astype(BF16)
    xb_ref[...] = xb
    pb = p_ref[...].astype(BF16)
    gate = jax.nn.sigmoid(jnp.dot(xb, wg_ref[...], preferred_element_type=F32) + bg_ref[...])
    res_ref[...] = ALPHA * x + gate * jnp.dot(pb, wp_ref[...], preferred_element_type=F32)


def _ple(x, p, layer, wg, bg, wp, *, tm=512):
    m, d = x.shape
    tm = min(tm, m)
    return pl.pallas_call(
        _ple_kernel,
        grid=(m // tm,),
        in_specs=[
            pl.BlockSpec((tm, d), lambda i: (i, 0)),
            pl.BlockSpec((None, tm, PLE_DIM), lambda i: (layer, i, 0)),
            pl.BlockSpec((d, d), lambda i: (0, 0)),
            pl.BlockSpec((1, d), lambda i: (0, 0)),
            pl.BlockSpec((PLE_DIM, d), lambda i: (0, 0)),
        ],
        out_specs=[pl.BlockSpec((tm, d), lambda i: (i, 0))] * 2,
        out_shape=[jax.ShapeDtypeStruct((m, d), F32), jax.ShapeDtypeStruct((m, d), BF16)],
        compiler_params=_cparams(("parallel",)),
        name="ple",
    )(x, p, wg, bg, wp)


def _inproj_kernel(x_ref, w_ref, o_ref):
    o_ref[...] = jnp.dot(x_ref[...].astype(BF16), w_ref[...], preferred_element_type=F32)


def _inproj(x, w, *, tm=512):
    m, d = x.shape
    n = w.shape[1]
    tm = min(tm, m)
    return pl.pallas_call(
        _inproj_kernel,
        grid=(n // IN_TN, m // tm),
        in_specs=[
            pl.BlockSpec((tm, d), lambda j, i: (i, 0)),
            pl.BlockSpec((d, IN_TN), lambda j, i: (0, j)),
        ],
        out_specs=pl.BlockSpec((tm, IN_TN), lambda j, i: (i, j)),
        out_shape=jax.ShapeDtypeStruct((m, n), F32),
        compiler_params=_cparams(("parallel", "parallel")),
        name="inproj",
    )(x, w)


def _rope_kernel(pos_ref, freq_ref, cos_ref, sin_ref):
    ang = pos_ref[...].astype(F32) * freq_ref[...]
    cos_ref[...] = jnp.cos(ang)
    sin_ref[...] = jnp.sin(ang)


def _rope_tables(pos_col, *, tm=1024):
    m = pos_col.shape[0]
    tm = min(tm, m)
    half = MLA_ROPE // 2
    inv = 1.0 / (ROPE_THETA ** (np.arange(0, MLA_ROPE, 2, dtype=np.float64) / MLA_ROPE))
    freq = jnp.asarray(np.tile(inv, LANES // half)[None, :], F32)
    return pl.pallas_call(
        _rope_kernel,
        grid=(m // tm,),
        in_specs=[pl.BlockSpec((tm, 1), lambda i: (i, 0)), pl.BlockSpec((1, LANES), lambda i: (0, 0))],
        out_specs=[pl.BlockSpec((tm, LANES), lambda i: (i, 0))] * 2,
        out_shape=[jax.ShapeDtypeStruct((m, LANES), F32)] * 2,
        compiler_params=_cparams(("parallel",)),
        name="rope_tables",
    )(pos_col, freq)


def _t5_bucket_static():
    n = np.arange(T5_TABLE)
    max_exact = T5_BUCKETS // 2
    nf = np.maximum(n, 1).astype(np.float64)
    large = max_exact + (np.log(nf / max_exact) / math.log(T5_MAX_DIST / max_exact)
                         * (T5_BUCKETS - max_exact)).astype(np.int64)
    large = np.minimum(large, T5_BUCKETS - 1)
    return np.where(n < max_exact, n, large)


def _bias_table_kernel(rb_ref, onehot_ref, o_ref):
    rb = rb_ref[...]
    oh = onehot_ref[...]
    rows = [jnp.sum(oh * rb[:, h:h + 1], axis=0, keepdims=True) for h in range(DIFF_HEADS)]
    o_ref[...] = jnp.concatenate(rows, axis=0) * LOG2E


def _bias_table(rel_bias):
    bucket = _t5_bucket_static()
    onehot = jnp.asarray((np.arange(T5_BUCKETS)[:, None] == bucket[None, :]).astype(np.float32))
    return pl.pallas_call(
        _bias_table_kernel,
        out_shape=jax.ShapeDtypeStruct((DIFF_HEADS, T5_TABLE), F32),
        name="t5_bias_table",
    )(rel_bias, onehot)


def _s5_disc_kernel(lr_ref, li_ref, ldt_ref, br_ref, bi_ref, apr_ref, api_ref, bbr_ref, bbi_ref):
    lr = lr_ref[...]
    li = li_ref[...]
    dt = jnp.exp(ldt_ref[...])
    k = (lax.broadcasted_iota(jnp.int32, (S5_STEPS, 1), 0) + 1).astype(F32)
    mag = jnp.exp(lr * dt * k)
    apr_ref[...] = mag * jnp.cos(li * dt * k)
    api_ref[...] = mag * jnp.sin(li * dt * k)
    mag1 = jnp.exp(lr * dt)
    ar = mag1 * jnp.cos(li * dt)
    ai = mag1 * jnp.sin(li * dt)
    den = lr * lr + li * li
    fr = ((ar - 1.0) * lr + ai * li) / den
    fi = (ai * lr - (ar - 1.0) * li) / den
    br = br_ref[...]
    bi = bi_ref[...]
    bbr_ref[...] = fr * br - fi * bi
    bbi_ref[...] = fr * bi + fi * br


def _s5_discretise(lam_re, lam_im, log_dt, b_re, b_im):
    n = S5_NSTATE
    lr = lam_re.reshape(1, n)
    li = lam_im.reshape(1, n)
    ldt = jnp.repeat(log_dt, S5_STATE).reshape(1, n)
    br = jnp.transpose(b_re, (2, 0, 1)).reshape(S5_GROUP, n)
    bi = jnp.transpose(b_im, (2, 0, 1)).reshape(S5_GROUP, n)
    return pl.pallas_call(
        _s5_disc_kernel,
        out_shape=[jax.ShapeDtypeStruct((S5_STEPS, n), F32)] * 2 + [jax.ShapeDtypeStruct((S5_GROUP, n), F32)] * 2,
        name="s5_discretise",
    )(lr, li, ldt, br, bi)


def _block_diag_in(bb):
    gh = S5_GROUPS // 2
    eye = jnp.eye(gh, dtype=bb.dtype)
    v = bb.reshape(S5_GROUP, 2, gh, S5_STATE)
    out = eye[None, :, None, :, None] * jnp.transpose(v, (1, 0, 2, 3))[:, None, :, :, :]
    return out.reshape(2, gh * S5_GROUP, S5_HALF)


def _block_diag_out(c):
    gh = S5_GROUPS // 2
    eye = jnp.eye(gh, dtype=c.dtype)
    v = jnp.transpose(c, (0, 2, 1)).reshape(2, gh, S5_STATE, S5_GROUP)
    out = v[:, :, :, None, :] * eye[None, :, None, :, None]
    return out.reshape(2, S5_HALF, gh * S5_GROUP)


def _s5_kernel(u0_ref, u1_ref, u2_ref, u3_ref, bh_ref, ch_ref, pwr_ref, pwi_ref, d_ref, wglu_ref, bglu_ref, o_ref,
               h_ref, hb_ref, cr_ref, ci_ref, cinr_ref, cini_ref, yo_ref):
    u_refs = (u0_ref, u1_ref, u2_ref, u3_ref)
    t = pl.program_id(1)
    nh = S5_HALF
    cw = S5_WIDTH // 2

    @pl.when(t == 0)
    def _():
        cr_ref[...] = jnp.zeros_like(cr_ref)
        ci_ref[...] = jnp.zeros_like(ci_ref)

    u = jnp.concatenate(
        [jnp.concatenate([r[pl.ds(j, SUBLANES, stride=S5_STEPS), :] for j in range(S5_STEPS)], axis=0)
         for r in u_refs], axis=1)
    ub = u.astype(BF16)
    ys = []
    for half in range(2):
        h_ref[:, pl.ds(2 * half * nh, 2 * nh)] = jnp.dot(ub[:, half * cw:(half + 1) * cw], bh_ref[half],
                                                          preferred_element_type=F32)
    for half in range(2):
        st = pl.ds(half * nh, nh)
        re = pl.ds(2 * half * nh, nh)
        im = pl.ds((2 * half + 1) * nh, nh)
        ar = pwr_ref[0:SUBLANES, st]
        ai = pwi_ref[0:SUBLANES, st]

        def scan(j, carry, re=re, im=im, ar=ar, ai=ai):
            hr, hi = carry
            r0 = pl.multiple_of(j * SUBLANES, SUBLANES)
            nr = ar * hr - ai * hi + h_ref[pl.ds(r0, SUBLANES), re]
            ni = ar * hi + ai * hr + h_ref[pl.ds(r0, SUBLANES), im]
            h_ref[pl.ds(r0, SUBLANES), re] = nr
            h_ref[pl.ds(r0, SUBLANES), im] = ni
            return nr, ni

        zero = jnp.zeros((SUBLANES, nh), F32)
        er, ei = lax.fori_loop(0, S5_STEPS, scan, (zero, zero), unroll=True)

        a_seg_r = pwr_ref[S5_TM - 1:S5_TM, st]
        a_seg_i = pwi_ref[S5_TM - 1:S5_TM, st]
        cr = cr_ref[:, st]
        ci = ci_ref[:, st]
        for s in range(SUBLANES):
            cinr_ref[s:s + 1, :] = cr
            cini_ref[s:s + 1, :] = ci
            cr, ci = (er[s:s + 1] + a_seg_r * cr - a_seg_i * ci, ei[s:s + 1] + a_seg_r * ci + a_seg_i * cr)
        cr_ref[:, st] = cr
        ci_ref[:, st] = ci

        cin_r = cinr_ref[...]
        cin_i = cini_ref[...]

        def fix(jj, _, re=re, im=im, st=st, cin_r=cin_r, cin_i=cin_i):
            r0 = pl.multiple_of(jj * 2 * SUBLANES, 2 * SUBLANES)
            out_r, out_i = [], []
            for k in range(2):
                rows = pl.ds(r0 + k * SUBLANES, SUBLANES)
                pr = pwr_ref[rows, st]
                pi = pwi_ref[rows, st]
                out_r.append(h_ref[rows, re] + pr * cin_r - pi * cin_i)
                out_i.append(h_ref[rows, im] + pr * cin_i + pi * cin_r)
            hb_ref[pl.ds(r0, 2 * SUBLANES), re] = jnp.concatenate(out_r, axis=0).astype(BF16)
            hb_ref[pl.ds(r0, 2 * SUBLANES), im] = jnp.concatenate(out_i, axis=0).astype(BF16)
            return 0

        lax.fori_loop(0, S5_STEPS // 2, fix, 0, unroll=True)
        ys.append(jnp.dot(hb_ref[:, pl.ds(2 * half * nh, 2 * nh)], ch_ref[half], preferred_element_type=F32))

    y = jnp.concatenate(ys, axis=1) + d_ref[...] * u
    c0 = math.sqrt(2.0 / math.pi)
    y = 0.5 * y * (1.0 + jnp.tanh(c0 * (y + 0.044715 * (y * y * y))))
    gate = jax.nn.sigmoid(jnp.dot(y.astype(BF16), wglu_ref[...], preferred_element_type=F32) + bglu_ref[...])
    out = y * gate
    tm = out.shape[0]
    for c in range(S5_WIDTH // LANES):
        yo_ref[pl.ds(c * tm, tm), :] = out[:, c * LANES:(c + 1) * LANES]
    for c in range(S5_WIDTH // LANES):
        for s in range(SUBLANES):
            o_ref[s * S5_STEPS:(s + 1) * S5_STEPS, c * LANES:(c + 1) * LANES] = (
                yo_ref[pl.ds(c * tm + s, S5_STEPS, stride=SUBLANES), :].astype(BF16))


def _s5_mixer(z, bh, ch, pwr, pwi, d, wglu, bglu, *, bsz, n_pos):
    tm = S5_TM
    nt = n_pos // tm
    m = bsz * n_pos
    const = lambda b, t: (0, 0)
    const3 = lambda b, t: (0, 0, 0)
    return pl.pallas_call(
        _s5_kernel,
        grid=(bsz, nt),
        in_specs=[
            *[pl.BlockSpec((tm, LANES), lambda b, t, c=c: (b * nt + t, OFF_S5 // LANES + c))
              for c in range(S5_WIDTH // LANES)],
            pl.BlockSpec(bh.shape, const3),
            pl.BlockSpec(ch.shape, const3),
            pl.BlockSpec((S5_TM, S5_NSTATE), const),
            pl.BlockSpec((S5_TM, S5_NSTATE), const),
            pl.BlockSpec((1, S5_WIDTH), const),
            pl.BlockSpec((S5_WIDTH, S5_WIDTH), const),
            pl.BlockSpec((1, S5_WIDTH), const),
        ],
        out_specs=pl.BlockSpec((tm, S5_WIDTH), lambda b, t: (b * nt + t, 0)),
        out_shape=jax.ShapeDtypeStruct((m, S5_WIDTH), BF16),
        scratch_shapes=[pltpu.VMEM((tm, 2 * S5_NSTATE), F32), pltpu.VMEM((tm, 2 * S5_NSTATE), BF16),
                        pltpu.VMEM((1, S5_NSTATE), F32), pltpu.VMEM((1, S5_NSTATE), F32),
                        pltpu.VMEM((8, S5_HALF), F32), pltpu.VMEM((8, S5_HALF), F32),
                        pltpu.VMEM((S5_WIDTH // LANES * tm, LANES), F32)],
        compiler_params=_cparams(("parallel", "arbitrary")),
        name="s5_mixer",
    )(z, z, z, z, bh, ch, pwr, pwi, d, wglu, bglu)


def _rms(x, g):
    return x * lax.rsqrt(jnp.mean(x * x, axis=-1, keepdims=True) + RMS_EPS) * g


def _mla_prep_kernel(cq_ref, ckv_ref, kr_ref, krr_ref, cos_ref, sin_ref, qg_ref, kvg_ref,
                     wq1_ref, wq2_ref, wkv_ref, q_ref, k_ref, vt_ref):
    cos = cos_ref[...]
    sin = sin_ref[...]
    scale = (MLA_NOPE + MLA_ROPE) ** -0.5 * LOG2E
    ones = jnp.ones((VT_ROWS - MLA_V, cos.shape[0]), BF16)
    cqn = _rms(cq_ref[...], qg_ref[...]).astype(BF16)
    z1 = jnp.dot(cqn, wq1_ref[...], preferred_element_type=F32)
    z2 = jnp.dot(cqn, wq2_ref[...], preferred_element_type=F32)
    ckvn = _rms(ckv_ref[...], kvg_ref[...]).astype(BF16)
    zkv = jnp.dot(ckvn, wkv_ref[...], preferred_element_type=F32)
    k_rope = (kr_ref[...] * cos + krr_ref[...] * sin).astype(BF16)
    for h in range(MLA_HEADS):
        a = h * MLA_QK_PAD
        q_nope = z1[:, a:a + MLA_NOPE]
        q_rope = z1[:, a + MLA_NOPE:a + MLA_QK_PAD] * cos + z2[:, h * LANES:(h + 1) * LANES] * sin
        q_ref[0, h, :, 0:MLA_NOPE] = (q_nope * scale).astype(BF16)
        q_ref[0, h, :, MLA_NOPE:MLA_QK_PAD] = (q_rope * scale).astype(BF16)
        k_ref[0, h, :, 0:MLA_NOPE] = zkv[:, a:a + MLA_NOPE].astype(BF16)
        k_ref[0, h, :, MLA_NOPE:MLA_QK_PAD] = k_rope
        vt_ref[0, 0, h, 0:MLA_V, :] = jnp.transpose(zkv[:, a + MLA_NOPE:a + MLA_NOPE + MLA_V]).astype(BF16)
        vt_ref[0, 0, h, MLA_V:VT_ROWS, :] = ones


def _softmax_step_t(st, vt, m_ref, acc_ref, idx, shift=None):
    m_old = m_ref[idx]
    cur = jnp.max(st, axis=0, keepdims=True)
    if shift is not None:
        cur = cur + shift
    m_new = jnp.maximum(m_old, cur)
    p = jnp.exp2(st - (m_new if shift is None else m_new - shift)).astype(BF16)
    corr = jnp.exp2(m_old - m_new)
    acc_ref[idx] = corr * acc_ref[idx] + jnp.dot(vt, p, preferred_element_type=F32)
    m_ref[idx] = m_new


def _causal_mask_t(t):
    k = lax.broadcasted_iota(jnp.int32, (t, t), 0)
    q = lax.broadcasted_iota(jnp.int32, (t, t), 1)
    return q >= k


def _finish_t(acc):
    return jnp.transpose(acc[0:MLA_V] / acc[MLA_V:MLA_V + 1])


def _causal_tiles(nb):
    pairs = [(qi, ki) for qi in range(nb) for ki in range(qi + 1)]
    return (jnp.asarray([p[0] for p in pairs], jnp.int32), jnp.asarray([p[1] for p in pairs], jnp.int32))


def _mla_flash_kernel(qi_ref, ki_ref, q_ref, k_ref, vt_ref, o_ref, m_ref, acc_ref, *, t):
    qi = qi_ref[pl.program_id(1)]
    ki = ki_ref[pl.program_id(1)]

    @pl.when(ki == 0)
    def _():
        m_ref[...] = jnp.full_like(m_ref, NEG_INF)
        acc_ref[...] = jnp.zeros_like(acc_ref)

    def step(masked):
        mask = _causal_mask_t(t) if masked else None
        def scores(h):
            st = lax.dot_general(k_ref[0, h], q_ref[0, h], (((1,), (1,)), ((), ())), preferred_element_type=F32)
            return jnp.where(mask, st, NEG_INF) if masked else st

        sts = [scores(h) for h in range(MLA_HEADS)]
        for h in range(MLA_HEADS):
            _softmax_step_t(sts[h], vt_ref[0, 0, h], m_ref, acc_ref, h)

    @pl.when(ki < qi)
    def _():
        step(False)

    @pl.when(ki == qi)
    def _():
        step(True)
        for h in range(MLA_HEADS):
            o_ref[0, :, h * MLA_V:(h + 1) * MLA_V] = _finish_t(acc_ref[h]).astype(BF16)


def _mla_flash(q, k, vt, *, t=ATTN_TILE):
    bsz, nh, n_pos, _ = q.shape
    t = min(t, n_pos)
    nb = n_pos // t
    qi_tab, ki_tab = _causal_tiles(nb)
    return pl.pallas_call(
        functools.partial(_mla_flash_kernel, t=t),
        grid_spec=pltpu.PrefetchScalarGridSpec(
            num_scalar_prefetch=2,
            grid=(bsz, qi_tab.shape[0]),
            in_specs=[
                pl.BlockSpec((1, nh, t, MLA_QK_PAD), lambda b, s, qt, kt: (b, 0, qt[s], 0)),
                pl.BlockSpec((1, nh, t, MLA_QK_PAD), lambda b, s, qt, kt: (b, 0, kt[s], 0)),
                pl.BlockSpec((1, 1, nh, VT_ROWS, t), lambda b, s, qt, kt: (b, kt[s], 0, 0, 0)),
            ],
            out_specs=pl.BlockSpec((1, t, nh * MLA_V), lambda b, s, qt, kt: (b, qt[s], 0)),
            scratch_shapes=[pltpu.VMEM((nh, 1, t), F32), pltpu.VMEM((nh, VT_ROWS, t), F32)],
        ),
        out_shape=jax.ShapeDtypeStruct((bsz, n_pos, nh * MLA_V), BF16),
        compiler_params=_cparams(("parallel", "arbitrary")),
        name="mla_flash",
    )(qi_tab, ki_tab, q, k, vt)


def _diff_prep_kernel(q_ref, k_ref, v_ref, qm_ref, kb_ref, vt_ref):
    hw = 2 * DIFF_QK
    tm = q_ref.shape[0]
    lane = lax.broadcasted_iota(jnp.int32, (tm, hw), 1)
    scale = DIFF_QK ** -0.5 * LOG2E
    ones = jnp.ones((VT_ROWS - DIFF_V, tm), BF16)
    kb_ref[...] = k_ref[...].astype(BF16)
    for h in range(DIFF_HEADS):
        qh = q_ref[:, h * hw:(h + 1) * hw] * scale
        qm_ref[0, 2 * h] = jnp.where(lane < DIFF_QK, qh, 0.0).astype(BF16)
        qm_ref[0, 2 * h + 1] = jnp.where(lane >= DIFF_QK, qh, 0.0).astype(BF16)
        vt_ref[0, 0, h, 0:DIFF_V, :] = jnp.transpose(v_ref[:, h * DIFF_V:(h + 1) * DIFF_V]).astype(BF16)
        vt_ref[0, 0, h, DIFF_V:VT_ROWS, :] = ones


N_MLA_PREP_IN = 11
N_DIFF_PREP_IN = 3


def _attn_prep_kernel(*refs):
    n_in = N_MLA_PREP_IN + N_DIFF_PREP_IN
    ins, outs = refs[:n_in], refs[n_in:]
    _mla_prep_kernel(*ins[:N_MLA_PREP_IN], *outs[:3])
    _diff_prep_kernel(*ins[N_MLA_PREP_IN:], *outs[3:])


def _attn_prep(z, cos, sin, qg, kvg, wq1, wq2, wkv, *, bsz, n_pos, tm=ATTN_TILE):
    tm = min(tm, n_pos)
    nt = n_pos // tm
    w = DIFF_HEADS * DIFF_V
    const = lambda b, t: (0, 0)
    zspec = lambda width, off: pl.BlockSpec((tm, width), lambda b, t: (b * nt + t, off // width))
    hspec = lambda width: pl.BlockSpec((1, MLA_HEADS, tm, width), lambda b, t: (b, 0, t, 0))
    slab = lambda nh: pl.BlockSpec((1, 1, nh, VT_ROWS, tm), lambda b, t: (b, t, 0, 0, 0))
    return pl.pallas_call(
        _attn_prep_kernel,
        grid=(bsz, nt),
        in_specs=[
            zspec(MLA_Q_RANK, OFF_CQ), zspec(MLA_KV_RANK, OFF_CKV), zspec(LANES, OFF_KR), zspec(LANES, OFF_KRR),
            pl.BlockSpec((tm, LANES), lambda b, t: (b * nt + t, 0)),
            pl.BlockSpec((tm, LANES), lambda b, t: (b * nt + t, 0)),
            pl.BlockSpec((1, MLA_Q_RANK), const), pl.BlockSpec((1, MLA_KV_RANK), const),
            pl.BlockSpec(wq1.shape, const), pl.BlockSpec(wq2.shape, const), pl.BlockSpec(wkv.shape, const),
            zspec(w, OFF_DQ), zspec(w, OFF_DK), zspec(w, OFF_DV),
        ],
        out_specs=[hspec(MLA_QK_PAD), hspec(MLA_QK_PAD), slab(MLA_HEADS),
                   pl.BlockSpec((1, 2 * DIFF_HEADS, tm, 2 * DIFF_QK), lambda b, t: (b, 0, t, 0)),
                   pl.BlockSpec((tm, w), lambda b, t: (b * nt + t, 0)),
                   slab(DIFF_HEADS)],
        out_shape=[jax.ShapeDtypeStruct((bsz, MLA_HEADS, n_pos, MLA_QK_PAD), BF16),
                   jax.ShapeDtypeStruct((bsz, MLA_HEADS, n_pos, MLA_QK_PAD), BF16),
                   jax.ShapeDtypeStruct((bsz, nt, MLA_HEADS, VT_ROWS, tm), BF16),
                   jax.ShapeDtypeStruct((bsz, 2 * DIFF_HEADS, n_pos, 2 * DIFF_QK), BF16),
                   jax.ShapeDtypeStruct((bsz * n_pos, w), BF16),
                   jax.ShapeDtypeStruct((bsz, nt, DIFF_HEADS, VT_ROWS, tm), BF16)],
        compiler_params=_cparams(("parallel", "parallel")),
        name="attn_prep",
    )(z, z, z, z, cos, sin, qg, kvg, wq1, wq2, wkv, z, z, z)


def _diff_kernel(qi_ref, ki_ref, qmin_ref, kmax_ref, qm_ref, k_ref, vt_ref, pq_ref, pk_ref, tab_ref,
                 lq1_ref, lk1_ref, lq2_ref, lk2_ref, sg_ref, o_ref, m_ref, acc_ref, *, t, tiles_per_row, lambda_init):
    qi = qi_ref[pl.program_id(1)]
    ki = ki_ref[pl.program_id(1)]
    nh = DIFF_HEADS
    hw = 2 * DIFF_QK

    @pl.when(ki == 0)
    def _():
        m_ref[...] = jnp.full_like(m_ref, NEG_INF)
        acc_ref[...] = jnp.zeros_like(acc_ref)

    nc = t // LANES
    qbase = (pl.program_id(0) * tiles_per_row + qi) * nc
    kbase = (pl.program_id(0) * tiles_per_row + ki) * nc

    def is_far(r, c):
        return qmin_ref[qbase + c] - kmax_ref[kbase + r] >= T5_TABLE - 1

    def all_of(pairs):
        out = None
        for r, c in pairs:
            out = is_far(r, c) if out is None else jnp.logical_and(out, is_far(r, c))
        return out

    def step(plan):
        sub = lambda r, c: (slice(r * LANES, (r + 1) * LANES), slice(c * LANES, (c + 1) * LANES))
        dist, local_mask = {}, None
        for r in range(nc):
            for c in range(nc):
                if plan[r][c] in ("gather", "diag"):
                    rows, cols = sub(r, c)
                    dist[r, c] = jnp.clip(pq_ref[0, :, cols] - pk_ref[rows, :], 0, T5_TABLE - 1)
                if plan[r][c] == "diag":
                    local_mask = _causal_mask_t(LANES)
        sts = []
        for h in range(nh):
            kh = k_ref[:, h * hw:(h + 1) * hw]
            shift = tab_ref[h:h + 1, T5_TABLE - 1:T5_TABLE]
            tab = jnp.broadcast_to(tab_ref[h:h + 1, :], (LANES, T5_TABLE))
            delta = {}
            for (r, c), d in dist.items():
                dl = jnp.take_along_axis(tab, d, axis=1, mode="promise_in_bounds") - shift
                delta[r, c] = jnp.where(local_mask, dl, NEG_INF) if plan[r][c] == "diag" else dl
            for mp in range(2):
                raw = lax.dot_general(kh, qm_ref[0, 2 * h + mp], (((1,), (1,)), ((), ())),
                                      preferred_element_type=F32)
                if delta or any("neg" in row for row in plan):
                    blocks = []
                    for r in range(nc):
                        row = []
                        for c in range(nc):
                            rows, cols = sub(r, c)
                            if plan[r][c] == "neg":
                                row.append(jnp.full((LANES, LANES), NEG_INF, F32))
                            elif (r, c) in delta:
                                row.append(raw[rows, cols] + delta[r, c])
                            else:
                                row.append(raw[rows, cols])
                        blocks.append(jnp.concatenate(row, axis=1))
                    raw = jnp.concatenate(blocks, axis=0)
                sts.append(raw)
        for j in range(2 * nh):
            shift = tab_ref[j // 2:j // 2 + 1, T5_TABLE - 1:T5_TABLE]
            _softmax_step_t(sts[j], vt_ref[0, 0, j // 2], m_ref, acc_ref, j, shift)

    every = [(r, c) for r in range(nc) for c in range(nc)]
    corner = (nc - 1, 0)
    plan_far = [["const"] * nc for _ in range(nc)]
    plan_corner = [["gather" if (r, c) == corner else "const" for c in range(nc)] for r in range(nc)]
    plan_full = [["gather"] * nc for _ in range(nc)]
    diag_kind = lambda r, c, beyond: "neg" if c < r else "diag" if c == r else "gather" if c == r + 1 else beyond
    plan_band = [[diag_kind(r, c, "const") for c in range(nc)] for r in range(nc)]
    plan_diag = [[diag_kind(r, c, "gather") for c in range(nc)] for r in range(nc)]

    below = ki < qi
    all_far = all_of(every)
    corner_far = all_of([p for p in every if p != corner])
    band_far = all_of([(r, c) for r, c in every if c >= r + 2])

    @pl.when(jnp.logical_and(below, all_far))
    def _():
        step(plan_far)

    @pl.when(jnp.logical_and(below, jnp.logical_and(corner_far, jnp.logical_not(all_far))))
    def _():
        step(plan_corner)

    @pl.when(jnp.logical_and(below, jnp.logical_not(corner_far)))
    def _():
        step(plan_full)

    @pl.when(jnp.logical_and(ki == qi, band_far))
    def _():
        step(plan_band)

    @pl.when(jnp.logical_and(ki == qi, jnp.logical_not(band_far)))
    def _():
        step(plan_diag)

    @pl.when(ki == qi)
    def _():
        lam =(jnp.exp(jnp.sum(lq1_ref[...] * lk1_ref[...], axis=-1, keepdims=True))
               - jnp.exp(jnp.sum(lq2_ref[...] * lk2_ref[...], axis=-1, keepdims=True)) + lambda_init)
        for h in range(nh):
            o = _finish_t(acc_ref[2 * h]) - lam * _finish_t(acc_ref[2 * h + 1])
            o = _rms(o, sg_ref[...]) * (1.0 - lambda_init)
            o_ref[:, h * DIFF_V:(h + 1) * DIFF_V] = o.astype(BF16)


def _diff_attn(qm, kb, vt, pos_q, pos_k, table, lq1, lk1, lq2, lk2, sg, *, bsz, n_pos, lambda_init, t=ATTN_TILE):
    t = min(t, n_pos)
    nb = n_pos // t
    w = DIFF_HEADS * DIFF_V
    const = lambda b, s, qt, kt, qmin, kmax: (0, 0)
    qi_tab, ki_tab = _causal_tiles(nb)
    chunks = pos_q.reshape(bsz * n_pos // LANES, LANES)
    qmin = jnp.min(chunks, axis=1)
    kmax = jnp.max(chunks, axis=1)
    return pl.pallas_call(
        functools.partial(_diff_kernel, t=t, tiles_per_row=nb, lambda_init=lambda_init),
        grid_spec=pltpu.PrefetchScalarGridSpec(
            num_scalar_prefetch=4,
            grid=(bsz, qi_tab.shape[0]),
            in_specs=[
                pl.BlockSpec((1, 2 * DIFF_HEADS, t, 2 * DIFF_QK), lambda b, s, qt, kt, qmin, kmax: (b, 0, qt[s], 0)),
                pl.BlockSpec((t, w), lambda b, s, qt, kt, qmin, kmax: (b * nb + kt[s], 0)),
                pl.BlockSpec((1, 1, DIFF_HEADS, VT_ROWS, t), lambda b, s, qt, kt, qmin, kmax: (b, kt[s], 0, 0, 0)),
                pl.BlockSpec((1, 1, t), lambda b, s, qt, kt, qmin, kmax: (b * nb + qt[s], 0, 0)),
                pl.BlockSpec((t, LANES), lambda b, s, qt, kt, qmin, kmax: (b * nb + kt[s], 0)),
                pl.BlockSpec((DIFF_HEADS, T5_TABLE), const),
                pl.BlockSpec((1, DIFF_QK), const), pl.BlockSpec((1, DIFF_QK), const),
                pl.BlockSpec((1, DIFF_QK), const), pl.BlockSpec((1, DIFF_QK), const),
                pl.BlockSpec((1, DIFF_V), const),
            ],
            out_specs=pl.BlockSpec((t, w), lambda b, s, qt, kt, qmin, kmax: (b * nb + qt[s], 0)),
            scratch_shapes=[pltpu.VMEM((2 * DIFF_HEADS, 1, t), F32),
                            pltpu.VMEM((2 * DIFF_HEADS, VT_ROWS, t), F32)],
        ),
        out_shape=jax.ShapeDtypeStruct((bsz * n_pos, w), BF16),
        compiler_params=_cparams(("parallel", "arbitrary")),
        name="diff_attn",
    )(qi_tab, ki_tab, qmin, kmax, qm, kb, vt, pos_q, pos_k, table, lq1, lk1, lq2, lk2, sg)


def _ret_kernel(q_ref, qr_ref, k_ref, kr_ref, v_ref, g_ref, cos_ref, sin_ref, o_ref, st_ref, *, tm):
    t = pl.program_id(1)
    c = RET_CHUNK
    nh = RET_HEADS
    w = nh * RET_QK

    @pl.when(t == 0)
    def _():
        st_ref[...] = jnp.zeros_like(st_ref)

    log_gamma = [math.log(1.0 - 2.0 ** (-5.0 - h)) for h in range(nh)]
    lane = lax.broadcasted_iota(jnp.int32, (1, w), 1)
    lg_lane = jnp.zeros((1, w), F32)
    for h in range(nh):
        lg_lane = jnp.where(lane // RET_QK == h, log_gamma[h], lg_lane)
    tok = lax.broadcasted_iota(jnp.int32, (c, 1), 0).astype(F32)
    q_decay = jnp.exp(lg_lane * (tok + 1.0))
    k_decay = jnp.exp(lg_lane * (c - 1.0 - tok))
    ri = lax.broadcasted_iota(jnp.int32, (c, c), 0)
    ci = lax.broadcasted_iota(jnp.int32, (c, c), 1)
    rel = (ri - ci).astype(F32)
    intra = [jnp.where(rel >= 0, jnp.exp(log_gamma[h] * jnp.maximum(rel, 0.0)), 0.0) for h in range(nh)]
    head_lanes = [(lane // RET_QK == h) for h in range(nh)]

    states = [st_ref[h * RET_QK:(h + 1) * RET_QK, :] for h in range(nh)]
    n_chunks = tm // c
    chunk_rows = [slice(j * c, (j + 1) * c) for j in range(n_chunks)]
    head_cols = [slice(h * RET_V, (h + 1) * RET_V) for h in range(nh)]

    vb, qdh, scores, kvs = {}, {}, {}, {}
    for j, rows in enumerate(chunk_rows):
        cos = jnp.concatenate([cos_ref[rows, :]] * (w // LANES), axis=1)
        sin = jnp.concatenate([sin_ref[rows, :]] * (w // LANES), axis=1)
        q = q_ref[rows, :] * cos + qr_ref[rows, :] * sin
        k = (k_ref[rows, :] * cos + kr_ref[rows, :] * sin) * (RET_QK ** -0.5)
        kb = k.astype(BF16)
        qd = q * q_decay
        kdt = jnp.transpose(k * k_decay).astype(BF16)
        for h in range(nh):
            vb[j, h] = v_ref[rows, head_cols[h]].astype(BF16)
            qh = jnp.where(head_lanes[h], q, 0.0).astype(BF16)
            scores[j, h] = lax.dot_general(qh, kb, (((1,), (1,)), ((), ())), preferred_element_type=F32) * intra[h]
            qdh[j, h] = jnp.where(head_lanes[h], qd, 0.0).astype(BF16)
            kvs[j, h] = jnp.dot(kdt[h * RET_QK:(h + 1) * RET_QK, :], vb[j, h], preferred_element_type=F32)

    state_b = []
    for j in range(n_chunks):
        state_b.append(jnp.concatenate(states, axis=0).astype(BF16))
        states = [states[h] * math.exp(log_gamma[h] * c) + kvs[j, h] for h in range(nh)]
    for h in range(nh):
        st_ref[h * RET_QK:(h + 1) * RET_QK, :] = states[h]

    for j, rows in enumerate(chunk_rows):
        for h in range(nh):
            inner = jnp.dot(scores[j, h].astype(BF16), vb[j, h], preferred_element_type=F32)
            cross = jnp.dot(qdh[j, h], state_b[j], preferred_element_type=F32)
            o = inner + cross
            mu = jnp.mean(o, axis=-1, keepdims=True)
            oc = o - mu
            var = jnp.mean(oc * oc, axis=-1, keepdims=True)
            o = oc * lax.rsqrt(var + LN_EPS)
            gh = g_ref[rows, head_cols[h]]
            o_ref[rows, head_cols[h]] = (gh * jax.nn.sigmoid(gh) * o).astype(BF16)


def _retention(z, cos, sin, *, bsz, n_pos, tm=512):
    tm = min(tm, n_pos)
    nt = n_pos // tm
    w = RET_HEADS * RET_QK
    wv = RET_HEADS * RET_V
    zspec = lambda width, off: pl.BlockSpec((tm, width), lambda b, t: (b * nt + t, off // width))
    tspec = pl.BlockSpec((tm, LANES), lambda b, t: (b * nt + t, 0))
    return pl.pallas_call(
        functools.partial(_ret_kernel, tm=tm),
        grid=(bsz, nt),
        in_specs=[zspec(w, OFF_RQ), zspec(w, OFF_RQR), zspec(w, OFF_RK), zspec(w, OFF_RKR),
                  zspec(wv, OFF_RV), zspec(wv, OFF_RG), tspec, tspec],
        out_specs=pl.BlockSpec((tm, wv), lambda b, t: (b * nt + t, 0)),
        out_shape=jax.ShapeDtypeStruct((bsz * n_pos, wv), BF16),
        scratch_shapes=[pltpu.VMEM((w, RET_V), F32)],
        compiler_params=_cparams(("parallel", "arbitrary")),
        name="retention",
    )(z, z, z, z, z, z, cos, sin)


def _outproj_ln_kernel(y0_ref, y1_ref, y2_ref, y3_ref, w_ref, x_ref, g_ref, b_ref, o_ref):
    parts = 4
    half = o_ref.shape[0] // parts
    accs = []
    for r in range(parts):
        rows = slice(r * half, (r + 1) * half)
        acc = None
        for j, y_ref in enumerate((y0_ref, y1_ref, y2_ref, y3_ref)):
            part = jnp.dot(y_ref[rows, :], w_ref[j * 512:(j + 1) * 512, :], preferred_element_type=F32)
            acc = part if acc is None else acc + part
        accs.append(acc)
    for r in range(parts):
        rows = slice(r * half, (r + 1) * half)
        o_ref[rows, :] = _layer_norm(ALPHA * x_ref[rows, :] + accs[r], g_ref[...], b_ref[...])


def _outproj_ln(ys, w, x, g, b, *, tm=512):
    m, d = x.shape
    tm = min(tm, m)
    yspec = pl.BlockSpec((tm, 512), lambda i: (i, 0))
    return pl.pallas_call(
        _outproj_ln_kernel,
        grid=(m // tm,),
        in_specs=[yspec, yspec, yspec, yspec,
                  pl.BlockSpec(w.shape, lambda i: (0, 0)),
                  pl.BlockSpec((tm, d), lambda i: (i, 0)),
                  pl.BlockSpec((1, d), lambda i: (0, 0)),
                  pl.BlockSpec((1, d), lambda i: (0, 0))],
        out_specs=pl.BlockSpec((tm, d), lambda i: (i, 0)),
        out_shape=jax.ShapeDtypeStruct((m, d), F32),
        compiler_params=_cparams(("parallel",)),
        name="outproj_ln",
    )(*ys, w, x, g, b)


def _rot_cols(w, heads, dim):
    k = w.shape[0]
    w = w.reshape(k, heads, 2, dim // 2)
    return jnp.concatenate([-w[:, :, 1], w[:, :, 0]], axis=-1).reshape(k, heads * dim)


def _pad_cols(w, width):
    return jnp.pad(w, ((0, 0), (0, width - w.shape[1])))


def _wide_w_in(w_in):
    sizes = (S5_WIDTH, MLA_Q_RANK, MLA_KV_RANK, MLA_ROPE,
             RET_HEADS * RET_QK, RET_HEADS * RET_QK, RET_HEADS * RET_V, RET_HEADS * RET_V,
             DIFF_HEADS * 2 * DIFF_QK, DIFF_HEADS * 2 * DIFF_QK, DIFF_HEADS * DIFF_V)
    offs = np.concatenate([[0], np.cumsum(sizes)])
    (s5_u, cq, ckv, kr, rq, rk, rv, rg, dq, dk, dv) = [w_in[:, offs[i]:offs[i + 1]] for i in range(len(sizes))]
    cols = [s5_u, cq, rv, rg, dq, dk, dv,
            rq, _rot_cols(rq, RET_HEADS, RET_QK), rk, _rot_cols(rk, RET_HEADS, RET_QK),
            ckv, _pad_cols(kr, LANES), _pad_cols(_rot_cols(kr, 1, MLA_ROPE), LANES),
            jnp.zeros((w_in.shape[0], IN_WIDE - IN_USED), w_in.dtype)]
    return jnp.concatenate(cols, axis=1).astype(BF16)


def _mla_q_weights(w_uq):
    k = w_uq.shape[0]
    w = w_uq.reshape(k, MLA_HEADS, MLA_NOPE + MLA_ROPE)
    rope = w[:, :, MLA_NOPE:]
    pad = MLA_QK_PAD - MLA_NOPE - MLA_ROPE
    w1 = jnp.pad(w, ((0, 0), (0, 0), (0, pad))).reshape(k, MLA_HEADS * MLA_QK_PAD)
    rot = _rot_cols(rope.reshape(k, MLA_HEADS * MLA_ROPE), MLA_HEADS, MLA_ROPE).reshape(k, MLA_HEADS, MLA_ROPE)
    w2 = jnp.pad(rot, ((0, 0), (0, 0), (0, LANES - MLA_ROPE))).reshape(k, MLA_HEADS * LANES)
    return w1.astype(BF16), w2.astype(BF16)


def kernel(x, p, positions, rel_bias, ffn1_w_gate, ffn1_w_up, ffn1_w_down, ln1_g, ln1_b, w_in, w_out, ln2_g, ln2_b, s5_lambda_re, s5_lambda_im, s5_log_dt, s5_b_re, s5_b_im, s5_c_re, s5_c_im, s5_d, s5_w_glu, s5_b_glu, mla_q_norm_g, mla_w_uq, mla_kv_norm_g, mla_w_ukv, diff_lambda_q1, diff_lambda_k1, diff_lambda_q2, diff_lambda_k2, diff_subln_g, ffn2_w_gate, ffn2_w_up, ffn2_w_down, ple_w_gate, ple_b_gate, ple_w_proj, ln3_g, ln3_b):
    bsz, n_pos, d = x.shape
    m = bsz * n_pos
    depth = ffn1_w_gate.shape[0]
    assert d == D_MODEL and n_pos % S5_TM == 0 and n_pos % ATTN_TILE == 0 and m % FFN_TM == 0, (x.shape,)
    xf = x.reshape(m, d)
    row = lambda v: v.reshape(1, -1)

    cos, sin = _rope_tables(positions.reshape(m, 1))
    table = _bias_table(rel_bias)
    t_diff = ATTN_TILE
    pos_q = positions.reshape(bsz * (n_pos // t_diff), 1, t_diff)
    pos_k = jnp.broadcast_to(positions.reshape(m, 1), (m, LANES))

    for i in range(depth):
        f1g, f1u, f1d = _cast_layer(i, ffn1_w_gate, ffn1_w_up, ffn1_w_down, col_tile=(FFN_TF, FFN_TF))
        f2g, f2u, f2d, pwg, wo = _cast_layer(i, ffn2_w_gate, ffn2_w_up, ffn2_w_down, ple_w_gate, w_out,
                                             col_tile=(FFN_TF, FFN_TF))
        xf = _ffn_ln(xf, f1g, f1u, f1d, row(ln1_g[i]), row(ln1_b[i]))
        z = _inproj(xf, _wide_w_in(w_in[i]))

        apr, api, bbr, bbi = _s5_discretise(s5_lambda_re[i], s5_lambda_im[i], s5_log_dt[i], s5_b_re[i], s5_b_im[i])
        bh = jnp.concatenate([_block_diag_in(bbr), _block_diag_in(bbi)], axis=2).astype(BF16)
        ch = jnp.concatenate([_block_diag_out(s5_c_re[i]), -_block_diag_out(s5_c_im[i])], axis=1).astype(BF16)
        apr = jnp.repeat(apr, S5_TM // S5_STEPS, axis=0)
        api = jnp.repeat(api, S5_TM // S5_STEPS, axis=0)
        y_s5 = _s5_mixer(z, bh, ch, apr, api, row(s5_d[i]), s5_w_glu[i].astype(BF16), row(s5_b_glu[i]),
                         bsz=bsz, n_pos=n_pos)

        wq1, wq2 = _mla_q_weights(mla_w_uq[i])
        q, k, vt, dqm, dkb, dvt = _attn_prep(z, cos, sin, row(mla_q_norm_g[i]), row(mla_kv_norm_g[i]), wq1, wq2,
                                             mla_w_ukv[i].astype(BF16), bsz=bsz, n_pos=n_pos)
        y_mla = _mla_flash(q, k, vt).reshape(m, MLA_HEADS * MLA_V)

        y_ret = _retention(z, cos, sin, bsz=bsz, n_pos=n_pos)

        lambda_init = 0.8 - 0.6 * math.exp(-0.3 * i)
        y_diff = _diff_attn(dqm, dkb, dvt, pos_q, pos_k, table, row(diff_lambda_q1[i]), row(diff_lambda_k1[i]),
                            row(diff_lambda_q2[i]), row(diff_lambda_k2[i]), row(diff_subln_g[i]),
                            bsz=bsz, n_pos=n_pos, lambda_init=lambda_init, t=t_diff)

        xf = _outproj_ln((y_s5, y_mla, y_ret, y_diff), wo, xf, row(ln2_g[i]), row(ln2_b[i]))

        res, xb = _ple(xf, p.reshape(depth, m, PLE_DIM), i, pwg, row(ple_b_gate[i]), ple_w_proj[i].astype(BF16))
        xf = _ffn_ln(xb, f2g, f2u, f2d, row(ln3_g[i]), row(ln3_b[i]), res)
    return xf.reshape(bsz, n_pos, d)
```

```python
import functools
import math

import numpy as np
import jax
import jax.numpy as jnp
from jax import lax
from jax.experimental import pallas as pl
from jax.experimental.pallas import tpu as pltpu

F32 = jnp.float32
BF16 = jnp.bfloat16

LANES = 128
SUBLANES = 8
VMEM_BYTES = 64 * 1024 * 1024
VMEM_LIMIT_BYTES = VMEM_BYTES - 8 * 1024 * 1024
FFN_VMEM_LIMIT_BYTES = VMEM_BYTES - 4 * 1024 * 1024

D_MODEL = 2048
DEPTH = 2
PLE_DIM = 256
D_FF = 5632
ALPHA = (2 * DEPTH) ** 0.25
ROPE_THETA = 10000.0
NEG_INF = -1e30
LN_EPS = 1e-5
RMS_EPS = 1e-6

S5_WIDTH = 512
S5_GROUP = 16
S5_GROUPS = 32
S5_STATE = 64
S5_NSTATE = S5_GROUPS * S5_STATE
S5_HALF = S5_NSTATE // 2
S5_TM = 512
S5_STEPS = S5_TM // SUBLANES

MLA_HEADS = 4
MLA_Q_RANK = 512
MLA_KV_RANK = 128
MLA_NOPE = 128
MLA_ROPE = 64
MLA_V = 128
MLA_QK_PAD = 256

RET_HEADS = 4
RET_QK = 64
RET_V = 128
RET_CHUNK = 128

DIFF_HEADS = 4
DIFF_QK = 64
DIFF_V = 128

T5_BUCKETS = 32
T5_MAX_DIST = 128
T5_TABLE = 128

LOG2E = math.log2(math.e)
VT_ROWS = 144
ATTN_TILE = 512
FFN_TM = 1024
FFN_TF = 512

OFF_S5 = 0
OFF_CQ = 512
OFF_RV = 1024
OFF_RG = 1536
OFF_DQ = 2048
OFF_DK = 2560
OFF_DV = 3072
OFF_RQ = 3584
OFF_RQR = 3840
OFF_RK = 4096
OFF_RKR = 4352
OFF_CKV = 4608
OFF_KR = 4736
OFF_KRR = 4864
IN_USED = 4992
IN_WIDE = 5120
IN_TN = 2560


def _cparams(sem, vmem_limit_bytes=VMEM_LIMIT_BYTES):
    return pltpu.CompilerParams(dimension_semantics=sem, vmem_limit_bytes=vmem_limit_bytes)


def _layer_norm(y, g, b):
    mu = jnp.mean(y, axis=-1, keepdims=True)
    yc = y - mu
    var = jnp.mean(yc * yc, axis=-1, keepdims=True)
    return yc * lax.rsqrt(var + LN_EPS) * g + b


CAST_STEPS = 16


def _cast_kernel(*refs):
    n = len(refs) // 2
    for src, dst in zip(refs[:n], refs[n:]):
        if len(dst.shape) == 3:
            tile = dst.shape[2]
            for j in range(dst.shape[0]):
                dst[j] = src[0, :, j * tile:(j + 1) * tile].astype(BF16)
        else:
            dst[...] = src[0].astype(BF16)


def _cast_layer(layer, *stacked, col_tile=()):
    in_specs, out_specs, out_shape = [], [], []
    for k, w in enumerate(stacked):
        _, r, c = w.shape
        tr = r // CAST_STEPS
        in_specs.append(pl.BlockSpec((1, tr, c), lambda s: (layer, s, 0)))
        t = col_tile[k] if k < len(col_tile) else None
        if t:
            out_specs.append(pl.BlockSpec((c // t, tr, t), lambda s: (0, s, 0)))
            out_shape.append(jax.ShapeDtypeStruct((c // t, r, t), BF16))
        else:
            out_specs.append(pl.BlockSpec((tr, c), lambda s: (s, 0)))
            out_shape.append(jax.ShapeDtypeStruct((r, c), BF16))
    return pl.pallas_call(
        _cast_kernel,
        grid=(CAST_STEPS,),
        in_specs=in_specs,
        out_specs=out_specs,
        out_shape=out_shape,
        compiler_params=_cparams(("parallel",)),
        name="cast_weights",
    )(*stacked)


FFN_ROW_CHUNK = 128


def _ffn_ln_kernel(*refs, nf, tm, has_res):
    if has_res:
        xb_ref, res_ref, wg_ref, wu_ref, wd_ref, g_ref, b_ref, o_ref = refs
    else:
        res_ref, wg_ref, wu_ref, wd_ref, g_ref, b_ref, o_ref, xb_ref = refs
    f = pl.program_id(1)
    chunks = tm // FFN_ROW_CHUNK

    def rows(c):
        return pl.ds(pl.multiple_of(c * FFN_ROW_CHUNK, FFN_ROW_CHUNK), FFN_ROW_CHUNK)

    if not has_res:
        @pl.when(f == 0)
        def _():
            def cast(c, _):
                xb_ref[rows(c), :] = res_ref[rows(c), :].astype(BF16)
                return 0
            lax.fori_loop(0, chunks, cast, 0)

    def hidden():
        xb = xb_ref[...]
        gate = jnp.dot(xb, wg_ref[...], preferred_element_type=F32)
        up = jnp.dot(xb, wu_ref[...], preferred_element_type=F32)
        return (gate * jax.nn.sigmoid(gate) * up).astype(BF16)

    @pl.when(f == 0)
    def _():
        o_ref[...] = jnp.dot(hidden(), wd_ref[...], preferred_element_type=F32)

    @pl.when(jnp.logical_and(f > 0, f < nf - 1))
    def _():
        o_ref[...] += jnp.dot(hidden(), wd_ref[...], preferred_element_type=F32)

    @pl.when(f == nf - 1)
    def _():
        h = hidden()
        groups = [slice(c * FFN_ROW_CHUNK, (c + 1) * FFN_ROW_CHUNK) for c in range(chunks)]
        parts = [jnp.dot(h[r, :], wd_ref[...], preferred_element_type=F32) for r in groups]
        for r, part in zip(groups, parts):
            res = res_ref[r, :]
            y = (res if has_res else ALPHA * res) + 0.5 * (o_ref[r, :] + part)
            o_ref[r, :] = _layer_norm(y, g_ref[...], b_ref[...])


def _ffn_ln(x, wg, wu, wd, g, b, res=None, *, tm=FFN_TM):
    m, d = x.shape
    nf, _, tf = wg.shape
    tm = min(tm, m)
    assert nf >= 2 and wd.shape[0] == nf * tf, (wg.shape, wd.shape)
    xspec = pl.BlockSpec((tm, d), lambda i, f: (i, 0))
    in_specs = [xspec] * (2 if res is not None else 1) + [
        pl.BlockSpec((None, d, tf), lambda i, f: (f, 0, 0)),
        pl.BlockSpec((None, d, tf), lambda i, f: (f, 0, 0)),
        pl.BlockSpec((tf, d), lambda i, f: (f, 0)),
        pl.BlockSpec((1, d), lambda i, f: (0, 0)),
        pl.BlockSpec((1, d), lambda i, f: (0, 0)),
    ]
    args = ([x, res] if res is not None else [x]) + [wg, wu, wd, g, b]
    return pl.pallas_call(
        functools.partial(_ffn_ln_kernel, nf=nf, tm=tm, has_res=res is not None),
        grid=(m // tm, nf),
        in_specs=in_specs,
        out_specs=pl.BlockSpec((tm, d), lambda i, f: (i, 0)),
        out_shape=jax.ShapeDtypeStruct((m, d), F32),
        scratch_shapes=[] if res is not None else [pltpu.VMEM((tm, d), BF16)],
        compiler_params=_cparams(("parallel", "arbitrary"), FFN_VMEM_LIMIT_BYTES),
        name="ffn_ln",
    )(*args)


def _ple_kernel(x_ref, p_ref, wg_ref, bg_ref, wp_ref, res_ref, xb_ref):
    x = x_ref[...]
    xb = x.astype(BF16)
    xb_ref[...] = xb
    pb = p_ref[...].astype(BF16)
    gate = jax.nn.sigmoid(jnp.dot(xb, wg_ref[...], preferred_element_type=F32) + bg_ref[...])
    res_ref[...] = ALPHA * x + gate * jnp.dot(pb, wp_ref[...], preferred_element_type=F32)


def _ple(x, p, layer, wg, bg, wp, *, tm=512):
    m, d = x.shape
    tm = min(tm, m)
    return pl.pallas_call(
        _ple_kernel,
        grid=(m // tm,),
        in_specs=[
            pl.BlockSpec((tm, d), lambda i: (i, 0)),
            pl.BlockSpec((None, tm, PLE_DIM), lambda i: (layer, i, 0)),
            pl.BlockSpec((d, d), lambda i: (0, 0)),
            pl.BlockSpec((1, d), lambda i: (0, 0)),
            pl.BlockSpec((PLE_DIM, d), lambda i: (0, 0)),
        ],
        out_specs=[pl.BlockSpec((tm, d), lambda i: (i, 0))] * 2,
        out_shape=[jax.ShapeDtypeStruct((m, d), F32), jax.ShapeDtypeStruct((m, d), BF16)],
        compiler_params=_cparams(("parallel",)),
        name="ple",
    )(x, p, wg, bg, wp)


def _inproj_kernel(x_ref, w_ref, o_ref):
    o_ref[...] = jnp.dot(x_ref[...].astype(BF16), w_ref[...], preferred_element_type=F32)


def _inproj(x, w, *, tm=512):
    m, d = x.shape
    n = w.shape[1]
    tm = min(tm, m)
    return pl.pallas_call(
        _inproj_kernel,
        grid=(n // IN_TN, m // tm),
        in_specs=[
            pl.BlockSpec((tm, d), lambda j, i: (i, 0)),
            pl.BlockSpec((d, IN_TN), lambda j, i: (0, j)),
        ],
        out_specs=pl.BlockSpec((tm, IN_TN), lambda j, i: (i, j)),
        out_shape=jax.ShapeDtypeStruct((m, n), F32),
        compiler_params=_cparams(("parallel", "parallel")),
        name="inproj",
    )(x, w)


def _rope_kernel(pos_ref, freq_ref, cos_ref, sin_ref):
    ang = pos_ref[...].astype(F32) * freq_ref[...]
    cos_ref[...] = jnp.cos(ang)
    sin_ref[...] = jnp.sin(ang)


def _rope_tables(pos_col, *, tm=1024):
    m = pos_col.shape[0]
    tm = min(tm, m)
    half = MLA_ROPE // 2
    inv = 1.0 / (ROPE_THETA ** (np.arange(0, MLA_ROPE, 2, dtype=np.float64) / MLA_ROPE))
    freq = jnp.asarray(np.tile(inv, LANES // half)[None, :], F32)
    return pl.pallas_call(
        _rope_kernel,
        grid=(m // tm,),
        in_specs=[pl.BlockSpec((tm, 1), lambda i: (i, 0)), pl.BlockSpec((1, LANES), lambda i: (0, 0))],
        out_specs=[pl.BlockSpec((tm, LANES), lambda i: (i, 0))] * 2,
        out_shape=[jax.ShapeDtypeStruct((m, LANES), F32)] * 2,
        compiler_params=_cparams(("parallel",)),
        name="rope_tables",
    )(pos_col, freq)


def _t5_bucket_static():
    n = np.arange(T5_TABLE)
    max_exact = T5_BUCKETS // 2
    nf = np.maximum(n, 1).astype(np.float64)
    large = max_exact + (np.log(nf / max_exact) / math.log(T5_MAX_DIST / max_exact)
                         * (T5_BUCKETS - max_exact)).astype(np.int64)
    large = np.minimum(large, T5_BUCKETS - 1)
    return np.where(n < max_exact, n, large)


def _bias_table_kernel(rb_ref, onehot_ref, o_ref):
    rb = rb_ref[...]
    oh = onehot_ref[...]
    rows = [jnp.sum(oh * rb[:, h:h + 1], axis=0, keepdims=True) for h in range(DIFF_HEADS)]
    o_ref[...] = jnp.concatenate(rows, axis=0) * LOG2E


def _bias_table(rel_bias):
    bucket = _t5_bucket_static()
    onehot = jnp.asarray((np.arange(T5_BUCKETS)[:, None] == bucket[None, :]).astype(np.float32))
    return pl.pallas_call(
        _bias_table_kernel,
        out_shape=jax.ShapeDtypeStruct((DIFF_HEADS, T5_TABLE), F32),
        name="t5_bias_table",
    )(rel_bias, onehot)


def _s5_disc_kernel(lr_ref, li_ref, ldt_ref, br_ref, bi_ref, apr_ref, api_ref, bbr_ref, bbi_ref):
    lr = lr_ref[...]
    li = li_ref[...]
    dt = jnp.exp(ldt_ref[...])
    k = (lax.broadcasted_iota(jnp.int32, (S5_STEPS, 1), 0) + 1).astype(F32)
    mag = jnp.exp(lr * dt * k)
    apr_ref[...] = mag * jnp.cos(li * dt * k)
    api_ref[...] = mag * jnp.sin(li * dt * k)
    mag1 = jnp.exp(lr * dt)
    ar = mag1 * jnp.cos(li * dt)
    ai = mag1 * jnp.sin(li * dt)
    den = lr * lr + li * li
    fr = ((ar - 1.0) * lr + ai * li) / den
    fi = (ai * lr - (ar - 1.0) * li) / den
    br = br_ref[...]
    bi = bi_ref[...]
    bbr_ref[...] = fr * br - fi * bi
    bbi_ref[...] = fr * bi + fi * br


def _s5_discretise(lam_re, lam_im, log_dt, b_re, b_im):
    n = S5_NSTATE
    lr = lam_re.reshape(1, n)
    li = lam_im.reshape(1, n)
    ldt = jnp.repeat(log_dt, S5_STATE).reshape(1, n)
    br = jnp.transpose(b_re, (2, 0, 1)).reshape(S5_GROUP, n)
    bi = jnp.transpose(b_im, (2, 0, 1)).reshape(S5_GROUP, n)
    return pl.pallas_call(
        _s5_disc_kernel,
        out_shape=[jax.ShapeDtypeStruct((S5_STEPS, n), F32)] * 2 + [jax.ShapeDtypeStruct((S5_GROUP, n), F32)] * 2,
        name="s5_discretise",
    )(lr, li, ldt, br, bi)


def _block_diag_in(bb):
    gh = S5_GROUPS // 2
    eye = jnp.eye(gh, dtype=bb.dtype)
    v = bb.reshape(S5_GROUP, 2, gh, S5_STATE)
    out = eye[None, :, None, :, None] * jnp.transpose(v, (1, 0, 2, 3))[:, None, :, :, :]
    return out.reshape(2, gh * S5_GROUP, S5_HALF)


def _block_diag_out(c):
    gh = S5_GROUPS // 2
    eye = jnp.eye(gh, dtype=c.dtype)
    v = jnp.transpose(c, (0, 2, 1)).reshape(2, gh, S5_STATE, S5_GROUP)
    out = v[:, :, :, None, :] * eye[None, :, None, :, None]
    return out.reshape(2, S5_HALF, gh * S5_GROUP)


def _s5_kernel(u_ref, bh_ref, ch_ref, pwr_ref, pwi_ref, d_ref, wglu_ref, bglu_ref, o_ref,
               h_ref, hb_ref, cr_ref, ci_ref, cinr_ref, cini_ref, yo_ref):
    t = pl.program_id(1)
    nh = S5_HALF
    cw = S5_WIDTH // 2

    @pl.when(t == 0)
    def _():
        cr_ref[...] = jnp.zeros_like(cr_ref)
        ci_ref[...] = jnp.zeros_like(ci_ref)

    tm_in = u_ref.shape[0]
    ncol = S5_WIDTH // LANES
    for c in range(ncol):
        yo_ref[pl.ds(c * tm_in, tm_in), :] = u_ref[:, c * LANES:(c + 1) * LANES]
    u = jnp.concatenate(
        [jnp.concatenate([yo_ref[pl.ds(c * tm_in + j, SUBLANES, stride=S5_STEPS), :] for j in range(S5_STEPS)],
                         axis=0) for c in range(ncol)], axis=1)
    ub = u.astype(BF16)
    ys = []
    for half in range(2):
        h_ref[:, pl.ds(2 * half * nh, 2 * nh)] = jnp.dot(ub[:, half * cw:(half + 1) * cw], bh_ref[half],
                                                          preferred_element_type=F32)
    for half in range(2):
        st = pl.ds(half * nh, nh)
        re = pl.ds(2 * half * nh, nh)
        im = pl.ds((2 * half + 1) * nh, nh)
        ar = pwr_ref[0:SUBLANES, st]
        ai = pwi_ref[0:SUBLANES, st]

        def scan(j, carry, re=re, im=im, ar=ar, ai=ai):
            hr, hi = carry
            r0 = pl.multiple_of(j * SUBLANES, SUBLANES)
            nr = ar * hr - ai * hi + h_ref[pl.ds(r0, SUBLANES), re]
            ni = ar * hi + ai * hr + h_ref[pl.ds(r0, SUBLANES), im]
            h_ref[pl.ds(r0, SUBLANES), re] = nr
            h_ref[pl.ds(r0, SUBLANES), im] = ni
            return nr, ni

        zero = jnp.zeros((SUBLANES, nh), F32)
        er, ei = lax.fori_loop(0, S5_STEPS, scan, (zero, zero), unroll=True)

        a_seg_r = pwr_ref[S5_TM - 1:S5_TM, st]
        a_seg_i = pwi_ref[S5_TM - 1:S5_TM, st]
        cr = cr_ref[:, st]
        ci = ci_ref[:, st]
        for s in range(SUBLANES):
            cinr_ref[s:s + 1, :] = cr
            cini_ref[s:s + 1, :] = ci
            cr, ci = (er[s:s + 1] + a_seg_r * cr - a_seg_i * ci, ei[s:s + 1] + a_seg_r * ci + a_seg_i * cr)
        cr_ref[:, st] = cr
        ci_ref[:, st] = ci

        cin_r = cinr_ref[...]
        cin_i = cini_ref[...]

        def fix(jj, _, re=re, im=im, st=st, cin_r=cin_r, cin_i=cin_i):
            r0 = pl.multiple_of(jj * 2 * SUBLANES, 2 * SUBLANES)
            out_r, out_i = [], []
            for k in range(2):
                rows = pl.ds(r0 + k * SUBLANES, SUBLANES)
                pr = pwr_ref[rows, st]
                pi = pwi_ref[rows, st]
                out_r.append(h_ref[rows, re] + pr * cin_r - pi * cin_i)
                out_i.append(h_ref[rows, im] + pr * cin_i + pi * cin_r)
            hb_ref[pl.ds(r0, 2 * SUBLANES), re] = jnp.concatenate(out_r, axis=0).astype(BF16)
            hb_ref[pl.ds(r0, 2 * SUBLANES), im] = jnp.concatenate(out_i, axis=0).astype(BF16)
            return 0

        lax.fori_loop(0, S5_STEPS // 2, fix, 0, unroll=True)
        ys.append(jnp.dot(hb_ref[:, pl.ds(2 * half * nh, 2 * nh)], ch_ref[half], preferred_element_type=F32))

    y = jnp.concatenate(ys, axis=1) + d_ref[...] * u
    c0 = math.sqrt(2.0 / math.pi)
    y = 0.5 * y * (1.0 + jnp.tanh(c0 * (y + 0.044715 * (y * y * y))))
    gate = jax.nn.sigmoid(jnp.dot(y.astype(BF16), wglu_ref[...], preferred_element_type=F32) + bglu_ref[...])
    out = y * gate
    tm = out.shape[0]
    for c in range(S5_WIDTH // LANES):
        yo_ref[pl.ds(c * tm, tm), :] = out[:, c * LANES:(c + 1) * LANES]
    for c in range(S5_WIDTH // LANES):
        for s in range(SUBLANES):
            o_ref[s * S5_STEPS:(s + 1) * S5_STEPS, c * LANES:(c + 1) * LANES] = (
                yo_ref[pl.ds(c * tm + s, S5_STEPS, stride=SUBLANES), :].astype(BF16))


def _s5_mixer(z, bh, ch, pwr, pwi, d, wglu, bglu, *, bsz, n_pos):
    tm = S5_TM
    nt = n_pos // tm
    m = bsz * n_pos
    const = lambda b, t: (0, 0)
    const3 = lambda b, t: (0, 0, 0)
    return pl.pallas_call(
        _s5_kernel,
        grid=(bsz, nt),
        in_specs=[
            pl.BlockSpec((tm, S5_WIDTH), lambda b, t: (b * nt + t, OFF_S5 // S5_WIDTH)),
            pl.BlockSpec(bh.shape, const3),
            pl.BlockSpec(ch.shape, const3),
            pl.BlockSpec((S5_TM, S5_NSTATE), const),
            pl.BlockSpec((S5_TM, S5_NSTATE), const),
            pl.BlockSpec((1, S5_WIDTH), const),
            pl.BlockSpec((S5_WIDTH, S5_WIDTH), const),
            pl.BlockSpec((1, S5_WIDTH), const),
        ],
        out_specs=pl.BlockSpec((tm, S5_WIDTH), lambda b, t: (b * nt + t, 0)),
        out_shape=jax.ShapeDtypeStruct((m, S5_WIDTH), BF16),
        scratch_shapes=[pltpu.VMEM((tm, 2 * S5_NSTATE), F32), pltpu.VMEM((tm, 2 * S5_NSTATE), BF16),
                        pltpu.VMEM((1, S5_NSTATE), F32), pltpu.VMEM((1, S5_NSTATE), F32),
                        pltpu.VMEM((8, S5_HALF), F32), pltpu.VMEM((8, S5_HALF), F32),
                        pltpu.VMEM((S5_WIDTH // LANES * tm, LANES), F32)],
        compiler_params=_cparams(("parallel", "arbitrary")),
        name="s5_mixer",
    )(z, bh, ch, pwr, pwi, d, wglu, bglu)


def _rms(x, g):
    return x * lax.rsqrt(jnp.mean(x * x, axis=-1, keepdims=True) + RMS_EPS) * g


def _mla_prep_kernel(cq_ref, ckv_ref, kr_ref, krr_ref, cos_ref, sin_ref, qg_ref, kvg_ref,
                     wq1_ref, wq2_ref, wkv_ref, q_ref, k_ref, vt_ref):
    cos = cos_ref[...]
    sin = sin_ref[...]
    scale = (MLA_NOPE + MLA_ROPE) ** -0.5 * LOG2E
    ones = jnp.ones((VT_ROWS - MLA_V, cos.shape[0]), BF16)
    cqn = _rms(cq_ref[...], qg_ref[...]).astype(BF16)
    z1 = jnp.dot(cqn, wq1_ref[...], preferred_element_type=F32)
    z2 = jnp.dot(cqn, wq2_ref[...], preferred_element_type=F32)
    ckvn = _rms(ckv_ref[...], kvg_ref[...]).astype(BF16)
    zkv = jnp.dot(ckvn, wkv_ref[...], preferred_element_type=F32)
    k_rope = (kr_ref[...] * cos + krr_ref[...] * sin).astype(BF16)
    for h in range(MLA_HEADS):
        a = h * MLA_QK_PAD
        q_nope = z1[:, a:a + MLA_NOPE]
        q_rope = z1[:, a + MLA_NOPE:a + MLA_QK_PAD] * cos + z2[:, h * LANES:(h + 1) * LANES] * sin
        q_ref[0, h, :, 0:MLA_NOPE] = (q_nope * scale).astype(BF16)
        q_ref[0, h, :, MLA_NOPE:MLA_QK_PAD] = (q_rope * scale).astype(BF16)
        k_ref[0, h, :, 0:MLA_NOPE] = zkv[:, a:a + MLA_NOPE].astype(BF16)
        k_ref[0, h, :, MLA_NOPE:MLA_QK_PAD] = k_rope
        vt_ref[0, 0, h, 0:MLA_V, :] = jnp.transpose(zkv[:, a + MLA_NOPE:a + MLA_NOPE + MLA_V]).astype(BF16)
        vt_ref[0, 0, h, MLA_V:VT_ROWS, :] = ones


def _softmax_step_t(st, vt, m_ref, acc_ref, idx, shift=None):
    m_old = m_ref[idx]
    cur = jnp.max(st, axis=0, keepdims=True)
    if shift is not None:
        cur = cur + shift
    m_new = jnp.maximum(m_old, cur)
    p = jnp.exp2(st - (m_new if shift is None else m_new - shift)).astype(BF16)
    corr = jnp.exp2(m_old - m_new)
    acc_ref[idx] = corr * acc_ref[idx] + jnp.dot(vt, p, preferred_element_type=F32)
    m_ref[idx] = m_new


def _causal_mask_t(t):
    k = lax.broadcasted_iota(jnp.int32, (t, t), 0)
    q = lax.broadcasted_iota(jnp.int32, (t, t), 1)
    return q >= k


def _finish_t(acc):
    return jnp.transpose(acc[0:MLA_V] / acc[MLA_V:MLA_V + 1])


def _causal_tiles(nb):
    pairs = [(qi, ki) for qi in range(nb) for ki in range(qi + 1)]
    return (jnp.asarray([p[0] for p in pairs], jnp.int32), jnp.asarray([p[1] for p in pairs], jnp.int32))


def _mla_flash_kernel(qi_ref, ki_ref, q_ref, k_ref, vt_ref, o_ref, m_ref, acc_ref, *, t):
    qi = qi_ref[pl.program_id(1)]
    ki = ki_ref[pl.program_id(1)]

    @pl.when(ki == 0)
    def _():
        m_ref[...] = jnp.full_like(m_ref, NEG_INF)
        acc_ref[...] = jnp.zeros_like(acc_ref)

    def step(masked):
        mask = _causal_mask_t(t) if masked else None
        def scores(h):
            st = lax.dot_general(k_ref[0, h], q_ref[0, h], (((1,), (1,)), ((), ())), preferred_element_type=F32)
            return jnp.where(mask, st, NEG_INF) if masked else st

        sts = [scores(h) for h in range(MLA_HEADS)]
        for h in range(MLA_HEADS):
            _softmax_step_t(sts[h], vt_ref[0, 0, h], m_ref, acc_ref, h)

    @pl.when(ki < qi)
    def _():
        step(False)

    @pl.when(ki == qi)
    def _():
        step(True)
        for h in range(MLA_HEADS):
            o_ref[0, :, h * MLA_V:(h + 1) * MLA_V] = _finish_t(acc_ref[h]).astype(BF16)


def _mla_flash(q, k, vt, *, t=ATTN_TILE):
    bsz, nh, n_pos, _ = q.shape
    t = min(t, n_pos)
    nb = n_pos // t
    qi_tab, ki_tab = _causal_tiles(nb)
    return pl.pallas_call(
        functools.partial(_mla_flash_kernel, t=t),
        grid_spec=pltpu.PrefetchScalarGridSpec(
            num_scalar_prefetch=2,
            grid=(bsz, qi_tab.shape[0]),
            in_specs=[
                pl.BlockSpec((1, nh, t, MLA_QK_PAD), lambda b, s, qt, kt: (b, 0, qt[s], 0)),
                pl.BlockSpec((1, nh, t, MLA_QK_PAD), lambda b, s, qt, kt: (b, 0, kt[s], 0)),
                pl.BlockSpec((1, 1, nh, VT_ROWS, t), lambda b, s, qt, kt: (b, kt[s], 0, 0, 0)),
            ],
            out_specs=pl.BlockSpec((1, t, nh * MLA_V), lambda b, s, qt, kt: (b, qt[s], 0)),
            scratch_shapes=[pltpu.VMEM((nh, 1, t), F32), pltpu.VMEM((nh, VT_ROWS, t), F32)],
        ),
        out_shape=jax.ShapeDtypeStruct((bsz, n_pos, nh * MLA_V), BF16),
        compiler_params=_cparams(("parallel", "arbitrary")),
        name="mla_flash",
    )(qi_tab, ki_tab, q, k, vt)


def _diff_prep_kernel(q_ref, k_ref, v_ref, qm_ref, kb_ref, vt_ref):
    hw = 2 * DIFF_QK
    tm = q_ref.shape[0]
    lane = lax.broadcasted_iota(jnp.int32, (tm, hw), 1)
    scale = DIFF_QK ** -0.5 * LOG2E
    ones = jnp.ones((VT_ROWS - DIFF_V, tm), BF16)
    kb_ref[...] = k_ref[...].astype(BF16)
    for h in range(DIFF_HEADS):
        qh = q_ref[:, h * hw:(h + 1) * hw] * scale
        qm_ref[0, 2 * h] = jnp.where(lane < DIFF_QK, qh, 0.0).astype(BF16)
        qm_ref[0, 2 * h + 1] = jnp.where(lane >= DIFF_QK, qh, 0.0).astype(BF16)
        vt_ref[0, 0, h, 0:DIFF_V, :] = jnp.transpose(v_ref[:, h * DIFF_V:(h + 1) * DIFF_V]).astype(BF16)
        vt_ref[0, 0, h, DIFF_V:VT_ROWS, :] = ones


N_MLA_PREP_IN = 11
N_DIFF_PREP_IN = 3


def _attn_prep_kernel(*refs):
    n_in = N_MLA_PREP_IN + N_DIFF_PREP_IN
    ins, outs = refs[:n_in], refs[n_in:]
    _mla_prep_kernel(*ins[:N_MLA_PREP_IN], *outs[:3])
    _diff_prep_kernel(*ins[N_MLA_PREP_IN:], *outs[3:])


def _attn_prep(z, cos, sin, qg, kvg, wq1, wq2, wkv, *, bsz, n_pos, tm=ATTN_TILE):
    tm = min(tm, n_pos)
    nt = n_pos // tm
    w = DIFF_HEADS * DIFF_V
    const = lambda b, t: (0, 0)
    zspec = lambda width, off: pl.BlockSpec((tm, width), lambda b, t: (b * nt + t, off // width))
    hspec = lambda width: pl.BlockSpec((1, MLA_HEADS, tm, width), lambda b, t: (b, 0, t, 0))
    slab = lambda nh: pl.BlockSpec((1, 1, nh, VT_ROWS, tm), lambda b, t: (b, t, 0, 0, 0))
    return pl.pallas_call(
        _attn_prep_kernel,
        grid=(bsz, nt),
        in_specs=[
            zspec(MLA_Q_RANK, OFF_CQ), zspec(MLA_KV_RANK, OFF_CKV), zspec(LANES, OFF_KR), zspec(LANES, OFF_KRR),
            pl.BlockSpec((tm, LANES), lambda b, t: (b * nt + t, 0)),
            pl.BlockSpec((tm, LANES), lambda b, t: (b * nt + t, 0)),
            pl.BlockSpec((1, MLA_Q_RANK), const), pl.BlockSpec((1, MLA_KV_RANK), const),
            pl.BlockSpec(wq1.shape, const), pl.BlockSpec(wq2.shape, const), pl.BlockSpec(wkv.shape, const),
            zspec(w, OFF_DQ), zspec(w, OFF_DK), zspec(w, OFF_DV),
        ],
        out_specs=[hspec(MLA_QK_PAD), hspec(MLA_QK_PAD), slab(MLA_HEADS),
                   pl.BlockSpec((1, 2 * DIFF_HEADS, tm, 2 * DIFF_QK), lambda b, t: (b, 0, t, 0)),
                   pl.BlockSpec((tm, w), lambda b, t: (b * nt + t, 0)),
                   slab(DIFF_HEADS)],
        out_shape=[jax.ShapeDtypeStruct((bsz, MLA_HEADS, n_pos, MLA_QK_PAD), BF16),
                   jax.ShapeDtypeStruct((bsz, MLA_HEADS, n_pos, MLA_QK_PAD), BF16),
                   jax.ShapeDtypeStruct((bsz, nt, MLA_HEADS, VT_ROWS, tm), BF16),
                   jax.ShapeDtypeStruct((bsz, 2 * DIFF_HEADS, n_pos, 2 * DIFF_QK), BF16),
                   jax.ShapeDtypeStruct((bsz * n_pos, w), BF16),
                   jax.ShapeDtypeStruct((bsz, nt, DIFF_HEADS, VT_ROWS, tm), BF16)],
        compiler_params=_cparams(("parallel", "parallel")),
        name="attn_prep",
    )(z, z, z, z, cos, sin, qg, kvg, wq1, wq2, wkv, z, z, z)


def _diff_kernel(qi_ref, ki_ref, qmin_ref, kmax_ref, qm_ref, k_ref, vt_ref, pq_ref, pk_ref, tab_ref,
                 lq1_ref, lk1_ref, lq2_ref, lk2_ref, sg_ref, o_ref, m_ref, acc_ref, *, t, tiles_per_row, lambda_init):
    qi = qi_ref[pl.program_id(1)]
    ki = ki_ref[pl.program_id(1)]
    nh = DIFF_HEADS
    hw = 2 * DIFF_QK

    @pl.when(ki == 0)
    def _():
        m_ref[...] = jnp.full_like(m_ref, NEG_INF)
        acc_ref[...] = jnp.zeros_like(acc_ref)

    nc = t // LANES
    qbase = (pl.program_id(0) * tiles_per_row + qi) * nc
    kbase = (pl.program_id(0) * tiles_per_row + ki) * nc

    def is_far(r, c):
        return qmin_ref[qbase + c] - kmax_ref[kbase + r] >= T5_TABLE - 1

    def all_of(pairs):
        out = None
        for r, c in pairs:
            out = is_far(r, c) if out is None else jnp.logical_and(out, is_far(r, c))
        return out

    def step(plan):
        sub = lambda r, c: (slice(r * LANES, (r + 1) * LANES), slice(c * LANES, (c + 1) * LANES))
        dist, local_mask = {}, None
        for r in range(nc):
            for c in range(nc):
                if plan[r][c] in ("gather", "diag"):
                    rows, cols = sub(r, c)
                    dist[r, c] = jnp.clip(pq_ref[0, :, cols] - pk_ref[rows, :], 0, T5_TABLE - 1)
                if plan[r][c] == "diag":
                    local_mask = _causal_mask_t(LANES)
        sts = []
        for h in range(nh):
            kh = k_ref[:, h * hw:(h + 1) * hw]
            shift = tab_ref[h:h + 1, T5_TABLE - 1:T5_TABLE]
            tab = jnp.broadcast_to(tab_ref[h:h + 1, :], (LANES, T5_TABLE))
            delta = {}
            for (r, c), d in dist.items():
                dl = jnp.take_along_axis(tab, d, axis=1, mode="promise_in_bounds") - shift
                delta[r, c] = jnp.where(local_mask, dl, NEG_INF) if plan[r][c] == "diag" else dl
            for mp in range(2):
                raw = lax.dot_general(kh, qm_ref[0, 2 * h + mp], (((1,), (1,)), ((), ())),
                                      preferred_element_type=F32)
                if delta or any("neg" in row for row in plan):
                    blocks = []
                    for r in range(nc):
                        row = []
                        for c in range(nc):
                            rows, cols = sub(r, c)
                            if plan[r][c] == "neg":
                                row.append(jnp.full((LANES, LANES), NEG_INF, F32))
                            elif (r, c) in delta:
                                row.append(raw[rows, cols] + delta[r, c])
                            else:
                                row.append(raw[rows, cols])
                        blocks.append(jnp.concatenate(row, axis=1))
                    raw = jnp.concatenate(blocks, axis=0)
                sts.append(raw)
        for j in range(2 * nh):
            shift = tab_ref[j // 2:j // 2 + 1, T5_TABLE - 1:T5_TABLE]
            _softmax_step_t(sts[j], vt_ref[0, 0, j // 2], m_ref, acc_ref, j, shift)

    every = [(r, c) for r in range(nc) for c in range(nc)]
    corner = (nc - 1, 0)
    plan_far = [["const"] * nc for _ in range(nc)]
    plan_corner = [["gather" if (r, c) == corner else "const" for c in range(nc)] for r in range(nc)]
    plan_full = [["gather"] * nc for _ in range(nc)]
    diag_kind = lambda r, c, beyond: "neg" if c < r else "diag" if c == r else "gather" if c == r + 1 else beyond
    plan_band = [[diag_kind(r, c, "const") for c in range(nc)] for r in range(nc)]
    plan_diag = [[diag_kind(r, c, "gather") for c in range(nc)] for r in range(nc)]

    below = ki < qi
    all_far = all_of(every)
    corner_far = all_of([p for p in every if p != corner])
    band_far = all_of([(r, c) for r, c in every if c >= r + 2])

    @pl.when(jnp.logical_and(below, all_far))
    def _():
        step(plan_far)

    @pl.when(jnp.logical_and(below, jnp.logical_and(corner_far, jnp.logical_not(all_far))))
    def _():
        step(plan_corner)

    @pl.when(jnp.logical_and(below, jnp.logical_not(corner_far)))
    def _():
        step(plan_full)

    @pl.when(jnp.logical_and(ki == qi, band_far))
    def _():
        step(plan_band)

    @pl.when(jnp.logical_and(ki == qi, jnp.logical_not(band_far)))
    def _():
        step(plan_diag)

    @pl.when(ki == qi)
    def _():
        lam =(jnp.exp(jnp.sum(lq1_ref[...] * lk1_ref[...], axis=-1, keepdims=True))
               - jnp.exp(jnp.sum(lq2_ref[...] * lk2_ref[...], axis=-1, keepdims=True)) + lambda_init)
        for h in range(nh):
            o = _finish_t(acc_ref[2 * h]) - lam * _finish_t(acc_ref[2 * h + 1])
            o = _rms(o, sg_ref[...]) * (1.0 - lambda_init)
            o_ref[:, h * DIFF_V:(h + 1) * DIFF_V] = o.astype(BF16)


def _diff_attn(qm, kb, vt, pos_q, pos_k, table, lq1, lk1, lq2, lk2, sg, *, bsz, n_pos, lambda_init, t=ATTN_TILE):
    t = min(t, n_pos)
    nb = n_pos // t
    w = DIFF_HEADS * DIFF_V
    const = lambda b, s, qt, kt, qmin, kmax: (0, 0)
    qi_tab, ki_tab = _causal_tiles(nb)
    chunks = pos_q.reshape(bsz * n_pos // LANES, LANES)
    qmin = jnp.min(chunks, axis=1)
    kmax = jnp.max(chunks, axis=1)
    return pl.pallas_call(
        functools.partial(_diff_kernel, t=t, tiles_per_row=nb, lambda_init=lambda_init),
        grid_spec=pltpu.PrefetchScalarGridSpec(
            num_scalar_prefetch=4,
            grid=(bsz, qi_tab.shape[0]),
            in_specs=[
                pl.BlockSpec((1, 2 * DIFF_HEADS, t, 2 * DIFF_QK), lambda b, s, qt, kt, qmin, kmax: (b, 0, qt[s], 0)),
                pl.BlockSpec((t, w), lambda b, s, qt, kt, qmin, kmax: (b * nb + kt[s], 0)),
                pl.BlockSpec((1, 1, DIFF_HEADS, VT_ROWS, t), lambda b, s, qt, kt, qmin, kmax: (b, kt[s], 0, 0, 0)),
                pl.BlockSpec((1, 1, t), lambda b, s, qt, kt, qmin, kmax: (b * nb + qt[s], 0, 0)),
                pl.BlockSpec((t, LANES), lambda b, s, qt, kt, qmin, kmax: (b * nb + kt[s], 0)),
                pl.BlockSpec((DIFF_HEADS, T5_TABLE), const),
                pl.BlockSpec((1, DIFF_QK), const), pl.BlockSpec((1, DIFF_QK), const),
                pl.BlockSpec((1, DIFF_QK), const), pl.BlockSpec((1, DIFF_QK), const),
                pl.BlockSpec((1, DIFF_V), const),
            ],
            out_specs=pl.BlockSpec((t, w), lambda b, s, qt, kt, qmin, kmax: (b * nb + qt[s], 0)),
            scratch_shapes=[pltpu.VMEM((2 * DIFF_HEADS, 1, t), F32),
                            pltpu.VMEM((2 * DIFF_HEADS, VT_ROWS, t), F32)],
        ),
        out_shape=jax.ShapeDtypeStruct((bsz * n_pos, w), BF16),
        compiler_params=_cparams(("parallel", "arbitrary")),
        name="diff_attn",
    )(qi_tab, ki_tab, qmin, kmax, qm, kb, vt, pos_q, pos_k, table, lq1, lk1, lq2, lk2, sg)


def _ret_kernel(qq_ref, kk_ref, v_ref, g_ref, cos_ref, sin_ref, o_ref, st_ref, *, tm):
    t = pl.program_id(1)
    c = RET_CHUNK
    nh = RET_HEADS
    w = nh * RET_QK

    @pl.when(t == 0)
    def _():
        st_ref[...] = jnp.zeros_like(st_ref)

    log_gamma = [math.log(1.0 - 2.0 ** (-5.0 - h)) for h in range(nh)]
    lane = lax.broadcasted_iota(jnp.int32, (1, w), 1)
    lg_lane = jnp.zeros((1, w), F32)
    for h in range(nh):
        lg_lane = jnp.where(lane // RET_QK == h, log_gamma[h], lg_lane)
    tok = lax.broadcasted_iota(jnp.int32, (c, 1), 0).astype(F32)
    q_decay = jnp.exp(lg_lane * (tok + 1.0))
    k_decay = jnp.exp(lg_lane * (c - 1.0 - tok))
    ri = lax.broadcasted_iota(jnp.int32, (c, c), 0)
    ci = lax.broadcasted_iota(jnp.int32, (c, c), 1)
    rel = (ri - ci).astype(F32)
    intra = [jnp.where(rel >= 0, jnp.exp(log_gamma[h] * jnp.maximum(rel, 0.0)), 0.0) for h in range(nh)]
    head_lanes = [(lane // RET_QK == h) for h in range(nh)]

    states = [st_ref[h * RET_QK:(h + 1) * RET_QK, :] for h in range(nh)]
    n_chunks = tm // c
    chunk_rows = [slice(j * c, (j + 1) * c) for j in range(n_chunks)]
    head_cols = [slice(h * RET_V, (h + 1) * RET_V) for h in range(nh)]

    vb, qdh, scores, kvs = {}, {}, {}, {}
    for j, rows in enumerate(chunk_rows):
        cos = jnp.concatenate([cos_ref[rows, :]] * (w // LANES), axis=1)
        sin = jnp.concatenate([sin_ref[rows, :]] * (w // LANES), axis=1)
        q = qq_ref[rows, 0:w] * cos + qq_ref[rows, w:2 * w] * sin
        k = (kk_ref[rows, 0:w] * cos + kk_ref[rows, w:2 * w] * sin) * (RET_QK ** -0.5)
        kb = k.astype(BF16)
        qd = q * q_decay
        kdt = jnp.transpose(k * k_decay).astype(BF16)
        for h in range(nh):
            vb[j, h] = v_ref[rows, head_cols[h]].astype(BF16)
            qh = jnp.where(head_lanes[h], q, 0.0).astype(BF16)
            scores[j, h] = lax.dot_general(qh, kb, (((1,), (1,)), ((), ())), preferred_element_type=F32) * intra[h]
            qdh[j, h] = jnp.where(head_lanes[h], qd, 0.0).astype(BF16)
            kvs[j, h] = jnp.dot(kdt[h * RET_QK:(h + 1) * RET_QK, :], vb[j, h], preferred_element_type=F32)

    state_b = []
    for j in range(n_chunks):
        state_b.append(jnp.concatenate(states, axis=0).astype(BF16))
        states = [states[h] * math.exp(log_gamma[h] * c) + kvs[j, h] for h in range(nh)]
    for h in range(nh):
        st_ref[h * RET_QK:(h + 1) * RET_QK, :] = states[h]

    for j, rows in enumerate(chunk_rows):
        for h in range(nh):
            inner = jnp.dot(scores[j, h].astype(BF16), vb[j, h], preferred_element_type=F32)
            cross = jnp.dot(qdh[j, h], state_b[j], preferred_element_type=F32)
            o = inner + cross
            mu = jnp.mean(o, axis=-1, keepdims=True)
            oc = o - mu
            var = jnp.mean(oc * oc, axis=-1, keepdims=True)
            o = oc * lax.rsqrt(var + LN_EPS)
            gh = g_ref[rows, head_cols[h]]
            o_ref[rows, head_cols[h]] = (gh * jax.nn.sigmoid(gh) * o).astype(BF16)


def _retention(z, cos, sin, *, bsz, n_pos, tm=512):
    tm = min(tm, n_pos)
    nt = n_pos // tm
    w = RET_HEADS * RET_QK
    wv = RET_HEADS * RET_V
    zspec = lambda width, off: pl.BlockSpec((tm, width), lambda b, t: (b * nt + t, off // width))
    tspec = pl.BlockSpec((tm, LANES), lambda b, t: (b * nt + t, 0))
    return pl.pallas_call(
        functools.partial(_ret_kernel, tm=tm),
        grid=(bsz, nt),
        in_specs=[zspec(2 * w, OFF_RQ), zspec(2 * w, OFF_RK),
                  zspec(wv, OFF_RV), zspec(wv, OFF_RG), tspec, tspec],
        out_specs=pl.BlockSpec((tm, wv), lambda b, t: (b * nt + t, 0)),
        out_shape=jax.ShapeDtypeStruct((bsz * n_pos, wv), BF16),
        scratch_shapes=[pltpu.VMEM((w, RET_V), F32)],
        compiler_params=_cparams(("parallel", "arbitrary")),
        name="retention",
    )(z, z, z, z, cos, sin)


def _outproj_ln_kernel(y0_ref, y1_ref, y2_ref, y3_ref, w_ref, x_ref, g_ref, b_ref, o_ref):
    parts = 4
    half = o_ref.shape[0] // parts
    accs = []
    for r in range(parts):
        rows = slice(r * half, (r + 1) * half)
        acc = None
        for j, y_ref in enumerate((y0_ref, y1_ref, y2_ref, y3_ref)):
            part = jnp.dot(y_ref[rows, :], w_ref[j * 512:(j + 1) * 512, :], preferred_element_type=F32)
            acc = part if acc is None else acc + part
        accs.append(acc)
    for r in range(parts):
        rows = slice(r * half, (r + 1) * half)
        o_ref[rows, :] = _layer_norm(ALPHA * x_ref[rows, :] + accs[r], g_ref[...], b_ref[...])


def _outproj_ln(ys, w, x, g, b, *, tm=512):
    m, d = x.shape
    tm = min(tm, m)
    yspec = pl.BlockSpec((tm, 512), lambda i: (i, 0))
    return pl.pallas_call(
        _outproj_ln_kernel,
        grid=(m // tm,),
        in_specs=[yspec, yspec, yspec, yspec,
                  pl.BlockSpec(w.shape, lambda i: (0, 0)),
                  pl.BlockSpec((tm, d), lambda i: (i, 0)),
                  pl.BlockSpec((1, d), lambda i: (0, 0)),
                  pl.BlockSpec((1, d), lambda i: (0, 0))],
        out_specs=pl.BlockSpec((tm, d), lambda i: (i, 0)),
        out_shape=jax.ShapeDtypeStruct((m, d), F32),
        compiler_params=_cparams(("parallel",)),
        name="outproj_ln",
    )(*ys, w, x, g, b)


def _rot_cols(w, heads, dim):
    k = w.shape[0]
    w = w.reshape(k, heads, 2, dim // 2)
    return jnp.concatenate([-w[:, :, 1], w[:, :, 0]], axis=-1).reshape(k, heads * dim)


def _pad_cols(w, width):
    return jnp.pad(w, ((0, 0), (0, width - w.shape[1])))


def _wide_w_in(w_in):
    sizes = (S5_WIDTH, MLA_Q_RANK, MLA_KV_RANK, MLA_ROPE,
             RET_HEADS * RET_QK, RET_HEADS * RET_QK, RET_HEADS * RET_V, RET_HEADS * RET_V,
             DIFF_HEADS * 2 * DIFF_QK, DIFF_HEADS * 2 * DIFF_QK, DIFF_HEADS * DIFF_V)
    offs = np.concatenate([[0], np.cumsum(sizes)])
    (s5_u, cq, ckv, kr, rq, rk, rv, rg, dq, dk, dv) = [w_in[:, offs[i]:offs[i + 1]] for i in range(len(sizes))]
    cols = [s5_u, cq, rv, rg, dq, dk, dv,
            rq, _rot_cols(rq, RET_HEADS, RET_QK), rk, _rot_cols(rk, RET_HEADS, RET_QK),
            ckv, _pad_cols(kr, LANES), _pad_cols(_rot_cols(kr, 1, MLA_ROPE), LANES),
            jnp.zeros((w_in.shape[0], IN_WIDE - IN_USED), w_in.dtype)]
    return jnp.concatenate(cols, axis=1).astype(BF16)


def _mla_q_weights(w_uq):
    k = w_uq.shape[0]
    w = w_uq.reshape(k, MLA_HEADS, MLA_NOPE + MLA_ROPE)
    rope = w[:, :, MLA_NOPE:]
    pad = MLA_QK_PAD - MLA_NOPE - MLA_ROPE
    w1 = jnp.pad(w, ((0, 0), (0, 0), (0, pad))).reshape(k, MLA_HEADS * MLA_QK_PAD)
    rot = _rot_cols(rope.reshape(k, MLA_HEADS * MLA_ROPE), MLA_HEADS, MLA_ROPE).reshape(k, MLA_HEADS, MLA_ROPE)
    w2 = jnp.pad(rot, ((0, 0), (0, 0), (0, LANES - MLA_ROPE))).reshape(k, MLA_HEADS * LANES)
    return w1.astype(BF16), w2.astype(BF16)


def kernel(x, p, positions, rel_bias, ffn1_w_gate, ffn1_w_up, ffn1_w_down, ln1_g, ln1_b, w_in, w_out, ln2_g, ln2_b, s5_lambda_re, s5_lambda_im, s5_log_dt, s5_b_re, s5_b_im, s5_c_re, s5_c_im, s5_d, s5_w_glu, s5_b_glu, mla_q_norm_g, mla_w_uq, mla_kv_norm_g, mla_w_ukv, diff_lambda_q1, diff_lambda_k1, diff_lambda_q2, diff_lambda_k2, diff_subln_g, ffn2_w_gate, ffn2_w_up, ffn2_w_down, ple_w_gate, ple_b_gate, ple_w_proj, ln3_g, ln3_b):
    bsz, n_pos, d = x.shape
    m = bsz * n_pos
    depth = ffn1_w_gate.shape[0]
    assert d == D_MODEL and n_pos % S5_TM == 0 and n_pos % ATTN_TILE == 0 and m % FFN_TM == 0, (x.shape,)
    xf = x.reshape(m, d)
    row = lambda v: v.reshape(1, -1)

    cos, sin = _rope_tables(positions.reshape(m, 1))
    table = _bias_table(rel_bias)
    t_diff = ATTN_TILE
    pos_q = positions.reshape(bsz * (n_pos // t_diff), 1, t_diff)
    pos_k = jnp.broadcast_to(positions.reshape(m, 1), (m, LANES))

    for i in range(depth):
        f1g, f1u, f1d = _cast_layer(i, ffn1_w_gate, ffn1_w_up, ffn1_w_down, col_tile=(FFN_TF, FFN_TF))
        f2g, f2u, f2d, pwg, wo = _cast_layer(i, ffn2_w_gate, ffn2_w_up, ffn2_w_down, ple_w_gate, w_out,
                                             col_tile=(FFN_TF, FFN_TF))
        xf = _ffn_ln(xf, f1g, f1u, f1d, row(ln1_g[i]), row(ln1_b[i]))
        z = _inproj(xf, _wide_w_in(w_in[i]))

        apr, api, bbr, bbi = _s5_discretise(s5_lambda_re[i], s5_lambda_im[i], s5_log_dt[i], s5_b_re[i], s5_b_im[i])
        bh = jnp.concatenate([_block_diag_in(bbr), _block_diag_in(bbi)], axis=2).astype(BF16)
        ch = jnp.concatenate([_block_diag_out(s5_c_re[i]), -_block_diag_out(s5_c_im[i])], axis=1).astype(BF16)
        apr = jnp.repeat(apr, S5_TM // S5_STEPS, axis=0)
        api = jnp.repeat(api, S5_TM // S5_STEPS, axis=0)
        y_s5 = _s5_mixer(z, bh, ch, apr, api, row(s5_d[i]), s5_w_glu[i].astype(BF16), row(s5_b_glu[i]),
                         bsz=bsz, n_pos=n_pos)

        wq1, wq2 = _mla_q_weights(mla_w_uq[i])
        q, k, vt, dqm, dkb, dvt = _attn_prep(z, cos, sin, row(mla_q_norm_g[i]), row(mla_kv_norm_g[i]), wq1, wq2,
                                             mla_w_ukv[i].astype(BF16), bsz=bsz, n_pos=n_pos)
        y_mla = _mla_flash(q, k, vt).reshape(m, MLA_HEADS * MLA_V)

        y_ret = _retention(z, cos, sin, bsz=bsz, n_pos=n_pos)

        lambda_init = 0.8 - 0.6 * math.exp(-0.3 * i)
        y_diff = _diff_attn(dqm, dkb, dvt, pos_q, pos_k, table, row(diff_lambda_q1[i]), row(diff_lambda_k1[i]),
                            row(diff_lambda_q2[i]), row(diff_lambda_k2[i]), row(diff_subln_g[i]),
                            bsz=bsz, n_pos=n_pos, lambda_init=lambda_init, t=t_diff)

        xf = _outproj_ln((y_s5, y_mla, y_ret, y_diff), wo, xf, row(ln2_g[i]), row(ln2_b[i]))

        res, xb = _ple(xf, p.reshape(depth, m, PLE_DIM), i, pwg, row(ple_b_gate[i]), ple_w_proj[i].astype(BF16))
        xf = _ffn_ln(xb, f2g, f2u, f2d, row(ln3_g[i]), row(ln3_b[i]), res)
    return xf.reshape(bsz, n_pos, d)
```
